```python
import math
import jax, jax.numpy as jnp
from jax import lax
import numpy as np

D_MODEL = 1024
BATCH = 8
SEQ = 4096
DEPTH = 2

N_A_LAYERS = max(1, DEPTH // 2)
N_B_LAYERS = DEPTH - N_A_LAYERS
SGU_CHUNK = 128
SGU_GROUPS = 8
D_SGU = 3 * D_MODEL
SGU_GROUP_DIM = D_SGU // SGU_GROUPS
HEAD_DIM = 64
N_HEADS = D_MODEL // HEAD_DIM
MOBA_BLOCK = 256
MOBA_TOPK = 3
Q_CHUNK = 128
N_EXPERTS = 32
TOP_K = 4
D_EXPERT = D_MODEL
SWIGLU_LIMIT = 7.0
SWIGLU_ALPHA = 1.702
EXPERT_BLOCK = 256
EPS = 1e-6

kernel_name = 'hybrid_sgu_moba_moe'


def _rmsnorm(x, g):
    xf = x.astype(jnp.float32)
    y = xf * lax.rsqrt(jnp.mean(xf * xf, axis=-1, keepdims=True) + EPS)
    return (y * g.astype(jnp.float32)).astype(x.dtype)


def _modulate(x, g, shift, scale):
    return _rmsnorm(x, g) * (1 + scale[:, None, :]) + shift[:, None, :]


def _alibi_slopes():
    return 2.0 ** (-8.0 * jnp.arange(1, N_HEADS + 1, dtype=jnp.float32) / N_HEADS)


def _chunked_sgu(h, w_in, b_in, v_g, w_s, b_s, w_out):
    B, S, _ = h.shape
    z = jax.nn.gelu(h @ w_in + b_in)
    u, v = jnp.split(z, 2, axis=-1)
    v = _rmsnorm(v, v_g).reshape(B, S // SGU_CHUNK, SGU_CHUNK, SGU_GROUPS, SGU_GROUP_DIM)
    causal = jnp.tril(jnp.ones((SGU_CHUNK, SGU_CHUNK), dtype=bool))
    ws = jnp.where(causal, w_s, jnp.zeros((), w_s.dtype))
    mixed = jnp.einsum('gts,bnsgc->bntgc', ws, v) + b_s.T[:, :, None]
    y = u * mixed.reshape(B, S, D_SGU)
    return y @ w_out


def _shared_kv(x, c, kv_ada_w, kv_ada_b, kv_norm_g, w_kv, k_norm_g):
    B, S, _ = x.shape
    shift, scale = jnp.split(jax.nn.silu(c) @ kv_ada_w + kv_ada_b, 2, axis=-1)
    h = _modulate(x, kv_norm_g, shift, scale)
    k, v = jnp.split(h @ w_kv, 2, axis=-1)
    k = _rmsnorm(k.reshape(B, S, N_HEADS, HEAD_DIM), k_norm_g)
    v = v.reshape(B, S, N_HEADS, HEAD_DIM)
    n_blocks = -(-S // MOBA_BLOCK)
    pad = n_blocks * MOBA_BLOCK - S

    def to_blocks(t):
        t = jnp.pad(t, ((0, 0), (0, pad), (0, 0), (0, 0)))
        return t.reshape(B, n_blocks, MOBA_BLOCK, N_HEADS, HEAD_DIM).transpose(0, 3, 1, 2, 4)

    kb, vb = to_blocks(k), to_blocks(v)
    k_mean = jnp.mean(kb.astype(jnp.float32), axis=3).astype(kb.dtype)
    return kb, vb, k_mean


def _moba_attention(q, kb, vb, k_mean):
    B, S, H, Dh = q.shape
    n_blocks = kb.shape[2]
    n_sel = min(MOBA_TOPK, n_blocks)
    n_chunks = S // Q_CHUNK
    scale = HEAD_DIM ** -0.5
    slopes = _alibi_slopes()
    qh = q.transpose(0, 2, 1, 3)
    head_ix = jnp.arange(H)[:, None, None]
    blk_ar = jnp.arange(n_blocks)
    key_ar = jnp.arange(MOBA_BLOCK)

    def chunk(args):
        b, ci = args
        t0 = ci * Q_CHUNK
        own = t0 // MOBA_BLOCK
        qc = lax.dynamic_slice_in_dim(qh[b], t0, Q_CHUNK, axis=1)
        kbb, vbb = kb[b], vb[b]
        qpos = t0 + jnp.arange(Q_CHUNK)
        gate = jnp.einsum('hqd,hnd->hqn', qc, k_mean[b]).astype(jnp.float32)
        gate = jnp.where(blk_ar < own, gate, -jnp.inf)
        _, sel = lax.top_k(gate, n_sel)
        sel_valid = jnp.arange(n_sel) < own
        ksel = kbb[head_ix, sel]
        vsel = vbb[head_ix, sel]
        s_sel = jnp.einsum('hqd,hqjkd->hqjk', qc, ksel).astype(jnp.float32) * scale
        pos_sel = sel[..., None] * MOBA_BLOCK + key_ar
        s_sel = s_sel - slopes[:, None, None, None] * jnp.abs(qpos[None, :, None, None] - pos_sel).astype(jnp.float32)
        s_sel = jnp.where(sel_valid[None, None, :, None], s_sel, -jnp.inf)
        kown, vown = kbb[:, own], vbb[:, own]
        d_own = qpos[:, None] - (own * MOBA_BLOCK + key_ar)[None, :]
        s_own = jnp.einsum('hqd,hkd->hqk', qc, kown).astype(jnp.float32) * scale
        s_own = s_own - slopes[:, None, None] * jnp.abs(d_own).astype(jnp.float32)
        s_own = jnp.where(d_own >= 0, s_own, -jnp.inf)
        n_k = n_sel * MOBA_BLOCK
        p = jax.nn.softmax(jnp.concatenate([s_sel.reshape(H, Q_CHUNK, n_k), s_own], axis=-1), axis=-1).astype(vb.dtype)
        p_sel = p[..., :n_k].reshape(H, Q_CHUNK, n_sel, MOBA_BLOCK)
        p_own = p[..., n_k:]
        o = jnp.einsum('hqjk,hqjkd->hqd', p_sel, vsel) + jnp.einsum('hqk,hkd->hqd', p_own, vown)
        return o.transpose(1, 0, 2).reshape(Q_CHUNK, H * Dh)

    b_ids = jnp.repeat(jnp.arange(B), n_chunks)
    c_ids = jnp.tile(jnp.arange(n_chunks), B)
    out = lax.map(chunk, (b_ids, c_ids))
    return out.reshape(B, S, H * Dh)


def _moba_mixer(h, w_q, q_g, w_o, kb, vb, k_mean):
    B, S, _ = h.shape
    q = _rmsnorm((h @ w_q).reshape(B, S, N_HEADS, HEAD_DIM), q_g)
    return _moba_attention(q, kb, vb, k_mean) @ w_o


def _moe(h, w_router, b_router, w_gate, b_gate, w_up, b_up, w_down, b_down):
    B, S, D = h.shape
    N = B * S
    xf = h.reshape(N, D)
    logits = (xf @ w_router + b_router).astype(jnp.float32)
    top_val, top_idx = lax.top_k(logits, TOP_K)
    probs = jax.nn.softmax(top_val, axis=-1)
    A = N * TOP_K
    e_flat = top_idx.reshape(A)
    tok_flat = jnp.arange(A) // TOP_K
    p_flat = probs.reshape(A)
    order = jnp.argsort(e_flat, stable=True)
    e_s, tok_s, p_s = e_flat[order], tok_flat[order], p_flat[order]
    counts = jnp.bincount(e_flat, length=N_EXPERTS)
    padded = (counts + EXPERT_BLOCK - 1) // EXPERT_BLOCK * EXPERT_BLOCK
    start = jnp.cumsum(counts) - counts
    pend = jnp.cumsum(padded)
    pstart = pend - padded
    dest = pstart[e_s] + jnp.arange(A) - start[e_s]
    R = A + N_EXPERTS * EXPERT_BLOCK
    n_blk = R // EXPERT_BLOCK
    row_tok = jnp.full((R,), N, dtype=jnp.int32).at[dest].set(tok_s.astype(jnp.int32))
    row_p = jnp.zeros((R,), jnp.float32).at[dest].set(p_s)
    blk_exp = jnp.minimum(jnp.searchsorted(pend, jnp.arange(n_blk) * EXPERT_BLOCK, side='right'), N_EXPERTS - 1)
    xpad = jnp.concatenate([xf, jnp.zeros((1, D), xf.dtype)], axis=0)

    def expert_block(args):
        toks, e = args
        xb = xpad[toks]
        g = jnp.minimum(xb @ w_gate[e] + b_gate[e], SWIGLU_LIMIT)
        u = jnp.clip(xb @ w_up[e] + b_up[e], -SWIGLU_LIMIT, SWIGLU_LIMIT)
        a = g * jax.nn.sigmoid(SWIGLU_ALPHA * g) * (u + 1)
        return a @ w_down[e] + b_down[e]

    y = lax.map(expert_block, (row_tok.reshape(n_blk, EXPERT_BLOCK), blk_exp))
    y = y.reshape(R, D) * row_p[:, None].astype(y.dtype)
    out = jax.ops.segment_sum(y, row_tok, num_segments=N + 1)[:N]
    return out.reshape(B, S, D)


def setup_inputs(seed: int = 0) -> dict:
    key = jax.random.key(seed)
    ks = iter(jax.random.split(key, 40))
    f32 = jnp.float32

    def nrm(shape, s):
        return jax.random.normal(next(ks), shape, f32) * s

    def gain(shape):
        return 1.0 + nrm(shape, 0.02)

    D, HD = D_MODEL, N_HEADS * HEAD_DIM
    return {
        'x': nrm((BATCH, SEQ, D), 1.0),
        'c': nrm((BATCH, D), 1.0),
        'ada_w': nrm((DEPTH, D, 6 * D), 0.5 * D ** -0.5),
        'ada_b': nrm((DEPTH, 6 * D), 0.02),
        'norm1_g': gain((DEPTH, D)),
        'norm2_g': gain((DEPTH, D)),
        'sgu_w_in': nrm((N_A_LAYERS, D, 2 * D_SGU), D ** -0.5),
        'sgu_b_in': nrm((N_A_LAYERS, 2 * D_SGU), 0.02),
        'sgu_v_g': gain((N_A_LAYERS, D_SGU)),
        'sgu_w_s': nrm((N_A_LAYERS, SGU_GROUPS, SGU_CHUNK, SGU_CHUNK), SGU_CHUNK ** -0.5),
        'sgu_b_s': 1.0 + nrm((N_A_LAYERS, SGU_GROUPS, SGU_CHUNK), 0.1),
        'sgu_w_out': nrm((N_A_LAYERS, D_SGU, D), D_SGU ** -0.5),
        'kv_ada_w': nrm((D, 2 * D), 0.5 * D ** -0.5),
        'kv_ada_b': nrm((2 * D,), 0.02),
        'kv_norm_g': gain((D,)),
        'w_kv': nrm((D, 2 * HD), D ** -0.5),
        'k_norm_g': gain((HEAD_DIM,)),
        'attn_w_q': nrm((N_B_LAYERS, D, HD), D ** -0.5),
        'q_norm_g': gain((N_B_LAYERS, HEAD_DIM)),
        'attn_w_o': nrm((N_B_LAYERS, HD, D), HD ** -0.5),
        'moe_w_router': nrm((DEPTH, D, N_EXPERTS), D ** -0.5),
        'moe_b_router': nrm((DEPTH, N_EXPERTS), 0.01),
        'moe_w_gate': nrm((DEPTH, N_EXPERTS, D, D_EXPERT), D ** -0.5),
        'moe_b_gate': nrm((DEPTH, N_EXPERTS, D_EXPERT), 0.02),
        'moe_w_up': nrm((DEPTH, N_EXPERTS, D, D_EXPERT), D ** -0.5),
        'moe_b_up': nrm((DEPTH, N_EXPERTS, D_EXPERT), 0.02),
        'moe_w_down': nrm((DEPTH, N_EXPERTS, D_EXPERT, D), D_EXPERT ** -0.5),
        'moe_b_down': nrm((DEPTH, N_EXPERTS, D), 0.02),
    }


def reference(x, c, ada_w, ada_b, norm1_g, norm2_g, sgu_w_in, sgu_b_in, sgu_v_g, sgu_w_s, sgu_b_s, sgu_w_out,
              kv_ada_w, kv_ada_b, kv_norm_g, w_kv, k_norm_g, attn_w_q, q_norm_g, attn_w_o,
              moe_w_router, moe_b_router, moe_w_gate, moe_b_gate, moe_w_up, moe_b_up, moe_w_down, moe_b_down):
    c_act = jax.nn.silu(c)
    kb = vb = k_mean = None
    for l in range(DEPTH):
        mod = c_act @ ada_w[l] + ada_b[l]
        sh1, sc1, g1, sh2, sc2, g2 = jnp.split(mod, 6, axis=-1)
        if l < N_A_LAYERS:
            h = _modulate(x, norm1_g[l], sh1, sc1)
            m = _chunked_sgu(h, sgu_w_in[l], sgu_b_in[l], sgu_v_g[l], sgu_w_s[l], sgu_b_s[l], sgu_w_out[l])
        else:
            if l == N_A_LAYERS:
                kb, vb, k_mean = _shared_kv(x, c, kv_ada_w, kv_ada_b, kv_norm_g, w_kv, k_norm_g)
            j = l - N_A_LAYERS
            h = _modulate(x, norm1_g[l], sh1, sc1)
            m = _moba_mixer(h, attn_w_q[j], q_norm_g[j], attn_w_o[j], kb, vb, k_mean)
        x = x + g1[:, None, :] * m
        h2 = _modulate(x, norm2_g[l], sh2, sc2)
        x = x + g2[:, None, :] * _moe(h2, moe_w_router[l], moe_b_router[l], moe_w_gate[l], moe_b_gate[l],
                                       moe_w_up[l], moe_b_up[l], moe_w_down[l], moe_b_down[l])
    return x
```

```python
import functools
import math

import jax
import jax.numpy as jnp
import numpy as np
from jax import lax
from jax.experimental import pallas as pl
from jax.experimental.pallas import tpu as pltpu

F32 = jnp.float32
BF16 = jnp.bfloat16
HIGHEST = lax.Precision.HIGHEST
SDS = jax.ShapeDtypeStruct

D_MODEL = 1024
SGU_CHUNK = 128
SGU_GROUPS = 8
D_SGU = 3 * D_MODEL
SGU_GROUP_DIM = D_SGU // SGU_GROUPS
HEAD_DIM = 64
N_HEADS = D_MODEL // HEAD_DIM
MOBA_BLOCK = 256
MOBA_TOPK = 3
Q_CHUNK = 128
N_EXPERTS = 32
TOP_K = 4
SWIGLU_LIMIT = 7.0
SWIGLU_ALPHA = 1.702
EXPERT_BLOCK = 256
EPS = 1e-6

VMEM_LIMIT_BYTES = 56 * 1024 * 1024
NT_DIMS = (((1,), (1,)), ((), ()))


def _cparams(sem):
    return pltpu.CompilerParams(dimension_semantics=sem, vmem_limit_bytes=VMEM_LIMIT_BYTES)


def _dot(a, b, **kw):
    return jnp.dot(a, b, preferred_element_type=F32, **kw)


def _norm_mod(x, g, sc, sh):
    ms = jnp.mean(x * x, axis=-1, keepdims=True)
    return (x * lax.rsqrt(ms + EPS) * g) * (1.0 + sc) + sh


def _ada_body(c_ref, w_ref, b_ref, o_ref):
    c = c_ref[...]
    ca = c * jax.nn.sigmoid(c)
    o_ref[...] = _dot(ca, w_ref[...], precision=HIGHEST) + b_ref[...]


def _ada(c, w, b):
    bsz, d = c.shape
    n = w.shape[-1]
    tn = 1024
    return pl.pallas_call(
        _ada_body,
        grid=(n // tn,),
        in_specs=[
            pl.BlockSpec((bsz, d), lambda j: (0, 0)),
            pl.BlockSpec((d, tn), lambda j: (0, j)),
            pl.BlockSpec((1, tn), lambda j: (0, j)),
        ],
        out_specs=pl.BlockSpec((bsz, tn), lambda j: (0, j)),
        out_shape=SDS((bsz, n), F32),
        compiler_params=_cparams(("arbitrary",)),
        name="ada",
    )(c, w, b.reshape(1, n))


SGU_TS = 256
SGU_CB = 768


def _gelu_tanh(x):
    c = np.float32(np.sqrt(2.0 / np.pi))
    return x * (0.5 * (1.0 + jnp.tanh(c * (x + 0.044715 * (x * x * x)))))


def _sgu_body(x_ref, mod_ref, n1g_ref, win_ref, bin_ref, vg_ref, ws_ref, bst_ref, wout_ref,
              o_ref, u_scr, v_scr, y_scr):
    ts = x_ref.shape[1]
    x = x_ref[0]
    mod = mod_ref[0]
    sh1, sc1, g1 = mod[0:1], mod[1:2], mod[2:3]
    h = _norm_mod(x, n1g_ref[...], sc1, sh1).astype(BF16)

    for j in range(D_SGU // SGU_CB):
        cs = slice(j * SGU_CB, (j + 1) * SGU_CB)
        u_scr[:, cs] = _gelu_tanh(_dot(h, win_ref[:, cs]) + bin_ref[:, cs])
    ssq = jnp.zeros((ts, 1), F32)
    for j in range(D_SGU // SGU_CB):
        cs = slice(j * SGU_CB, (j + 1) * SGU_CB)
        ws_cols = slice(D_SGU + j * SGU_CB, D_SGU + (j + 1) * SGU_CB)
        z = _gelu_tanh(_dot(h, win_ref[:, ws_cols]) + bin_ref[:, ws_cols])
        v_scr[:, cs] = z
        ssq = ssq + jnp.sum(z * z, axis=-1, keepdims=True)
    rs = lax.rsqrt(ssq / D_SGU + EPS)

    r_i = lax.broadcasted_iota(jnp.int32, (SGU_CHUNK, SGU_CHUNK), 0)
    c_i = lax.broadcasted_iota(jnp.int32, (SGU_CHUNK, SGU_CHUNK), 1)
    causal = c_i <= r_i
    for g in range(SGU_GROUPS):
        cs = slice(g * SGU_GROUP_DIM, (g + 1) * SGU_GROUP_DIM)
        wsg = jnp.where(causal, ws_ref[g], 0.0).astype(BF16)
        bsg = bst_ref[:, g:g + 1]
        for n in range(ts // SGU_CHUNK):
            rows = slice(n * SGU_CHUNK, (n + 1) * SGU_CHUNK)
            vn = (v_scr[rows, cs] * rs[rows] * vg_ref[:, cs]).astype(BF16)
            mixed = _dot(wsg, vn) + bsg
            y_scr[rows, cs] = (u_scr[rows, cs] * mixed).astype(BF16)

    m = _dot(y_scr[...], wout_ref[...])
    o_ref[0] = x + g1 * m


def _sgu(x, mod, n1g, w_in, b_in, v_g, w_s, b_s, w_out):
    bsz, s, d = x.shape
    ts = SGU_TS
    const = lambda *shape: pl.BlockSpec(shape, lambda b, i: (0,) * len(shape),
                                        pipeline_mode=pl.Buffered(1))
    return pl.pallas_call(
        _sgu_body,
        grid=(bsz, s // ts),
        in_specs=[
            pl.BlockSpec((1, ts, d), lambda b, i: (b, i, 0)),
            pl.BlockSpec((1, 6, d), lambda b, i: (b, 0, 0)),
            const(1, d),
            const(d, 2 * D_SGU),
            const(1, 2 * D_SGU),
            const(1, D_SGU),
            const(SGU_GROUPS, SGU_CHUNK, SGU_CHUNK),
            const(SGU_CHUNK, SGU_GROUPS),
            const(D_SGU, d),
        ],
        out_specs=pl.BlockSpec((1, ts, d), lambda b, i: (b, i, 0)),
        out_shape=SDS((bsz, s, d), F32),
        scratch_shapes=[
            pltpu.VMEM((ts, D_SGU), F32),
            pltpu.VMEM((ts, D_SGU), F32),
            pltpu.VMEM((ts, D_SGU), BF16),
        ],
        compiler_params=_cparams(("parallel", "arbitrary")),
        name="sgu",
    )(x, mod, n1g.reshape(1, d), w_in.astype(BF16), b_in.reshape(1, -1), v_g.reshape(1, -1),
      w_s, b_s.T, w_out.astype(BF16))


ROUTER_TS = 512


def _router_body(x_ref, mod_ref, n2g_ref, wr_ref, br_ref, h_ref, idx_ref, p_ref):
    x = x_ref[0]
    mod = mod_ref[0]
    sh2, sc2 = mod[3:4], mod[4:5]
    h2 = _norm_mod(x, n2g_ref[...], sc2, sh2)
    h_ref[0] = h2
    logits = _dot(h2, wr_ref[...], precision=HIGHEST) + br_ref[...]
    lane = lax.broadcasted_iota(jnp.int32, logits.shape, 1)
    work = logits
    vals, idxs = [], []
    for _ in range(TOP_K):
        mx = jnp.max(work, axis=-1, keepdims=True)
        ix = jnp.min(jnp.where(work == mx, lane, N_EXPERTS), axis=-1, keepdims=True)
        vals.append(mx)
        idxs.append(ix)
        work = jnp.where(lane == ix, -jnp.inf, work)
    tv = jnp.concatenate(vals, axis=-1)
    e = jnp.exp(tv - vals[0])
    p_ref[0] = e / jnp.sum(e, axis=-1, keepdims=True)
    idx_ref[0] = jnp.concatenate(idxs, axis=-1)


def _router(x, mod, n2g, w_r, b_r):
    bsz, s, d = x.shape
    ts = min(ROUTER_TS, s)
    tok = lambda w: pl.BlockSpec((1, ts, w), lambda b, i: (b, i, 0))
    return pl.pallas_call(
        _router_body,
        grid=(bsz, s // ts),
        in_specs=[
            tok(d),
            pl.BlockSpec((1, 6, d), lambda b, i: (b, 0, 0)),
            pl.BlockSpec((1, d), lambda b, i: (0, 0)),
            pl.BlockSpec((d, N_EXPERTS), lambda b, i: (0, 0)),
            pl.BlockSpec((1, N_EXPERTS), lambda b, i: (0, 0)),
        ],
        out_specs=[tok(d), tok(TOP_K), tok(TOP_K)],
        out_shape=[SDS((bsz, s, d), F32), SDS((bsz, s, TOP_K), jnp.int32), SDS((bsz, s, TOP_K), F32)],
        compiler_params=_cparams(("parallel", "arbitrary")),
        name="router",
    )(x, mod, n2g.reshape(1, d), w_r, b_r.reshape(1, -1))


def _route_meta(top_idx):
    n = top_idx.shape[0]
    onehot = jnp.any(top_idx[:, :, None] == jnp.arange(N_EXPERTS)[None, None, :], axis=1).astype(jnp.int32)
    csum = jnp.cumsum(onehot, axis=0)
    counts = csum[-1]
    rank = csum - onehot
    padded = (counts + EXPERT_BLOCK - 1) // EXPERT_BLOCK * EXPERT_BLOCK
    pend = jnp.cumsum(padded)
    pstart = pend - padded
    dest = jnp.take_along_axis(pstart[None, :] + rank, top_idx, axis=1).astype(jnp.int32)
    n_rows = n * TOP_K + N_EXPERTS * EXPERT_BLOCK
    n_blk = n_rows // EXPERT_BLOCK
    blk_exp = jnp.minimum(
        jnp.searchsorted(pend, jnp.arange(n_blk) * EXPERT_BLOCK, side="right"), N_EXPERTS - 1).astype(jnp.int32)
    n_used = (pend[-1] // EXPERT_BLOCK).astype(jnp.int32).reshape(1)
    return dest, blk_exp, n_used, n_rows


MOVE_TT = 256


def _dispatch_body(dest_ref, h_ref, xs_in_ref, xs_ref, sem):
    del xs_in_ref
    tt = h_ref.shape[0]

    def row_copy(t, d):
        return pltpu.make_async_copy(h_ref.at[pl.ds(t, 1)], xs_ref.at[pl.ds(d, 1)], sem)

    def issue(t, carry):
        for k in range(TOP_K):
            row_copy(t, dest_ref[0, t * TOP_K + k]).start()
        return carry

    lax.fori_loop(0, tt, issue, 0)

    def drain(t, carry):
        for k in range(TOP_K):
            row_copy(t, dest_ref[0, t * TOP_K + k]).wait()
        return carry

    lax.fori_loop(0, tt, drain, 0)


def _dispatch(h, dest, n_rows):
    n, d = h.shape
    tt = MOVE_TT
    dest3 = dest.reshape(n // tt, 1, tt * TOP_K)
    xs0 = jnp.zeros((n_rows, d), h.dtype)
    return pl.pallas_call(
        _dispatch_body,
        grid=(n // tt,),
        in_specs=[
            pl.BlockSpec((None, 1, tt * TOP_K), lambda i: (i, 0, 0), memory_space=pltpu.SMEM),
            pl.BlockSpec((tt, d), lambda i: (i, 0)),
            pl.BlockSpec(memory_space=pl.ANY),
        ],
        out_specs=pl.BlockSpec(memory_space=pl.ANY),
        out_shape=SDS((n_rows, d), h.dtype),
        scratch_shapes=[pltpu.SemaphoreType.DMA(())],
        input_output_aliases={2: 0},
        compiler_params=_cparams(("arbitrary",)),
        name="dispatch",
    )(dest3, h, xs0)


def _ffn_body(be_ref, nu_ref, xs_ref, wg_ref, bg_ref, wu_ref, bu_ref, wd_ref, bd_ref, y_ref):
    del be_ref

    @pl.when(pl.program_id(0) < nu_ref[0])
    def _():
        x = xs_ref[...].astype(BF16)
        g = jnp.minimum(_dot(x, wg_ref[...]) + bg_ref[...], SWIGLU_LIMIT)
        u = jnp.clip(_dot(x, wu_ref[...]) + bu_ref[...], -SWIGLU_LIMIT, SWIGLU_LIMIT)
        a = g * jax.nn.sigmoid(SWIGLU_ALPHA * g) * (u + 1.0)
        y_ref[...] = _dot(a.astype(BF16), wd_ref[...]) + bd_ref[...]

    @pl.when(pl.program_id(0) >= nu_ref[0])
    def _():
        y_ref[...] = jnp.zeros(y_ref.shape, y_ref.dtype)


def _ffn(xs, blk_exp, n_used, w_gate, b_gate, w_up, b_up, w_down, b_down):
    n_rows, d = xs.shape
    e, _, de = w_gate.shape
    n_blk = n_rows // EXPERT_BLOCK
    wspec = lambda r, c: pl.BlockSpec((None, r, c), lambda i, be, nu: (be[i], 0, 0))
    return pl.pallas_call(
        _ffn_body,
        grid_spec=pltpu.PrefetchScalarGridSpec(
            num_scalar_prefetch=2,
            grid=(n_blk,),
            in_specs=[
                pl.BlockSpec((EXPERT_BLOCK, d), lambda i, be, nu: (i, 0)),
                wspec(d, de), wspec(1, de),
                wspec(d, de), wspec(1, de),
                wspec(de, d), wspec(1, d),
            ],
            out_specs=pl.BlockSpec((EXPERT_BLOCK, d), lambda i, be, nu: (i, 0)),
        ),
        out_shape=SDS((n_rows, d), F32),
        compiler_params=_cparams(("arbitrary",)),
        name="ffn",
    )(blk_exp, n_used, xs, w_gate.astype(BF16), b_gate.reshape(e, 1, de), w_up.astype(BF16),
      b_up.reshape(e, 1, de), w_down.astype(BF16), b_down.reshape(e, 1, d))


def _combine_body(dest_ref, x_ref, mod_ref, p_ref, y_ref, o_ref, buf, sem):
    tt = x_ref.shape[0]

    def row_copy(t, k, d):
        return pltpu.make_async_copy(y_ref.at[pl.ds(d, 1)], buf.at[k, pl.ds(t, 1)], sem)

    def issue(t, carry):
        for k in range(TOP_K):
            row_copy(t, k, dest_ref[0, t * TOP_K + k]).start()
        return carry

    lax.fori_loop(0, tt, issue, 0)

    def drain(t, carry):
        for k in range(TOP_K):
            row_copy(t, k, dest_ref[0, t * TOP_K + k]).wait()
        return carry

    lax.fori_loop(0, tt, drain, 0)

    p = p_ref[...]
    acc = p[:, 0:1] * buf[0]
    for k in range(1, TOP_K):
        acc = acc + p[:, k:k + 1] * buf[k]
    g2 = mod_ref[0][5:6]
    o_ref[...] = x_ref[...] + g2 * acc


def _combine(x, mod, p, y, dest, seq):
    n, d = x.shape
    tt = MOVE_TT
    dest3 = dest.reshape(n // tt, 1, tt * TOP_K)
    per_b = seq // tt
    return pl.pallas_call(
        _combine_body,
        grid=(n // tt,),
        in_specs=[
            pl.BlockSpec((None, 1, tt * TOP_K), lambda i: (i, 0, 0), memory_space=pltpu.SMEM),
            pl.BlockSpec((tt, d), lambda i: (i, 0)),
            pl.BlockSpec((1, 6, d), lambda i: (i // per_b, 0, 0)),
            pl.BlockSpec((tt, TOP_K), lambda i: (i, 0)),
            pl.BlockSpec(memory_space=pl.ANY),
        ],
        out_specs=pl.BlockSpec((tt, d), lambda i: (i, 0)),
        out_shape=SDS((n, d), F32),
        scratch_shapes=[pltpu.VMEM((TOP_K, tt, d), F32), pltpu.SemaphoreType.DMA(())],
        compiler_params=_cparams(("arbitrary",)),
        name="combine",
    )(dest3, x, mod, p, y)


def _moe(x, mod, n2g, w_r, b_r, w_gate, b_gate, w_up, b_up, w_down, b_down):
    bsz, s, d = x.shape
    n = bsz * s
    h2, top_idx, probs = _router(x, mod, n2g, w_r, b_r)
    dest, blk_exp, n_used, n_rows = _route_meta(top_idx.reshape(n, TOP_K))
    xs = _dispatch(h2.reshape(n, d), dest, n_rows)
    y = _ffn(xs, blk_exp, n_used, w_gate, b_gate, w_up, b_up, w_down, b_down)
    out = _combine(x.reshape(n, d), mod, probs.reshape(n, TOP_K), y, dest, s)
    return out.reshape(bsz, s, d)


def _seg_mats():
    r = lax.broadcasted_iota(jnp.int32, (D_MODEL, N_HEADS), 0) // HEAD_DIM
    c = lax.broadcasted_iota(jnp.int32, (D_MODEL, N_HEADS), 1)
    seg = jnp.where(r == c, 1.0, 0.0).astype(BF16)
    rt = lax.broadcasted_iota(jnp.int32, (N_HEADS, D_MODEL), 0)
    ct = lax.broadcasted_iota(jnp.int32, (N_HEADS, D_MODEL), 1) // HEAD_DIM
    seg_t = jnp.where(rt == ct, 1.0, 0.0).astype(BF16)
    return seg, seg_t


def _split_dot(a, b):
    hi = a.astype(BF16)
    lo = (a - hi.astype(F32)).astype(BF16)
    return _dot(hi, b) + _dot(lo, b)


def _head_rmsnorm(x, g, seg, seg_t):
    ms = _split_dot(x * x, seg) * (1.0 / HEAD_DIM)
    r = lax.rsqrt(ms + EPS)
    return x * _split_dot(r, seg_t) * g


def _kvq_body(x_ref, mod_ref, kvmod_ref, n1g_ref, kvg_ref, wq_ref, wkv_ref, qg_ref, kg_ref,
              q_ref, k_ref, vt_ref, km_ref):
    x = x_ref[0]
    mod = mod_ref[0]
    kvmod = kvmod_ref[0]
    ms = jnp.mean(x * x, axis=-1, keepdims=True)
    xn = x * lax.rsqrt(ms + EPS)
    hq = ((xn * n1g_ref[...]) * (1.0 + mod[1:2]) + mod[0:1]).astype(BF16)
    hk = ((xn * kvg_ref[...]) * (1.0 + kvmod[1:2]) + kvmod[0:1]).astype(BF16)
    seg, seg_t = _seg_mats()
    q = _head_rmsnorm(_dot(hq, wq_ref[...]), qg_ref[...], seg, seg_t)
    q_ref[0] = q
    kv = _dot(hk, wkv_ref[...])
    k = _head_rmsnorm(kv[:, :D_MODEL], kg_ref[...], seg, seg_t)
    k_ref[0, 0] = k.astype(BF16)
    km_ref[0, 0] = jnp.mean(k, axis=0, keepdims=True)
    vt_ref[0, 0] = kv[:, D_MODEL:].T.astype(BF16)


def _kvq(x, mod, kvmod, n1g, kvg, w_q, w_kv, q_g, k_g):
    bsz, s, d = x.shape
    nb = s // MOBA_BLOCK
    ts = MOBA_BLOCK
    const = lambda *shape: pl.BlockSpec(shape, lambda b, i: (0,) * len(shape))
    return pl.pallas_call(
        _kvq_body,
        grid=(bsz, nb),
        in_specs=[
            pl.BlockSpec((1, ts, d), lambda b, i: (b, i, 0)),
            pl.BlockSpec((1, 6, d), lambda b, i: (b, 0, 0)),
            pl.BlockSpec((1, 2, d), lambda b, i: (b, 0, 0)),
            const(1, d), const(1, d), const(d, d), const(d, 2 * d), const(1, d), const(1, d),
        ],
        out_specs=[
            pl.BlockSpec((1, ts, d), lambda b, i: (b, i, 0)),
            pl.BlockSpec((1, 1, ts, d), lambda b, i: (b, i, 0, 0)),
            pl.BlockSpec((1, 1, d, ts), lambda b, i: (b, i, 0, 0)),
            pl.BlockSpec((1, 1, 1, d), lambda b, i: (b, i, 0, 0)),
        ],
        out_shape=[
            SDS((bsz, s, d), F32),
            SDS((bsz, nb, ts, d), BF16),
            SDS((bsz, nb, d, ts), BF16),
            SDS((bsz, nb, 1, d), F32),
        ],
        compiler_params=_cparams(("parallel", "arbitrary")),
        name="kvq",
    )(x, mod, kvmod, n1g.reshape(1, d), kvg.reshape(1, d), w_q.astype(BF16), w_kv.astype(BF16),
      jnp.tile(q_g, N_HEADS).reshape(1, d), jnp.tile(k_g, N_HEADS).reshape(1, d))


def _attn_body(slopes_ref, q_ref, k_ref, vt_ref, km_ref, o_ref):
    hp = pl.program_id(1)
    c = pl.program_id(2)
    nb = k_ref.shape[1]
    t0 = c * Q_CHUNK
    own = t0 // MOBA_BLOCK
    q = q_ref[0]
    km = km_ref[0]
    dcol = lax.broadcasted_iota(jnp.int32, (1, 2 * HEAD_DIM), 1)
    blk_row = lax.broadcasted_iota(jnp.int32, (nb, 1), 0)
    kpos0 = lax.broadcasted_iota(jnp.int32, (MOBA_BLOCK, 1), 0)
    qpos = t0 + lax.broadcasted_iota(jnp.int32, (1, Q_CHUNK), 1)
    valid = blk_row < own

    outs = []
    for hd in range(2):
        slope = slopes_ref[2 * hp + hd]
        in_head = (dcol >= hd * HEAD_DIM) & (dcol < (hd + 1) * HEAD_DIM)
        qm = jnp.where(in_head, q, 0.0)
        gate = lax.dot_general(km, qm, NT_DIMS, precision=HIGHEST, preferred_element_type=F32)
        gate = jnp.where(valid, gate, -jnp.inf)
        rank = jnp.zeros(gate.shape, jnp.int32)
        for i in range(nb):
            gi = gate[i:i + 1, :]
            ahead = (gi > gate) | ((gi == gate) & (blk_row > i))
            rank = rank + ahead.astype(jnp.int32)
        sel = jnp.where(valid & (rank < MOBA_TOPK), 1.0, 0.0)
        qs = (qm * (HEAD_DIM ** -0.5)).astype(BF16)

        def scores(j):
            kj = k_ref[0, j]
            st = lax.dot_general(kj, qs, NT_DIMS, preferred_element_type=F32)
            dist = qpos - (j * MOBA_BLOCK + kpos0)
            return st - slope * jnp.abs(dist).astype(F32), dist

        def update(j, st, carry):
            m, l, acc = carry
            m_new = jnp.maximum(m, jnp.max(st, axis=0, keepdims=True))
            m_safe = jnp.where(m_new == -jnp.inf, 0.0, m_new)
            p = jnp.exp(st - m_safe)
            alpha = jnp.exp(m - m_safe)
            l = alpha * l + jnp.sum(p, axis=0, keepdims=True)
            vj = vt_ref[0, j, hd * HEAD_DIM:(hd + 1) * HEAD_DIM, :]
            acc = alpha * acc + _dot(vj, p.astype(BF16))
            return m_new, l, acc

        def past_block(j, carry):
            st, _ = scores(j)
            sel_j = jnp.sum(jnp.where(blk_row == j, sel, 0.0), axis=0, keepdims=True)
            st = jnp.where(sel_j > 0.5, st, -jnp.inf)
            return update(j, st, carry)

        init = (jnp.full((1, Q_CHUNK), -jnp.inf, F32), jnp.zeros((1, Q_CHUNK), F32),
                jnp.zeros((HEAD_DIM, Q_CHUNK), F32))
        carry = lax.fori_loop(0, own, past_block, init)
        st, dist = scores(own)
        st = jnp.where(dist >= 0, st, -jnp.inf)
        _, l, acc = update(own, st, carry)
        outs.append(acc / l)
    o_ref[0] = jnp.concatenate(outs, axis=0).T.astype(o_ref.dtype)


def _attn(q, kb, vtb, kmean):
    bsz, s, d = q.shape
    nb = s // MOBA_BLOCK
    w = 2 * HEAD_DIM
    slopes = (2.0 ** (-8.0 * jnp.arange(1, N_HEADS + 1, dtype=F32) / N_HEADS)).astype(F32)
    return pl.pallas_call(
        _attn_body,
        grid_spec=pltpu.PrefetchScalarGridSpec(
            num_scalar_prefetch=1,
            grid=(bsz, N_HEADS // 2, s // Q_CHUNK),
            in_specs=[
                pl.BlockSpec((1, Q_CHUNK, w), lambda b, h, c, sl: (b, c, h)),
                pl.BlockSpec((1, nb, MOBA_BLOCK, w), lambda b, h, c, sl: (b, 0, 0, h)),
                pl.BlockSpec((1, nb, w, MOBA_BLOCK), lambda b, h, c, sl: (b, 0, h, 0)),
                pl.BlockSpec((1, nb, w), lambda b, h, c, sl: (b, 0, h)),
            ],
            out_specs=pl.BlockSpec((1, Q_CHUNK, w), lambda b, h, c, sl: (b, c, h)),
        ),
        out_shape=SDS((bsz, s, d), BF16),
        compiler_params=_cparams(("parallel", "parallel", "arbitrary")),
        name="attn",
    )(slopes, q, kb, vtb, kmean)


OPROJ_TS = 512


def _oproj_body(a_ref, x_ref, mod_ref, wo_ref, o_ref):
    g1 = mod_ref[0][2:3]
    o_ref[0] = x_ref[0] + g1 * _dot(a_ref[0], wo_ref[...])


def _oproj(a, x, mod, w_o):
    bsz, s, d = x.shape
    ts = min(OPROJ_TS, s)
    tok = pl.BlockSpec((1, ts, d), lambda b, i: (b, i, 0))
    return pl.pallas_call(
        _oproj_body,
        grid=(bsz, s // ts),
        in_specs=[tok, tok, pl.BlockSpec((1, 6, d), lambda b, i: (b, 0, 0)),
                  pl.BlockSpec((d, d), lambda b, i: (0, 0))],
        out_specs=tok,
        out_shape=SDS((bsz, s, d), F32),
        compiler_params=_cparams(("parallel", "arbitrary")),
        name="oproj",
    )(a, x, mod, w_o.astype(BF16))


def kernel(x, c, ada_w, ada_b, norm1_g, norm2_g, sgu_w_in, sgu_b_in, sgu_v_g, sgu_w_s, sgu_b_s, sgu_w_out,
           kv_ada_w, kv_ada_b, kv_norm_g, w_kv, k_norm_g, attn_w_q, q_norm_g, attn_w_o,
           moe_w_router, moe_b_router, moe_w_gate, moe_b_gate, moe_w_up, moe_b_up, moe_w_down, moe_b_down):
    bsz, s, d = x.shape
    moe = lambda l, xx, mod: _moe(xx, mod, norm2_g[l], moe_w_router[l], moe_b_router[l],
                                  moe_w_gate[l], moe_b_gate[l], moe_w_up[l], moe_b_up[l],
                                  moe_w_down[l], moe_b_down[l])
    mod0 = _ada(c, ada_w[0], ada_b[0]).reshape(bsz, 6, d)
    x = _sgu(x, mod0, norm1_g[0], sgu_w_in[0], sgu_b_in[0], sgu_v_g[0], sgu_w_s[0], sgu_b_s[0], sgu_w_out[0])
    x = moe(0, x, mod0)
    mod1 = _ada(c, ada_w[1], ada_b[1]).reshape(bsz, 6, d)
    kvmod = _ada(c, kv_ada_w, kv_ada_b).reshape(bsz, 2, d)
    q, kb, vtb, kmean = _kvq(x, mod1, kvmod, norm1_g[1], kv_norm_g, attn_w_q[0], w_kv, q_norm_g[0], k_norm_g)
    a = _attn(q, kb, vtb, kmean.reshape(bsz, s // MOBA_BLOCK, d))
    x = _oproj(a, x, mod1, attn_w_o[0])
    x = moe(1, x, mod1)
    return x
```

```python
import functools
import math

import jax
import jax.numpy as jnp
import numpy as np
from jax import lax
from jax.experimental import pallas as pl
from jax.experimental.pallas import tpu as pltpu

F32 = jnp.float32
BF16 = jnp.bfloat16
HIGHEST = lax.Precision.HIGHEST
SDS = jax.ShapeDtypeStruct

D_MODEL = 1024
SGU_CHUNK = 128
SGU_GROUPS = 8
D_SGU = 3 * D_MODEL
SGU_GROUP_DIM = D_SGU // SGU_GROUPS
HEAD_DIM = 64
N_HEADS = D_MODEL // HEAD_DIM
MOBA_BLOCK = 256
MOBA_TOPK = 3
ATT_U = 2
ATT_ONES = 16
LOG2E = math.log2(math.e)
N_EXPERTS = 32
TOP_K = 4
SWIGLU_LIMIT = 7.0
SWIGLU_ALPHA = 1.702
EXPERT_BLOCK = 256
EPS = 1e-6

VMEM_LIMIT_BYTES = 56 * 1024 * 1024
NT_DIMS = (((1,), (1,)), ((), ()))


def _cparams(sem):
    return pltpu.CompilerParams(dimension_semantics=sem, vmem_limit_bytes=VMEM_LIMIT_BYTES)


def _dot(a, b, **kw):
    return jnp.dot(a, b, preferred_element_type=F32, **kw)


def _norm_mod(x, g, sc, sh):
    ms = jnp.mean(x * x, axis=-1, keepdims=True)
    return (x * lax.rsqrt(ms + EPS) * g) * (1.0 + sc) + sh


def _ada_body(c_ref, w_ref, b_ref, o_ref):
    c = c_ref[...]
    ca = c * jax.nn.sigmoid(c)
    o_ref[...] = _dot(ca, w_ref[...], precision=HIGHEST) + b_ref[...]


def _ada(c, w, b):
    bsz, d = c.shape
    n = w.shape[-1]
    tn = 1024
    return pl.pallas_call(
        _ada_body,
        grid=(n // tn,),
        in_specs=[
            pl.BlockSpec((bsz, d), lambda j: (0, 0)),
            pl.BlockSpec((d, tn), lambda j: (0, j)),
            pl.BlockSpec((1, tn), lambda j: (0, j)),
        ],
        out_specs=pl.BlockSpec((bsz, tn), lambda j: (0, j)),
        out_shape=SDS((bsz, n), F32),
        compiler_params=_cparams(("arbitrary",)),
        name="ada",
    )(c, w, b.reshape(1, n))


SGU_TS = 256
SGU_CB = 768


def _gelu_tanh(x):
    c = np.float32(np.sqrt(2.0 / np.pi))
    return x * (0.5 * (1.0 + jnp.tanh(c * (x + 0.044715 * (x * x * x)))))


def _sgu_body(x_ref, mod_ref, n1g_ref, win_ref, bin_ref, vg_ref, ws_ref, bst_ref, wout_ref,
              o_ref, u_scr, v_scr, y_scr):
    ts = x_ref.shape[1]
    x = x_ref[0]
    mod = mod_ref[0]
    sh1, sc1, g1 = mod[0:1], mod[1:2], mod[2:3]
    h = _norm_mod(x, n1g_ref[...], sc1, sh1).astype(BF16)

    for j in range(D_SGU // SGU_CB):
        cs = slice(j * SGU_CB, (j + 1) * SGU_CB)
        u_scr[:, cs] = _gelu_tanh(_dot(h, win_ref[:, cs]) + bin_ref[:, cs])
    ssq = jnp.zeros((ts, 1), F32)
    for j in range(D_SGU // SGU_CB):
        cs = slice(j * SGU_CB, (j + 1) * SGU_CB)
        ws_cols = slice(D_SGU + j * SGU_CB, D_SGU + (j + 1) * SGU_CB)
        z = _gelu_tanh(_dot(h, win_ref[:, ws_cols]) + bin_ref[:, ws_cols])
        v_scr[:, cs] = z
        ssq = ssq + jnp.sum(z * z, axis=-1, keepdims=True)
    rs = lax.rsqrt(ssq / D_SGU + EPS)

    r_i = lax.broadcasted_iota(jnp.int32, (SGU_CHUNK, SGU_CHUNK), 0)
    c_i = lax.broadcasted_iota(jnp.int32, (SGU_CHUNK, SGU_CHUNK), 1)
    causal = c_i <= r_i
    for g in range(SGU_GROUPS):
        cs = slice(g * SGU_GROUP_DIM, (g + 1) * SGU_GROUP_DIM)
        wsg = jnp.where(causal, ws_ref[g], 0.0).astype(BF16)
        bsg = bst_ref[:, g:g + 1]
        for n in range(ts // SGU_CHUNK):
            rows = slice(n * SGU_CHUNK, (n + 1) * SGU_CHUNK)
            vn = (v_scr[rows, cs] * rs[rows] * vg_ref[:, cs]).astype(BF16)
            mixed = _dot(wsg, vn) + bsg
            y_scr[rows, cs] = (u_scr[rows, cs] * mixed).astype(BF16)

    m = _dot(y_scr[...], wout_ref[...])
    o_ref[0] = x + g1 * m


def _sgu(x, mod, n1g, w_in, b_in, v_g, w_s, b_s, w_out):
    bsz, s, d = x.shape
    ts = SGU_TS
    const = lambda *shape: pl.BlockSpec(shape, lambda b, i: (0,) * len(shape),
                                        pipeline_mode=pl.Buffered(1))
    return pl.pallas_call(
        _sgu_body,
        grid=(bsz, s // ts),
        in_specs=[
            pl.BlockSpec((1, ts, d), lambda b, i: (b, i, 0)),
            pl.BlockSpec((1, 6, d), lambda b, i: (b, 0, 0)),
            const(1, d),
            const(d, 2 * D_SGU),
            const(1, 2 * D_SGU),
            const(1, D_SGU),
            const(SGU_GROUPS, SGU_CHUNK, SGU_CHUNK),
            const(SGU_CHUNK, SGU_GROUPS),
            const(D_SGU, d),
        ],
        out_specs=pl.BlockSpec((1, ts, d), lambda b, i: (b, i, 0)),
        out_shape=SDS((bsz, s, d), F32),
        scratch_shapes=[
            pltpu.VMEM((ts, D_SGU), F32),
            pltpu.VMEM((ts, D_SGU), F32),
            pltpu.VMEM((ts, D_SGU), BF16),
        ],
        compiler_params=_cparams(("parallel", "arbitrary")),
        name="sgu",
    )(x, mod, n1g.reshape(1, d), w_in.astype(BF16), b_in.reshape(1, -1), v_g.reshape(1, -1),
      w_s, b_s.T, w_out.astype(BF16))


ROUTER_TS = 512


def _router_body(x_ref, mod_ref, n2g_ref, wr_ref, br_ref, h_ref, idx_ref, p_ref):
    x = x_ref[0]
    mod = mod_ref[0]
    sh2, sc2 = mod[3:4], mod[4:5]
    h2 = _norm_mod(x, n2g_ref[...], sc2, sh2)
    h_ref[0] = h2
    logits = _dot(h2, wr_ref[...], precision=HIGHEST) + br_ref[...]
    lane = lax.broadcasted_iota(jnp.int32, logits.shape, 1)
    work = logits
    vals, idxs = [], []
    for _ in range(TOP_K):
        mx = jnp.max(work, axis=-1, keepdims=True)
        ix = jnp.min(jnp.where(work == mx, lane, N_EXPERTS), axis=-1, keepdims=True)
        vals.append(mx)
        idxs.append(ix)
        work = jnp.where(lane == ix, -jnp.inf, work)
    tv = jnp.concatenate(vals, axis=-1)
    e = jnp.exp(tv - vals[0])
    p_ref[0] = e / jnp.sum(e, axis=-1, keepdims=True)
    idx_ref[0] = jnp.concatenate(idxs, axis=-1)


def _router(x, mod, n2g, w_r, b_r):
    bsz, s, d = x.shape
    ts = min(ROUTER_TS, s)
    tok = lambda w: pl.BlockSpec((1, ts, w), lambda b, i: (b, i, 0))
    return pl.pallas_call(
        _router_body,
        grid=(bsz, s // ts),
        in_specs=[
            tok(d),
            pl.BlockSpec((1, 6, d), lambda b, i: (b, 0, 0)),
            pl.BlockSpec((1, d), lambda b, i: (0, 0)),
            pl.BlockSpec((d, N_EXPERTS), lambda b, i: (0, 0)),
            pl.BlockSpec((1, N_EXPERTS), lambda b, i: (0, 0)),
        ],
        out_specs=[tok(d), tok(TOP_K), tok(TOP_K)],
        out_shape=[SDS((bsz, s, d), F32), SDS((bsz, s, TOP_K), jnp.int32), SDS((bsz, s, TOP_K), F32)],
        compiler_params=_cparams(("parallel", "arbitrary")),
        name="router",
    )(x, mod, n2g.reshape(1, d), w_r, b_r.reshape(1, -1))


def _route_meta(top_idx):
    n = top_idx.shape[0]
    onehot = jnp.any(top_idx[:, :, None] == jnp.arange(N_EXPERTS)[None, None, :], axis=1).astype(jnp.int32)
    csum = jnp.cumsum(onehot, axis=0)
    counts = csum[-1]
    rank = csum - onehot
    padded = (counts + EXPERT_BLOCK - 1) // EXPERT_BLOCK * EXPERT_BLOCK
    pend = jnp.cumsum(padded)
    pstart = pend - padded
    dest = jnp.take_along_axis(pstart[None, :] + rank, top_idx, axis=1).astype(jnp.int32)
    n_rows = n * TOP_K + N_EXPERTS * EXPERT_BLOCK
    n_blk = n_rows // EXPERT_BLOCK
    blk_exp = jnp.minimum(
        jnp.searchsorted(pend, jnp.arange(n_blk) * EXPERT_BLOCK, side="right"), N_EXPERTS - 1).astype(jnp.int32)
    n_used = (pend[-1] // EXPERT_BLOCK).astype(jnp.int32).reshape(1)
    return dest, blk_exp, n_used, n_rows


MOVE_TT = 256


def _dispatch_body(dest_ref, h_ref, xs_in_ref, xs_ref, sem):
    del xs_in_ref
    tt = h_ref.shape[0]

    def row_copy(t, d):
        return pltpu.make_async_copy(h_ref.at[pl.ds(t, 1)], xs_ref.at[pl.ds(d, 1)], sem)

    def issue(t, carry):
        for k in range(TOP_K):
            row_copy(t, dest_ref[0, t * TOP_K + k]).start()
        return carry

    lax.fori_loop(0, tt, issue, 0)

    def drain(t, carry):
        for k in range(TOP_K):
            row_copy(t, dest_ref[0, t * TOP_K + k]).wait()
        return carry

    lax.fori_loop(0, tt, drain, 0)


def _dispatch(h, dest, n_rows):
    n, d = h.shape
    tt = MOVE_TT
    dest3 = dest.reshape(n // tt, 1, tt * TOP_K)
    xs0 = jnp.zeros((n_rows, d), h.dtype)
    return pl.pallas_call(
        _dispatch_body,
        grid=(n // tt,),
        in_specs=[
            pl.BlockSpec((None, 1, tt * TOP_K), lambda i: (i, 0, 0), memory_space=pltpu.SMEM),
            pl.BlockSpec((tt, d), lambda i: (i, 0)),
            pl.BlockSpec(memory_space=pl.ANY),
        ],
        out_specs=pl.BlockSpec(memory_space=pl.ANY),
        out_shape=SDS((n_rows, d), h.dtype),
        scratch_shapes=[pltpu.SemaphoreType.DMA(())],
        input_output_aliases={2: 0},
        compiler_params=_cparams(("arbitrary",)),
        name="dispatch",
    )(dest3, h, xs0)


def _ffn_body(be_ref, nu_ref, xs_ref, wg_ref, bg_ref, wu_ref, bu_ref, wd_ref, bd_ref, y_ref):
    del be_ref

    @pl.when(pl.program_id(0) < nu_ref[0])
    def _():
        x = xs_ref[...].astype(BF16)
        g = jnp.minimum(_dot(x, wg_ref[...]) + bg_ref[...], SWIGLU_LIMIT)
        u = jnp.clip(_dot(x, wu_ref[...]) + bu_ref[...], -SWIGLU_LIMIT, SWIGLU_LIMIT)
        a = g * jax.nn.sigmoid(SWIGLU_ALPHA * g) * (u + 1.0)
        y_ref[...] = _dot(a.astype(BF16), wd_ref[...]) + bd_ref[...]

    @pl.when(pl.program_id(0) >= nu_ref[0])
    def _():
        y_ref[...] = jnp.zeros(y_ref.shape, y_ref.dtype)


def _ffn(xs, blk_exp, n_used, w_gate, b_gate, w_up, b_up, w_down, b_down):
    n_rows, d = xs.shape
    e, _, de = w_gate.shape
    n_blk = n_rows // EXPERT_BLOCK
    wspec = lambda r, c: pl.BlockSpec((None, r, c), lambda i, be, nu: (be[i], 0, 0))
    return pl.pallas_call(
        _ffn_body,
        grid_spec=pltpu.PrefetchScalarGridSpec(
            num_scalar_prefetch=2,
            grid=(n_blk,),
            in_specs=[
                pl.BlockSpec((EXPERT_BLOCK, d), lambda i, be, nu: (i, 0)),
                wspec(d, de), wspec(1, de),
                wspec(d, de), wspec(1, de),
                wspec(de, d), wspec(1, d),
            ],
            out_specs=pl.BlockSpec((EXPERT_BLOCK, d), lambda i, be, nu: (i, 0)),
        ),
        out_shape=SDS((n_rows, d), F32),
        compiler_params=_cparams(("arbitrary",)),
        name="ffn",
    )(blk_exp, n_used, xs, w_gate.astype(BF16), b_gate.reshape(e, 1, de), w_up.astype(BF16),
      b_up.reshape(e, 1, de), w_down.astype(BF16), b_down.reshape(e, 1, d))


def _combine_body(dest_ref, x_ref, mod_ref, p_ref, y_ref, o_ref, buf, sem):
    tt = x_ref.shape[0]

    def row_copy(t, k, d):
        return pltpu.make_async_copy(y_ref.at[pl.ds(d, 1)], buf.at[k, pl.ds(t, 1)], sem)

    def issue(t, carry):
        for k in range(TOP_K):
            row_copy(t, k, dest_ref[0, t * TOP_K + k]).start()
        return carry

    lax.fori_loop(0, tt, issue, 0)

    def drain(t, carry):
        for k in range(TOP_K):
            row_copy(t, k, dest_ref[0, t * TOP_K + k]).wait()
        return carry

    lax.fori_loop(0, tt, drain, 0)

    p = p_ref[...]
    acc = p[:, 0:1] * buf[0]
    for k in range(1, TOP_K):
        acc = acc + p[:, k:k + 1] * buf[k]
    g2 = mod_ref[0][5:6]
    o_ref[...] = x_ref[...] + g2 * acc


def _combine(x, mod, p, y, dest, seq):
    n, d = x.shape
    tt = MOVE_TT
    dest3 = dest.reshape(n // tt, 1, tt * TOP_K)
    per_b = seq // tt
    return pl.pallas_call(
        _combine_body,
        grid=(n // tt,),
        in_specs=[
            pl.BlockSpec((None, 1, tt * TOP_K), lambda i: (i, 0, 0), memory_space=pltpu.SMEM),
            pl.BlockSpec((tt, d), lambda i: (i, 0)),
            pl.BlockSpec((1, 6, d), lambda i: (i // per_b, 0, 0)),
            pl.BlockSpec((tt, TOP_K), lambda i: (i, 0)),
            pl.BlockSpec(memory_space=pl.ANY),
        ],
        out_specs=pl.BlockSpec((tt, d), lambda i: (i, 0)),
        out_shape=SDS((n, d), F32),
        scratch_shapes=[pltpu.VMEM((TOP_K, tt, d), F32), pltpu.SemaphoreType.DMA(())],
        compiler_params=_cparams(("arbitrary",)),
        name="combine",
    )(dest3, x, mod, p, y)


def _moe(x, mod, n2g, w_r, b_r, w_gate, b_gate, w_up, b_up, w_down, b_down):
    bsz, s, d = x.shape
    n = bsz * s
    h2, top_idx, probs = _router(x, mod, n2g, w_r, b_r)
    dest, blk_exp, n_used, n_rows = _route_meta(top_idx.reshape(n, TOP_K))
    xs = _dispatch(h2.reshape(n, d), dest, n_rows)
    y = _ffn(xs, blk_exp, n_used, w_gate, b_gate, w_up, b_up, w_down, b_down)
    out = _combine(x.reshape(n, d), mod, probs.reshape(n, TOP_K), y, dest, s)
    return out.reshape(bsz, s, d)


def _seg_mats():
    r = lax.broadcasted_iota(jnp.int32, (D_MODEL, N_HEADS), 0) // HEAD_DIM
    c = lax.broadcasted_iota(jnp.int32, (D_MODEL, N_HEADS), 1)
    seg = jnp.where(r == c, 1.0, 0.0).astype(BF16)
    rt = lax.broadcasted_iota(jnp.int32, (N_HEADS, D_MODEL), 0)
    ct = lax.broadcasted_iota(jnp.int32, (N_HEADS, D_MODEL), 1) // HEAD_DIM
    seg_t = jnp.where(rt == ct, 1.0, 0.0).astype(BF16)
    return seg, seg_t


def _split_dot(a, b):
    hi = a.astype(BF16)
    lo = (a - hi.astype(F32)).astype(BF16)
    return _dot(hi, b) + _dot(lo, b)


def _head_rmsnorm(x, g, seg, seg_t):
    ms = _split_dot(x * x, seg) * (1.0 / HEAD_DIM)
    r = lax.rsqrt(ms + EPS)
    return x * _split_dot(r, seg_t) * g


def _kvq_body(x_ref, mod_ref, kvmod_ref, n1g_ref, kvg_ref, wq_ref, wkv_ref, qg_ref, kg_ref,
              q_ref, k_ref, vt_ref, km_ref):
    x = x_ref[0]
    mod = mod_ref[0]
    kvmod = kvmod_ref[0]
    ms = jnp.mean(x * x, axis=-1, keepdims=True)
    xn = x * lax.rsqrt(ms + EPS)
    hq = ((xn * n1g_ref[...]) * (1.0 + mod[1:2]) + mod[0:1]).astype(BF16)
    hk = ((xn * kvg_ref[...]) * (1.0 + kvmod[1:2]) + kvmod[0:1]).astype(BF16)
    seg, seg_t = _seg_mats()
    q = _head_rmsnorm(_dot(hq, wq_ref[...]), qg_ref[...], seg, seg_t)
    q_ref[0] = q
    kv = _dot(hk, wkv_ref[...])
    k = _head_rmsnorm(kv[:, :D_MODEL], kg_ref[...], seg, seg_t)
    k_ref[0, 0] = k.astype(BF16)
    for u in range(ATT_U):
        rows = slice(u * MOBA_BLOCK, (u + 1) * MOBA_BLOCK)
        km_ref[0, 0, u:u + 1, :] = jnp.mean(k[rows], axis=0, keepdims=True)
    vt_ref[0, 0] = kv[:, D_MODEL:].T.astype(BF16)


def _kvq(x, mod, kvmod, n1g, kvg, w_q, w_kv, q_g, k_g):
    bsz, s, d = x.shape
    ts = ATT_U * MOBA_BLOCK
    nb = s // ts
    const = lambda *shape: pl.BlockSpec(shape, lambda b, i: (0,) * len(shape))
    return pl.pallas_call(
        _kvq_body,
        grid=(bsz, nb),
        in_specs=[
            pl.BlockSpec((1, ts, d), lambda b, i: (b, i, 0)),
            pl.BlockSpec((1, 6, d), lambda b, i: (b, 0, 0)),
            pl.BlockSpec((1, 2, d), lambda b, i: (b, 0, 0)),
            const(1, d), const(1, d), const(d, d), const(d, 2 * d), const(1, d), const(1, d),
        ],
        out_specs=[
            pl.BlockSpec((1, ts, d), lambda b, i: (b, i, 0)),
            pl.BlockSpec((1, 1, ts, d), lambda b, i: (b, i, 0, 0)),
            pl.BlockSpec((1, 1, d, ts), lambda b, i: (b, i, 0, 0)),
            pl.BlockSpec((1, 1, ATT_U, d), lambda b, i: (b, i, 0, 0)),
        ],
        out_shape=[
            SDS((bsz, s, d), F32),
            SDS((bsz, nb, ts, d), BF16),
            SDS((bsz, nb, d, ts), BF16),
            SDS((bsz, nb, ATT_U, d), F32),
        ],
        compiler_params=_cparams(("parallel", "arbitrary")),
        name="kvq",
    )(x, mod, kvmod, n1g.reshape(1, d), kvg.reshape(1, d), w_q.astype(BF16), w_kv.astype(BF16),
      jnp.tile(q_g, N_HEADS).reshape(1, d), jnp.tile(k_g, N_HEADS).reshape(1, d))


def _attn_body(slopes_ref, q_ref, k_ref, vt_ref, km_ref, o_ref, bias_scr, s_scr):
    hp = pl.program_id(1)
    own = pl.program_id(2)
    bs = MOBA_BLOCK
    nb = k_ref.shape[1] * ATT_U

    @pl.when(own == 0)
    def _():
        rel = (lax.broadcasted_iota(jnp.int32, (bs, bs), 1)
               - lax.broadcasted_iota(jnp.int32, (bs, bs), 0))
        relf = rel.astype(F32)
        for hd in range(2):
            a = (slopes_ref[2 * hp + hd] * LOG2E) * relf
            bias_scr[hd, 0] = a
            bias_scr[hd, 1] = jnp.where(rel >= 0, a, jnp.inf)

    q = q_ref[0]
    km = km_ref[0]
    dcol = lax.broadcasted_iota(jnp.int32, (1, 2 * HEAD_DIM), 1)
    blk_row = lax.broadcasted_iota(jnp.int32, (nb, 1), 0)
    valid = blk_row < own

    qs, selbias = [], []
    for hd in range(2):
        slope = slopes_ref[2 * hp + hd]
        qm = jnp.where((dcol >= hd * HEAD_DIM) & (dcol < (hd + 1) * HEAD_DIM), q, 0.0)
        gate = lax.dot_general(km, qm, NT_DIMS, precision=HIGHEST, preferred_element_type=F32)
        gate = jnp.where(valid, gate, -jnp.inf)
        rank = jnp.zeros(gate.shape, F32)
        for i in range(nb):
            gi = gate[i:i + 1, :]
            tie = jnp.where(blk_row > i, 1.0, 0.0)
            rank = rank + jnp.where(gi > gate, 1.0, jnp.where(gi == gate, tie, 0.0))
        chosen = jnp.where(valid, jnp.where(rank < MOBA_TOPK, 1.0, 0.0),
                           jnp.where(blk_row == own, 1.0, 0.0))
        off = (own - blk_row).astype(F32) * (bs * LOG2E * slope)
        selbias.append(jnp.where(chosen > 0.5, -off, -jnp.inf))
        qs.append((qm * (LOG2E * HEAD_DIM ** -0.5)).astype(BF16))

    last = k_ref.shape[1] - 1
    ones = jnp.ones((ATT_ONES, ATT_U * bs), BF16)

    def qk(it, hd):
        return lax.dot_general(k_ref[0, it], qs[hd], NT_DIMS, preferred_element_type=F32)

    def visit(it, slot, carry):
        out = []
        for hd in range(2):
            m, acc = carry[hd]
            s_scr[1 - slot, hd] = qk(jnp.minimum(it + 1, last), hd)
            rows, biases = [], []
            mx = m
            for u in range(ATT_U):
                j = it * ATT_U + u
                rows.append(jnp.sum(jnp.where(blk_row == j, selbias[hd], 0.0), axis=0, keepdims=True))
                biases.append((j == own).astype(jnp.int32))
                t = (s_scr[slot, hd, u * bs:(u + 1) * bs] - bias_scr[hd, biases[u]]) + rows[u]
                mx = jnp.maximum(mx, jnp.max(t, axis=0, keepdims=True))
            m_safe = jnp.where(mx == -jnp.inf, 0.0, mx)
            alpha = jnp.exp2(m - m_safe)
            ps = []
            for u in range(ATT_U):
                x = s_scr[slot, hd, u * bs:(u + 1) * bs] - (bias_scr[hd, biases[u]] - (rows[u] - m_safe))
                ps.append(jnp.exp2(x.astype(BF16)))
            p = jnp.concatenate(ps, axis=0)
            vj = jnp.concatenate([vt_ref[0, it, hd * HEAD_DIM:(hd + 1) * HEAD_DIM, :], ones], axis=0)
            acc = alpha * acc + _dot(vj, p)
            out.append((mx, acc))
        return tuple(out)

    def body(i2, carry):
        carry = visit(2 * i2, 0, carry)
        return visit(2 * i2 + 1, 1, carry)

    for hd in range(2):
        s_scr[0, hd] = qk(0, hd)
    init = tuple((jnp.full((1, bs), -jnp.inf, F32), jnp.zeros((HEAD_DIM + ATT_ONES, bs), F32))
                 for hd in range(2))
    n_visits = own // ATT_U + 1
    carry = lax.fori_loop(0, (n_visits + 1) // 2, body, init)
    outs = [acc[:HEAD_DIM] / acc[HEAD_DIM:HEAD_DIM + 1] for (_, acc) in carry]
    o_ref[0] = jnp.concatenate(outs, axis=0).T.astype(o_ref.dtype)


def _attn(q, kb, vtb, kmean):
    bsz, s, d = q.shape
    nsb = s // (ATT_U * MOBA_BLOCK)
    assert nsb % 2 == 0, "the attention loop visits key super-blocks in pairs"
    nb = s // MOBA_BLOCK
    w = 2 * HEAD_DIM
    slopes = (2.0 ** (-8.0 * jnp.arange(1, N_HEADS + 1, dtype=F32) / N_HEADS)).astype(F32)
    return pl.pallas_call(
        _attn_body,
        grid_spec=pltpu.PrefetchScalarGridSpec(
            num_scalar_prefetch=1,
            grid=(bsz, N_HEADS // 2, nb),
            in_specs=[
                pl.BlockSpec((1, MOBA_BLOCK, w), lambda b, h, c, sl: (b, c, h)),
                pl.BlockSpec((1, nsb, ATT_U * MOBA_BLOCK, w), lambda b, h, c, sl: (b, 0, 0, h)),
                pl.BlockSpec((1, nsb, w, ATT_U * MOBA_BLOCK), lambda b, h, c, sl: (b, 0, h, 0)),
                pl.BlockSpec((1, nb, w), lambda b, h, c, sl: (b, 0, h)),
            ],
            out_specs=pl.BlockSpec((1, MOBA_BLOCK, w), lambda b, h, c, sl: (b, c, h)),
            scratch_shapes=[pltpu.VMEM((2, 2, MOBA_BLOCK, MOBA_BLOCK), F32),
                            pltpu.VMEM((2, 2, ATT_U * MOBA_BLOCK, MOBA_BLOCK), F32)],
        ),
        out_shape=SDS((bsz, s, d), BF16),
        compiler_params=_cparams(("parallel", "parallel", "arbitrary")),
        name="attn",
    )(slopes, q, kb, vtb, kmean)


OPROJ_TS = 512


def _oproj_body(a_ref, x_ref, mod_ref, wo_ref, o_ref):
    g1 = mod_ref[0][2:3]
    o_ref[0] = x_ref[0] + g1 * _dot(a_ref[0], wo_ref[...])


def _oproj(a, x, mod, w_o):
    bsz, s, d = x.shape
    ts = min(OPROJ_TS, s)
    tok = pl.BlockSpec((1, ts, d), lambda b, i: (b, i, 0))
    return pl.pallas_call(
        _oproj_body,
        grid=(bsz, s // ts),
        in_specs=[tok, tok, pl.BlockSpec((1, 6, d), lambda b, i: (b, 0, 0)),
                  pl.BlockSpec((d, d), lambda b, i: (0, 0))],
        out_specs=tok,
        out_shape=SDS((bsz, s, d), F32),
        compiler_params=_cparams(("parallel", "arbitrary")),
        name="oproj",
    )(a, x, mod, w_o.astype(BF16))


def kernel(x, c, ada_w, ada_b, norm1_g, norm2_g, sgu_w_in, sgu_b_in, sgu_v_g, sgu_w_s, sgu_b_s, sgu_w_out,
           kv_ada_w, kv_ada_b, kv_norm_g, w_kv, k_norm_g, attn_w_q, q_norm_g, attn_w_o,
           moe_w_router, moe_b_router, moe_w_gate, moe_b_gate, moe_w_up, moe_b_up, moe_w_down, moe_b_down):
    bsz, s, d = x.shape
    moe = lambda l, xx, mod: _moe(xx, mod, norm2_g[l], moe_w_router[l], moe_b_router[l],
                                  moe_w_gate[l], moe_b_gate[l], moe_w_up[l], moe_b_up[l],
                                  moe_w_down[l], moe_b_down[l])
    mod0 = _ada(c, ada_w[0], ada_b[0]).reshape(bsz, 6, d)
    x = _sgu(x, mod0, norm1_g[0], sgu_w_in[0], sgu_b_in[0], sgu_v_g[0], sgu_w_s[0], sgu_b_s[0], sgu_w_out[0])
    x = moe(0, x, mod0)
    mod1 = _ada(c, ada_w[1], ada_b[1]).reshape(bsz, 6, d)
    kvmod = _ada(c, kv_ada_w, kv_ada_b).reshape(bsz, 2, d)
    q, kb, vtb, kmean = _kvq(x, mod1, kvmod, norm1_g[1], kv_norm_g, attn_w_q[0], w_kv, q_norm_g[0], k_norm_g)
    a = _attn(q, kb, vtb, kmean.reshape(bsz, s // MOBA_BLOCK, d))
    x = _oproj(a, x, mod1, attn_w_o[0])
    x = moe(1, x, mod1)
    return x
```

```python
import functools
import math

import jax
import jax.numpy as jnp
import numpy as np
from jax import lax
from jax.experimental import pallas as pl
from jax.experimental.pallas import tpu as pltpu

F32 = jnp.float32
BF16 = jnp.bfloat16
HIGHEST = lax.Precision.HIGHEST
SDS = jax.ShapeDtypeStruct

D_MODEL = 1024
SGU_CHUNK = 128
SGU_GROUPS = 8
D_SGU = 3 * D_MODEL
SGU_GROUP_DIM = D_SGU // SGU_GROUPS
HEAD_DIM = 64
N_HEADS = D_MODEL // HEAD_DIM
MOBA_BLOCK = 256
MOBA_TOPK = 3
ATT_U = 2
ATT_ONES = 16
LOG2E = math.log2(math.e)
N_EXPERTS = 32
TOP_K = 4
SWIGLU_LIMIT = 7.0
SWIGLU_ALPHA = 1.702
EXPERT_BLOCK = 256
EPS = 1e-6

VMEM_LIMIT_BYTES = 56 * 1024 * 1024
NT_DIMS = (((1,), (1,)), ((), ()))


def _cparams(sem):
    return pltpu.CompilerParams(dimension_semantics=sem, vmem_limit_bytes=VMEM_LIMIT_BYTES)


def _dot(a, b, **kw):
    return jnp.dot(a, b, preferred_element_type=F32, **kw)


def _norm_mod(x, g, sc, sh):
    ms = jnp.mean(x * x, axis=-1, keepdims=True)
    return (x * lax.rsqrt(ms + EPS) * g) * (1.0 + sc) + sh


def _ada_body(c_ref, w_ref, b_ref, o_ref):
    c = c_ref[...]
    ca = c * jax.nn.sigmoid(c)
    o_ref[...] = _dot(ca, w_ref[...], precision=HIGHEST) + b_ref[...]


def _ada(c, w, b, layer=0):
    bsz, d = c.shape
    n = w.shape[-1]
    tn = 1024
    w = w.reshape(-1, d, n)
    return pl.pallas_call(
        _ada_body,
        grid=(n // tn,),
        in_specs=[
            pl.BlockSpec((bsz, d), lambda j: (0, 0)),
            pl.BlockSpec((None, d, tn), lambda j: (layer, 0, j)),
            pl.BlockSpec((1, tn), lambda j: (0, j)),
        ],
        out_specs=pl.BlockSpec((bsz, tn), lambda j: (0, j)),
        out_shape=SDS((bsz, n), F32),
        compiler_params=_cparams(("arbitrary",)),
        name="ada",
    )(c, w, b.reshape(1, n))


SGU_TS = 256
SGU_CB = 768


def _gelu_tanh(x):
    c = np.float32(np.sqrt(2.0 / np.pi))
    return x * (0.5 * (1.0 + jnp.tanh(c * (x + 0.044715 * (x * x * x)))))


def _sgu_body(x_ref, mod_ref, n1g_ref, win_ref, bin_ref, vg_ref, ws_ref, bst_ref, wout_ref,
              o_ref, u_scr, v_scr, y_scr):
    ts = x_ref.shape[1]
    x = x_ref[0]
    mod = mod_ref[0]
    sh1, sc1, g1 = mod[0:1], mod[1:2], mod[2:3]
    h = _norm_mod(x, n1g_ref[...], sc1, sh1).astype(BF16)

    for j in range(D_SGU // SGU_CB):
        cs = slice(j * SGU_CB, (j + 1) * SGU_CB)
        u_scr[:, cs] = _gelu_tanh(_dot(h, win_ref[:, cs]) + bin_ref[:, cs])
    ssq = jnp.zeros((ts, 1), F32)
    for j in range(D_SGU // SGU_CB):
        cs = slice(j * SGU_CB, (j + 1) * SGU_CB)
        ws_cols = slice(D_SGU + j * SGU_CB, D_SGU + (j + 1) * SGU_CB)
        z = _gelu_tanh(_dot(h, win_ref[:, ws_cols]) + bin_ref[:, ws_cols])
        v_scr[:, cs] = z
        ssq = ssq + jnp.sum(z * z, axis=-1, keepdims=True)
    rs = lax.rsqrt(ssq / D_SGU + EPS)

    r_i = lax.broadcasted_iota(jnp.int32, (SGU_CHUNK, SGU_CHUNK), 0)
    c_i = lax.broadcasted_iota(jnp.int32, (SGU_CHUNK, SGU_CHUNK), 1)
    causal = c_i <= r_i
    for g in range(SGU_GROUPS):
        cs = slice(g * SGU_GROUP_DIM, (g + 1) * SGU_GROUP_DIM)
        wsg = jnp.where(causal, ws_ref[g], 0.0).astype(BF16)
        bsg = bst_ref[:, g:g + 1]
        for n in range(ts // SGU_CHUNK):
            rows = slice(n * SGU_CHUNK, (n + 1) * SGU_CHUNK)
            vn = (v_scr[rows, cs] * rs[rows] * vg_ref[:, cs]).astype(BF16)
            mixed = _dot(wsg, vn) + bsg
            y_scr[rows, cs] = (u_scr[rows, cs] * mixed).astype(BF16)

    m = _dot(y_scr[...], wout_ref[...])
    o_ref[0] = x + g1 * m


def _sgu(x, mod, n1g, w_in, b_in, v_g, w_s, b_s, w_out):
    bsz, s, d = x.shape
    ts = SGU_TS
    const = lambda *shape: pl.BlockSpec(shape, lambda b, i: (0,) * len(shape),
                                        pipeline_mode=pl.Buffered(1))
    return pl.pallas_call(
        _sgu_body,
        grid=(bsz, s // ts),
        in_specs=[
            pl.BlockSpec((1, ts, d), lambda b, i: (b, i, 0)),
            pl.BlockSpec((1, 6, d), lambda b, i: (b, 0, 0)),
            const(1, d),
            const(d, 2 * D_SGU),
            const(1, 2 * D_SGU),
            const(1, D_SGU),
            const(SGU_GROUPS, SGU_CHUNK, SGU_CHUNK),
            const(SGU_CHUNK, SGU_GROUPS),
            const(D_SGU, d),
        ],
        out_specs=pl.BlockSpec((1, ts, d), lambda b, i: (b, i, 0)),
        out_shape=SDS((bsz, s, d), F32),
        scratch_shapes=[
            pltpu.VMEM((ts, D_SGU), F32),
            pltpu.VMEM((ts, D_SGU), F32),
            pltpu.VMEM((ts, D_SGU), BF16),
        ],
        compiler_params=_cparams(("parallel", "arbitrary")),
        name="sgu",
    )(x, mod, n1g.reshape(1, d), w_in.astype(BF16), b_in.reshape(1, -1), v_g.reshape(1, -1),
      w_s, b_s.T, w_out.astype(BF16))


ROUTER_TS = 512


def _router_body(x_ref, mod_ref, n2g_ref, wr_ref, br_ref, h_ref, idx_ref, p_ref, rk_ref, cnt_ref, run_scr):
    @pl.when((pl.program_id(0) == 0) & (pl.program_id(1) == 0))
    def _():
        run_scr[...] = jnp.zeros(run_scr.shape, run_scr.dtype)

    x = x_ref[0]
    mod = mod_ref[0]
    sh2, sc2 = mod[3:4], mod[4:5]
    h2 = _norm_mod(x, n2g_ref[...], sc2, sh2)
    h_ref[0] = h2
    logits = _dot(h2, wr_ref[...], precision=HIGHEST) + br_ref[...]
    lane = lax.broadcasted_iota(jnp.int32, logits.shape, 1)
    work = logits
    vals, idxs = [], []
    for _ in range(TOP_K):
        mx = jnp.max(work, axis=-1, keepdims=True)
        ix = jnp.min(jnp.where(work == mx, lane, N_EXPERTS), axis=-1, keepdims=True)
        vals.append(mx)
        idxs.append(ix)
        work = jnp.where(lane == ix, -jnp.inf, work)
    tv = jnp.concatenate(vals, axis=-1)
    e = jnp.exp(tv - vals[0])
    p_ref[0] = e / jnp.sum(e, axis=-1, keepdims=True)
    idx_ref[0] = jnp.concatenate(idxs, axis=-1)

    ts = x.shape[0]
    onehots = [lane == ix for ix in idxs]
    mask = jnp.zeros(logits.shape, F32)
    for oh in onehots:
        mask = mask + jnp.where(oh, 1.0, 0.0)
    earlier = (lax.broadcasted_iota(jnp.int32, (ts, ts), 1) < lax.broadcasted_iota(jnp.int32, (ts, ts), 0))
    before = _dot(jnp.where(earlier, 1.0, 0.0).astype(BF16), mask.astype(BF16)) + run_scr[...]
    rk = [jnp.sum(jnp.where(oh, before, 0.0), axis=-1, keepdims=True) for oh in onehots]
    rk_ref[0] = jnp.concatenate(rk, axis=-1).astype(jnp.int32)
    run_scr[...] = run_scr[...] + jnp.sum(mask, axis=0, keepdims=True)
    cnt_ref[...] = run_scr[...].astype(jnp.int32)


def _router(x, mod, n2g, w_r, b_r):
    bsz, s, d = x.shape
    ts = min(ROUTER_TS, s)
    tok = lambda w: pl.BlockSpec((1, ts, w), lambda b, i: (b, i, 0))
    return pl.pallas_call(
        _router_body,
        grid=(bsz, s // ts),
        in_specs=[
            tok(d),
            pl.BlockSpec((1, 6, d), lambda b, i: (b, 0, 0)),
            pl.BlockSpec((1, d), lambda b, i: (0, 0)),
            pl.BlockSpec((d, N_EXPERTS), lambda b, i: (0, 0)),
            pl.BlockSpec((1, N_EXPERTS), lambda b, i: (0, 0)),
        ],
        out_specs=[tok(d), tok(TOP_K), tok(TOP_K), tok(TOP_K),
                   pl.BlockSpec((1, N_EXPERTS), lambda b, i: (0, 0))],
        out_shape=[SDS((bsz, s, d), F32), SDS((bsz, s, TOP_K), jnp.int32), SDS((bsz, s, TOP_K), F32),
                   SDS((bsz, s, TOP_K), jnp.int32), SDS((1, N_EXPERTS), jnp.int32)],
        scratch_shapes=[pltpu.VMEM((1, N_EXPERTS), F32)],
        compiler_params=_cparams(("arbitrary", "arbitrary")),
        name="router",
    )(x, mod, n2g.reshape(1, d), w_r, b_r.reshape(1, -1))


def _route_meta(top_idx, rank, counts):
    n = top_idx.shape[0]
    padded = (counts + EXPERT_BLOCK - 1) // EXPERT_BLOCK * EXPERT_BLOCK
    pend = jnp.cumsum(padded)
    pstart = pend - padded
    dest = (pstart[top_idx] + rank).astype(jnp.int32)
    n_rows = n * TOP_K + N_EXPERTS * EXPERT_BLOCK
    n_blk = n_rows // EXPERT_BLOCK
    blk_exp = jnp.minimum(
        jnp.searchsorted(pend, jnp.arange(n_blk) * EXPERT_BLOCK, side="right"), N_EXPERTS - 1).astype(jnp.int32)
    n_used = (pend[-1] // EXPERT_BLOCK).astype(jnp.int32).reshape(1)
    pad_blk = jnp.where(padded > 0, pend - EXPERT_BLOCK, -1)
    tail = pend[-1] + jnp.arange(N_EXPERTS) * EXPERT_BLOCK
    tail_blk = jnp.where(tail < n_rows, tail, -1)
    pad_blk = jnp.concatenate([pad_blk, tail_blk]).astype(jnp.int32)
    return dest, blk_exp, n_used, pad_blk, n_rows


MOVE_TT = 256


def _dispatch_body(pad_ref, dest_ref, h_ref, xs_ref, zero_scr, sem, zsem):
    tt = h_ref.shape[0]

    @pl.when(pl.program_id(0) == 0)
    def _():
        zero_scr[...] = jnp.zeros(zero_scr.shape, zero_scr.dtype)

        def zero_copy(e):
            row0 = pl.multiple_of(pad_ref[e], EXPERT_BLOCK)
            return pltpu.make_async_copy(zero_scr, xs_ref.at[pl.ds(row0, EXPERT_BLOCK)], zsem)

        for e in range(2 * N_EXPERTS):
            pl.when(pad_ref[e] >= 0)(lambda e=e: zero_copy(e).start())
        for e in range(2 * N_EXPERTS):
            pl.when(pad_ref[e] >= 0)(lambda e=e: zero_copy(e).wait())

    def issue(t, carry):
        for k in range(TOP_K):
            d = dest_ref[0, t * TOP_K + k]
            pltpu.make_async_copy(h_ref.at[pl.ds(t, 1)], xs_ref.at[pl.ds(d, 1)], sem).start(priority=k % 2)
        return carry

    lax.fori_loop(0, tt, issue, 0)
    for k in range(TOP_K):
        pltpu.make_async_copy(h_ref, h_ref, sem).wait()


def _dispatch(h, dest, pad_blk, n_rows):
    n, d = h.shape
    tt = MOVE_TT
    dest3 = dest.reshape(n // tt, 1, tt * TOP_K)
    return pl.pallas_call(
        _dispatch_body,
        grid_spec=pltpu.PrefetchScalarGridSpec(
            num_scalar_prefetch=1,
            grid=(n // tt,),
            in_specs=[
                pl.BlockSpec((None, 1, tt * TOP_K), lambda i, pad: (i, 0, 0), memory_space=pltpu.SMEM),
                pl.BlockSpec((tt, d), lambda i, pad: (i, 0)),
            ],
            out_specs=pl.BlockSpec(memory_space=pl.ANY),
            scratch_shapes=[pltpu.VMEM((EXPERT_BLOCK, d), h.dtype), pltpu.SemaphoreType.DMA(()),
                            pltpu.SemaphoreType.DMA(())],
        ),
        out_shape=SDS((n_rows, d), h.dtype),
        compiler_params=_cparams(("arbitrary",)),
        name="dispatch",
    )(pad_blk, dest3, h)


def _ffn_body(be_ref, nu_ref, xs_ref, wg_ref, bg_ref, wu_ref, bu_ref, wd_ref, bd_ref, y_ref,
              wg_bf, wu_bf, wd_bf):
    i = pl.program_id(0)

    @pl.when((i == 0) | (be_ref[i] != be_ref[jnp.maximum(i - 1, 0)]))
    def _():
        wg_bf[...] = wg_ref[...].astype(BF16)
        wu_bf[...] = wu_ref[...].astype(BF16)
        wd_bf[...] = wd_ref[...].astype(BF16)

    @pl.when(i < nu_ref[0])
    def _():
        x = xs_ref[...].astype(BF16)
        g = jnp.minimum(_dot(x, wg_bf[...]) + bg_ref[...], SWIGLU_LIMIT)
        u = jnp.clip(_dot(x, wu_bf[...]) + bu_ref[...], -SWIGLU_LIMIT, SWIGLU_LIMIT)
        a = g * jax.nn.sigmoid(SWIGLU_ALPHA * g) * (u + 1.0)
        y_ref[...] = _dot(a.astype(BF16), wd_bf[...]) + bd_ref[...]

    @pl.when(i >= nu_ref[0])
    def _():
        y_ref[...] = jnp.zeros(y_ref.shape, y_ref.dtype)


def _ffn(xs, blk_exp, n_used, layer, w_gate, b_gate, w_up, b_up, w_down, b_down):
    n_rows, d = xs.shape
    _, e, _, de = w_gate.shape
    n_blk = n_rows // EXPERT_BLOCK
    wspec = lambda r, c: pl.BlockSpec((None, None, r, c), lambda i, be, nu: (layer, be[i], 0, 0))
    xs_map = lambda i, be, nu: (jnp.minimum(i, jnp.maximum(nu[0] - 1, 0)), 0)
    return pl.pallas_call(
        _ffn_body,
        grid_spec=pltpu.PrefetchScalarGridSpec(
            num_scalar_prefetch=2,
            grid=(n_blk,),
            in_specs=[
                pl.BlockSpec((EXPERT_BLOCK, d), xs_map),
                wspec(d, de), wspec(1, de),
                wspec(d, de), wspec(1, de),
                wspec(de, d), wspec(1, d),
            ],
            out_specs=pl.BlockSpec((EXPERT_BLOCK, d), lambda i, be, nu: (i, 0)),
            scratch_shapes=[pltpu.VMEM((d, de), BF16), pltpu.VMEM((d, de), BF16), pltpu.VMEM((de, d), BF16)],
        ),
        out_shape=SDS((n_rows, d), F32),
        compiler_params=_cparams(("arbitrary",)),
        name="ffn",
    )(blk_exp, n_used, xs, w_gate, b_gate.reshape(-1, e, 1, de), w_up, b_up.reshape(-1, e, 1, de),
      w_down, b_down.reshape(-1, e, 1, d))


def _combine_body(dest_ref, x_ref, mod_ref, p_ref, y_ref, o_ref, buf, sem):
    tt = x_ref.shape[0]

    def issue(t, carry):
        for k in range(TOP_K):
            d = dest_ref[0, t * TOP_K + k]
            pltpu.make_async_copy(y_ref.at[pl.ds(d, 1)], buf.at[k, pl.ds(t, 1)], sem).start(priority=k % 2)
        return carry

    lax.fori_loop(0, tt, issue, 0)
    pltpu.make_async_copy(buf, buf, sem).wait()

    p = p_ref[...]
    acc = p[:, 0:1] * buf[0]
    for k in range(1, TOP_K):
        acc = acc + p[:, k:k + 1] * buf[k]
    g2 = mod_ref[0][5:6]
    o_ref[...] = x_ref[...] + g2 * acc


def _combine(x, mod, p, y, dest, seq):
    n, d = x.shape
    tt = MOVE_TT
    dest3 = dest.reshape(n // tt, 1, tt * TOP_K)
    per_b = seq // tt
    return pl.pallas_call(
        _combine_body,
        grid=(n // tt,),
        in_specs=[
            pl.BlockSpec((None, 1, tt * TOP_K), lambda i: (i, 0, 0), memory_space=pltpu.SMEM),
            pl.BlockSpec((tt, d), lambda i: (i, 0)),
            pl.BlockSpec((1, 6, d), lambda i: (i // per_b, 0, 0)),
            pl.BlockSpec((tt, TOP_K), lambda i: (i, 0)),
            pl.BlockSpec(memory_space=pl.ANY),
        ],
        out_specs=pl.BlockSpec((tt, d), lambda i: (i, 0)),
        out_shape=SDS((n, d), F32),
        scratch_shapes=[pltpu.VMEM((TOP_K, tt, d), F32), pltpu.SemaphoreType.DMA(())],
        compiler_params=_cparams(("arbitrary",)),
        name="combine",
    )(dest3, x, mod, p, y)


def _moe(x, mod, layer, n2g, w_r, b_r, w_gate, b_gate, w_up, b_up, w_down, b_down):
    bsz, s, d = x.shape
    n = bsz * s
    h2, top_idx, probs, rank, counts = _router(x, mod, n2g, w_r, b_r)
    dest, blk_exp, n_used, pad_blk, n_rows = _route_meta(
        top_idx.reshape(n, TOP_K), rank.reshape(n, TOP_K), counts.reshape(N_EXPERTS))
    xs = _dispatch(h2.reshape(n, d), dest, pad_blk, n_rows)
    y = _ffn(xs, blk_exp, n_used, layer, w_gate, b_gate, w_up, b_up, w_down, b_down)
    out = _combine(x.reshape(n, d), mod, probs.reshape(n, TOP_K), y, dest, s)
    return out.reshape(bsz, s, d)


def _seg_mats():
    r = lax.broadcasted_iota(jnp.int32, (D_MODEL, N_HEADS), 0) // HEAD_DIM
    c = lax.broadcasted_iota(jnp.int32, (D_MODEL, N_HEADS), 1)
    seg = jnp.where(r == c, 1.0, 0.0).astype(BF16)
    rt = lax.broadcasted_iota(jnp.int32, (N_HEADS, D_MODEL), 0)
    ct = lax.broadcasted_iota(jnp.int32, (N_HEADS, D_MODEL), 1) // HEAD_DIM
    seg_t = jnp.where(rt == ct, 1.0, 0.0).astype(BF16)
    return seg, seg_t


def _split_dot(a, b):
    hi = a.astype(BF16)
    lo = (a - hi.astype(F32)).astype(BF16)
    return _dot(hi, b) + _dot(lo, b)


def _head_rmsnorm(x, g, seg, seg_t):
    ms = _split_dot(x * x, seg) * (1.0 / HEAD_DIM)
    r = lax.rsqrt(ms + EPS)
    return x * _split_dot(r, seg_t) * g


def _kvq_body(x_ref, mod_ref, kvmod_ref, n1g_ref, kvg_ref, wq_ref, wkv_ref, qg_ref, kg_ref,
              q_ref, k_ref, vt_ref, km_ref):
    x = x_ref[0]
    mod = mod_ref[0]
    kvmod = kvmod_ref[0]
    ms = jnp.mean(x * x, axis=-1, keepdims=True)
    xn = x * lax.rsqrt(ms + EPS)
    hq = ((xn * n1g_ref[...]) * (1.0 + mod[1:2]) + mod[0:1]).astype(BF16)
    hk = ((xn * kvg_ref[...]) * (1.0 + kvmod[1:2]) + kvmod[0:1]).astype(BF16)
    seg, seg_t = _seg_mats()
    q = _head_rmsnorm(_dot(hq, wq_ref[...]), qg_ref[...], seg, seg_t)
    q_ref[0] = q
    kv = _dot(hk, wkv_ref[...])
    k = _head_rmsnorm(kv[:, :D_MODEL], kg_ref[...], seg, seg_t)
    k_ref[0, 0] = k.astype(BF16)
    for u in range(ATT_U):
        rows = slice(u * MOBA_BLOCK, (u + 1) * MOBA_BLOCK)
        km_ref[0, 0, u:u + 1, :] = jnp.mean(k[rows], axis=0, keepdims=True)
    vt_ref[0, 0] = kv[:, D_MODEL:].T.astype(BF16)


def _kvq(x, mod, kvmod, n1g, kvg, w_q, w_kv, q_g, k_g):
    bsz, s, d = x.shape
    ts = ATT_U * MOBA_BLOCK
    nb = s // ts
    const = lambda *shape: pl.BlockSpec(shape, lambda b, i: (0,) * len(shape))
    return pl.pallas_call(
        _kvq_body,
        grid=(bsz, nb),
        in_specs=[
            pl.BlockSpec((1, ts, d), lambda b, i: (b, i, 0)),
            pl.BlockSpec((1, 6, d), lambda b, i: (b, 0, 0)),
            pl.BlockSpec((1, 2, d), lambda b, i: (b, 0, 0)),
            const(1, d), const(1, d), const(d, d), const(d, 2 * d), const(1, d), const(1, d),
        ],
        out_specs=[
            pl.BlockSpec((1, ts, d), lambda b, i: (b, i, 0)),
            pl.BlockSpec((1, 1, ts, d), lambda b, i: (b, i, 0, 0)),
            pl.BlockSpec((1, 1, d, ts), lambda b, i: (b, i, 0, 0)),
            pl.BlockSpec((1, 1, ATT_U, d), lambda b, i: (b, i, 0, 0)),
        ],
        out_shape=[
            SDS((bsz, s, d), F32),
            SDS((bsz, nb, ts, d), BF16),
            SDS((bsz, nb, d, ts), BF16),
            SDS((bsz, nb, ATT_U, d), F32),
        ],
        compiler_params=_cparams(("parallel", "arbitrary")),
        name="kvq",
    )(x, mod, kvmod, n1g.reshape(1, d), kvg.reshape(1, d), w_q.astype(BF16), w_kv.astype(BF16),
      jnp.tile(q_g, N_HEADS).reshape(1, d), jnp.tile(k_g, N_HEADS).reshape(1, d))


def _attn_body(slopes_ref, q_ref, k_ref, vt_ref, km_ref, o_ref, bias_scr, s_scr):
    hp = pl.program_id(1)
    own = pl.program_id(2)
    bs = MOBA_BLOCK
    nb = k_ref.shape[1] * ATT_U

    @pl.when(own == 0)
    def _():
        rel = (lax.broadcasted_iota(jnp.int32, (bs, bs), 1)
               - lax.broadcasted_iota(jnp.int32, (bs, bs), 0))
        relf = rel.astype(F32)
        for hd in range(2):
            a = (slopes_ref[2 * hp + hd] * LOG2E) * relf
            bias_scr[hd, 0] = a
            bias_scr[hd, 1] = jnp.where(rel >= 0, a, jnp.inf)

    q = q_ref[0]
    km = km_ref[0]
    dcol = lax.broadcasted_iota(jnp.int32, (1, 2 * HEAD_DIM), 1)
    blk_row = lax.broadcasted_iota(jnp.int32, (nb, 1), 0)
    valid = blk_row < own

    qs, selbias = [], []
    for hd in range(2):
        slope = slopes_ref[2 * hp + hd]
        qm = jnp.where((dcol >= hd * HEAD_DIM) & (dcol < (hd + 1) * HEAD_DIM), q, 0.0)
        gate = lax.dot_general(km, qm, NT_DIMS, precision=HIGHEST, preferred_element_type=F32)
        gate = jnp.where(valid, gate, -jnp.inf)
        rank = jnp.zeros(gate.shape, F32)
        for i in range(nb):
            gi = gate[i:i + 1, :]
            tie = jnp.where(blk_row > i, 1.0, 0.0)
            rank = rank + jnp.where(gi > gate, 1.0, jnp.where(gi == gate, tie, 0.0))
        chosen = jnp.where(valid, jnp.where(rank < MOBA_TOPK, 1.0, 0.0),
                           jnp.where(blk_row == own, 1.0, 0.0))
        off = (own - blk_row).astype(F32) * (bs * LOG2E * slope)
        selbias.append(jnp.where(chosen > 0.5, -off, -jnp.inf))
        qs.append((qm * (LOG2E * HEAD_DIM ** -0.5)).astype(BF16))

    last = k_ref.shape[1] - 1
    ones = jnp.ones((ATT_ONES, ATT_U * bs), BF16)

    def qk(it, hd):
        return lax.dot_general(k_ref[0, it], qs[hd], NT_DIMS, preferred_element_type=F32)

    def visit(it, slot, carry):
        out = []
        for hd in range(2):
            m, acc = carry[hd]
            s_scr[1 - slot, hd] = qk(jnp.minimum(it + 1, last), hd)
            rows, biases = [], []
            mx = m
            for u in range(ATT_U):
                j = it * ATT_U + u
                rows.append(jnp.sum(jnp.where(blk_row == j, selbias[hd], 0.0), axis=0, keepdims=True))
                biases.append((j == own).astype(jnp.int32))
                t = (s_scr[slot, hd, u * bs:(u + 1) * bs] - bias_scr[hd, biases[u]]) + rows[u]
                mx = jnp.maximum(mx, jnp.max(t, axis=0, keepdims=True))
            m_safe = jnp.where(mx == -jnp.inf, 0.0, mx)
            alpha = jnp.exp2(m - m_safe)
            ps = []
            for u in range(ATT_U):
                x = s_scr[slot, hd, u * bs:(u + 1) * bs] - (bias_scr[hd, biases[u]] - (rows[u] - m_safe))
                ps.append(jnp.exp2(x.astype(BF16)))
            p = jnp.concatenate(ps, axis=0)
            vj = jnp.concatenate([vt_ref[0, it, hd * HEAD_DIM:(hd + 1) * HEAD_DIM, :], ones], axis=0)
            acc = alpha * acc + _dot(vj, p)
            out.append((mx, acc))
        return tuple(out)

    def body(i2, carry):
        carry = visit(2 * i2, 0, carry)
        return visit(2 * i2 + 1, 1, carry)

    for hd in range(2):
        s_scr[0, hd] = qk(0, hd)
    init = tuple((jnp.full((1, bs), -jnp.inf, F32), jnp.zeros((HEAD_DIM + ATT_ONES, bs), F32))
                 for hd in range(2))
    n_visits = own // ATT_U + 1
    carry = lax.fori_loop(0, (n_visits + 1) // 2, body, init)
    outs = [acc[:HEAD_DIM] / acc[HEAD_DIM:HEAD_DIM + 1] for (_, acc) in carry]
    o_ref[0] = jnp.concatenate(outs, axis=0).T.astype(o_ref.dtype)


def _attn(q, kb, vtb, kmean):
    bsz, s, d = q.shape
    nsb = s // (ATT_U * MOBA_BLOCK)
    assert nsb % 2 == 0, "the attention loop visits key super-blocks in pairs"
    nb = s // MOBA_BLOCK
    w = 2 * HEAD_DIM
    slopes = (2.0 ** (-8.0 * jnp.arange(1, N_HEADS + 1, dtype=F32) / N_HEADS)).astype(F32)
    return pl.pallas_call(
        _attn_body,
        grid_spec=pltpu.PrefetchScalarGridSpec(
            num_scalar_prefetch=1,
            grid=(bsz, N_HEADS // 2, nb),
            in_specs=[
                pl.BlockSpec((1, MOBA_BLOCK, w), lambda b, h, c, sl: (b, c, h)),
                pl.BlockSpec((1, nsb, ATT_U * MOBA_BLOCK, w), lambda b, h, c, sl: (b, 0, 0, h)),
                pl.BlockSpec((1, nsb, w, ATT_U * MOBA_BLOCK), lambda b, h, c, sl: (b, 0, h, 0)),
                pl.BlockSpec((1, nb, w), lambda b, h, c, sl: (b, 0, h)),
            ],
            out_specs=pl.BlockSpec((1, MOBA_BLOCK, w), lambda b, h, c, sl: (b, c, h)),
            scratch_shapes=[pltpu.VMEM((2, 2, MOBA_BLOCK, MOBA_BLOCK), F32),
                            pltpu.VMEM((2, 2, ATT_U * MOBA_BLOCK, MOBA_BLOCK), F32)],
        ),
        out_shape=SDS((bsz, s, d), BF16),
        compiler_params=_cparams(("parallel", "parallel", "arbitrary")),
        name="attn",
    )(slopes, q, kb, vtb, kmean)


OPROJ_TS = 512


def _oproj_body(a_ref, x_ref, mod_ref, wo_ref, o_ref):
    g1 = mod_ref[0][2:3]
    o_ref[0] = x_ref[0] + g1 * _dot(a_ref[0], wo_ref[...])


def _oproj(a, x, mod, w_o):
    bsz, s, d = x.shape
    ts = min(OPROJ_TS, s)
    tok = pl.BlockSpec((1, ts, d), lambda b, i: (b, i, 0))
    return pl.pallas_call(
        _oproj_body,
        grid=(bsz, s // ts),
        in_specs=[tok, tok, pl.BlockSpec((1, 6, d), lambda b, i: (b, 0, 0)),
                  pl.BlockSpec((d, d), lambda b, i: (0, 0))],
        out_specs=tok,
        out_shape=SDS((bsz, s, d), F32),
        compiler_params=_cparams(("parallel", "arbitrary")),
        name="oproj",
    )(a, x, mod, w_o.astype(BF16))


def kernel(x, c, ada_w, ada_b, norm1_g, norm2_g, sgu_w_in, sgu_b_in, sgu_v_g, sgu_w_s, sgu_b_s, sgu_w_out,
           kv_ada_w, kv_ada_b, kv_norm_g, w_kv, k_norm_g, attn_w_q, q_norm_g, attn_w_o,
           moe_w_router, moe_b_router, moe_w_gate, moe_b_gate, moe_w_up, moe_b_up, moe_w_down, moe_b_down):
    bsz, s, d = x.shape
    moe = lambda l, xx, mod: _moe(xx, mod, l, norm2_g[l], moe_w_router[l], moe_b_router[l],
                                  moe_w_gate, moe_b_gate, moe_w_up, moe_b_up, moe_w_down, moe_b_down)
    mod0 = _ada(c, ada_w, ada_b[0], 0).reshape(bsz, 6, d)
    x = _sgu(x, mod0, norm1_g[0], sgu_w_in[0], sgu_b_in[0], sgu_v_g[0], sgu_w_s[0], sgu_b_s[0], sgu_w_out[0])
    x = moe(0, x, mod0)
    mod1 = _ada(c, ada_w, ada_b[1], 1).reshape(bsz, 6, d)
    kvmod = _ada(c, kv_ada_w, kv_ada_b).reshape(bsz, 2, d)
    q, kb, vtb, kmean = _kvq(x, mod1, kvmod, norm1_g[1], kv_norm_g, attn_w_q[0], w_kv, q_norm_g[0], k_norm_g)
    a = _attn(q, kb, vtb, kmean.reshape(bsz, s // MOBA_BLOCK, d))
    x = _oproj(a, x, mod1, attn_w_o[0])
    x = moe(1, x, mod1)
    return x
```

```python
import functools
import math

import jax
import jax.numpy as jnp
import numpy as np
from jax import lax
from jax.experimental import pallas as pl
from jax.experimental.pallas import tpu as pltpu

F32 = jnp.float32
BF16 = jnp.bfloat16
HIGHEST = lax.Precision.HIGHEST
SDS = jax.ShapeDtypeStruct

D_MODEL = 1024
SGU_CHUNK = 128
SGU_GROUPS = 8
D_SGU = 3 * D_MODEL
SGU_GROUP_DIM = D_SGU // SGU_GROUPS
HEAD_DIM = 64
N_HEADS = D_MODEL // HEAD_DIM
MOBA_BLOCK = 256
MOBA_TOPK = 3
ATT_U = 2
ATT_ONES = 16
LOG2E = math.log2(math.e)
N_EXPERTS = 32
TOP_K = 4
SWIGLU_LIMIT = 7.0
SWIGLU_ALPHA = 1.702
EXPERT_BLOCK = 256
EPS = 1e-6

VMEM_LIMIT_BYTES = 56 * 1024 * 1024
NT_DIMS = (((1,), (1,)), ((), ()))


def _cparams(sem):
    return pltpu.CompilerParams(dimension_semantics=sem, vmem_limit_bytes=VMEM_LIMIT_BYTES)


def _dot(a, b, **kw):
    return jnp.dot(a, b, preferred_element_type=F32, **kw)


def _norm_mod(x, g, sc, sh):
    ms = jnp.mean(x * x, axis=-1, keepdims=True)
    return (x * lax.rsqrt(ms + EPS) * g) * (1.0 + sc) + sh


def _ada_body(c_ref, w_ref, b_ref, o_ref):
    c = c_ref[...]
    ca = c * jax.nn.sigmoid(c)
    o_ref[...] = _dot(ca, w_ref[...], precision=HIGHEST) + b_ref[...]


def _ada(c, w, b, layer=0):
    bsz, d = c.shape
    n = w.shape[-1]
    tn = 1024
    w = w.reshape(-1, d, n)
    return pl.pallas_call(
        _ada_body,
        grid=(n // tn,),
        in_specs=[
            pl.BlockSpec((bsz, d), lambda j: (0, 0)),
            pl.BlockSpec((None, d, tn), lambda j: (layer, 0, j)),
            pl.BlockSpec((1, tn), lambda j: (0, j)),
        ],
        out_specs=pl.BlockSpec((bsz, tn), lambda j: (0, j)),
        out_shape=SDS((bsz, n), F32),
        compiler_params=_cparams(("arbitrary",)),
        name="ada",
    )(c, w, b.reshape(1, n))


SGU_TS = 256
SGU_CB = 768


def _gelu_tanh(x):
    c = np.float32(np.sqrt(2.0 / np.pi))
    return x * (0.5 * (1.0 + jnp.tanh(c * (x + 0.044715 * (x * x * x)))))


def _sgu_body(x_ref, mod_ref, n1g_ref, win_ref, bin_ref, vg_ref, ws_ref, bst_ref, wout_ref,
              o_ref, u_scr, v_scr, y_scr):
    ts = x_ref.shape[1]
    x = x_ref[0]
    mod = mod_ref[0]
    sh1, sc1, g1 = mod[0:1], mod[1:2], mod[2:3]
    h = _norm_mod(x, n1g_ref[...], sc1, sh1).astype(BF16)

    for j in range(D_SGU // SGU_CB):
        cs = slice(j * SGU_CB, (j + 1) * SGU_CB)
        u_scr[:, cs] = _gelu_tanh(_dot(h, win_ref[:, cs]) + bin_ref[:, cs])
    ssq = jnp.zeros((ts, 1), F32)
    for j in range(D_SGU // SGU_CB):
        cs = slice(j * SGU_CB, (j + 1) * SGU_CB)
        ws_cols = slice(D_SGU + j * SGU_CB, D_SGU + (j + 1) * SGU_CB)
        z = _gelu_tanh(_dot(h, win_ref[:, ws_cols]) + bin_ref[:, ws_cols])
        v_scr[:, cs] = z
        ssq = ssq + jnp.sum(z * z, axis=-1, keepdims=True)
    rs = lax.rsqrt(ssq / D_SGU + EPS)

    r_i = lax.broadcasted_iota(jnp.int32, (SGU_CHUNK, SGU_CHUNK), 0)
    c_i = lax.broadcasted_iota(jnp.int32, (SGU_CHUNK, SGU_CHUNK), 1)
    causal = c_i <= r_i
    for g in range(SGU_GROUPS):
        cs = slice(g * SGU_GROUP_DIM, (g + 1) * SGU_GROUP_DIM)
        wsg = jnp.where(causal, ws_ref[g], 0.0).astype(BF16)
        bsg = bst_ref[:, g:g + 1]
        for n in range(ts // SGU_CHUNK):
            rows = slice(n * SGU_CHUNK, (n + 1) * SGU_CHUNK)
            vn = (v_scr[rows, cs] * rs[rows] * vg_ref[:, cs]).astype(BF16)
            mixed = _dot(wsg, vn) + bsg
            y_scr[rows, cs] = (u_scr[rows, cs] * mixed).astype(BF16)

    m = _dot(y_scr[...], wout_ref[...])
    o_ref[0] = x + g1 * m


def _sgu(x, mod, n1g, w_in, b_in, v_g, w_s, b_s, w_out):
    bsz, s, d = x.shape
    ts = SGU_TS
    const = lambda *shape: pl.BlockSpec(shape, lambda b, i: (0,) * len(shape),
                                        pipeline_mode=pl.Buffered(1))
    return pl.pallas_call(
        _sgu_body,
        grid=(bsz, s // ts),
        in_specs=[
            pl.BlockSpec((1, ts, d), lambda b, i: (b, i, 0)),
            pl.BlockSpec((1, 6, d), lambda b, i: (b, 0, 0)),
            const(1, d),
            const(d, 2 * D_SGU),
            const(1, 2 * D_SGU),
            const(1, D_SGU),
            const(SGU_GROUPS, SGU_CHUNK, SGU_CHUNK),
            const(SGU_CHUNK, SGU_GROUPS),
            const(D_SGU, d),
        ],
        out_specs=pl.BlockSpec((1, ts, d), lambda b, i: (b, i, 0)),
        out_shape=SDS((bsz, s, d), F32),
        scratch_shapes=[
            pltpu.VMEM((ts, D_SGU), F32),
            pltpu.VMEM((ts, D_SGU), F32),
            pltpu.VMEM((ts, D_SGU), BF16),
        ],
        compiler_params=_cparams(("parallel", "arbitrary")),
        name="sgu",
    )(x, mod, n1g.reshape(1, d), w_in.astype(BF16), b_in.reshape(1, -1), v_g.reshape(1, -1),
      w_s, b_s.T, w_out.astype(BF16))


ROUTER_TS = 512


def _router_body(x_ref, mod_ref, n2g_ref, wr_ref, br_ref, h_ref, idx_ref, p_ref, rk_ref, cnt_ref, run_scr):
    @pl.when((pl.program_id(0) == 0) & (pl.program_id(1) == 0))
    def _():
        run_scr[...] = jnp.zeros(run_scr.shape, run_scr.dtype)

    x = x_ref[0]
    mod = mod_ref[0]
    sh2, sc2 = mod[3:4], mod[4:5]
    h2 = _norm_mod(x, n2g_ref[...], sc2, sh2)
    h_ref[0] = h2
    logits = _dot(h2, wr_ref[...], precision=HIGHEST) + br_ref[...]
    lane = lax.broadcasted_iota(jnp.int32, logits.shape, 1)
    work = logits
    vals, idxs = [], []
    for _ in range(TOP_K):
        mx = jnp.max(work, axis=-1, keepdims=True)
        ix = jnp.min(jnp.where(work == mx, lane, N_EXPERTS), axis=-1, keepdims=True)
        vals.append(mx)
        idxs.append(ix)
        work = jnp.where(lane == ix, -jnp.inf, work)
    tv = jnp.concatenate(vals, axis=-1)
    e = jnp.exp(tv - vals[0])
    p_ref[0] = e / jnp.sum(e, axis=-1, keepdims=True)
    idx_ref[0] = jnp.concatenate(idxs, axis=-1)

    ts = x.shape[0]
    onehots = [lane == ix for ix in idxs]
    mask = jnp.zeros(logits.shape, F32)
    for oh in onehots:
        mask = mask + jnp.where(oh, 1.0, 0.0)
    earlier = (lax.broadcasted_iota(jnp.int32, (ts, ts), 1) < lax.broadcasted_iota(jnp.int32, (ts, ts), 0))
    before = _dot(jnp.where(earlier, 1.0, 0.0).astype(BF16), mask.astype(BF16)) + run_scr[...]
    rk = [jnp.sum(jnp.where(oh, before, 0.0), axis=-1, keepdims=True) for oh in onehots]
    rk_ref[0] = jnp.concatenate(rk, axis=-1).astype(jnp.int32)
    run_scr[...] = run_scr[...] + jnp.sum(mask, axis=0, keepdims=True)
    cnt_ref[...] = run_scr[...].astype(jnp.int32)


def _router(x, mod, n2g, w_r, b_r):
    bsz, s, d = x.shape
    ts = min(ROUTER_TS, s)
    tok = lambda w: pl.BlockSpec((1, ts, w), lambda b, i: (b, i, 0))
    return pl.pallas_call(
        _router_body,
        grid=(bsz, s // ts),
        in_specs=[
            tok(d),
            pl.BlockSpec((1, 6, d), lambda b, i: (b, 0, 0)),
            pl.BlockSpec((1, d), lambda b, i: (0, 0)),
            pl.BlockSpec((d, N_EXPERTS), lambda b, i: (0, 0)),
            pl.BlockSpec((1, N_EXPERTS), lambda b, i: (0, 0)),
        ],
        out_specs=[tok(d), tok(TOP_K), tok(TOP_K), tok(TOP_K),
                   pl.BlockSpec((1, N_EXPERTS), lambda b, i: (0, 0))],
        out_shape=[SDS((bsz, s, d), F32), SDS((bsz, s, TOP_K), jnp.int32), SDS((bsz, s, TOP_K), F32),
                   SDS((bsz, s, TOP_K), jnp.int32), SDS((1, N_EXPERTS), jnp.int32)],
        scratch_shapes=[pltpu.VMEM((1, N_EXPERTS), F32)],
        compiler_params=_cparams(("arbitrary", "arbitrary")),
        name="router",
    )(x, mod, n2g.reshape(1, d), w_r, b_r.reshape(1, -1))


def _route_meta(top_idx, rank, counts):
    n = top_idx.shape[0]
    padded = (counts + EXPERT_BLOCK - 1) // EXPERT_BLOCK * EXPERT_BLOCK
    pend = jnp.cumsum(padded)
    pstart = pend - padded
    dest = (pstart[top_idx] + rank).astype(jnp.int32)
    n_rows = n * TOP_K + N_EXPERTS * EXPERT_BLOCK
    n_blk = n_rows // EXPERT_BLOCK
    blk_row0 = jnp.arange(n_blk) * EXPERT_BLOCK
    blk_exp = jnp.minimum(jnp.sum(pend[None, :] <= blk_row0[:, None], axis=1), N_EXPERTS - 1).astype(jnp.int32)
    n_used = (pend[-1] // EXPERT_BLOCK).astype(jnp.int32).reshape(1)
    pad_blk = jnp.where(padded > 0, pend - EXPERT_BLOCK, -1)
    tail = pend[-1] + jnp.arange(N_EXPERTS) * EXPERT_BLOCK
    tail_blk = jnp.where(tail < n_rows, tail, -1)
    pad_blk = jnp.concatenate([pad_blk, tail_blk]).astype(jnp.int32)
    return dest, blk_exp, n_used, pad_blk, n_rows


MOVE_TT = 256


def _dispatch_body(pad_ref, dest_ref, h_ref, xs_ref, zero_scr, sem, zsem):
    tt = h_ref.shape[0]

    @pl.when(pl.program_id(0) == 0)
    def _():
        zero_scr[...] = jnp.zeros(zero_scr.shape, zero_scr.dtype)

        def zero_copy(e):
            row0 = pl.multiple_of(pad_ref[e], EXPERT_BLOCK)
            return pltpu.make_async_copy(zero_scr, xs_ref.at[pl.ds(row0, EXPERT_BLOCK)], zsem)

        for e in range(2 * N_EXPERTS):
            pl.when(pad_ref[e] >= 0)(lambda e=e: zero_copy(e).start())
        for e in range(2 * N_EXPERTS):
            pl.when(pad_ref[e] >= 0)(lambda e=e: zero_copy(e).wait())

    def issue(t, carry):
        for k in range(TOP_K):
            d = dest_ref[0, t * TOP_K + k]
            pltpu.make_async_copy(h_ref.at[pl.ds(t, 1)], xs_ref.at[pl.ds(d, 1)], sem).start(priority=k % 2)
        return carry

    lax.fori_loop(0, tt, issue, 0)
    for k in range(TOP_K):
        pltpu.make_async_copy(h_ref, h_ref, sem).wait()


def _dispatch(h, dest, pad_blk, n_rows):
    n, d = h.shape
    tt = MOVE_TT
    dest3 = dest.reshape(n // tt, 1, tt * TOP_K)
    return pl.pallas_call(
        _dispatch_body,
        grid_spec=pltpu.PrefetchScalarGridSpec(
            num_scalar_prefetch=1,
            grid=(n // tt,),
            in_specs=[
                pl.BlockSpec((None, 1, tt * TOP_K), lambda i, pad: (i, 0, 0), memory_space=pltpu.SMEM),
                pl.BlockSpec((tt, d), lambda i, pad: (i, 0)),
            ],
            out_specs=pl.BlockSpec(memory_space=pl.ANY),
            scratch_shapes=[pltpu.VMEM((EXPERT_BLOCK, d), h.dtype), pltpu.SemaphoreType.DMA(()),
                            pltpu.SemaphoreType.DMA(())],
        ),
        out_shape=SDS((n_rows, d), h.dtype),
        compiler_params=_cparams(("arbitrary",)),
        name="dispatch",
    )(pad_blk, dest3, h)


def _ffn_body(be_ref, nu_ref, xs_ref, wg_ref, bg_ref, wu_ref, bu_ref, wd_ref, bd_ref, y_ref,
              wg_bf, wu_bf, wd_bf):
    i = pl.program_id(0)

    @pl.when((i == 0) | (be_ref[i] != be_ref[jnp.maximum(i - 1, 0)]))
    def _():
        wg_bf[...] = wg_ref[...].astype(BF16)
        wu_bf[...] = wu_ref[...].astype(BF16)
        wd_bf[...] = wd_ref[...].astype(BF16)

    @pl.when(i < nu_ref[0])
    def _():
        x = xs_ref[...].astype(BF16)
        g = jnp.minimum(_dot(x, wg_bf[...]) + bg_ref[...], SWIGLU_LIMIT)
        u = jnp.clip(_dot(x, wu_bf[...]) + bu_ref[...], -SWIGLU_LIMIT, SWIGLU_LIMIT)
        a = g * jax.nn.sigmoid(SWIGLU_ALPHA * g) * (u + 1.0)
        y_ref[...] = _dot(a.astype(BF16), wd_bf[...]) + bd_ref[...]

    @pl.when(i >= nu_ref[0])
    def _():
        y_ref[...] = jnp.zeros(y_ref.shape, y_ref.dtype)


def _ffn(xs, blk_exp, n_used, layer, w_gate, b_gate, w_up, b_up, w_down, b_down):
    n_rows, d = xs.shape
    _, e, _, de = w_gate.shape
    n_blk = n_rows // EXPERT_BLOCK
    wspec = lambda r, c: pl.BlockSpec((None, None, r, c), lambda i, be, nu: (layer, be[i], 0, 0))
    xs_map = lambda i, be, nu: (jnp.minimum(i, jnp.maximum(nu[0] - 1, 0)), 0)
    return pl.pallas_call(
        _ffn_body,
        grid_spec=pltpu.PrefetchScalarGridSpec(
            num_scalar_prefetch=2,
            grid=(n_blk,),
            in_specs=[
                pl.BlockSpec((EXPERT_BLOCK, d), xs_map),
                wspec(d, de), wspec(1, de),
                wspec(d, de), wspec(1, de),
                wspec(de, d), wspec(1, d),
            ],
            out_specs=pl.BlockSpec((EXPERT_BLOCK, d), lambda i, be, nu: (i, 0)),
            scratch_shapes=[pltpu.VMEM((d, de), BF16), pltpu.VMEM((d, de), BF16), pltpu.VMEM((de, d), BF16)],
        ),
        out_shape=SDS((n_rows, d), F32),
        compiler_params=_cparams(("arbitrary",)),
        name="ffn",
    )(blk_exp, n_used, xs, w_gate, b_gate.reshape(-1, e, 1, de), w_up, b_up.reshape(-1, e, 1, de),
      w_down, b_down.reshape(-1, e, 1, d))


def _combine_body(dest_ref, x_ref, mod_ref, p_ref, y_ref, o_ref, buf, sem):
    tt = x_ref.shape[0]

    def issue(t, carry):
        for k in range(TOP_K):
            d = dest_ref[0, t * TOP_K + k]
            pltpu.make_async_copy(y_ref.at[pl.ds(d, 1)], buf.at[k, pl.ds(t, 1)], sem).start(priority=k % 2)
        return carry

    lax.fori_loop(0, tt, issue, 0)
    pltpu.make_async_copy(buf, buf, sem).wait()

    p = p_ref[...]
    acc = p[:, 0:1] * buf[0]
    for k in range(1, TOP_K):
        acc = acc + p[:, k:k + 1] * buf[k]
    g2 = mod_ref[0][5:6]
    o_ref[...] = x_ref[...] + g2 * acc


def _combine(x, mod, p, y, dest, seq):
    n, d = x.shape
    tt = MOVE_TT
    dest3 = dest.reshape(n // tt, 1, tt * TOP_K)
    per_b = seq // tt
    return pl.pallas_call(
        _combine_body,
        grid=(n // tt,),
        in_specs=[
            pl.BlockSpec((None, 1, tt * TOP_K), lambda i: (i, 0, 0), memory_space=pltpu.SMEM),
            pl.BlockSpec((tt, d), lambda i: (i, 0)),
            pl.BlockSpec((1, 6, d), lambda i: (i // per_b, 0, 0)),
            pl.BlockSpec((tt, TOP_K), lambda i: (i, 0)),
            pl.BlockSpec(memory_space=pl.ANY),
        ],
        out_specs=pl.BlockSpec((tt, d), lambda i: (i, 0)),
        out_shape=SDS((n, d), F32),
        scratch_shapes=[pltpu.VMEM((TOP_K, tt, d), F32), pltpu.SemaphoreType.DMA(())],
        compiler_params=_cparams(("arbitrary",)),
        name="combine",
    )(dest3, x, mod, p, y)


def _moe(x, mod, layer, n2g, w_r, b_r, w_gate, b_gate, w_up, b_up, w_down, b_down):
    bsz, s, d = x.shape
    n = bsz * s
    h2, top_idx, probs, rank, counts = _router(x, mod, n2g, w_r, b_r)
    dest, blk_exp, n_used, pad_blk, n_rows = _route_meta(
        top_idx.reshape(n, TOP_K), rank.reshape(n, TOP_K), counts.reshape(N_EXPERTS))
    xs = _dispatch(h2.reshape(n, d), dest, pad_blk, n_rows)
    y = _ffn(xs, blk_exp, n_used, layer, w_gate, b_gate, w_up, b_up, w_down, b_down)
    out = _combine(x.reshape(n, d), mod, probs.reshape(n, TOP_K), y, dest, s)
    return out.reshape(bsz, s, d)


def _seg_mats():
    r = lax.broadcasted_iota(jnp.int32, (D_MODEL, N_HEADS), 0) // HEAD_DIM
    c = lax.broadcasted_iota(jnp.int32, (D_MODEL, N_HEADS), 1)
    seg = jnp.where(r == c, 1.0, 0.0).astype(BF16)
    rt = lax.broadcasted_iota(jnp.int32, (N_HEADS, D_MODEL), 0)
    ct = lax.broadcasted_iota(jnp.int32, (N_HEADS, D_MODEL), 1) // HEAD_DIM
    seg_t = jnp.where(rt == ct, 1.0, 0.0).astype(BF16)
    return seg, seg_t


def _split_dot(a, b):
    hi = a.astype(BF16)
    lo = (a - hi.astype(F32)).astype(BF16)
    return _dot(hi, b) + _dot(lo, b)


def _head_rmsnorm(x, g, seg, seg_t):
    ms = _split_dot(x * x, seg) * (1.0 / HEAD_DIM)
    r = lax.rsqrt(ms + EPS)
    return x * _split_dot(r, seg_t) * g


def _kvq_body(x_ref, mod_ref, kvmod_ref, n1g_ref, kvg_ref, wq_ref, wkv_ref, qg_ref, kg_ref,
              q_ref, k_ref, vt_ref, km_ref):
    x = x_ref[0]
    mod = mod_ref[0]
    kvmod = kvmod_ref[0]
    ms = jnp.mean(x * x, axis=-1, keepdims=True)
    xn = x * lax.rsqrt(ms + EPS)
    hq = ((xn * n1g_ref[...]) * (1.0 + mod[1:2]) + mod[0:1]).astype(BF16)
    hk = ((xn * kvg_ref[...]) * (1.0 + kvmod[1:2]) + kvmod[0:1]).astype(BF16)
    seg, seg_t = _seg_mats()
    q = _head_rmsnorm(_dot(hq, wq_ref[...]), qg_ref[...], seg, seg_t)
    q_ref[0] = q
    kv = _dot(hk, wkv_ref[...])
    k = _head_rmsnorm(kv[:, :D_MODEL], kg_ref[...], seg, seg_t)
    k_ref[0, 0] = k.astype(BF16)
    for u in range(ATT_U):
        rows = slice(u * MOBA_BLOCK, (u + 1) * MOBA_BLOCK)
        km_ref[0, 0, u:u + 1, :] = jnp.mean(k[rows], axis=0, keepdims=True)
    vt_ref[0, 0] = kv[:, D_MODEL:].T.astype(BF16)


def _kvq(x, mod, kvmod, n1g, kvg, w_q, w_kv, q_g, k_g):
    bsz, s, d = x.shape
    ts = ATT_U * MOBA_BLOCK
    nb = s // ts
    const = lambda *shape: pl.BlockSpec(shape, lambda b, i: (0,) * len(shape))
    return pl.pallas_call(
        _kvq_body,
        grid=(bsz, nb),
        in_specs=[
            pl.BlockSpec((1, ts, d), lambda b, i: (b, i, 0)),
            pl.BlockSpec((1, 6, d), lambda b, i: (b, 0, 0)),
            pl.BlockSpec((1, 2, d), lambda b, i: (b, 0, 0)),
            const(1, d), const(1, d), const(d, d), const(d, 2 * d), const(1, d), const(1, d),
        ],
        out_specs=[
            pl.BlockSpec((1, ts, d), lambda b, i: (b, i, 0)),
            pl.BlockSpec((1, 1, ts, d), lambda b, i: (b, i, 0, 0)),
            pl.BlockSpec((1, 1, d, ts), lambda b, i: (b, i, 0, 0)),
            pl.BlockSpec((1, 1, ATT_U, d), lambda b, i: (b, i, 0, 0)),
        ],
        out_shape=[
            SDS((bsz, s, d), F32),
            SDS((bsz, nb, ts, d), BF16),
            SDS((bsz, nb, d, ts), BF16),
            SDS((bsz, nb, ATT_U, d), F32),
        ],
        compiler_params=_cparams(("parallel", "arbitrary")),
        name="kvq",
    )(x, mod, kvmod, n1g.reshape(1, d), kvg.reshape(1, d), w_q.astype(BF16), w_kv.astype(BF16),
      jnp.tile(q_g, N_HEADS).reshape(1, d), jnp.tile(k_g, N_HEADS).reshape(1, d))


def _attn_body(slopes_ref, q_ref, k_ref, vt_ref, km_ref, o_ref, bias_scr, s_scr):
    hp = pl.program_id(1)
    own = pl.program_id(2)
    bs = MOBA_BLOCK
    nb = k_ref.shape[1] * ATT_U

    @pl.when(own == 0)
    def _():
        rel = (lax.broadcasted_iota(jnp.int32, (bs, bs), 1)
               - lax.broadcasted_iota(jnp.int32, (bs, bs), 0))
        relf = rel.astype(F32)
        for hd in range(2):
            a = (slopes_ref[2 * hp + hd] * LOG2E) * relf
            bias_scr[hd, 0] = a
            bias_scr[hd, 1] = jnp.where(rel >= 0, a, jnp.inf)

    q = q_ref[0]
    km = km_ref[0]
    dcol = lax.broadcasted_iota(jnp.int32, (1, 2 * HEAD_DIM), 1)
    blk_row = lax.broadcasted_iota(jnp.int32, (nb, 1), 0)
    valid = blk_row < own

    in_head = [(dcol >= hd * HEAD_DIM) & (dcol < (hd + 1) * HEAD_DIM) for hd in range(2)]
    km2 = jnp.concatenate([jnp.where(in_head[hd], km, 0.0) for hd in range(2)], axis=0)
    gates = lax.dot_general(km2, q, NT_DIMS, precision=HIGHEST, preferred_element_type=F32)

    qs, selbias = [], []
    for hd in range(2):
        slope = slopes_ref[2 * hp + hd]
        qm = jnp.where(in_head[hd], q, 0.0)
        gate = jnp.where(valid, gates[hd * nb:(hd + 1) * nb], -jnp.inf)
        rank = jnp.zeros(gate.shape, F32)
        for i in range(nb):
            gi = gate[i:i + 1, :]
            tie = jnp.where(blk_row > i, 1.0, 0.0)
            rank = rank + jnp.where(gi > gate, 1.0, jnp.where(gi == gate, tie, 0.0))
        chosen = jnp.where(valid, jnp.where(rank < MOBA_TOPK, 1.0, 0.0),
                           jnp.where(blk_row == own, 1.0, 0.0))
        off = (own - blk_row).astype(F32) * (bs * LOG2E * slope)
        selbias.append(jnp.where(chosen > 0.5, -off, -jnp.inf))
        qs.append((qm * (LOG2E * HEAD_DIM ** -0.5)).astype(BF16))

    last = k_ref.shape[1] - 1
    ones = jnp.ones((ATT_ONES, ATT_U * bs), BF16)

    def qk(it, hd):
        return lax.dot_general(k_ref[0, it], qs[hd], NT_DIMS, preferred_element_type=F32)

    def visit(it, slot, carry):
        out = []
        for hd in range(2):
            m, acc = carry[hd]
            s_scr[1 - slot, hd] = qk(jnp.minimum(it + 1, last), hd)
            rows, biases = [], []
            mx = m
            for u in range(ATT_U):
                j = it * ATT_U + u
                rows.append(jnp.sum(jnp.where(blk_row == j, selbias[hd], 0.0), axis=0, keepdims=True))
                biases.append((j == own).astype(jnp.int32))
                t = (s_scr[slot, hd, u * bs:(u + 1) * bs] - bias_scr[hd, biases[u]]) + rows[u]
                mx = jnp.maximum(mx, jnp.max(t, axis=0, keepdims=True))
            m_safe = jnp.where(mx == -jnp.inf, 0.0, mx)
            alpha = jnp.exp2(m - m_safe)
            ps = []
            for u in range(ATT_U):
                x = s_scr[slot, hd, u * bs:(u + 1) * bs] - (bias_scr[hd, biases[u]] - (rows[u] - m_safe))
                ps.append(jnp.exp2(x.astype(BF16)))
            p = jnp.concatenate(ps, axis=0)
            vj = jnp.concatenate([vt_ref[0, it, hd * HEAD_DIM:(hd + 1) * HEAD_DIM, :], ones], axis=0)
            acc = alpha * acc + _dot(vj, p)
            out.append((mx, acc))
        return tuple(out)

    def body(i2, carry):
        carry = visit(2 * i2, 0, carry)
        return visit(2 * i2 + 1, 1, carry)

    for hd in range(2):
        s_scr[0, hd] = qk(0, hd)
    init = tuple((jnp.full((1, bs), -jnp.inf, F32), jnp.zeros((HEAD_DIM + ATT_ONES, bs), F32))
                 for hd in range(2))
    n_visits = own // ATT_U + 1
    carry = lax.fori_loop(0, (n_visits + 1) // 2, body, init)
    outs = [acc[:HEAD_DIM] / acc[HEAD_DIM:HEAD_DIM + 1] for (_, acc) in carry]
    o_ref[0] = jnp.concatenate(outs, axis=0).astype(o_ref.dtype)


def _attn(q, kb, vtb, kmean):
    bsz, s, d = q.shape
    nsb = s // (ATT_U * MOBA_BLOCK)
    assert nsb % 2 == 0, "the attention loop visits key super-blocks in pairs"
    nb = s // MOBA_BLOCK
    w = 2 * HEAD_DIM
    slopes = (2.0 ** (-8.0 * jnp.arange(1, N_HEADS + 1, dtype=F32) / N_HEADS)).astype(F32)
    return pl.pallas_call(
        _attn_body,
        grid_spec=pltpu.PrefetchScalarGridSpec(
            num_scalar_prefetch=1,
            grid=(bsz, N_HEADS // 2, nb),
            in_specs=[
                pl.BlockSpec((1, MOBA_BLOCK, w), lambda b, h, c, sl: (b, c, h)),
                pl.BlockSpec((1, nsb, ATT_U * MOBA_BLOCK, w), lambda b, h, c, sl: (b, 0, 0, h)),
                pl.BlockSpec((1, nsb, w, ATT_U * MOBA_BLOCK), lambda b, h, c, sl: (b, 0, h, 0)),
                pl.BlockSpec((1, nb, w), lambda b, h, c, sl: (b, 0, h)),
            ],
            out_specs=pl.BlockSpec((1, w, MOBA_BLOCK), lambda b, h, c, sl: (b, h, c)),
            scratch_shapes=[pltpu.VMEM((2, 2, MOBA_BLOCK, MOBA_BLOCK), F32),
                            pltpu.VMEM((2, 2, ATT_U * MOBA_BLOCK, MOBA_BLOCK), F32)],
        ),
        out_shape=SDS((bsz, d, s), BF16),
        compiler_params=_cparams(("parallel", "parallel", "arbitrary")),
        name="attn",
    )(slopes, q, kb, vtb, kmean)


OPROJ_TS = 512


def _oproj_body(at_ref, x_ref, mod_ref, wo_ref, o_ref):
    g1 = mod_ref[0][2:3]
    m = lax.dot_general(at_ref[0], wo_ref[...], (((0,), (0,)), ((), ())), preferred_element_type=F32)
    o_ref[0] = x_ref[0] + g1 * m


def _oproj(at, x, mod, w_o):
    bsz, s, d = x.shape
    ts = min(OPROJ_TS, s)
    tok = pl.BlockSpec((1, ts, d), lambda b, i: (b, i, 0))
    return pl.pallas_call(
        _oproj_body,
        grid=(bsz, s // ts),
        in_specs=[pl.BlockSpec((1, d, ts), lambda b, i: (b, 0, i)), tok,
                  pl.BlockSpec((1, 6, d), lambda b, i: (b, 0, 0)),
                  pl.BlockSpec((d, d), lambda b, i: (0, 0))],
        out_specs=tok,
        out_shape=SDS((bsz, s, d), F32),
        compiler_params=_cparams(("parallel", "arbitrary")),
        name="oproj",
    )(at, x, mod, w_o.astype(BF16))


def kernel(x, c, ada_w, ada_b, norm1_g, norm2_g, sgu_w_in, sgu_b_in, sgu_v_g, sgu_w_s, sgu_b_s, sgu_w_out,
           kv_ada_w, kv_ada_b, kv_norm_g, w_kv, k_norm_g, attn_w_q, q_norm_g, attn_w_o,
           moe_w_router, moe_b_router, moe_w_gate, moe_b_gate, moe_w_up, moe_b_up, moe_w_down, moe_b_down):
    bsz, s, d = x.shape
    moe = lambda l, xx, mod: _moe(xx, mod, l, norm2_g[l], moe_w_router[l], moe_b_router[l],
                                  moe_w_gate, moe_b_gate, moe_w_up, moe_b_up, moe_w_down, moe_b_down)
    mod0 = _ada(c, ada_w, ada_b[0], 0).reshape(bsz, 6, d)
    x = _sgu(x, mod0, norm1_g[0], sgu_w_in[0], sgu_b_in[0], sgu_v_g[0], sgu_w_s[0], sgu_b_s[0], sgu_w_out[0])
    x = moe(0, x, mod0)
    mod1 = _ada(c, ada_w, ada_b[1], 1).reshape(bsz, 6, d)
    kvmod = _ada(c, kv_ada_w, kv_ada_b).reshape(bsz, 2, d)
    q, kb, vtb, kmean = _kvq(x, mod1, kvmod, norm1_g[1], kv_norm_g, attn_w_q[0], w_kv, q_norm_g[0], k_norm_g)
    a = _attn(q, kb, vtb, kmean.reshape(bsz, s // MOBA_BLOCK, d))
    x = _oproj(a, x, mod1, attn_w_o[0])
    x = moe(1, x, mod1)
    return x
```

```python
import functools
import math

import jax
import jax.numpy as jnp
import numpy as np
from jax import lax
from jax.experimental import pallas as pl
from jax.experimental.pallas import tpu as pltpu

F32 = jnp.float32
BF16 = jnp.bfloat16
HIGHEST = lax.Precision.HIGHEST
SDS = jax.ShapeDtypeStruct

D_MODEL = 1024
SGU_CHUNK = 128
SGU_GROUPS = 8
D_SGU = 3 * D_MODEL
SGU_GROUP_DIM = D_SGU // SGU_GROUPS
HEAD_DIM = 64
N_HEADS = D_MODEL // HEAD_DIM
MOBA_BLOCK = 256
MOBA_TOPK = 3
ATT_U = 2
ATT_ONES = 16
LOG2E = math.log2(math.e)
N_EXPERTS = 32
TOP_K = 4
SWIGLU_LIMIT = 7.0
SWIGLU_ALPHA = 1.702
EXPERT_BLOCK = 512
EPS = 1e-6

VMEM_LIMIT_BYTES = 56 * 1024 * 1024
NT_DIMS = (((1,), (1,)), ((), ()))


def _cparams(sem):
    return pltpu.CompilerParams(dimension_semantics=sem, vmem_limit_bytes=VMEM_LIMIT_BYTES)


def _dot(a, b, **kw):
    return jnp.dot(a, b, preferred_element_type=F32, **kw)


LANES = 128
ROW_TILE = D_MODEL // LANES


def _store_row_tiles(ref, val):
    rows = val.shape[0]
    for j in range(ROW_TILE):
        ref[pl.ds(j, rows, stride=ROW_TILE), :] = val[:, j * LANES:(j + 1) * LANES]


def _load_row_tiles(ref, rows):
    return jnp.concatenate([ref[pl.ds(j, rows, stride=ROW_TILE), :] for j in range(ROW_TILE)], axis=-1)


def _norm_mod(x, g, sc, sh):
    ms = jnp.mean(x * x, axis=-1, keepdims=True)
    return (x * lax.rsqrt(ms + EPS) * g) * (1.0 + sc) + sh


def _ada_body(c_ref, w_ref, b_ref, o_ref):
    c = c_ref[...]
    ca = c * jax.nn.sigmoid(c)
    o_ref[...] = _dot(ca, w_ref[...], precision=HIGHEST) + b_ref[...]


def _ada(c, w, b, layer=0):
    bsz, d = c.shape
    n = w.shape[-1]
    tn = 1024
    w = w.reshape(-1, d, n)
    return pl.pallas_call(
        _ada_body,
        grid=(n // tn,),
        in_specs=[
            pl.BlockSpec((bsz, d), lambda j: (0, 0)),
            pl.BlockSpec((None, d, tn), lambda j: (layer, 0, j)),
            pl.BlockSpec((1, tn), lambda j: (0, j)),
        ],
        out_specs=pl.BlockSpec((bsz, tn), lambda j: (0, j)),
        out_shape=SDS((bsz, n), F32),
        compiler_params=_cparams(("arbitrary",)),
        name="ada",
    )(c, w, b.reshape(1, n))


SGU_TS = 256
SGU_CB = 768


def _gelu_tanh(x):
    c = np.float32(np.sqrt(2.0 / np.pi))
    return x * (0.5 * (1.0 + jnp.tanh(c * (x + 0.044715 * (x * x * x)))))


def _sgu_body(x_ref, mod_ref, n1g_ref, win_ref, bin_ref, vg_ref, ws_ref, bst_ref, wout_ref,
              o_ref, u_scr, v_scr, y_scr):
    ts = x_ref.shape[1]
    x = x_ref[0]
    mod = mod_ref[0]
    sh1, sc1, g1 = mod[0:1], mod[1:2], mod[2:3]
    h = _norm_mod(x, n1g_ref[...], sc1, sh1).astype(BF16)

    for j in range(D_SGU // SGU_CB):
        cs = slice(j * SGU_CB, (j + 1) * SGU_CB)
        u_scr[:, cs] = _gelu_tanh(_dot(h, win_ref[:, cs]) + bin_ref[:, cs])
    ssq = jnp.zeros((ts, 1), F32)
    for j in range(D_SGU // SGU_CB):
        cs = slice(j * SGU_CB, (j + 1) * SGU_CB)
        ws_cols = slice(D_SGU + j * SGU_CB, D_SGU + (j + 1) * SGU_CB)
        z = _gelu_tanh(_dot(h, win_ref[:, ws_cols]) + bin_ref[:, ws_cols])
        v_scr[:, cs] = z
        ssq = ssq + jnp.sum(z * z, axis=-1, keepdims=True)
    rs = lax.rsqrt(ssq / D_SGU + EPS)

    r_i = lax.broadcasted_iota(jnp.int32, (SGU_CHUNK, SGU_CHUNK), 0)
    c_i = lax.broadcasted_iota(jnp.int32, (SGU_CHUNK, SGU_CHUNK), 1)
    causal = c_i <= r_i
    for g in range(SGU_GROUPS):
        cs = slice(g * SGU_GROUP_DIM, (g + 1) * SGU_GROUP_DIM)
        wsg = jnp.where(causal, ws_ref[g], 0.0).astype(BF16)
        bsg = bst_ref[:, g:g + 1]
        for n in range(ts // SGU_CHUNK):
            rows = slice(n * SGU_CHUNK, (n + 1) * SGU_CHUNK)
            vn = (v_scr[rows, cs] * rs[rows] * vg_ref[:, cs]).astype(BF16)
            mixed = _dot(wsg, vn) + bsg
            y_scr[rows, cs] = (u_scr[rows, cs] * mixed).astype(BF16)

    m = _dot(y_scr[...], wout_ref[...])
    o_ref[0] = x + g1 * m


def _sgu(x, mod, n1g, w_in, b_in, v_g, w_s, b_s, w_out):
    bsz, s, d = x.shape
    ts = SGU_TS
    const = lambda *shape: pl.BlockSpec(shape, lambda b, i: (0,) * len(shape),
                                        pipeline_mode=pl.Buffered(1))
    return pl.pallas_call(
        _sgu_body,
        grid=(bsz, s // ts),
        in_specs=[
            pl.BlockSpec((1, ts, d), lambda b, i: (b, i, 0)),
            pl.BlockSpec((1, 6, d), lambda b, i: (b, 0, 0)),
            const(1, d),
            const(d, 2 * D_SGU),
            const(1, 2 * D_SGU),
            const(1, D_SGU),
            const(SGU_GROUPS, SGU_CHUNK, SGU_CHUNK),
            const(SGU_CHUNK, SGU_GROUPS),
            const(D_SGU, d),
        ],
        out_specs=pl.BlockSpec((1, ts, d), lambda b, i: (b, i, 0)),
        out_shape=SDS((bsz, s, d), F32),
        scratch_shapes=[
            pltpu.VMEM((ts, D_SGU), F32),
            pltpu.VMEM((ts, D_SGU), F32),
            pltpu.VMEM((ts, D_SGU), BF16),
        ],
        compiler_params=_cparams(("parallel", "arbitrary")),
        name="sgu",
    )(x, mod, n1g.reshape(1, d), w_in.astype(BF16), b_in.reshape(1, -1), v_g.reshape(1, -1),
      w_s, b_s.T, w_out.astype(BF16))


ROUTER_TS = 512


def _router_body(x_ref, mod_ref, n2g_ref, wr_ref, br_ref, h_ref, idx_ref, p_ref, rk_ref, cnt_ref, run_scr):
    @pl.when((pl.program_id(0) == 0) & (pl.program_id(1) == 0))
    def _():
        run_scr[...] = jnp.zeros(run_scr.shape, run_scr.dtype)

    x = x_ref[0]
    mod = mod_ref[0]
    sh2, sc2 = mod[3:4], mod[4:5]
    h2 = _norm_mod(x, n2g_ref[...], sc2, sh2)
    _store_row_tiles(h_ref, h2)
    logits = _dot(h2, wr_ref[...], precision=HIGHEST) + br_ref[...]
    lane = lax.broadcasted_iota(jnp.int32, logits.shape, 1)
    work = logits
    vals, idxs = [], []
    for _ in range(TOP_K):
        mx = jnp.max(work, axis=-1, keepdims=True)
        ix = jnp.min(jnp.where(work == mx, lane, N_EXPERTS), axis=-1, keepdims=True)
        vals.append(mx)
        idxs.append(ix)
        work = jnp.where(lane == ix, -jnp.inf, work)
    tv = jnp.concatenate(vals, axis=-1)
    e = jnp.exp(tv - vals[0])
    p_ref[0] = e / jnp.sum(e, axis=-1, keepdims=True)
    idx_ref[0] = jnp.concatenate(idxs, axis=-1)

    ts = x.shape[0]
    onehots = [lane == ix for ix in idxs]
    mask = jnp.zeros(logits.shape, F32)
    for oh in onehots:
        mask = mask + jnp.where(oh, 1.0, 0.0)
    earlier = (lax.broadcasted_iota(jnp.int32, (ts, ts), 1) < lax.broadcasted_iota(jnp.int32, (ts, ts), 0))
    before = _dot(jnp.where(earlier, 1.0, 0.0).astype(BF16), mask.astype(BF16)) + run_scr[...]
    rk = [jnp.sum(jnp.where(oh, before, 0.0), axis=-1, keepdims=True) for oh in onehots]
    rk_ref[0] = jnp.concatenate(rk, axis=-1).astype(jnp.int32)
    run_scr[...] = run_scr[...] + jnp.sum(mask, axis=0, keepdims=True)
    cnt_ref[...] = run_scr[...].astype(jnp.int32)


def _router(x, mod, n2g, w_r, b_r):
    bsz, s, d = x.shape
    ts = min(ROUTER_TS, s)
    tok = lambda w: pl.BlockSpec((1, ts, w), lambda b, i: (b, i, 0))
    return pl.pallas_call(
        _router_body,
        grid=(bsz, s // ts),
        in_specs=[
            tok(d),
            pl.BlockSpec((1, 6, d), lambda b, i: (b, 0, 0)),
            pl.BlockSpec((1, d), lambda b, i: (0, 0)),
            pl.BlockSpec((d, N_EXPERTS), lambda b, i: (0, 0)),
            pl.BlockSpec((1, N_EXPERTS), lambda b, i: (0, 0)),
        ],
        out_specs=[pl.BlockSpec((ts * ROW_TILE, LANES), lambda b, i: (b * (s // ts) + i, 0)),
                   tok(TOP_K), tok(TOP_K), tok(TOP_K),
                   pl.BlockSpec((1, N_EXPERTS), lambda b, i: (0, 0))],
        out_shape=[SDS((bsz * s * ROW_TILE, LANES), F32), SDS((bsz, s, TOP_K), jnp.int32),
                   SDS((bsz, s, TOP_K), F32),
                   SDS((bsz, s, TOP_K), jnp.int32), SDS((1, N_EXPERTS), jnp.int32)],
        scratch_shapes=[pltpu.VMEM((1, N_EXPERTS), F32)],
        compiler_params=_cparams(("arbitrary", "arbitrary")),
        name="router",
    )(x, mod, n2g.reshape(1, d), w_r, b_r.reshape(1, -1))


def _route_meta(top_idx, rank, counts):
    n = top_idx.shape[0]
    padded = (counts + EXPERT_BLOCK - 1) // EXPERT_BLOCK * EXPERT_BLOCK
    pend = jnp.cumsum(padded)
    pstart = pend - padded
    dest = (pstart[top_idx] + rank).astype(jnp.int32)
    n_rows = n * TOP_K + N_EXPERTS * EXPERT_BLOCK
    n_blk = n_rows // EXPERT_BLOCK
    blk_row0 = jnp.arange(n_blk) * EXPERT_BLOCK
    blk_exp = jnp.minimum(jnp.sum(pend[None, :] <= blk_row0[:, None], axis=1), N_EXPERTS - 1)
    n_used = (pend[-1] // EXPERT_BLOCK).astype(jnp.int32).reshape(1)
    eids = jnp.arange(N_EXPERTS)
    has_rows = counts > 0
    last_exp = jnp.max(jnp.where(has_rows, eids, 0))
    blk_exp = jnp.where(blk_row0 < pend[-1], blk_exp, last_exp).astype(jnp.int32)
    nxt_e = jnp.min(jnp.where((eids[None, :] > eids[:, None]) & has_rows[None, :], eids[None, :], N_EXPERTS), axis=1)
    nxt_e = jnp.where(nxt_e < N_EXPERTS, nxt_e, -1)
    slot_e = (jnp.cumsum(has_rows.astype(jnp.int32)) - 1) % 2
    blk_nxt = nxt_e[blk_exp].astype(jnp.int32)
    blk_slot = slot_e[blk_exp].astype(jnp.int32)
    pad_blk = jnp.where(padded > 0, pend - EXPERT_BLOCK, -1)
    tail = pend[-1] + jnp.arange(N_EXPERTS) * EXPERT_BLOCK
    tail_blk = jnp.where(tail < n_rows, tail, -1)
    pad_blk = jnp.concatenate([pad_blk, tail_blk]).astype(jnp.int32)
    return dest, (blk_exp, n_used, blk_nxt, blk_slot), pad_blk, n_rows


MOVE_TT = 256


def _dispatch_body(pad_ref, dest_ref, h_ref, xs_ref, zero_scr, sem, zsem):
    tt = h_ref.shape[0] // ROW_TILE
    blk = EXPERT_BLOCK * ROW_TILE

    @pl.when(pl.program_id(0) == 0)
    def _():
        zero_scr[...] = jnp.zeros(zero_scr.shape, zero_scr.dtype)

        def zero_copy(e):
            row0 = pl.multiple_of(pad_ref[e], blk)
            return pltpu.make_async_copy(zero_scr, xs_ref.at[pl.ds(row0, blk)], zsem)

        for e in range(2 * N_EXPERTS):
            pl.when(pad_ref[e] >= 0)(lambda e=e: zero_copy(e).start())
        for e in range(2 * N_EXPERTS):
            pl.when(pad_ref[e] >= 0)(lambda e=e: zero_copy(e).wait())

    def issue(t, carry):
        src = h_ref.at[pl.ds(pl.multiple_of(t * ROW_TILE, ROW_TILE), ROW_TILE)]
        for k in range(TOP_K):
            d = pl.multiple_of(dest_ref[0, t * TOP_K + k], ROW_TILE)
            pltpu.make_async_copy(src, xs_ref.at[pl.ds(d, ROW_TILE)], sem).start(priority=k % 2)
        return carry

    lax.fori_loop(0, tt, issue, 0)
    for k in range(TOP_K):
        pltpu.make_async_copy(h_ref, h_ref, sem).wait()


def _dispatch(h, dest8, pad_blk, n_rows):
    n = h.shape[0] // ROW_TILE
    tt = MOVE_TT
    dest3 = dest8.reshape(n // tt, 1, tt * TOP_K)
    return pl.pallas_call(
        _dispatch_body,
        grid_spec=pltpu.PrefetchScalarGridSpec(
            num_scalar_prefetch=1,
            grid=(n // tt,),
            in_specs=[
                pl.BlockSpec((None, 1, tt * TOP_K), lambda i, pad: (i, 0, 0), memory_space=pltpu.SMEM),
                pl.BlockSpec((tt * ROW_TILE, LANES), lambda i, pad: (i, 0)),
            ],
            out_specs=pl.BlockSpec(memory_space=pl.ANY),
            scratch_shapes=[pltpu.VMEM((EXPERT_BLOCK * ROW_TILE, LANES), h.dtype), pltpu.SemaphoreType.DMA(()),
                            pltpu.SemaphoreType.DMA(())],
        ),
        out_shape=SDS((n_rows * ROW_TILE, LANES), h.dtype),
        compiler_params=_cparams(("arbitrary",)),
        name="dispatch",
    )(pad_blk * ROW_TILE, dest3, h)


def _ffn_body(layer, be_ref, nu_ref, nx_ref, sl_ref, xs_ref, wg_hbm, bg_ref, wu_hbm, bu_ref, wd_hbm, bd_ref,
              y_ref, w_f32, w_bf, sem):
    i = pl.program_id(0)
    slot = sl_ref[i]

    def fetch(e, s):
        return [pltpu.make_async_copy(w.at[layer, e], w_f32.at[s, n], sem.at[s])
                for n, w in enumerate((wg_hbm, wu_hbm, wd_hbm))]

    @pl.when(i == 0)
    def _():
        for c in fetch(be_ref[0], slot):
            c.start()

    @pl.when((i == 0) | (be_ref[i] != be_ref[jnp.maximum(i - 1, 0)]))
    def _():
        for c in fetch(be_ref[i], slot):
            c.wait()

        @pl.when(nx_ref[i] >= 0)
        def _():
            for c in fetch(nx_ref[i], 1 - slot):
                c.start()

        for n in range(3):
            w_bf[n] = w_f32[slot, n].astype(BF16)

    @pl.when(i < nu_ref[0])
    def _():
        x = _load_row_tiles(xs_ref, EXPERT_BLOCK).astype(BF16)
        g = jnp.minimum(_dot(x, w_bf[0]) + bg_ref[...], SWIGLU_LIMIT)
        u = jnp.clip(_dot(x, w_bf[1]) + bu_ref[...], -SWIGLU_LIMIT, SWIGLU_LIMIT)
        a = g * jax.nn.sigmoid(SWIGLU_ALPHA * g) * (u + 1.0)
        _store_row_tiles(y_ref, _dot(a.astype(BF16), w_bf[2]) + bd_ref[...])

    @pl.when(i >= nu_ref[0])
    def _():
        y_ref[...] = jnp.zeros(y_ref.shape, y_ref.dtype)


def _ffn(xs, blk_meta, layer, w_gate, b_gate, w_up, b_up, w_down, b_down):
    n_rows = xs.shape[0] // ROW_TILE
    _, e, d, de = w_gate.shape
    assert d == de == D_MODEL
    n_blk = n_rows // EXPERT_BLOCK
    blk = (EXPERT_BLOCK * ROW_TILE, LANES)
    bspec = lambda c: pl.BlockSpec((None, None, 1, c), lambda i, be, nu, nx, sl: (layer, be[i], 0, 0))
    hbm = pl.BlockSpec(memory_space=pl.ANY)
    xs_map = lambda i, be, nu, nx, sl: (jnp.minimum(i, jnp.maximum(nu[0] - 1, 0)), 0)
    return pl.pallas_call(
        functools.partial(_ffn_body, layer),
        grid_spec=pltpu.PrefetchScalarGridSpec(
            num_scalar_prefetch=4,
            grid=(n_blk,),
            in_specs=[
                pl.BlockSpec(blk, xs_map),
                hbm, bspec(de), hbm, bspec(de), hbm, bspec(d),
            ],
            out_specs=pl.BlockSpec(blk, lambda i, be, nu, nx, sl: (i, 0)),
            scratch_shapes=[pltpu.VMEM((2, 3, d, de), F32), pltpu.VMEM((3, d, de), BF16),
                            pltpu.SemaphoreType.DMA((2,))],
        ),
        out_shape=SDS((n_rows * ROW_TILE, LANES), F32),
        compiler_params=_cparams(("arbitrary",)),
        name="ffn",
    )(*blk_meta, xs, w_gate, b_gate.reshape(-1, e, 1, de), w_up, b_up.reshape(-1, e, 1, de),
      w_down, b_down.reshape(-1, e, 1, d))


def _combine_body(dest_ref, x_ref, mod_ref, p_ref, y_ref, o_ref, buf, sem):
    tt = x_ref.shape[0]

    def issue(t, carry):
        t8 = pl.multiple_of(t * ROW_TILE, ROW_TILE)
        for k in range(TOP_K):
            d = pl.multiple_of(dest_ref[0, t * TOP_K + k], ROW_TILE)
            pltpu.make_async_copy(y_ref.at[pl.ds(d, ROW_TILE)], buf.at[k, pl.ds(t8, ROW_TILE)],
                                  sem).start(priority=k % 2)
        return carry

    lax.fori_loop(0, tt, issue, 0)
    pltpu.make_async_copy(buf, buf, sem).wait()

    p = p_ref[...]
    g2 = mod_ref[0][5:6]
    for j in range(ROW_TILE):
        cols = slice(j * LANES, (j + 1) * LANES)
        acc = p[:, 0:1] * buf[0, pl.ds(j, tt, stride=ROW_TILE), :]
        for k in range(1, TOP_K):
            acc = acc + p[:, k:k + 1] * buf[k, pl.ds(j, tt, stride=ROW_TILE), :]
        o_ref[:, cols] = x_ref[:, cols] + g2[:, cols] * acc


def _combine(x, mod, p, y, dest8, seq):
    n, d = x.shape
    tt = MOVE_TT
    dest3 = dest8.reshape(n // tt, 1, tt * TOP_K)
    per_b = seq // tt
    return pl.pallas_call(
        _combine_body,
        grid=(n // tt,),
        in_specs=[
            pl.BlockSpec((None, 1, tt * TOP_K), lambda i: (i, 0, 0), memory_space=pltpu.SMEM),
            pl.BlockSpec((tt, d), lambda i: (i, 0)),
            pl.BlockSpec((1, 6, d), lambda i: (i // per_b, 0, 0)),
            pl.BlockSpec((tt, TOP_K), lambda i: (i, 0)),
            pl.BlockSpec(memory_space=pl.ANY),
        ],
        out_specs=pl.BlockSpec((tt, d), lambda i: (i, 0)),
        out_shape=SDS((n, d), F32),
        scratch_shapes=[pltpu.VMEM((TOP_K, tt * ROW_TILE, LANES), F32), pltpu.SemaphoreType.DMA(())],
        compiler_params=_cparams(("arbitrary",)),
        name="combine",
    )(dest3, x, mod, p, y)


def _moe(x, mod, layer, n2g, w_r, b_r, w_gate, b_gate, w_up, b_up, w_down, b_down):
    bsz, s, d = x.shape
    n = bsz * s
    h2, top_idx, probs, rank, counts = _router(x, mod, n2g, w_r, b_r)
    dest, blk_meta, pad_blk, n_rows = _route_meta(
        top_idx.reshape(n, TOP_K), rank.reshape(n, TOP_K), counts.reshape(N_EXPERTS))
    dest8 = dest * ROW_TILE
    xs = _dispatch(h2, dest8, pad_blk, n_rows)
    y = _ffn(xs, blk_meta, layer, w_gate, b_gate, w_up, b_up, w_down, b_down)
    out = _combine(x.reshape(n, d), mod, probs.reshape(n, TOP_K), y, dest8, s)
    return out.reshape(bsz, s, d)


def _seg_mats():
    r = lax.broadcasted_iota(jnp.int32, (D_MODEL, N_HEADS), 0) // HEAD_DIM
    c = lax.broadcasted_iota(jnp.int32, (D_MODEL, N_HEADS), 1)
    seg = jnp.where(r == c, 1.0, 0.0).astype(BF16)
    rt = lax.broadcasted_iota(jnp.int32, (N_HEADS, D_MODEL), 0)
    ct = lax.broadcasted_iota(jnp.int32, (N_HEADS, D_MODEL), 1) // HEAD_DIM
    seg_t = jnp.where(rt == ct, 1.0, 0.0).astype(BF16)
    return seg, seg_t


def _split_dot(a, b):
    hi = a.astype(BF16)
    lo = (a - hi.astype(F32)).astype(BF16)
    return _dot(hi, b) + _dot(lo, b)


def _head_rmsnorm(x, g, seg, seg_t):
    ms = _split_dot(x * x, seg) * (1.0 / HEAD_DIM)
    r = lax.rsqrt(ms + EPS)
    return x * _split_dot(r, seg_t) * g


def _kvq_body(x_ref, mod_ref, kvmod_ref, n1g_ref, kvg_ref, wq_ref, wkv_ref, qg_ref, kg_ref,
              q_ref, k_ref, vt_ref, km_ref):
    x = x_ref[0]
    mod = mod_ref[0]
    kvmod = kvmod_ref[0]
    ms = jnp.mean(x * x, axis=-1, keepdims=True)
    xn = x * lax.rsqrt(ms + EPS)
    hq = ((xn * n1g_ref[...]) * (1.0 + mod[1:2]) + mod[0:1]).astype(BF16)
    hk = ((xn * kvg_ref[...]) * (1.0 + kvmod[1:2]) + kvmod[0:1]).astype(BF16)
    seg, seg_t = _seg_mats()
    q = _head_rmsnorm(_dot(hq, wq_ref[...]), qg_ref[...], seg, seg_t)
    q_ref[0] = q
    kv = _dot(hk, wkv_ref[...])
    k = _head_rmsnorm(kv[:, :D_MODEL], kg_ref[...], seg, seg_t)
    k_ref[0, 0] = k.astype(BF16)
    for u in range(ATT_U):
        rows = slice(u * MOBA_BLOCK, (u + 1) * MOBA_BLOCK)
        km_ref[0, 0, u:u + 1, :] = jnp.mean(k[rows], axis=0, keepdims=True)
    vt_ref[0, 0] = kv[:, D_MODEL:].T.astype(BF16)


def _kvq(x, mod, kvmod, n1g, kvg, w_q, w_kv, q_g, k_g):
    bsz, s, d = x.shape
    ts = ATT_U * MOBA_BLOCK
    nb = s // ts
    const = lambda *shape: pl.BlockSpec(shape, lambda b, i: (0,) * len(shape))
    return pl.pallas_call(
        _kvq_body,
        grid=(bsz, nb),
        in_specs=[
            pl.BlockSpec((1, ts, d), lambda b, i: (b, i, 0)),
            pl.BlockSpec((1, 6, d), lambda b, i: (b, 0, 0)),
            pl.BlockSpec((1, 2, d), lambda b, i: (b, 0, 0)),
            const(1, d), const(1, d), const(d, d), const(d, 2 * d), const(1, d), const(1, d),
        ],
        out_specs=[
            pl.BlockSpec((1, ts, d), lambda b, i: (b, i, 0)),
            pl.BlockSpec((1, 1, ts, d), lambda b, i: (b, i, 0, 0)),
            pl.BlockSpec((1, 1, d, ts), lambda b, i: (b, i, 0, 0)),
            pl.BlockSpec((1, 1, ATT_U, d), lambda b, i: (b, i, 0, 0)),
        ],
        out_shape=[
            SDS((bsz, s, d), F32),
            SDS((bsz, nb, ts, d), BF16),
            SDS((bsz, nb, d, ts), BF16),
            SDS((bsz, nb, ATT_U, d), F32),
        ],
        compiler_params=_cparams(("parallel", "arbitrary")),
        name="kvq",
    )(x, mod, kvmod, n1g.reshape(1, d), kvg.reshape(1, d), w_q.astype(BF16), w_kv.astype(BF16),
      jnp.tile(q_g, N_HEADS).reshape(1, d), jnp.tile(k_g, N_HEADS).reshape(1, d))


def _attn_body(slopes_ref, q_ref, k_ref, vt_ref, km_ref, o_ref, bias_scr, s_scr):
    hp = pl.program_id(1)
    own = pl.program_id(2)
    bs = MOBA_BLOCK
    nb = k_ref.shape[1] * ATT_U

    @pl.when(own == 0)
    def _():
        rel = (lax.broadcasted_iota(jnp.int32, (bs, bs), 1)
               - lax.broadcasted_iota(jnp.int32, (bs, bs), 0))
        relf = rel.astype(F32)
        for hd in range(2):
            a = (slopes_ref[2 * hp + hd] * LOG2E) * relf
            bias_scr[hd, 0] = a
            bias_scr[hd, 1] = jnp.where(rel >= 0, a, jnp.inf)

    q = q_ref[0]
    km = km_ref[0]
    dcol = lax.broadcasted_iota(jnp.int32, (1, 2 * HEAD_DIM), 1)
    blk_row = lax.broadcasted_iota(jnp.int32, (nb, 1), 0)
    valid = blk_row < own

    in_head = [(dcol >= hd * HEAD_DIM) & (dcol < (hd + 1) * HEAD_DIM) for hd in range(2)]
    km2 = jnp.concatenate([jnp.where(in_head[hd], km, 0.0) for hd in range(2)], axis=0)
    gates = lax.dot_general(km2, q, NT_DIMS, precision=HIGHEST, preferred_element_type=F32)

    qs, selbias = [], []
    for hd in range(2):
        slope = slopes_ref[2 * hp + hd]
        qm = jnp.where(in_head[hd], q, 0.0)
        gate = jnp.where(valid, gates[hd * nb:(hd + 1) * nb], -jnp.inf)
        rank = jnp.zeros(gate.shape, F32)
        for i in range(nb):
            gi = gate[i:i + 1, :]
            tie = jnp.where(blk_row > i, 1.0, 0.0)
            rank = rank + jnp.where(gi > gate, 1.0, jnp.where(gi == gate, tie, 0.0))
        chosen = jnp.where(valid, jnp.where(rank < MOBA_TOPK, 1.0, 0.0),
                           jnp.where(blk_row == own, 1.0, 0.0))
        off = (own - blk_row).astype(F32) * (bs * LOG2E * slope)
        selbias.append(jnp.where(chosen > 0.5, -off, -jnp.inf))
        qs.append((qm * (LOG2E * HEAD_DIM ** -0.5)).astype(BF16))

    last = k_ref.shape[1] - 1
    ones = jnp.ones((ATT_ONES, ATT_U * bs), BF16)

    def qk(it, hd):
        return lax.dot_general(k_ref[0, it], qs[hd], NT_DIMS, preferred_element_type=F32)

    def visit(it, slot, carry):
        out = []
        for hd in range(2):
            m, acc = carry[hd]
            s_scr[1 - slot, hd] = qk(jnp.minimum(it + 1, last), hd)
            rows, biases = [], []
            mx = m
            for u in range(ATT_U):
                j = it * ATT_U + u
                rows.append(jnp.sum(jnp.where(blk_row == j, selbias[hd], 0.0), axis=0, keepdims=True))
                biases.append((j == own).astype(jnp.int32))
                t = (s_scr[slot, hd, u * bs:(u + 1) * bs] - bias_scr[hd, biases[u]]) + rows[u]
                mx = jnp.maximum(mx, jnp.max(t, axis=0, keepdims=True))
            m_safe = jnp.where(mx == -jnp.inf, 0.0, mx)
            alpha = jnp.exp2(m - m_safe)
            ps = []
            for u in range(ATT_U):
                x = s_scr[slot, hd, u * bs:(u + 1) * bs] - (bias_scr[hd, biases[u]] - (rows[u] - m_safe))
                ps.append(jnp.exp2(x.astype(BF16)))
            p = jnp.concatenate(ps, axis=0)
            vj = jnp.concatenate([vt_ref[0, it, hd * HEAD_DIM:(hd + 1) * HEAD_DIM, :], ones], axis=0)
            acc = alpha * acc + _dot(vj, p)
            out.append((mx, acc))
        return tuple(out)

    def body(i2, carry):
        carry = visit(2 * i2, 0, carry)
        return visit(2 * i2 + 1, 1, carry)

    for hd in range(2):
        s_scr[0, hd] = qk(0, hd)
    init = tuple((jnp.full((1, bs), -jnp.inf, F32), jnp.zeros((HEAD_DIM + ATT_ONES, bs), F32))
                 for hd in range(2))
    n_visits = own // ATT_U + 1
    carry = lax.fori_loop(0, (n_visits + 1) // 2, body, init)
    outs = [acc[:HEAD_DIM] / acc[HEAD_DIM:HEAD_DIM + 1] for (_, acc) in carry]
    o_ref[0] = jnp.concatenate(outs, axis=0).astype(o_ref.dtype)


def _attn(q, kb, vtb, kmean):
    bsz, s, d = q.shape
    nsb = s // (ATT_U * MOBA_BLOCK)
    assert nsb % 2 == 0, "the attention loop visits key super-blocks in pairs"
    nb = s // MOBA_BLOCK
    w = 2 * HEAD_DIM
    slopes = (2.0 ** (-8.0 * jnp.arange(1, N_HEADS + 1, dtype=F32) / N_HEADS)).astype(F32)
    return pl.pallas_call(
        _attn_body,
        grid_spec=pltpu.PrefetchScalarGridSpec(
            num_scalar_prefetch=1,
            grid=(bsz, N_HEADS // 2, nb),
            in_specs=[
                pl.BlockSpec((1, MOBA_BLOCK, w), lambda b, h, c, sl: (b, c, h)),
                pl.BlockSpec((1, nsb, ATT_U * MOBA_BLOCK, w), lambda b, h, c, sl: (b, 0, 0, h)),
                pl.BlockSpec((1, nsb, w, ATT_U * MOBA_BLOCK), lambda b, h, c, sl: (b, 0, h, 0)),
                pl.BlockSpec((1, nb, w), lambda b, h, c, sl: (b, 0, h)),
            ],
            out_specs=pl.BlockSpec((1, w, MOBA_BLOCK), lambda b, h, c, sl: (b, h, c)),
            scratch_shapes=[pltpu.VMEM((2, 2, MOBA_BLOCK, MOBA_BLOCK), F32),
                            pltpu.VMEM((2, 2, ATT_U * MOBA_BLOCK, MOBA_BLOCK), F32)],
        ),
        out_shape=SDS((bsz, d, s), BF16),
        compiler_params=_cparams(("parallel", "parallel", "arbitrary")),
        name="attn",
    )(slopes, q, kb, vtb, kmean)


OPROJ_TS = 512


def _oproj_body(at_ref, x_ref, mod_ref, wo_ref, o_ref):
    g1 = mod_ref[0][2:3]
    m = lax.dot_general(at_ref[0], wo_ref[...], (((0,), (0,)), ((), ())), preferred_element_type=F32)
    o_ref[0] = x_ref[0] + g1 * m


def _oproj(at, x, mod, w_o):
    bsz, s, d = x.shape
    ts = min(OPROJ_TS, s)
    tok = pl.BlockSpec((1, ts, d), lambda b, i: (b, i, 0))
    return pl.pallas_call(
        _oproj_body,
        grid=(bsz, s // ts),
        in_specs=[pl.BlockSpec((1, d, ts), lambda b, i: (b, 0, i)), tok,
                  pl.BlockSpec((1, 6, d), lambda b, i: (b, 0, 0)),
                  pl.BlockSpec((d, d), lambda b, i: (0, 0))],
        out_specs=tok,
        out_shape=SDS((bsz, s, d), F32),
        compiler_params=_cparams(("parallel", "arbitrary")),
        name="oproj",
    )(at, x, mod, w_o.astype(BF16))


def kernel(x, c, ada_w, ada_b, norm1_g, norm2_g, sgu_w_in, sgu_b_in, sgu_v_g, sgu_w_s, sgu_b_s, sgu_w_out,
           kv_ada_w, kv_ada_b, kv_norm_g, w_kv, k_norm_g, attn_w_q, q_norm_g, attn_w_o,
           moe_w_router, moe_b_router, moe_w_gate, moe_b_gate, moe_w_up, moe_b_up, moe_w_down, moe_b_down):
    bsz, s, d = x.shape
    moe = lambda l, xx, mod: _moe(xx, mod, l, norm2_g[l], moe_w_router[l], moe_b_router[l],
                                  moe_w_gate, moe_b_gate, moe_w_up, moe_b_up, moe_w_down, moe_b_down)
    mod0 = _ada(c, ada_w, ada_b[0], 0).reshape(bsz, 6, d)
    x = _sgu(x, mod0, norm1_g[0], sgu_w_in[0], sgu_b_in[0], sgu_v_g[0], sgu_w_s[0], sgu_b_s[0], sgu_w_out[0])
    x = moe(0, x, mod0)
    mod1 = _ada(c, ada_w, ada_b[1], 1).reshape(bsz, 6, d)
    kvmod = _ada(c, kv_ada_w, kv_ada_b).reshape(bsz, 2, d)
    q, kb, vtb, kmean = _kvq(x, mod1, kvmod, norm1_g[1], kv_norm_g, attn_w_q[0], w_kv, q_norm_g[0], k_norm_g)
    a = _attn(q, kb, vtb, kmean.reshape(bsz, s // MOBA_BLOCK, d))
    x = _oproj(a, x, mod1, attn_w_o[0])
    x = moe(1, x, mod1)
    return x
```

```python
import functools
import math

import jax
import jax.numpy as jnp
import numpy as np
from jax import lax
from jax.experimental import pallas as pl
from jax.experimental.pallas import tpu as pltpu

F32 = jnp.float32
BF16 = jnp.bfloat16
HIGHEST = lax.Precision.HIGHEST
SDS = jax.ShapeDtypeStruct

D_MODEL = 1024
SGU_CHUNK = 128
SGU_GROUPS = 8
D_SGU = 3 * D_MODEL
SGU_GROUP_DIM = D_SGU // SGU_GROUPS
HEAD_DIM = 64
N_HEADS = D_MODEL // HEAD_DIM
MOBA_BLOCK = 256
MOBA_TOPK = 3
ATT_U = 2
ATT_ONES = 16
LOG2E = math.log2(math.e)
N_EXPERTS = 32
TOP_K = 4
SWIGLU_LIMIT = 7.0
SWIGLU_ALPHA = 1.702
EXPERT_BLOCK = 512
EPS = 1e-6

VMEM_LIMIT_BYTES = 56 * 1024 * 1024
NT_DIMS = (((1,), (1,)), ((), ()))


def _cparams(sem):
    return pltpu.CompilerParams(dimension_semantics=sem, vmem_limit_bytes=VMEM_LIMIT_BYTES)


def _dot(a, b, **kw):
    return jnp.dot(a, b, preferred_element_type=F32, **kw)


LANES = 128
ROW_TILE = D_MODEL // LANES


def _store_row_tiles(ref, val):
    rows = val.shape[0]
    for j in range(ROW_TILE):
        ref[pl.ds(j, rows, stride=ROW_TILE), :] = val[:, j * LANES:(j + 1) * LANES]


def _load_row_tiles(ref, rows):
    return jnp.concatenate([ref[pl.ds(j, rows, stride=ROW_TILE), :] for j in range(ROW_TILE)], axis=-1)


def _dot_3pass(a, b):
    a_hi = a.astype(BF16)
    a_lo = (a - a_hi.astype(F32)).astype(BF16)
    b_hi = b.astype(BF16)
    b_lo = (b - b_hi.astype(F32)).astype(BF16)
    return _dot(a_hi, b_hi) + (_dot(a_lo, b_hi) + _dot(a_hi, b_lo))


def _norm_mod(x, g, sc, sh):
    ms = jnp.mean(x * x, axis=-1, keepdims=True)
    return (x * lax.rsqrt(ms + EPS) * g) * (1.0 + sc) + sh


def _ada_body(c_ref, w_ref, b_ref, o_ref):
    c = c_ref[...]
    ca = c * jax.nn.sigmoid(c)
    o_ref[...] = _dot(ca, w_ref[...], precision=HIGHEST) + b_ref[...]


def _ada(c, w, b, layer=0):
    bsz, d = c.shape
    n = w.shape[-1]
    tn = 1024
    w = w.reshape(-1, d, n)
    return pl.pallas_call(
        _ada_body,
        grid=(n // tn,),
        in_specs=[
            pl.BlockSpec((bsz, d), lambda j: (0, 0)),
            pl.BlockSpec((None, d, tn), lambda j: (layer, 0, j)),
            pl.BlockSpec((1, tn), lambda j: (0, j)),
        ],
        out_specs=pl.BlockSpec((bsz, tn), lambda j: (0, j)),
        out_shape=SDS((bsz, n), F32),
        compiler_params=_cparams(("arbitrary",)),
        name="ada",
    )(c, w, b.reshape(1, n))


SGU_TS = 512
SGU_CB = 768


def _gelu_tanh(x):
    c = np.float32(np.sqrt(2.0 / np.pi))
    return x * (0.5 * (1.0 + jnp.tanh(c * (x + 0.044715 * (x * x * x)))))


def _sgu_body(x_ref, mod_ref, n1g_ref, win_ref, bin_ref, vg_ref, ws_ref, bst_ref, wout_ref,
              o_ref, u_scr, v_scr, y_scr):
    ts = x_ref.shape[1]
    x = x_ref[0]
    mod = mod_ref[0]
    sh1, sc1, g1 = mod[0:1], mod[1:2], mod[2:3]
    h = _norm_mod(x, n1g_ref[...], sc1, sh1).astype(BF16)

    for j in range(D_SGU // SGU_CB):
        cs = slice(j * SGU_CB, (j + 1) * SGU_CB)
        u_scr[:, cs] = _gelu_tanh(_dot(h, win_ref[:, cs]) + bin_ref[:, cs])
    ssq = jnp.zeros((ts, 1), F32)
    for j in range(D_SGU // SGU_CB):
        cs = slice(j * SGU_CB, (j + 1) * SGU_CB)
        ws_cols = slice(D_SGU + j * SGU_CB, D_SGU + (j + 1) * SGU_CB)
        z = _gelu_tanh(_dot(h, win_ref[:, ws_cols]) + bin_ref[:, ws_cols])
        v_scr[:, cs] = z
        ssq = ssq + jnp.sum(z * z, axis=-1, keepdims=True)
    rs = lax.rsqrt(ssq / D_SGU + EPS)

    r_i = lax.broadcasted_iota(jnp.int32, (SGU_CHUNK, SGU_CHUNK), 0)
    c_i = lax.broadcasted_iota(jnp.int32, (SGU_CHUNK, SGU_CHUNK), 1)
    causal = c_i <= r_i
    for g in range(SGU_GROUPS):
        cs = slice(g * SGU_GROUP_DIM, (g + 1) * SGU_GROUP_DIM)
        wsg = jnp.where(causal, ws_ref[g], 0.0).astype(BF16)
        bsg = bst_ref[:, g:g + 1]
        for n in range(ts // SGU_CHUNK):
            rows = slice(n * SGU_CHUNK, (n + 1) * SGU_CHUNK)
            vn = (v_scr[rows, cs] * rs[rows] * vg_ref[:, cs]).astype(BF16)
            mixed = _dot(wsg, vn) + bsg
            y_scr[rows, cs] = (u_scr[rows, cs] * mixed).astype(BF16)

    m = _dot(y_scr[...], wout_ref[...])
    o_ref[0] = x + g1 * m


def _sgu(x, mod, n1g, w_in, b_in, v_g, w_s, b_s, w_out):
    bsz, s, d = x.shape
    ts = SGU_TS
    const = lambda *shape: pl.BlockSpec(shape, lambda b, i: (0,) * len(shape),
                                        pipeline_mode=pl.Buffered(1))
    return pl.pallas_call(
        _sgu_body,
        grid=(bsz, s // ts),
        in_specs=[
            pl.BlockSpec((1, ts, d), lambda b, i: (b, i, 0)),
            pl.BlockSpec((1, 6, d), lambda b, i: (b, 0, 0)),
            const(1, d),
            const(d, 2 * D_SGU),
            const(1, 2 * D_SGU),
            const(1, D_SGU),
            const(SGU_GROUPS, SGU_CHUNK, SGU_CHUNK),
            const(SGU_CHUNK, SGU_GROUPS),
            const(D_SGU, d),
        ],
        out_specs=pl.BlockSpec((1, ts, d), lambda b, i: (b, i, 0)),
        out_shape=SDS((bsz, s, d), F32),
        scratch_shapes=[
            pltpu.VMEM((ts, D_SGU), F32),
            pltpu.VMEM((ts, D_SGU), F32),
            pltpu.VMEM((ts, D_SGU), BF16),
        ],
        compiler_params=_cparams(("parallel", "arbitrary")),
        name="sgu",
    )(x, mod, n1g.reshape(1, d), w_in.astype(BF16), b_in.reshape(1, -1), v_g.reshape(1, -1),
      w_s, b_s.T, w_out.astype(BF16))


ROUTER_TS = 512


def _router_body(x_ref, mod_ref, n2g_ref, wr_ref, br_ref, h_ref, idx_ref, p_ref, rk_ref, cnt_ref, run_scr):
    @pl.when((pl.program_id(0) == 0) & (pl.program_id(1) == 0))
    def _():
        run_scr[...] = jnp.zeros(run_scr.shape, run_scr.dtype)

    x = x_ref[0]
    mod = mod_ref[0]
    sh2, sc2 = mod[3:4], mod[4:5]
    h2 = _norm_mod(x, n2g_ref[...], sc2, sh2)
    _store_row_tiles(h_ref, h2)
    logits = _dot_3pass(h2, wr_ref[...]) + br_ref[...]
    lane = lax.broadcasted_iota(jnp.int32, logits.shape, 1)
    work = logits
    vals, idxs = [], []
    for _ in range(TOP_K):
        mx = jnp.max(work, axis=-1, keepdims=True)
        ix = jnp.min(jnp.where(work == mx, lane, N_EXPERTS), axis=-1, keepdims=True)
        vals.append(mx)
        idxs.append(ix)
        work = jnp.where(lane == ix, -jnp.inf, work)
    tv = jnp.concatenate(vals, axis=-1)
    e = jnp.exp(tv - vals[0])
    p_ref[0] = e / jnp.sum(e, axis=-1, keepdims=True)
    idx_ref[0] = jnp.concatenate(idxs, axis=-1)

    ts = x.shape[0]
    onehots = [lane == ix for ix in idxs]
    mask = jnp.zeros(logits.shape, F32)
    for oh in onehots:
        mask = mask + jnp.where(oh, 1.0, 0.0)
    earlier = (lax.broadcasted_iota(jnp.int32, (ts, ts), 1) < lax.broadcasted_iota(jnp.int32, (ts, ts), 0))
    before = _dot(jnp.where(earlier, 1.0, 0.0).astype(BF16), mask.astype(BF16)) + run_scr[...]
    rk = [jnp.sum(jnp.where(oh, before, 0.0), axis=-1, keepdims=True) for oh in onehots]
    rk_ref[0] = jnp.concatenate(rk, axis=-1).astype(jnp.int32)
    run_scr[...] = run_scr[...] + jnp.sum(mask, axis=0, keepdims=True)
    cnt_ref[...] = run_scr[...].astype(jnp.int32)


def _router(x, mod, n2g, w_r, b_r):
    bsz, s, d = x.shape
    ts = min(ROUTER_TS, s)
    tok = lambda w: pl.BlockSpec((1, ts, w), lambda b, i: (b, i, 0))
    return pl.pallas_call(
        _router_body,
        grid=(bsz, s // ts),
        in_specs=[
            tok(d),
            pl.BlockSpec((1, 6, d), lambda b, i: (b, 0, 0)),
            pl.BlockSpec((1, d), lambda b, i: (0, 0)),
            pl.BlockSpec((d, N_EXPERTS), lambda b, i: (0, 0)),
            pl.BlockSpec((1, N_EXPERTS), lambda b, i: (0, 0)),
        ],
        out_specs=[pl.BlockSpec((ts * ROW_TILE, LANES), lambda b, i: (b * (s // ts) + i, 0)),
                   tok(TOP_K), tok(TOP_K), tok(TOP_K),
                   pl.BlockSpec((1, N_EXPERTS), lambda b, i: (0, 0))],
        out_shape=[SDS((bsz * s * ROW_TILE, LANES), F32), SDS((bsz, s, TOP_K), jnp.int32),
                   SDS((bsz, s, TOP_K), F32),
                   SDS((bsz, s, TOP_K), jnp.int32), SDS((1, N_EXPERTS), jnp.int32)],
        scratch_shapes=[pltpu.VMEM((1, N_EXPERTS), F32)],
        compiler_params=_cparams(("arbitrary", "arbitrary")),
        name="router",
    )(x, mod, n2g.reshape(1, d), w_r, b_r.reshape(1, -1))


def _route_meta(top_idx, rank, counts):
    n = top_idx.shape[0]
    padded = (counts + EXPERT_BLOCK - 1) // EXPERT_BLOCK * EXPERT_BLOCK
    pend = jnp.cumsum(padded)
    pstart = pend - padded
    dest = (pstart[top_idx] + rank).astype(jnp.int32)
    n_rows = n * TOP_K + N_EXPERTS * EXPERT_BLOCK
    n_blk = n_rows // EXPERT_BLOCK
    blk_row0 = jnp.arange(n_blk) * EXPERT_BLOCK
    blk_exp = jnp.minimum(jnp.sum(pend[None, :] <= blk_row0[:, None], axis=1), N_EXPERTS - 1)
    n_used = (pend[-1] // EXPERT_BLOCK).astype(jnp.int32).reshape(1)
    eids = jnp.arange(N_EXPERTS)
    has_rows = counts > 0
    last_exp = jnp.max(jnp.where(has_rows, eids, 0))
    blk_exp = jnp.where(blk_row0 < pend[-1], blk_exp, last_exp).astype(jnp.int32)
    nxt_e = jnp.min(jnp.where((eids[None, :] > eids[:, None]) & has_rows[None, :], eids[None, :], N_EXPERTS), axis=1)
    nxt_e = jnp.where(nxt_e < N_EXPERTS, nxt_e, -1)
    slot_e = (jnp.cumsum(has_rows.astype(jnp.int32)) - 1) % 2
    blk_nxt = nxt_e[blk_exp].astype(jnp.int32)
    blk_slot = slot_e[blk_exp].astype(jnp.int32)
    pad_blk = jnp.where(padded > 0, pend - EXPERT_BLOCK, -1)
    tail = pend[-1] + jnp.arange(N_EXPERTS) * EXPERT_BLOCK
    tail_blk = jnp.where(tail < n_rows, tail, -1)
    pad_blk = jnp.concatenate([pad_blk, tail_blk]).astype(jnp.int32)
    return dest, (blk_exp, n_used, blk_nxt, blk_slot), pad_blk, n_rows


MOVE_TT = 1024
COMBINE_TT = 1024


def _dispatch_body(pad_ref, dest_ref, h_ref, xs_ref, zero_scr, sem, zsem):
    tt = h_ref.shape[0] // ROW_TILE
    blk = EXPERT_BLOCK * ROW_TILE

    @pl.when(pl.program_id(0) == 0)
    def _():
        zero_scr[...] = jnp.zeros(zero_scr.shape, zero_scr.dtype)

        def zero_copy(e):
            row0 = pl.multiple_of(pad_ref[e], blk)
            return pltpu.make_async_copy(zero_scr, xs_ref.at[pl.ds(row0, blk)], zsem)

        for e in range(2 * N_EXPERTS):
            pl.when(pad_ref[e] >= 0)(lambda e=e: zero_copy(e).start())
        for e in range(2 * N_EXPERTS):
            pl.when(pad_ref[e] >= 0)(lambda e=e: zero_copy(e).wait())

    def issue(t, carry):
        src = h_ref.at[pl.ds(pl.multiple_of(t * ROW_TILE, ROW_TILE), ROW_TILE)]
        for k in range(TOP_K):
            d = pl.multiple_of(dest_ref[0, t * TOP_K + k], ROW_TILE)
            pltpu.make_async_copy(src, xs_ref.at[pl.ds(d, ROW_TILE)], sem).start(priority=k % 2)
        return carry

    lax.fori_loop(0, tt, issue, 0)
    for k in range(TOP_K):
        pltpu.make_async_copy(h_ref, h_ref, sem).wait()


def _dispatch(h, dest8, pad_blk, n_rows):
    n = h.shape[0] // ROW_TILE
    tt = MOVE_TT
    dest3 = dest8.reshape(n // tt, 1, tt * TOP_K)
    return pl.pallas_call(
        _dispatch_body,
        grid_spec=pltpu.PrefetchScalarGridSpec(
            num_scalar_prefetch=1,
            grid=(n // tt,),
            in_specs=[
                pl.BlockSpec((None, 1, tt * TOP_K), lambda i, pad: (i, 0, 0), memory_space=pltpu.SMEM),
                pl.BlockSpec((tt * ROW_TILE, LANES), lambda i, pad: (i, 0)),
            ],
            out_specs=pl.BlockSpec(memory_space=pl.ANY),
            scratch_shapes=[pltpu.VMEM((EXPERT_BLOCK * ROW_TILE, LANES), h.dtype), pltpu.SemaphoreType.DMA(()),
                            pltpu.SemaphoreType.DMA(())],
        ),
        out_shape=SDS((n_rows * ROW_TILE, LANES), h.dtype),
        compiler_params=_cparams(("arbitrary",)),
        name="dispatch",
    )(pad_blk * ROW_TILE, dest3, h)


def _ffn_body(layer, be_ref, nu_ref, nx_ref, sl_ref, xs_ref, wg_hbm, bg_ref, wu_hbm, bu_ref, wd_hbm, bd_ref,
              y_ref, w_f32, w_bf, sem):
    i = pl.program_id(0)
    slot = sl_ref[i]

    def fetch(e, s):
        return [pltpu.make_async_copy(w.at[layer, e], w_f32.at[s, n], sem.at[s])
                for n, w in enumerate((wg_hbm, wu_hbm, wd_hbm))]

    @pl.when(i == 0)
    def _():
        for c in fetch(be_ref[0], slot):
            c.start()

    @pl.when((i == 0) | (be_ref[i] != be_ref[jnp.maximum(i - 1, 0)]))
    def _():
        for c in fetch(be_ref[i], slot):
            c.wait()

        @pl.when(nx_ref[i] >= 0)
        def _():
            for c in fetch(nx_ref[i], 1 - slot):
                c.start()

        for n in range(3):
            w_bf[n] = w_f32[slot, n].astype(BF16)

    @pl.when(i < nu_ref[0])
    def _():
        x = _load_row_tiles(xs_ref, EXPERT_BLOCK).astype(BF16)
        g = jnp.minimum(_dot(x, w_bf[0]) + bg_ref[...], SWIGLU_LIMIT)
        u = jnp.clip(_dot(x, w_bf[1]) + bu_ref[...], -SWIGLU_LIMIT, SWIGLU_LIMIT)
        a = g * jax.nn.sigmoid(SWIGLU_ALPHA * g) * (u + 1.0)
        _store_row_tiles(y_ref, _dot(a.astype(BF16), w_bf[2]) + bd_ref[...])

    @pl.when(i >= nu_ref[0])
    def _():
        y_ref[...] = jnp.zeros(y_ref.shape, y_ref.dtype)


def _ffn(xs, blk_meta, layer, w_gate, b_gate, w_up, b_up, w_down, b_down):
    n_rows = xs.shape[0] // ROW_TILE
    _, e, d, de = w_gate.shape
    assert d == de == D_MODEL
    n_blk = n_rows // EXPERT_BLOCK
    blk = (EXPERT_BLOCK * ROW_TILE, LANES)
    bspec = lambda c: pl.BlockSpec((None, None, 1, c), lambda i, be, nu, nx, sl: (layer, be[i], 0, 0))
    hbm = pl.BlockSpec(memory_space=pl.ANY)
    xs_map = lambda i, be, nu, nx, sl: (jnp.minimum(i, jnp.maximum(nu[0] - 1, 0)), 0)
    return pl.pallas_call(
        functools.partial(_ffn_body, layer),
        grid_spec=pltpu.PrefetchScalarGridSpec(
            num_scalar_prefetch=4,
            grid=(n_blk,),
            in_specs=[
                pl.BlockSpec(blk, xs_map),
                hbm, bspec(de), hbm, bspec(de), hbm, bspec(d),
            ],
            out_specs=pl.BlockSpec(blk, lambda i, be, nu, nx, sl: (i, 0)),
            scratch_shapes=[pltpu.VMEM((2, 3, d, de), F32), pltpu.VMEM((3, d, de), BF16),
                            pltpu.SemaphoreType.DMA((2,))],
        ),
        out_shape=SDS((n_rows * ROW_TILE, LANES), F32),
        compiler_params=_cparams(("arbitrary",)),
        name="ffn",
    )(*blk_meta, xs, w_gate, b_gate.reshape(-1, e, 1, de), w_up, b_up.reshape(-1, e, 1, de),
      w_down, b_down.reshape(-1, e, 1, d))


COMBINE_RC = 128


def _combine_body(dest_ref, x_ref, mod_ref, p_ref, y_ref, o_ref, buf, sem):
    tt = x_ref.shape[0]

    def issue(t, carry):
        t8 = pl.multiple_of(t * ROW_TILE, ROW_TILE)
        for k in range(TOP_K):
            d = pl.multiple_of(dest_ref[0, t * TOP_K + k], ROW_TILE)
            pltpu.make_async_copy(y_ref.at[pl.ds(d, ROW_TILE)], buf.at[k, pl.ds(t8, ROW_TILE)],
                                  sem).start(priority=k % 2)
        return carry

    lax.fori_loop(0, tt, issue, 0)
    pltpu.make_async_copy(buf, buf, sem).wait()

    g2 = mod_ref[0][5:6]
    for r0 in range(0, tt, COMBINE_RC):
        rows = slice(r0, r0 + COMBINE_RC)
        p = p_ref[rows, :]
        for j in range(ROW_TILE):
            cols = slice(j * LANES, (j + 1) * LANES)
            tile = lambda k: buf[k, pl.ds(r0 * ROW_TILE + j, COMBINE_RC, stride=ROW_TILE), :]
            acc = p[:, 0:1] * tile(0)
            for k in range(1, TOP_K):
                acc = acc + p[:, k:k + 1] * tile(k)
            o_ref[rows, cols] = x_ref[rows, cols] + g2[:, cols] * acc


def _combine(x, mod, p, y, dest8, seq):
    n, d = x.shape
    tt = COMBINE_TT
    n_steps = n // tt
    dest3 = dest8.reshape(n_steps, 1, tt * TOP_K)
    per_b = seq // tt
    return pl.pallas_call(
        _combine_body,
        grid=(n_steps,),
        in_specs=[
            pl.BlockSpec((None, 1, tt * TOP_K), lambda i: (i, 0, 0), memory_space=pltpu.SMEM),
            pl.BlockSpec((tt, d), lambda i: (i, 0)),
            pl.BlockSpec((1, 6, d), lambda i: (i // per_b, 0, 0)),
            pl.BlockSpec((tt, TOP_K), lambda i: (i, 0)),
            pl.BlockSpec(memory_space=pl.ANY),
        ],
        out_specs=pl.BlockSpec((tt, d), lambda i: (i, 0)),
        out_shape=SDS((n, d), F32),
        scratch_shapes=[pltpu.VMEM((TOP_K, tt * ROW_TILE, LANES), F32), pltpu.SemaphoreType.DMA(())],
        compiler_params=_cparams(("arbitrary",)),
        name="combine",
    )(dest3, x, mod, p, y)


def _moe(x, mod, layer, n2g, w_r, b_r, w_gate, b_gate, w_up, b_up, w_down, b_down):
    bsz, s, d = x.shape
    n = bsz * s
    h2, top_idx, probs, rank, counts = _router(x, mod, n2g, w_r, b_r)
    dest, blk_meta, pad_blk, n_rows = _route_meta(
        top_idx.reshape(n, TOP_K), rank.reshape(n, TOP_K), counts.reshape(N_EXPERTS))
    dest8 = dest * ROW_TILE
    xs = _dispatch(h2, dest8, pad_blk, n_rows)
    y = _ffn(xs, blk_meta, layer, w_gate, b_gate, w_up, b_up, w_down, b_down)
    out = _combine(x.reshape(n, d), mod, probs.reshape(n, TOP_K), y, dest8, s)
    return out.reshape(bsz, s, d)


def _seg_mats():
    r = lax.broadcasted_iota(jnp.int32, (D_MODEL, N_HEADS), 0) // HEAD_DIM
    c = lax.broadcasted_iota(jnp.int32, (D_MODEL, N_HEADS), 1)
    seg = jnp.where(r == c, 1.0, 0.0).astype(BF16)
    rt = lax.broadcasted_iota(jnp.int32, (N_HEADS, D_MODEL), 0)
    ct = lax.broadcasted_iota(jnp.int32, (N_HEADS, D_MODEL), 1) // HEAD_DIM
    seg_t = jnp.where(rt == ct, 1.0, 0.0).astype(BF16)
    return seg, seg_t


def _split_dot(a, b):
    hi = a.astype(BF16)
    lo = (a - hi.astype(F32)).astype(BF16)
    return _dot(hi, b) + _dot(lo, b)


def _head_rmsnorm(x, g, seg, seg_t):
    ms = _split_dot(x * x, seg) * (1.0 / HEAD_DIM)
    r = lax.rsqrt(ms + EPS)
    return x * _split_dot(r, seg_t) * g


def _kvq_body(x_ref, mod_ref, kvmod_ref, n1g_ref, kvg_ref, wq_ref, wkv_ref, qg_ref, kg_ref,
              q_ref, k_ref, vt_ref, km_ref):
    x = x_ref[0]
    mod = mod_ref[0]
    kvmod = kvmod_ref[0]
    ms = jnp.mean(x * x, axis=-1, keepdims=True)
    xn = x * lax.rsqrt(ms + EPS)
    hq = ((xn * n1g_ref[...]) * (1.0 + mod[1:2]) + mod[0:1]).astype(BF16)
    hk = ((xn * kvg_ref[...]) * (1.0 + kvmod[1:2]) + kvmod[0:1]).astype(BF16)
    seg, seg_t = _seg_mats()
    q = _head_rmsnorm(_dot(hq, wq_ref[...]), qg_ref[...], seg, seg_t)
    q_ref[0] = q
    kv = _dot(hk, wkv_ref[...])
    k = _head_rmsnorm(kv[:, :D_MODEL], kg_ref[...], seg, seg_t)
    k_ref[0, 0] = k.astype(BF16)
    for u in range(ATT_U):
        rows = slice(u * MOBA_BLOCK, (u + 1) * MOBA_BLOCK)
        km_ref[0, 0, u:u + 1, :] = jnp.mean(k[rows], axis=0, keepdims=True)
    vt_ref[0, 0] = kv[:, D_MODEL:].T.astype(BF16)


def _kvq(x, mod, kvmod, n1g, kvg, w_q, w_kv, q_g, k_g):
    bsz, s, d = x.shape
    ts = ATT_U * MOBA_BLOCK
    nb = s // ts
    const = lambda *shape: pl.BlockSpec(shape, lambda b, i: (0,) * len(shape))
    return pl.pallas_call(
        _kvq_body,
        grid=(bsz, nb),
        in_specs=[
            pl.BlockSpec((1, ts, d), lambda b, i: (b, i, 0)),
            pl.BlockSpec((1, 6, d), lambda b, i: (b, 0, 0)),
            pl.BlockSpec((1, 2, d), lambda b, i: (b, 0, 0)),
            const(1, d), const(1, d), const(d, d), const(d, 2 * d), const(1, d), const(1, d),
        ],
        out_specs=[
            pl.BlockSpec((1, ts, d), lambda b, i: (b, i, 0)),
            pl.BlockSpec((1, 1, ts, d), lambda b, i: (b, i, 0, 0)),
            pl.BlockSpec((1, 1, d, ts), lambda b, i: (b, i, 0, 0)),
            pl.BlockSpec((1, 1, ATT_U, d), lambda b, i: (b, i, 0, 0)),
        ],
        out_shape=[
            SDS((bsz, s, d), F32),
            SDS((bsz, nb, ts, d), BF16),
            SDS((bsz, nb, d, ts), BF16),
            SDS((bsz, nb, ATT_U, d), F32),
        ],
        compiler_params=_cparams(("parallel", "arbitrary")),
        name="kvq",
    )(x, mod, kvmod, n1g.reshape(1, d), kvg.reshape(1, d), w_q.astype(BF16), w_kv.astype(BF16),
      jnp.tile(q_g, N_HEADS).reshape(1, d), jnp.tile(k_g, N_HEADS).reshape(1, d))


def _attn_body(slopes_ref, q_ref, k_ref, vt_ref, km_ref, o_ref, bias_scr, s_scr):
    hp = pl.program_id(1)
    own = pl.program_id(2)
    bs = MOBA_BLOCK
    nb = k_ref.shape[1] * ATT_U

    @pl.when(own == 0)
    def _():
        rel = (lax.broadcasted_iota(jnp.int32, (bs, bs), 1)
               - lax.broadcasted_iota(jnp.int32, (bs, bs), 0))
        relf = rel.astype(F32)
        for hd in range(2):
            a = (slopes_ref[2 * hp + hd] * LOG2E) * relf
            bias_scr[hd, 0] = a
            bias_scr[hd, 1] = jnp.where(rel >= 0, a, jnp.inf)

    q = q_ref[0]
    km = km_ref[0]
    dcol = lax.broadcasted_iota(jnp.int32, (1, 2 * HEAD_DIM), 1)
    blk_row = lax.broadcasted_iota(jnp.int32, (nb, 1), 0)
    valid = blk_row < own

    in_head = [(dcol >= hd * HEAD_DIM) & (dcol < (hd + 1) * HEAD_DIM) for hd in range(2)]
    km2 = jnp.concatenate([jnp.where(in_head[hd], km, 0.0) for hd in range(2)], axis=0)
    gates = lax.dot_general(km2, q, NT_DIMS, precision=HIGHEST, preferred_element_type=F32)

    qs, selbias = [], []
    for hd in range(2):
        slope = slopes_ref[2 * hp + hd]
        qm = jnp.where(in_head[hd], q, 0.0)
        gate = jnp.where(valid, gates[hd * nb:(hd + 1) * nb], -jnp.inf)
        rank = jnp.zeros(gate.shape, F32)
        for i in range(nb):
            gi = gate[i:i + 1, :]
            tie = jnp.where(blk_row > i, 1.0, 0.0)
            rank = rank + jnp.where(gi > gate, 1.0, jnp.where(gi == gate, tie, 0.0))
        chosen = jnp.where(valid, jnp.where(rank < MOBA_TOPK, 1.0, 0.0),
                           jnp.where(blk_row == own, 1.0, 0.0))
        off = (own - blk_row).astype(F32) * (bs * LOG2E * slope)
        selbias.append(jnp.where(chosen > 0.5, -off, -jnp.inf))
        qs.append((qm * (LOG2E * HEAD_DIM ** -0.5)).astype(BF16))

    last = k_ref.shape[1] - 1
    ones = jnp.ones((ATT_ONES, ATT_U * bs), BF16)

    def qk(it, hd):
        return lax.dot_general(k_ref[0, it], qs[hd], NT_DIMS, preferred_element_type=F32)

    def visit(it, slot, carry, prefetch=True):
        out = []
        for hd in range(2):
            m, acc = carry[hd]
            if prefetch:
                s_scr[1 - slot, hd] = qk(jnp.minimum(it + 1, last), hd)
            rows, biases = [], []
            mx = m
            for u in range(ATT_U):
                j = it * ATT_U + u
                rows.append(jnp.sum(jnp.where(blk_row == j, selbias[hd], 0.0), axis=0, keepdims=True))
                biases.append((j == own).astype(jnp.int32))
                t = (s_scr[slot, hd, u * bs:(u + 1) * bs] - bias_scr[hd, biases[u]]) + rows[u]
                mx = jnp.maximum(mx, jnp.max(t, axis=0, keepdims=True))
            m_safe = jnp.where(mx == -jnp.inf, 0.0, mx)
            alpha = jnp.exp2(m - m_safe)
            ps = []
            for u in range(ATT_U):
                x = s_scr[slot, hd, u * bs:(u + 1) * bs] - (bias_scr[hd, biases[u]] - (rows[u] - m_safe))
                ps.append(jnp.exp2(x.astype(BF16)))
            p = jnp.concatenate(ps, axis=0)
            vj = jnp.concatenate([vt_ref[0, it, hd * HEAD_DIM:(hd + 1) * HEAD_DIM, :], ones], axis=0)
            acc = alpha * acc + _dot(vj, p)
            out.append((mx, acc))
        return tuple(out)

    def body(i2, carry):
        carry = visit(2 * i2, 0, carry)
        return visit(2 * i2 + 1, 1, carry)

    for hd in range(2):
        s_scr[0, hd] = qk(0, hd)
    init = tuple((jnp.full((1, bs), -jnp.inf, F32), jnp.zeros((HEAD_DIM + ATT_ONES, bs), F32))
                 for hd in range(2))
    n_visits = own // ATT_U + 1
    carry = lax.fori_loop(0, n_visits // 2, body, init)
    carry = lax.cond(n_visits % 2 == 1, lambda c: visit(n_visits - 1, 0, c, prefetch=False), lambda c: c, carry)
    outs = [acc[:HEAD_DIM] / acc[HEAD_DIM:HEAD_DIM + 1] for (_, acc) in carry]
    o_ref[0] = jnp.concatenate(outs, axis=0).astype(o_ref.dtype)


def _attn(q, kb, vtb, kmean):
    bsz, s, d = q.shape
    nsb = s // (ATT_U * MOBA_BLOCK)
    assert nsb % 2 == 0, "the attention loop visits key super-blocks in pairs"
    nb = s // MOBA_BLOCK
    w = 2 * HEAD_DIM
    slopes = (2.0 ** (-8.0 * jnp.arange(1, N_HEADS + 1, dtype=F32) / N_HEADS)).astype(F32)
    return pl.pallas_call(
        _attn_body,
        grid_spec=pltpu.PrefetchScalarGridSpec(
            num_scalar_prefetch=1,
            grid=(bsz, N_HEADS // 2, nb),
            in_specs=[
                pl.BlockSpec((1, MOBA_BLOCK, w), lambda b, h, c, sl: (b, c, h)),
                pl.BlockSpec((1, nsb, ATT_U * MOBA_BLOCK, w), lambda b, h, c, sl: (b, 0, 0, h)),
                pl.BlockSpec((1, nsb, w, ATT_U * MOBA_BLOCK), lambda b, h, c, sl: (b, 0, h, 0)),
                pl.BlockSpec((1, nb, w), lambda b, h, c, sl: (b, 0, h)),
            ],
            out_specs=pl.BlockSpec((1, w, MOBA_BLOCK), lambda b, h, c, sl: (b, h, c)),
            scratch_shapes=[pltpu.VMEM((2, 2, MOBA_BLOCK, MOBA_BLOCK), F32),
                            pltpu.VMEM((2, 2, ATT_U * MOBA_BLOCK, MOBA_BLOCK), F32)],
        ),
        out_shape=SDS((bsz, d, s), BF16),
        compiler_params=_cparams(("parallel", "parallel", "arbitrary")),
        name="attn",
    )(slopes, q, kb, vtb, kmean)


OPROJ_TS = 512


def _oproj_body(at_ref, x_ref, mod_ref, wo_ref, o_ref):
    g1 = mod_ref[0][2:3]
    m = lax.dot_general(at_ref[0], wo_ref[...], (((0,), (0,)), ((), ())), preferred_element_type=F32)
    o_ref[0] = x_ref[0] + g1 * m


def _oproj(at, x, mod, w_o):
    bsz, s, d = x.shape
    ts = min(OPROJ_TS, s)
    tok = pl.BlockSpec((1, ts, d), lambda b, i: (b, i, 0))
    return pl.pallas_call(
        _oproj_body,
        grid=(bsz, s // ts),
        in_specs=[pl.BlockSpec((1, d, ts), lambda b, i: (b, 0, i)), tok,
                  pl.BlockSpec((1, 6, d), lambda b, i: (b, 0, 0)),
                  pl.BlockSpec((d, d), lambda b, i: (0, 0))],
        out_specs=tok,
        out_shape=SDS((bsz, s, d), F32),
        compiler_params=_cparams(("parallel", "arbitrary")),
        name="oproj",
    )(at, x, mod, w_o.astype(BF16))


def kernel(x, c, ada_w, ada_b, norm1_g, norm2_g, sgu_w_in, sgu_b_in, sgu_v_g, sgu_w_s, sgu_b_s, sgu_w_out,
           kv_ada_w, kv_ada_b, kv_norm_g, w_kv, k_norm_g, attn_w_q, q_norm_g, attn_w_o,
           moe_w_router, moe_b_router, moe_w_gate, moe_b_gate, moe_w_up, moe_b_up, moe_w_down, moe_b_down):
    bsz, s, d = x.shape
    moe = lambda l, xx, mod: _moe(xx, mod, l, norm2_g[l], moe_w_router[l], moe_b_router[l],
                                  moe_w_gate, moe_b_gate, moe_w_up, moe_b_up, moe_w_down, moe_b_down)
    mod0 = _ada(c, ada_w, ada_b[0], 0).reshape(bsz, 6, d)
    x = _sgu(x, mod0, norm1_g[0], sgu_w_in[0], sgu_b_in[0], sgu_v_g[0], sgu_w_s[0], sgu_b_s[0], sgu_w_out[0])
    x = moe(0, x, mod0)
    mod1 = _ada(c, ada_w, ada_b[1], 1).reshape(bsz, 6, d)
    kvmod = _ada(c, kv_ada_w, kv_ada_b).reshape(bsz, 2, d)
    q, kb, vtb, kmean = _kvq(x, mod1, kvmod, norm1_g[1], kv_norm_g, attn_w_q[0], w_kv, q_norm_g[0], k_norm_g)
    a = _attn(q, kb, vtb, kmean.reshape(bsz, s // MOBA_BLOCK, d))
    x = _oproj(a, x, mod1, attn_w_o[0])
    x = moe(1, x, mod1)
    return x
```

```python
import functools
import math

import jax
import jax.numpy as jnp
import numpy as np
from jax import lax
from jax.experimental import pallas as pl
from jax.experimental.pallas import tpu as pltpu

F32 = jnp.float32
BF16 = jnp.bfloat16
HIGHEST = lax.Precision.HIGHEST
SDS = jax.ShapeDtypeStruct

D_MODEL = 1024
SGU_CHUNK = 128
SGU_GROUPS = 8
D_SGU = 3 * D_MODEL
SGU_GROUP_DIM = D_SGU // SGU_GROUPS
HEAD_DIM = 64
N_HEADS = D_MODEL // HEAD_DIM
MOBA_BLOCK = 256
MOBA_TOPK = 3
ATT_U = 2
ATT_ONES = 16
ATT_AUG = 128
LOG2E = math.log2(math.e)
N_EXPERTS = 32
TOP_K = 4
SWIGLU_LIMIT = 7.0
SWIGLU_ALPHA = 1.702
EXPERT_BLOCK = 512
EPS = 1e-6

VMEM_LIMIT_BYTES = 56 * 1024 * 1024
NT_DIMS = (((1,), (1,)), ((), ()))


def _cparams(sem):
    return pltpu.CompilerParams(dimension_semantics=sem, vmem_limit_bytes=VMEM_LIMIT_BYTES)


def _dot(a, b, **kw):
    return jnp.dot(a, b, preferred_element_type=F32, **kw)


LANES = 128
ROW_TILE = D_MODEL // LANES


def _store_row_tiles(ref, val):
    rows = val.shape[0]
    for j in range(ROW_TILE):
        ref[pl.ds(j, rows, stride=ROW_TILE), :] = val[:, j * LANES:(j + 1) * LANES]


def _load_row_tiles(ref, rows):
    return jnp.concatenate([ref[pl.ds(j, rows, stride=ROW_TILE), :] for j in range(ROW_TILE)], axis=-1)


def _dot_3pass(a, b):
    a_hi = a.astype(BF16)
    a_lo = (a - a_hi.astype(F32)).astype(BF16)
    b_hi = b.astype(BF16)
    b_lo = (b - b_hi.astype(F32)).astype(BF16)
    return _dot(a_hi, b_hi) + (_dot(a_lo, b_hi) + _dot(a_hi, b_lo))


def _norm_mod(x, g, sc, sh):
    ms = jnp.mean(x * x, axis=-1, keepdims=True)
    return (x * lax.rsqrt(ms + EPS) * g) * (1.0 + sc) + sh


def _ada_body(c_ref, w_ref, b_ref, o_ref):
    c = c_ref[...]
    ca = c * jax.nn.sigmoid(c)
    o_ref[...] = _dot(ca, w_ref[...], precision=HIGHEST) + b_ref[...]


def _ada(c, w, b, layer=0):
    bsz, d = c.shape
    n = w.shape[-1]
    tn = 1024
    w = w.reshape(-1, d, n)
    return pl.pallas_call(
        _ada_body,
        grid=(n // tn,),
        in_specs=[
            pl.BlockSpec((bsz, d), lambda j: (0, 0)),
            pl.BlockSpec((None, d, tn), lambda j: (layer, 0, j)),
            pl.BlockSpec((1, tn), lambda j: (0, j)),
        ],
        out_specs=pl.BlockSpec((bsz, tn), lambda j: (0, j)),
        out_shape=SDS((bsz, n), F32),
        compiler_params=_cparams(("arbitrary",)),
        name="ada",
    )(c, w, b.reshape(1, n))


SGU_TS = 512
SGU_CB = 768


def _gelu_tanh(x):
    c = np.float32(np.sqrt(2.0 / np.pi))
    return x * (0.5 * (1.0 + jnp.tanh(c * (x + 0.044715 * (x * x * x)))))


def _sgu_body(x_ref, mod_ref, n1g_ref, win_ref, bin_ref, vg_ref, ws_ref, bst_ref, wout_ref,
              o_ref, u_scr, v_scr, y_scr):
    ts = x_ref.shape[1]
    x = x_ref[0]
    mod = mod_ref[0]
    sh1, sc1, g1 = mod[0:1], mod[1:2], mod[2:3]
    h = _norm_mod(x, n1g_ref[...], sc1, sh1).astype(BF16)

    for j in range(D_SGU // SGU_CB):
        cs = slice(j * SGU_CB, (j + 1) * SGU_CB)
        u_scr[:, cs] = _gelu_tanh(_dot(h, win_ref[:, cs]) + bin_ref[:, cs])
    ssq = jnp.zeros((ts, 1), F32)
    for j in range(D_SGU // SGU_CB):
        cs = slice(j * SGU_CB, (j + 1) * SGU_CB)
        ws_cols = slice(D_SGU + j * SGU_CB, D_SGU + (j + 1) * SGU_CB)
        z = _gelu_tanh(_dot(h, win_ref[:, ws_cols]) + bin_ref[:, ws_cols])
        v_scr[:, cs] = z
        ssq = ssq + jnp.sum(z * z, axis=-1, keepdims=True)
    rs = lax.rsqrt(ssq / D_SGU + EPS)

    r_i = lax.broadcasted_iota(jnp.int32, (SGU_CHUNK, SGU_CHUNK), 0)
    c_i = lax.broadcasted_iota(jnp.int32, (SGU_CHUNK, SGU_CHUNK), 1)
    causal = c_i <= r_i
    for g in range(SGU_GROUPS):
        cs = slice(g * SGU_GROUP_DIM, (g + 1) * SGU_GROUP_DIM)
        wsg = jnp.where(causal, ws_ref[g], 0.0).astype(BF16)
        bsg = bst_ref[:, g:g + 1]
        for n in range(ts // SGU_CHUNK):
            rows = slice(n * SGU_CHUNK, (n + 1) * SGU_CHUNK)
            vn = (v_scr[rows, cs] * rs[rows] * vg_ref[:, cs]).astype(BF16)
            mixed = _dot(wsg, vn) + bsg
            y_scr[rows, cs] = (u_scr[rows, cs] * mixed).astype(BF16)

    m = _dot(y_scr[...], wout_ref[...])
    o_ref[0] = x + g1 * m


def _sgu(x, mod, n1g, w_in, b_in, v_g, w_s, b_s, w_out):
    bsz, s, d = x.shape
    ts = SGU_TS
    const = lambda *shape: pl.BlockSpec(shape, lambda b, i: (0,) * len(shape),
                                        pipeline_mode=pl.Buffered(1))
    return pl.pallas_call(
        _sgu_body,
        grid=(bsz, s // ts),
        in_specs=[
            pl.BlockSpec((1, ts, d), lambda b, i: (b, i, 0)),
            pl.BlockSpec((1, 6, d), lambda b, i: (b, 0, 0)),
            const(1, d),
            const(d, 2 * D_SGU),
            const(1, 2 * D_SGU),
            const(1, D_SGU),
            const(SGU_GROUPS, SGU_CHUNK, SGU_CHUNK),
            const(SGU_CHUNK, SGU_GROUPS),
            const(D_SGU, d),
        ],
        out_specs=pl.BlockSpec((1, ts, d), lambda b, i: (b, i, 0)),
        out_shape=SDS((bsz, s, d), F32),
        scratch_shapes=[
            pltpu.VMEM((ts, D_SGU), F32),
            pltpu.VMEM((ts, D_SGU), F32),
            pltpu.VMEM((ts, D_SGU), BF16),
        ],
        compiler_params=_cparams(("parallel", "arbitrary")),
        name="sgu",
    )(x, mod, n1g.reshape(1, d), w_in.astype(BF16), b_in.reshape(1, -1), v_g.reshape(1, -1),
      w_s, b_s.T, w_out.astype(BF16))


ROUTER_TS = 512


def _router_body(x_ref, mod_ref, n2g_ref, wr_ref, br_ref, h_ref, idx_ref, p_ref, rk_ref, cnt_ref, run_scr):
    @pl.when((pl.program_id(0) == 0) & (pl.program_id(1) == 0))
    def _():
        run_scr[...] = jnp.zeros(run_scr.shape, run_scr.dtype)

    x = x_ref[0]
    mod = mod_ref[0]
    sh2, sc2 = mod[3:4], mod[4:5]
    h2 = _norm_mod(x, n2g_ref[...], sc2, sh2)
    _store_row_tiles(h_ref, h2)
    logits = _dot_3pass(h2, wr_ref[...]) + br_ref[...]
    lane = lax.broadcasted_iota(jnp.int32, logits.shape, 1)
    work = logits
    vals, idxs = [], []
    for _ in range(TOP_K):
        mx = jnp.max(work, axis=-1, keepdims=True)
        ix = jnp.min(jnp.where(work == mx, lane, N_EXPERTS), axis=-1, keepdims=True)
        vals.append(mx)
        idxs.append(ix)
        work = jnp.where(lane == ix, -jnp.inf, work)
    tv = jnp.concatenate(vals, axis=-1)
    e = jnp.exp(tv - vals[0])
    p_ref[0] = e / jnp.sum(e, axis=-1, keepdims=True)
    idx_ref[0] = jnp.concatenate(idxs, axis=-1)

    ts = x.shape[0]
    onehots = [lane == ix for ix in idxs]
    mask = jnp.zeros(logits.shape, F32)
    for oh in onehots:
        mask = mask + jnp.where(oh, 1.0, 0.0)
    earlier = (lax.broadcasted_iota(jnp.int32, (ts, ts), 1) < lax.broadcasted_iota(jnp.int32, (ts, ts), 0))
    before = _dot(jnp.where(earlier, 1.0, 0.0).astype(BF16), mask.astype(BF16)) + run_scr[...]
    rk = [jnp.sum(jnp.where(oh, before, 0.0), axis=-1, keepdims=True) for oh in onehots]
    rk_ref[0] = jnp.concatenate(rk, axis=-1).astype(jnp.int32)
    run_scr[...] = run_scr[...] + jnp.sum(mask, axis=0, keepdims=True)
    cnt_ref[...] = run_scr[...].astype(jnp.int32)


def _router(x, mod, n2g, w_r, b_r):
    bsz, s, d = x.shape
    ts = min(ROUTER_TS, s)
    tok = lambda w: pl.BlockSpec((1, ts, w), lambda b, i: (b, i, 0))
    return pl.pallas_call(
        _router_body,
        grid=(bsz, s // ts),
        in_specs=[
            tok(d),
            pl.BlockSpec((1, 6, d), lambda b, i: (b, 0, 0)),
            pl.BlockSpec((1, d), lambda b, i: (0, 0)),
            pl.BlockSpec((d, N_EXPERTS), lambda b, i: (0, 0)),
            pl.BlockSpec((1, N_EXPERTS), lambda b, i: (0, 0)),
        ],
        out_specs=[pl.BlockSpec((ts * ROW_TILE, LANES), lambda b, i: (b * (s // ts) + i, 0)),
                   tok(TOP_K), tok(TOP_K), tok(TOP_K),
                   pl.BlockSpec((1, N_EXPERTS), lambda b, i: (0, 0))],
        out_shape=[SDS((bsz * s * ROW_TILE, LANES), F32), SDS((bsz, s, TOP_K), jnp.int32),
                   SDS((bsz, s, TOP_K), F32),
                   SDS((bsz, s, TOP_K), jnp.int32), SDS((1, N_EXPERTS), jnp.int32)],
        scratch_shapes=[pltpu.VMEM((1, N_EXPERTS), F32)],
        compiler_params=_cparams(("arbitrary", "arbitrary")),
        name="router",
    )(x, mod, n2g.reshape(1, d), w_r, b_r.reshape(1, -1))


def _route_meta(top_idx, rank, counts):
    n = top_idx.shape[0]
    padded = (counts + EXPERT_BLOCK - 1) // EXPERT_BLOCK * EXPERT_BLOCK
    pend = jnp.cumsum(padded)
    pstart = pend - padded
    dest = (pstart[top_idx] + rank).astype(jnp.int32)
    n_rows = n * TOP_K + N_EXPERTS * EXPERT_BLOCK
    n_blk = n_rows // EXPERT_BLOCK
    blk_row0 = jnp.arange(n_blk) * EXPERT_BLOCK
    blk_exp = jnp.minimum(jnp.sum(pend[None, :] <= blk_row0[:, None], axis=1), N_EXPERTS - 1)
    n_used = (pend[-1] // EXPERT_BLOCK).astype(jnp.int32).reshape(1)
    eids = jnp.arange(N_EXPERTS)
    has_rows = counts > 0
    last_exp = jnp.max(jnp.where(has_rows, eids, 0))
    blk_exp = jnp.where(blk_row0 < pend[-1], blk_exp, last_exp).astype(jnp.int32)
    nxt_e = jnp.min(jnp.where((eids[None, :] > eids[:, None]) & has_rows[None, :], eids[None, :], N_EXPERTS), axis=1)
    nxt_e = jnp.where(nxt_e < N_EXPERTS, nxt_e, -1)
    slot_e = (jnp.cumsum(has_rows.astype(jnp.int32)) - 1) % 2
    blk_nxt = nxt_e[blk_exp].astype(jnp.int32)
    blk_slot = slot_e[blk_exp].astype(jnp.int32)
    pad_blk = jnp.where(padded > 0, pend - EXPERT_BLOCK, -1)
    tail = pend[-1] + jnp.arange(N_EXPERTS) * EXPERT_BLOCK
    tail_blk = jnp.where(tail < n_rows, tail, -1)
    pad_blk = jnp.concatenate([pad_blk, tail_blk]).astype(jnp.int32)
    return dest, (blk_exp, n_used, blk_nxt, blk_slot), pad_blk, n_rows


MOVE_TT = 1024
COMBINE_TT = 1024


def _dispatch_body(pad_ref, dest_ref, h_ref, xs_ref, zero_scr, sem, zsem):
    tt = h_ref.shape[0] // ROW_TILE
    blk = EXPERT_BLOCK * ROW_TILE

    @pl.when(pl.program_id(0) == 0)
    def _():
        zero_scr[...] = jnp.zeros(zero_scr.shape, zero_scr.dtype)

        def zero_copy(e):
            row0 = pl.multiple_of(pad_ref[e], blk)
            return pltpu.make_async_copy(zero_scr, xs_ref.at[pl.ds(row0, blk)], zsem)

        for e in range(2 * N_EXPERTS):
            pl.when(pad_ref[e] >= 0)(lambda e=e: zero_copy(e).start())
        for e in range(2 * N_EXPERTS):
            pl.when(pad_ref[e] >= 0)(lambda e=e: zero_copy(e).wait())

    def issue(t, carry):
        src = h_ref.at[pl.ds(pl.multiple_of(t * ROW_TILE, ROW_TILE), ROW_TILE)]
        for k in range(TOP_K):
            d = pl.multiple_of(dest_ref[0, t * TOP_K + k], ROW_TILE)
            pltpu.make_async_copy(src, xs_ref.at[pl.ds(d, ROW_TILE)], sem).start(priority=k % 2)
        return carry

    lax.fori_loop(0, tt, issue, 0)
    for k in range(TOP_K):
        pltpu.make_async_copy(h_ref, h_ref, sem).wait()


def _dispatch(h, dest8, pad_blk, n_rows):
    n = h.shape[0] // ROW_TILE
    tt = MOVE_TT
    dest3 = dest8.reshape(n // tt, 1, tt * TOP_K)
    return pl.pallas_call(
        _dispatch_body,
        grid_spec=pltpu.PrefetchScalarGridSpec(
            num_scalar_prefetch=1,
            grid=(n // tt,),
            in_specs=[
                pl.BlockSpec((None, 1, tt * TOP_K), lambda i, pad: (i, 0, 0), memory_space=pltpu.SMEM),
                pl.BlockSpec((tt * ROW_TILE, LANES), lambda i, pad: (i, 0)),
            ],
            out_specs=pl.BlockSpec(memory_space=pl.ANY),
            scratch_shapes=[pltpu.VMEM((EXPERT_BLOCK * ROW_TILE, LANES), h.dtype), pltpu.SemaphoreType.DMA(()),
                            pltpu.SemaphoreType.DMA(())],
        ),
        out_shape=SDS((n_rows * ROW_TILE, LANES), h.dtype),
        compiler_params=_cparams(("arbitrary",)),
        name="dispatch",
    )(pad_blk * ROW_TILE, dest3, h)


def _ffn_body(layer, be_ref, nu_ref, nx_ref, sl_ref, xs_ref, wg_hbm, bg_ref, wu_hbm, bu_ref, wd_hbm, bd_ref,
              y_ref, w_f32, w_bf, sem):
    i = pl.program_id(0)
    slot = sl_ref[i]

    def fetch(e, s):
        return [pltpu.make_async_copy(w.at[layer, e], w_f32.at[s, n], sem.at[s])
                for n, w in enumerate((wg_hbm, wu_hbm, wd_hbm))]

    @pl.when(i == 0)
    def _():
        for c in fetch(be_ref[0], slot):
            c.start()

    @pl.when((i == 0) | (be_ref[i] != be_ref[jnp.maximum(i - 1, 0)]))
    def _():
        for c in fetch(be_ref[i], slot):
            c.wait()

        @pl.when(nx_ref[i] >= 0)
        def _():
            for c in fetch(nx_ref[i], 1 - slot):
                c.start()

        for n in range(3):
            w_bf[n] = w_f32[slot, n].astype(BF16)

    @pl.when(i < nu_ref[0])
    def _():
        x = _load_row_tiles(xs_ref, EXPERT_BLOCK).astype(BF16)
        g = jnp.minimum(_dot(x, w_bf[0]) + bg_ref[...], SWIGLU_LIMIT)
        u = jnp.clip(_dot(x, w_bf[1]) + bu_ref[...], -SWIGLU_LIMIT, SWIGLU_LIMIT)
        a = g * jax.nn.sigmoid(SWIGLU_ALPHA * g) * (u + 1.0)
        _store_row_tiles(y_ref, _dot(a.astype(BF16), w_bf[2]) + bd_ref[...])

    @pl.when(i >= nu_ref[0])
    def _():
        y_ref[...] = jnp.zeros(y_ref.shape, y_ref.dtype)


def _ffn(xs, blk_meta, layer, w_gate, b_gate, w_up, b_up, w_down, b_down):
    n_rows = xs.shape[0] // ROW_TILE
    _, e, d, de = w_gate.shape
    assert d == de == D_MODEL
    n_blk = n_rows // EXPERT_BLOCK
    blk = (EXPERT_BLOCK * ROW_TILE, LANES)
    bspec = lambda c: pl.BlockSpec((None, None, 1, c), lambda i, be, nu, nx, sl: (layer, be[i], 0, 0))
    hbm = pl.BlockSpec(memory_space=pl.ANY)
    xs_map = lambda i, be, nu, nx, sl: (jnp.minimum(i, jnp.maximum(nu[0] - 1, 0)), 0)
    return pl.pallas_call(
        functools.partial(_ffn_body, layer),
        grid_spec=pltpu.PrefetchScalarGridSpec(
            num_scalar_prefetch=4,
            grid=(n_blk,),
            in_specs=[
                pl.BlockSpec(blk, xs_map),
                hbm, bspec(de), hbm, bspec(de), hbm, bspec(d),
            ],
            out_specs=pl.BlockSpec(blk, lambda i, be, nu, nx, sl: (i, 0)),
            scratch_shapes=[pltpu.VMEM((2, 3, d, de), F32), pltpu.VMEM((3, d, de), BF16),
                            pltpu.SemaphoreType.DMA((2,))],
        ),
        out_shape=SDS((n_rows * ROW_TILE, LANES), F32),
        compiler_params=_cparams(("arbitrary",)),
        name="ffn",
    )(*blk_meta, xs, w_gate, b_gate.reshape(-1, e, 1, de), w_up, b_up.reshape(-1, e, 1, de),
      w_down, b_down.reshape(-1, e, 1, d))


COMBINE_RC = 128


def _combine_body(dest_ref, x_ref, mod_ref, p_ref, y_ref, o_ref, buf, sem):
    tt = x_ref.shape[0]

    def issue(t, carry):
        t8 = pl.multiple_of(t * ROW_TILE, ROW_TILE)
        for k in range(TOP_K):
            d = pl.multiple_of(dest_ref[0, t * TOP_K + k], ROW_TILE)
            pltpu.make_async_copy(y_ref.at[pl.ds(d, ROW_TILE)], buf.at[k, pl.ds(t8, ROW_TILE)],
                                  sem).start(priority=k % 2)
        return carry

    lax.fori_loop(0, tt, issue, 0)
    pltpu.make_async_copy(buf, buf, sem).wait()

    g2 = mod_ref[0][5:6]
    for r0 in range(0, tt, COMBINE_RC):
        rows = slice(r0, r0 + COMBINE_RC)
        p = p_ref[rows, :]
        for j in range(ROW_TILE):
            cols = slice(j * LANES, (j + 1) * LANES)
            tile = lambda k: buf[k, pl.ds(r0 * ROW_TILE + j, COMBINE_RC, stride=ROW_TILE), :]
            acc = p[:, 0:1] * tile(0)
            for k in range(1, TOP_K):
                acc = acc + p[:, k:k + 1] * tile(k)
            o_ref[rows, cols] = x_ref[rows, cols] + g2[:, cols] * acc


def _combine(x, mod, p, y, dest8, seq):
    n, d = x.shape
    tt = COMBINE_TT
    n_steps = n // tt
    dest3 = dest8.reshape(n_steps, 1, tt * TOP_K)
    per_b = seq // tt
    return pl.pallas_call(
        _combine_body,
        grid=(n_steps,),
        in_specs=[
            pl.BlockSpec((None, 1, tt * TOP_K), lambda i: (i, 0, 0), memory_space=pltpu.SMEM),
            pl.BlockSpec((tt, d), lambda i: (i, 0)),
            pl.BlockSpec((1, 6, d), lambda i: (i // per_b, 0, 0)),
            pl.BlockSpec((tt, TOP_K), lambda i: (i, 0)),
            pl.BlockSpec(memory_space=pl.ANY),
        ],
        out_specs=pl.BlockSpec((tt, d), lambda i: (i, 0)),
        out_shape=SDS((n, d), F32),
        scratch_shapes=[pltpu.VMEM((TOP_K, tt * ROW_TILE, LANES), F32), pltpu.SemaphoreType.DMA(())],
        compiler_params=_cparams(("arbitrary",)),
        name="combine",
    )(dest3, x, mod, p, y)


def _moe(x, mod, layer, n2g, w_r, b_r, w_gate, b_gate, w_up, b_up, w_down, b_down):
    bsz, s, d = x.shape
    n = bsz * s
    h2, top_idx, probs, rank, counts = _router(x, mod, n2g, w_r, b_r)
    dest, blk_meta, pad_blk, n_rows = _route_meta(
        top_idx.reshape(n, TOP_K), rank.reshape(n, TOP_K), counts.reshape(N_EXPERTS))
    dest8 = dest * ROW_TILE
    xs = _dispatch(h2, dest8, pad_blk, n_rows)
    y = _ffn(xs, blk_meta, layer, w_gate, b_gate, w_up, b_up, w_down, b_down)
    out = _combine(x.reshape(n, d), mod, probs.reshape(n, TOP_K), y, dest8, s)
    return out.reshape(bsz, s, d)


def _seg_mats():
    r = lax.broadcasted_iota(jnp.int32, (D_MODEL, N_HEADS), 0) // HEAD_DIM
    c = lax.broadcasted_iota(jnp.int32, (D_MODEL, N_HEADS), 1)
    seg = jnp.where(r == c, 1.0, 0.0).astype(BF16)
    rt = lax.broadcasted_iota(jnp.int32, (N_HEADS, D_MODEL), 0)
    ct = lax.broadcasted_iota(jnp.int32, (N_HEADS, D_MODEL), 1) // HEAD_DIM
    seg_t = jnp.where(rt == ct, 1.0, 0.0).astype(BF16)
    return seg, seg_t


def _split_dot(a, b):
    hi = a.astype(BF16)
    lo = (a - hi.astype(F32)).astype(BF16)
    return _dot(hi, b) + _dot(lo, b)


def _head_rmsnorm(x, g, seg, seg_t):
    ms = _split_dot(x * x, seg) * (1.0 / HEAD_DIM)
    r = lax.rsqrt(ms + EPS)
    return x * _split_dot(r, seg_t) * g


def _kvq_body(x_ref, mod_ref, kvmod_ref, n1g_ref, kvg_ref, wq_ref, wkv_ref, qg_ref, kg_ref,
              q_ref, k_ref, vt_ref, km_ref):
    x = x_ref[0]
    mod = mod_ref[0]
    kvmod = kvmod_ref[0]
    ms = jnp.mean(x * x, axis=-1, keepdims=True)
    xn = x * lax.rsqrt(ms + EPS)
    hq = ((xn * n1g_ref[...]) * (1.0 + mod[1:2]) + mod[0:1]).astype(BF16)
    hk = ((xn * kvg_ref[...]) * (1.0 + kvmod[1:2]) + kvmod[0:1]).astype(BF16)
    seg, seg_t = _seg_mats()
    q = _head_rmsnorm(_dot(hq, wq_ref[...]), qg_ref[...], seg, seg_t)
    q_ref[0] = q
    kv = _dot(hk, wkv_ref[...])
    k = _head_rmsnorm(kv[:, :D_MODEL], kg_ref[...], seg, seg_t)
    k_ref[0, 0] = k.astype(BF16)
    for u in range(ATT_U):
        rows = slice(u * MOBA_BLOCK, (u + 1) * MOBA_BLOCK)
        km_ref[0, 0, u:u + 1, :] = jnp.mean(k[rows], axis=0, keepdims=True)
    vt_ref[0, 0] = kv[:, D_MODEL:].T.astype(BF16)


def _kvq(x, mod, kvmod, n1g, kvg, w_q, w_kv, q_g, k_g):
    bsz, s, d = x.shape
    ts = ATT_U * MOBA_BLOCK
    nb = s // ts
    const = lambda *shape: pl.BlockSpec(shape, lambda b, i: (0,) * len(shape))
    return pl.pallas_call(
        _kvq_body,
        grid=(bsz, nb),
        in_specs=[
            pl.BlockSpec((1, ts, d), lambda b, i: (b, i, 0)),
            pl.BlockSpec((1, 6, d), lambda b, i: (b, 0, 0)),
            pl.BlockSpec((1, 2, d), lambda b, i: (b, 0, 0)),
            const(1, d), const(1, d), const(d, d), const(d, 2 * d), const(1, d), const(1, d),
        ],
        out_specs=[
            pl.BlockSpec((1, ts, d), lambda b, i: (b, i, 0)),
            pl.BlockSpec((1, 1, ts, d), lambda b, i: (b, i, 0, 0)),
            pl.BlockSpec((1, 1, d, ts), lambda b, i: (b, i, 0, 0)),
            pl.BlockSpec((1, 1, ATT_U, d), lambda b, i: (b, i, 0, 0)),
        ],
        out_shape=[
            SDS((bsz, s, d), F32),
            SDS((bsz, nb, ts, d), BF16),
            SDS((bsz, nb, d, ts), BF16),
            SDS((bsz, nb, ATT_U, d), F32),
        ],
        compiler_params=_cparams(("parallel", "arbitrary")),
        name="kvq",
    )(x, mod, kvmod, n1g.reshape(1, d), kvg.reshape(1, d), w_q.astype(BF16), w_kv.astype(BF16),
      jnp.tile(q_g, N_HEADS).reshape(1, d), jnp.tile(k_g, N_HEADS).reshape(1, d))


def _bf16_pieces(v):
    hi = v.astype(BF16).astype(F32)
    mid = (v - hi).astype(BF16).astype(F32)
    lo = ((v - hi) - mid).astype(BF16).astype(F32)
    return hi, mid, lo


def _attn_body(slopes_ref, q_ref, k_ref, vt_ref, km_ref, o_ref, cm_scr, kaug_scr, qaug_scr, s_scr):
    hp = pl.program_id(1)
    own = pl.program_id(2)
    bs = MOBA_BLOCK
    nb = k_ref.shape[1] * ATT_U

    @pl.when(own == 0)
    def _():
        rel = (lax.broadcasted_iota(jnp.int32, (bs, bs), 1)
               - lax.broadcasted_iota(jnp.int32, (bs, bs), 0))
        cm_scr[0] = jnp.zeros((bs, bs), F32)
        cm_scr[1] = jnp.where(rel >= 0, 0.0, -jnp.inf)
        klane = lax.broadcasted_iota(jnp.int32, (ATT_U * bs, ATT_AUG), 1)
        koff = (lax.broadcasted_iota(jnp.int32, (ATT_U * bs, ATT_AUG), 0) % bs).astype(F32)
        kaug_scr[...] = jnp.where(klane < 3, 1.0, jnp.where(klane < 6, koff, 0.0)).astype(BF16)
        qlane = lax.broadcasted_iota(jnp.int32, (bs, ATT_AUG), 1)
        qoff = lax.broadcasted_iota(jnp.int32, (bs, ATT_AUG), 0).astype(F32)
        for hd in range(2):
            s2 = jnp.full((bs, ATT_AUG), slopes_ref[2 * hp + hd], F32) * LOG2E
            pieces = _bf16_pieces(-(s2 * qoff)) + _bf16_pieces(s2)
            aug = jnp.zeros((bs, ATT_AUG), F32)
            for lane_ix, piece in enumerate(pieces):
                aug = jnp.where(qlane == lane_ix, piece, aug)
            qaug_scr[hd] = aug.astype(BF16)

    q = q_ref[0]
    km = km_ref[0]
    dcol = lax.broadcasted_iota(jnp.int32, (1, 2 * HEAD_DIM), 1)
    blk_row = lax.broadcasted_iota(jnp.int32, (nb, 1), 0)
    valid = blk_row < own

    in_head = [(dcol >= hd * HEAD_DIM) & (dcol < (hd + 1) * HEAD_DIM) for hd in range(2)]
    km2 = jnp.concatenate([jnp.where(in_head[hd], km, 0.0) for hd in range(2)], axis=0)
    gates = lax.dot_general(km2, q, NT_DIMS, precision=HIGHEST, preferred_element_type=F32)

    qs, selbias = [], []
    for hd in range(2):
        slope = slopes_ref[2 * hp + hd]
        qm = jnp.where(in_head[hd], q, 0.0)
        gate = jnp.where(valid, gates[hd * nb:(hd + 1) * nb], -jnp.inf)
        rank = jnp.zeros(gate.shape, F32)
        for i in range(nb):
            gi = gate[i:i + 1, :]
            tie = jnp.where(blk_row > i, 1.0, 0.0)
            rank = rank + jnp.where(gi > gate, 1.0, jnp.where(gi == gate, tie, 0.0))
        chosen = jnp.where(valid, jnp.where(rank < MOBA_TOPK, 1.0, 0.0),
                           jnp.where(blk_row == own, 1.0, 0.0))
        off = (own - blk_row).astype(F32) * (bs * LOG2E * slope)
        selbias.append(jnp.where(chosen > 0.5, -off, -jnp.inf))
        qs.append(jnp.concatenate([(qm * (LOG2E * HEAD_DIM ** -0.5)).astype(BF16), qaug_scr[hd]], axis=1))

    last = k_ref.shape[1] - 1
    ones = jnp.ones((ATT_ONES, ATT_U * bs), BF16)

    def qk(it, hd):
        kc = jnp.concatenate([k_ref[0, it], kaug_scr[...]], axis=1)
        return lax.dot_general(kc, qs[hd], NT_DIMS, preferred_element_type=F32)

    def visit(it, slot, carry, next_it, has_own):
        out = []
        if next_it is not None:
            for hd in range(2):
                s_scr[1 - slot, hd] = qk(jnp.minimum(next_it, last), hd)
        for hd in range(2):
            m, acc = carry[hd]

            def tile(u):
                t = s_scr[slot, hd, u * bs:(u + 1) * bs]
                if has_own:
                    t = t + cm_scr[(it * ATT_U + u == own).astype(jnp.int32)]
                return t

            rows = [jnp.sum(jnp.where(blk_row == it * ATT_U + u, selbias[hd], 0.0), axis=0, keepdims=True)
                    for u in range(ATT_U)]
            mx = m
            for u in range(ATT_U):
                mx = jnp.maximum(mx, jnp.max(tile(u), axis=0, keepdims=True) + rows[u])
            m_safe = jnp.where(mx == -jnp.inf, 0.0, mx)
            alpha = jnp.exp2(m - m_safe)
            p = jnp.concatenate([jnp.exp2((tile(u) + (rows[u] - m_safe)).astype(BF16)) for u in range(ATT_U)],
                                axis=0)
            vj = jnp.concatenate([vt_ref[0, it, hd * HEAD_DIM:(hd + 1) * HEAD_DIM, :], ones], axis=0)
            acc = alpha * acc + _dot(vj, p)
            out.append((mx, acc))
        return tuple(out)

    def body(i2, carry):
        carry = visit(2 * i2, 1, carry, 2 * i2 + 1, False)
        return visit(2 * i2 + 1, 0, carry, 2 * i2 + 2, False)

    n_past = own // ATT_U
    for hd in range(2):
        s_scr[0, hd] = qk(n_past, hd)
    init = tuple((jnp.full((1, bs), -jnp.inf, F32), jnp.zeros((HEAD_DIM + ATT_ONES, bs), F32))
                 for hd in range(2))
    carry = visit(n_past, 0, init, 0, True)
    carry = lax.fori_loop(0, n_past // 2, body, carry)
    carry = lax.cond(n_past % 2 == 1, lambda c: visit(n_past - 1, 1, c, None, False), lambda c: c, carry)
    outs = [acc[:HEAD_DIM] / acc[HEAD_DIM:HEAD_DIM + 1] for (_, acc) in carry]
    o_ref[0] = jnp.concatenate(outs, axis=0).astype(o_ref.dtype)


def _attn(q, kb, vtb, kmean):
    bsz, s, d = q.shape
    nsb = s // (ATT_U * MOBA_BLOCK)
    assert nsb % 2 == 0, "the attention loop visits key super-blocks in pairs"
    nb = s // MOBA_BLOCK
    w = 2 * HEAD_DIM
    slopes = (2.0 ** (-8.0 * jnp.arange(1, N_HEADS + 1, dtype=F32) / N_HEADS)).astype(F32)
    return pl.pallas_call(
        _attn_body,
        grid_spec=pltpu.PrefetchScalarGridSpec(
            num_scalar_prefetch=1,
            grid=(bsz, N_HEADS // 2, nb),
            in_specs=[
                pl.BlockSpec((1, MOBA_BLOCK, w), lambda b, h, c, sl: (b, c, h)),
                pl.BlockSpec((1, nsb, ATT_U * MOBA_BLOCK, w), lambda b, h, c, sl: (b, 0, 0, h)),
                pl.BlockSpec((1, nsb, w, ATT_U * MOBA_BLOCK), lambda b, h, c, sl: (b, 0, h, 0)),
                pl.BlockSpec((1, nb, w), lambda b, h, c, sl: (b, 0, h)),
            ],
            out_specs=pl.BlockSpec((1, w, MOBA_BLOCK), lambda b, h, c, sl: (b, h, c)),
            scratch_shapes=[pltpu.VMEM((2, MOBA_BLOCK, MOBA_BLOCK), F32),
                            pltpu.VMEM((ATT_U * MOBA_BLOCK, ATT_AUG), BF16),
                            pltpu.VMEM((2, MOBA_BLOCK, ATT_AUG), BF16),
                            pltpu.VMEM((2, 2, ATT_U * MOBA_BLOCK, MOBA_BLOCK), F32)],
        ),
        out_shape=SDS((bsz, d, s), BF16),
        compiler_params=_cparams(("parallel", "parallel", "arbitrary")),
        name="attn",
    )(slopes, q, kb, vtb, kmean)


OPROJ_TS = 512


def _oproj_body(at_ref, x_ref, mod_ref, wo_ref, o_ref):
    g1 = mod_ref[0][2:3]
    m = lax.dot_general(at_ref[0], wo_ref[...], (((0,), (0,)), ((), ())), preferred_element_type=F32)
    o_ref[0] = x_ref[0] + g1 * m


def _oproj(at, x, mod, w_o):
    bsz, s, d = x.shape
    ts = min(OPROJ_TS, s)
    tok = pl.BlockSpec((1, ts, d), lambda b, i: (b, i, 0))
    return pl.pallas_call(
        _oproj_body,
        grid=(bsz, s // ts),
        in_specs=[pl.BlockSpec((1, d, ts), lambda b, i: (b, 0, i)), tok,
                  pl.BlockSpec((1, 6, d), lambda b, i: (b, 0, 0)),
                  pl.BlockSpec((d, d), lambda b, i: (0, 0))],
        out_specs=tok,
        out_shape=SDS((bsz, s, d), F32),
        compiler_params=_cparams(("parallel", "arbitrary")),
        name="oproj",
    )(at, x, mod, w_o.astype(BF16))


def kernel(x, c, ada_w, ada_b, norm1_g, norm2_g, sgu_w_in, sgu_b_in, sgu_v_g, sgu_w_s, sgu_b_s, sgu_w_out,
           kv_ada_w, kv_ada_b, kv_norm_g, w_kv, k_norm_g, attn_w_q, q_norm_g, attn_w_o,
           moe_w_router, moe_b_router, moe_w_gate, moe_b_gate, moe_w_up, moe_b_up, moe_w_down, moe_b_down):
    bsz, s, d = x.shape
    moe = lambda l, xx, mod: _moe(xx, mod, l, norm2_g[l], moe_w_router[l], moe_b_router[l],
                                  moe_w_gate, moe_b_gate, moe_w_up, moe_b_up, moe_w_down, moe_b_down)
    mod0 = _ada(c, ada_w, ada_b[0], 0).reshape(bsz, 6, d)
    x = _sgu(x, mod0, norm1_g[0], sgu_w_in[0], sgu_b_in[0], sgu_v_g[0], sgu_w_s[0], sgu_b_s[0], sgu_w_out[0])
    x = moe(0, x, mod0)
    mod1 = _ada(c, ada_w, ada_b[1], 1).reshape(bsz, 6, d)
    kvmod = _ada(c, kv_ada_w, kv_ada_b).reshape(bsz, 2, d)
    q, kb, vtb, kmean = _kvq(x, mod1, kvmod, norm1_g[1], kv_norm_g, attn_w_q[0], w_kv, q_norm_g[0], k_norm_g)
    a = _attn(q, kb, vtb, kmean.reshape(bsz, s // MOBA_BLOCK, d))
    x = _oproj(a, x, mod1, attn_w_o[0])
    x = moe(1, x, mod1)
    return x
```

```python
import functools
import math

import jax
import jax.numpy as jnp
import numpy as np
from jax import lax
from jax.experimental import pallas as pl
from jax.experimental.pallas import tpu as pltpu

F32 = jnp.float32
BF16 = jnp.bfloat16
HIGHEST = lax.Precision.HIGHEST
SDS = jax.ShapeDtypeStruct

D_MODEL = 1024
SGU_CHUNK = 128
SGU_GROUPS = 8
D_SGU = 3 * D_MODEL
SGU_GROUP_DIM = D_SGU // SGU_GROUPS
HEAD_DIM = 64
N_HEADS = D_MODEL // HEAD_DIM
MOBA_BLOCK = 256
MOBA_TOPK = 3
ATT_U = 2
ATT_ONES = 16
ATT_AUG = 128
ATT_HP = 2
LOG2E = math.log2(math.e)
N_EXPERTS = 32
TOP_K = 4
SWIGLU_LIMIT = 7.0
SWIGLU_ALPHA = 1.702
EXPERT_BLOCK = 512
EPS = 1e-6

VMEM_LIMIT_BYTES = 56 * 1024 * 1024
NT_DIMS = (((1,), (1,)), ((), ()))


def _cparams(sem):
    return pltpu.CompilerParams(dimension_semantics=sem, vmem_limit_bytes=VMEM_LIMIT_BYTES)


def _dot(a, b, **kw):
    return jnp.dot(a, b, preferred_element_type=F32, **kw)


LANES = 128
ROW_TILE = D_MODEL // LANES


def _store_row_tiles(ref, val):
    rows = val.shape[0]
    for j in range(ROW_TILE):
        ref[pl.ds(j, rows, stride=ROW_TILE), :] = val[:, j * LANES:(j + 1) * LANES]


def _load_row_tiles(ref, rows):
    return jnp.concatenate([ref[pl.ds(j, rows, stride=ROW_TILE), :] for j in range(ROW_TILE)], axis=-1)


def _dot_3pass(a, b):
    a_hi = a.astype(BF16)
    a_lo = (a - a_hi.astype(F32)).astype(BF16)
    b_hi = b.astype(BF16)
    b_lo = (b - b_hi.astype(F32)).astype(BF16)
    return _dot(a_hi, b_hi) + (_dot(a_lo, b_hi) + _dot(a_hi, b_lo))


def _norm_mod(x, g, sc, sh):
    ms = jnp.mean(x * x, axis=-1, keepdims=True)
    return (x * lax.rsqrt(ms + EPS) * g) * (1.0 + sc) + sh


def _ada_body(c_ref, w_ref, b_ref, o_ref):
    c = c_ref[...]
    ca = c * jax.nn.sigmoid(c)
    o_ref[...] = _dot(ca, w_ref[...], precision=HIGHEST) + b_ref[...]


def _ada(c, w, b, layer=0):
    bsz, d = c.shape
    n = w.shape[-1]
    tn = 1024
    w = w.reshape(-1, d, n)
    return pl.pallas_call(
        _ada_body,
        grid=(n // tn,),
        in_specs=[
            pl.BlockSpec((bsz, d), lambda j: (0, 0)),
            pl.BlockSpec((None, d, tn), lambda j: (layer, 0, j)),
            pl.BlockSpec((1, tn), lambda j: (0, j)),
        ],
        out_specs=pl.BlockSpec((bsz, tn), lambda j: (0, j)),
        out_shape=SDS((bsz, n), F32),
        compiler_params=_cparams(("arbitrary",)),
        name="ada",
    )(c, w, b.reshape(1, n))


SGU_TS = 512
SGU_CB = 768


def _gelu_tanh(x):
    c = np.float32(np.sqrt(2.0 / np.pi))
    return x * (0.5 * (1.0 + jnp.tanh(c * (x + 0.044715 * (x * x * x)))))


def _sgu_body(x_ref, mod_ref, n1g_ref, win_ref, bin_ref, vg_ref, ws_ref, bst_ref, wout_ref,
              o_ref, u_scr, v_scr, y_scr):
    ts = x_ref.shape[1]
    x = x_ref[0]
    mod = mod_ref[0]
    sh1, sc1, g1 = mod[0:1], mod[1:2], mod[2:3]
    h = _norm_mod(x, n1g_ref[...], sc1, sh1).astype(BF16)

    for j in range(D_SGU // SGU_CB):
        cs = slice(j * SGU_CB, (j + 1) * SGU_CB)
        u_scr[:, cs] = _gelu_tanh(_dot(h, win_ref[:, cs]) + bin_ref[:, cs])
    ssq = jnp.zeros((ts, 1), F32)
    for j in range(D_SGU // SGU_CB):
        cs = slice(j * SGU_CB, (j + 1) * SGU_CB)
        ws_cols = slice(D_SGU + j * SGU_CB, D_SGU + (j + 1) * SGU_CB)
        z = _gelu_tanh(_dot(h, win_ref[:, ws_cols]) + bin_ref[:, ws_cols])
        v_scr[:, cs] = z
        ssq = ssq + jnp.sum(z * z, axis=-1, keepdims=True)
    rs = lax.rsqrt(ssq / D_SGU + EPS)

    r_i = lax.broadcasted_iota(jnp.int32, (SGU_CHUNK, SGU_CHUNK), 0)
    c_i = lax.broadcasted_iota(jnp.int32, (SGU_CHUNK, SGU_CHUNK), 1)
    causal = c_i <= r_i
    for g in range(SGU_GROUPS):
        cs = slice(g * SGU_GROUP_DIM, (g + 1) * SGU_GROUP_DIM)
        wsg = jnp.where(causal, ws_ref[g], 0.0).astype(BF16)
        bsg = bst_ref[:, g:g + 1]
        for n in range(ts // SGU_CHUNK):
            rows = slice(n * SGU_CHUNK, (n + 1) * SGU_CHUNK)
            vn = (v_scr[rows, cs] * rs[rows] * vg_ref[:, cs]).astype(BF16)
            mixed = _dot(wsg, vn) + bsg
            y_scr[rows, cs] = (u_scr[rows, cs] * mixed).astype(BF16)

    m = _dot(y_scr[...], wout_ref[...])
    o_ref[0] = x + g1 * m


def _sgu(x, mod, n1g, w_in, b_in, v_g, w_s, b_s, w_out):
    bsz, s, d = x.shape
    ts = SGU_TS
    const = lambda *shape: pl.BlockSpec(shape, lambda b, i: (0,) * len(shape),
                                        pipeline_mode=pl.Buffered(1))
    return pl.pallas_call(
        _sgu_body,
        grid=(bsz, s // ts),
        in_specs=[
            pl.BlockSpec((1, ts, d), lambda b, i: (b, i, 0)),
            pl.BlockSpec((1, 6, d), lambda b, i: (b, 0, 0)),
            const(1, d),
            const(d, 2 * D_SGU),
            const(1, 2 * D_SGU),
            const(1, D_SGU),
            const(SGU_GROUPS, SGU_CHUNK, SGU_CHUNK),
            const(SGU_CHUNK, SGU_GROUPS),
            const(D_SGU, d),
        ],
        out_specs=pl.BlockSpec((1, ts, d), lambda b, i: (b, i, 0)),
        out_shape=SDS((bsz, s, d), F32),
        scratch_shapes=[
            pltpu.VMEM((ts, D_SGU), F32),
            pltpu.VMEM((ts, D_SGU), F32),
            pltpu.VMEM((ts, D_SGU), BF16),
        ],
        compiler_params=_cparams(("parallel", "arbitrary")),
        name="sgu",
    )(x, mod, n1g.reshape(1, d), w_in.astype(BF16), b_in.reshape(1, -1), v_g.reshape(1, -1),
      w_s, b_s.T, w_out.astype(BF16))


ROUTER_TS = 512


def _router_body(x_ref, mod_ref, n2g_ref, wr_ref, br_ref, h_ref, idx_ref, p_ref, rk_ref, cnt_ref, run_scr):
    @pl.when((pl.program_id(0) == 0) & (pl.program_id(1) == 0))
    def _():
        run_scr[...] = jnp.zeros(run_scr.shape, run_scr.dtype)

    x = x_ref[0]
    mod = mod_ref[0]
    sh2, sc2 = mod[3:4], mod[4:5]
    h2 = _norm_mod(x, n2g_ref[...], sc2, sh2)
    _store_row_tiles(h_ref, h2)
    logits = _dot_3pass(h2, wr_ref[...]) + br_ref[...]
    lane = lax.broadcasted_iota(jnp.int32, logits.shape, 1)
    work = logits
    vals, idxs = [], []
    for _ in range(TOP_K):
        mx = jnp.max(work, axis=-1, keepdims=True)
        ix = jnp.min(jnp.where(work == mx, lane, N_EXPERTS), axis=-1, keepdims=True)
        vals.append(mx)
        idxs.append(ix)
        work = jnp.where(lane == ix, -jnp.inf, work)
    tv = jnp.concatenate(vals, axis=-1)
    e = jnp.exp(tv - vals[0])
    p_ref[0] = e / jnp.sum(e, axis=-1, keepdims=True)
    idx_ref[0] = jnp.concatenate(idxs, axis=-1)

    ts = x.shape[0]
    onehots = [lane == ix for ix in idxs]
    mask = jnp.zeros(logits.shape, F32)
    for oh in onehots:
        mask = mask + jnp.where(oh, 1.0, 0.0)
    earlier = (lax.broadcasted_iota(jnp.int32, (ts, ts), 1) < lax.broadcasted_iota(jnp.int32, (ts, ts), 0))
    before = _dot(jnp.where(earlier, 1.0, 0.0).astype(BF16), mask.astype(BF16)) + run_scr[...]
    rk = [jnp.sum(jnp.where(oh, before, 0.0), axis=-1, keepdims=True) for oh in onehots]
    rk_ref[0] = jnp.concatenate(rk, axis=-1).astype(jnp.int32)
    run_scr[...] = run_scr[...] + jnp.sum(mask, axis=0, keepdims=True)
    cnt_ref[...] = run_scr[...].astype(jnp.int32)


def _router(x, mod, n2g, w_r, b_r):
    bsz, s, d = x.shape
    ts = min(ROUTER_TS, s)
    tok = lambda w: pl.BlockSpec((1, ts, w), lambda b, i: (b, i, 0))
    return pl.pallas_call(
        _router_body,
        grid=(bsz, s // ts),
        in_specs=[
            tok(d),
            pl.BlockSpec((1, 6, d), lambda b, i: (b, 0, 0)),
            pl.BlockSpec((1, d), lambda b, i: (0, 0)),
            pl.BlockSpec((d, N_EXPERTS), lambda b, i: (0, 0)),
            pl.BlockSpec((1, N_EXPERTS), lambda b, i: (0, 0)),
        ],
        out_specs=[pl.BlockSpec((ts * ROW_TILE, LANES), lambda b, i: (b * (s // ts) + i, 0)),
                   tok(TOP_K), tok(TOP_K), tok(TOP_K),
                   pl.BlockSpec((1, N_EXPERTS), lambda b, i: (0, 0))],
        out_shape=[SDS((bsz * s * ROW_TILE, LANES), F32), SDS((bsz, s, TOP_K), jnp.int32),
                   SDS((bsz, s, TOP_K), F32),
                   SDS((bsz, s, TOP_K), jnp.int32), SDS((1, N_EXPERTS), jnp.int32)],
        scratch_shapes=[pltpu.VMEM((1, N_EXPERTS), F32)],
        compiler_params=_cparams(("arbitrary", "arbitrary")),
        name="router",
    )(x, mod, n2g.reshape(1, d), w_r, b_r.reshape(1, -1))


def _route_meta(top_idx, rank, counts):
    n = top_idx.shape[0]
    padded = (counts + EXPERT_BLOCK - 1) // EXPERT_BLOCK * EXPERT_BLOCK
    pend = jnp.cumsum(padded)
    pstart = pend - padded
    dest = (pstart[top_idx] + rank).astype(jnp.int32)
    n_rows = n * TOP_K + N_EXPERTS * EXPERT_BLOCK
    n_blk = n_rows // EXPERT_BLOCK
    blk_row0 = jnp.arange(n_blk) * EXPERT_BLOCK
    blk_exp = jnp.minimum(jnp.sum(pend[None, :] <= blk_row0[:, None], axis=1), N_EXPERTS - 1)
    n_used = (pend[-1] // EXPERT_BLOCK).astype(jnp.int32).reshape(1)
    eids = jnp.arange(N_EXPERTS)
    has_rows = counts > 0
    last_exp = jnp.max(jnp.where(has_rows, eids, 0))
    blk_exp = jnp.where(blk_row0 < pend[-1], blk_exp, last_exp).astype(jnp.int32)
    nxt_e = jnp.min(jnp.where((eids[None, :] > eids[:, None]) & has_rows[None, :], eids[None, :], N_EXPERTS), axis=1)
    nxt_e = jnp.where(nxt_e < N_EXPERTS, nxt_e, -1)
    slot_e = (jnp.cumsum(has_rows.astype(jnp.int32)) - 1) % 2
    blk_nxt = nxt_e[blk_exp].astype(jnp.int32)
    blk_slot = slot_e[blk_exp].astype(jnp.int32)
    pad_blk = jnp.where(padded > 0, pend - EXPERT_BLOCK, -1)
    tail = pend[-1] + jnp.arange(N_EXPERTS) * EXPERT_BLOCK
    tail_blk = jnp.where(tail < n_rows, tail, -1)
    pad_blk = jnp.concatenate([pad_blk, tail_blk]).astype(jnp.int32)
    return dest, (blk_exp, n_used, blk_nxt, blk_slot), pad_blk, n_rows


MOVE_TT = 1024
COMBINE_TT = 1024


def _dispatch_body(pad_ref, dest_ref, h_ref, xs_ref, zero_scr, sem, zsem):
    tt = h_ref.shape[0] // ROW_TILE
    blk = EXPERT_BLOCK * ROW_TILE

    @pl.when(pl.program_id(0) == 0)
    def _():
        zero_scr[...] = jnp.zeros(zero_scr.shape, zero_scr.dtype)

        def zero_copy(e):
            row0 = pl.multiple_of(pad_ref[e], blk)
            return pltpu.make_async_copy(zero_scr, xs_ref.at[pl.ds(row0, blk)], zsem)

        for e in range(2 * N_EXPERTS):
            pl.when(pad_ref[e] >= 0)(lambda e=e: zero_copy(e).start())
        for e in range(2 * N_EXPERTS):
            pl.when(pad_ref[e] >= 0)(lambda e=e: zero_copy(e).wait())

    def issue(t, carry):
        src = h_ref.at[pl.ds(pl.multiple_of(t * ROW_TILE, ROW_TILE), ROW_TILE)]
        for k in range(TOP_K):
            d = pl.multiple_of(dest_ref[0, t * TOP_K + k], ROW_TILE)
            pltpu.make_async_copy(src, xs_ref.at[pl.ds(d, ROW_TILE)], sem).start(priority=k % 2)
        return carry

    lax.fori_loop(0, tt, issue, 0)
    for k in range(TOP_K):
        pltpu.make_async_copy(h_ref, h_ref, sem).wait()


def _dispatch(h, dest8, pad_blk, n_rows):
    n = h.shape[0] // ROW_TILE
    tt = MOVE_TT
    dest3 = dest8.reshape(n // tt, 1, tt * TOP_K)
    return pl.pallas_call(
        _dispatch_body,
        grid_spec=pltpu.PrefetchScalarGridSpec(
            num_scalar_prefetch=1,
            grid=(n // tt,),
            in_specs=[
                pl.BlockSpec((None, 1, tt * TOP_K), lambda i, pad: (i, 0, 0), memory_space=pltpu.SMEM),
                pl.BlockSpec((tt * ROW_TILE, LANES), lambda i, pad: (i, 0)),
            ],
            out_specs=pl.BlockSpec(memory_space=pl.ANY),
            scratch_shapes=[pltpu.VMEM((EXPERT_BLOCK * ROW_TILE, LANES), h.dtype), pltpu.SemaphoreType.DMA(()),
                            pltpu.SemaphoreType.DMA(())],
        ),
        out_shape=SDS((n_rows * ROW_TILE, LANES), h.dtype),
        compiler_params=_cparams(("arbitrary",)),
        name="dispatch",
    )(pad_blk * ROW_TILE, dest3, h)


def _ffn_body(layer, be_ref, nu_ref, nx_ref, sl_ref, xs_ref, wg_hbm, bg_ref, wu_hbm, bu_ref, wd_hbm, bd_ref,
              y_ref, w_f32, w_bf, sem):
    i = pl.program_id(0)
    slot = sl_ref[i]

    def fetch(e, s):
        return [pltpu.make_async_copy(w.at[layer, e], w_f32.at[s, n], sem.at[s])
                for n, w in enumerate((wg_hbm, wu_hbm, wd_hbm))]

    @pl.when(i == 0)
    def _():
        for c in fetch(be_ref[0], slot):
            c.start()

    @pl.when((i == 0) | (be_ref[i] != be_ref[jnp.maximum(i - 1, 0)]))
    def _():
        for c in fetch(be_ref[i], slot):
            c.wait()

        @pl.when(nx_ref[i] >= 0)
        def _():
            for c in fetch(nx_ref[i], 1 - slot):
                c.start()

        for n in range(3):
            w_bf[n] = w_f32[slot, n].astype(BF16)

    @pl.when(i < nu_ref[0])
    def _():
        x = _load_row_tiles(xs_ref, EXPERT_BLOCK).astype(BF16)
        g = jnp.minimum(_dot(x, w_bf[0]) + bg_ref[...], SWIGLU_LIMIT)
        u = jnp.clip(_dot(x, w_bf[1]) + bu_ref[...], -SWIGLU_LIMIT, SWIGLU_LIMIT)
        a = g * jax.nn.sigmoid(SWIGLU_ALPHA * g) * (u + 1.0)
        _store_row_tiles(y_ref, _dot(a.astype(BF16), w_bf[2]) + bd_ref[...])

    @pl.when(i >= nu_ref[0])
    def _():
        y_ref[...] = jnp.zeros(y_ref.shape, y_ref.dtype)


def _ffn(xs, blk_meta, layer, w_gate, b_gate, w_up, b_up, w_down, b_down):
    n_rows = xs.shape[0] // ROW_TILE
    _, e, d, de = w_gate.shape
    assert d == de == D_MODEL
    n_blk = n_rows // EXPERT_BLOCK
    blk = (EXPERT_BLOCK * ROW_TILE, LANES)
    bspec = lambda c: pl.BlockSpec((None, None, 1, c), lambda i, be, nu, nx, sl: (layer, be[i], 0, 0))
    hbm = pl.BlockSpec(memory_space=pl.ANY)
    xs_map = lambda i, be, nu, nx, sl: (jnp.minimum(i, jnp.maximum(nu[0] - 1, 0)), 0)
    return pl.pallas_call(
        functools.partial(_ffn_body, layer),
        grid_spec=pltpu.PrefetchScalarGridSpec(
            num_scalar_prefetch=4,
            grid=(n_blk,),
            in_specs=[
                pl.BlockSpec(blk, xs_map),
                hbm, bspec(de), hbm, bspec(de), hbm, bspec(d),
            ],
            out_specs=pl.BlockSpec(blk, lambda i, be, nu, nx, sl: (i, 0)),
            scratch_shapes=[pltpu.VMEM((2, 3, d, de), F32), pltpu.VMEM((3, d, de), BF16),
                            pltpu.SemaphoreType.DMA((2,))],
        ),
        out_shape=SDS((n_rows * ROW_TILE, LANES), F32),
        compiler_params=_cparams(("arbitrary",)),
        name="ffn",
    )(*blk_meta, xs, w_gate, b_gate.reshape(-1, e, 1, de), w_up, b_up.reshape(-1, e, 1, de),
      w_down, b_down.reshape(-1, e, 1, d))


COMBINE_RC = 128


def _combine_body(dest_ref, x_ref, mod_ref, p_ref, y_ref, o_ref, buf, sem):
    tt = x_ref.shape[0]

    def issue(t, carry):
        t8 = pl.multiple_of(t * ROW_TILE, ROW_TILE)
        for k in range(TOP_K):
            d = pl.multiple_of(dest_ref[0, t * TOP_K + k], ROW_TILE)
            pltpu.make_async_copy(y_ref.at[pl.ds(d, ROW_TILE)], buf.at[k, pl.ds(t8, ROW_TILE)],
                                  sem).start(priority=k % 2)
        return carry

    lax.fori_loop(0, tt, issue, 0)
    pltpu.make_async_copy(buf, buf, sem).wait()

    g2 = mod_ref[0][5:6]
    for r0 in range(0, tt, COMBINE_RC):
        rows = slice(r0, r0 + COMBINE_RC)
        p = p_ref[rows, :]
        for j in range(ROW_TILE):
            cols = slice(j * LANES, (j + 1) * LANES)
            tile = lambda k: buf[k, pl.ds(r0 * ROW_TILE + j, COMBINE_RC, stride=ROW_TILE), :]
            acc = p[:, 0:1] * tile(0)
            for k in range(1, TOP_K):
                acc = acc + p[:, k:k + 1] * tile(k)
            o_ref[rows, cols] = x_ref[rows, cols] + g2[:, cols] * acc


def _combine(x, mod, p, y, dest8, seq):
    n, d = x.shape
    tt = COMBINE_TT
    n_steps = n // tt
    dest3 = dest8.reshape(n_steps, 1, tt * TOP_K)
    per_b = seq // tt
    return pl.pallas_call(
        _combine_body,
        grid=(n_steps,),
        in_specs=[
            pl.BlockSpec((None, 1, tt * TOP_K), lambda i: (i, 0, 0), memory_space=pltpu.SMEM),
            pl.BlockSpec((tt, d), lambda i: (i, 0)),
            pl.BlockSpec((1, 6, d), lambda i: (i // per_b, 0, 0)),
            pl.BlockSpec((tt, TOP_K), lambda i: (i, 0)),
            pl.BlockSpec(memory_space=pl.ANY),
        ],
        out_specs=pl.BlockSpec((tt, d), lambda i: (i, 0)),
        out_shape=SDS((n, d), F32),
        scratch_shapes=[pltpu.VMEM((TOP_K, tt * ROW_TILE, LANES), F32), pltpu.SemaphoreType.DMA(())],
        compiler_params=_cparams(("arbitrary",)),
        name="combine",
    )(dest3, x, mod, p, y)


def _moe(x, mod, layer, n2g, w_r, b_r, w_gate, b_gate, w_up, b_up, w_down, b_down):
    bsz, s, d = x.shape
    n = bsz * s
    h2, top_idx, probs, rank, counts = _router(x, mod, n2g, w_r, b_r)
    dest, blk_meta, pad_blk, n_rows = _route_meta(
        top_idx.reshape(n, TOP_K), rank.reshape(n, TOP_K), counts.reshape(N_EXPERTS))
    dest8 = dest * ROW_TILE
    xs = _dispatch(h2, dest8, pad_blk, n_rows)
    y = _ffn(xs, blk_meta, layer, w_gate, b_gate, w_up, b_up, w_down, b_down)
    out = _combine(x.reshape(n, d), mod, probs.reshape(n, TOP_K), y, dest8, s)
    return out.reshape(bsz, s, d)


def _seg_mats():
    r = lax.broadcasted_iota(jnp.int32, (D_MODEL, N_HEADS), 0) // HEAD_DIM
    c = lax.broadcasted_iota(jnp.int32, (D_MODEL, N_HEADS), 1)
    seg = jnp.where(r == c, 1.0, 0.0).astype(BF16)
    rt = lax.broadcasted_iota(jnp.int32, (N_HEADS, D_MODEL), 0)
    ct = lax.broadcasted_iota(jnp.int32, (N_HEADS, D_MODEL), 1) // HEAD_DIM
    seg_t = jnp.where(rt == ct, 1.0, 0.0).astype(BF16)
    return seg, seg_t


def _split_dot(a, b):
    hi = a.astype(BF16)
    lo = (a - hi.astype(F32)).astype(BF16)
    return _dot(hi, b) + _dot(lo, b)


def _head_rmsnorm(x, g, seg, seg_t):
    ms = _split_dot(x * x, seg) * (1.0 / HEAD_DIM)
    r = lax.rsqrt(ms + EPS)
    return x * _split_dot(r, seg_t) * g


def _kvq_body(x_ref, mod_ref, kvmod_ref, n1g_ref, kvg_ref, wq_ref, wkv_ref, qg_ref, kg_ref,
              q_ref, k_ref, vt_ref, km_ref):
    x = x_ref[0]
    mod = mod_ref[0]
    kvmod = kvmod_ref[0]
    ms = jnp.mean(x * x, axis=-1, keepdims=True)
    xn = x * lax.rsqrt(ms + EPS)
    hq = ((xn * n1g_ref[...]) * (1.0 + mod[1:2]) + mod[0:1]).astype(BF16)
    hk = ((xn * kvg_ref[...]) * (1.0 + kvmod[1:2]) + kvmod[0:1]).astype(BF16)
    seg, seg_t = _seg_mats()
    q = _head_rmsnorm(_dot(hq, wq_ref[...]), qg_ref[...], seg, seg_t)
    q_ref[0] = q
    kv = _dot(hk, wkv_ref[...])
    k = _head_rmsnorm(kv[:, :D_MODEL], kg_ref[...], seg, seg_t)
    k_ref[0, 0] = k.astype(BF16)
    for u in range(ATT_U):
        rows = slice(u * MOBA_BLOCK, (u + 1) * MOBA_BLOCK)
        km_ref[0, 0, u:u + 1, :] = jnp.mean(k[rows], axis=0, keepdims=True)
    vt_ref[0, 0] = kv[:, D_MODEL:].T.astype(BF16)


def _kvq(x, mod, kvmod, n1g, kvg, w_q, w_kv, q_g, k_g):
    bsz, s, d = x.shape
    ts = ATT_U * MOBA_BLOCK
    nb = s // ts
    const = lambda *shape: pl.BlockSpec(shape, lambda b, i: (0,) * len(shape))
    return pl.pallas_call(
        _kvq_body,
        grid=(bsz, nb),
        in_specs=[
            pl.BlockSpec((1, ts, d), lambda b, i: (b, i, 0)),
            pl.BlockSpec((1, 6, d), lambda b, i: (b, 0, 0)),
            pl.BlockSpec((1, 2, d), lambda b, i: (b, 0, 0)),
            const(1, d), const(1, d), const(d, d), const(d, 2 * d), const(1, d), const(1, d),
        ],
        out_specs=[
            pl.BlockSpec((1, ts, d), lambda b, i: (b, i, 0)),
            pl.BlockSpec((1, 1, ts, d), lambda b, i: (b, i, 0, 0)),
            pl.BlockSpec((1, 1, d, ts), lambda b, i: (b, i, 0, 0)),
            pl.BlockSpec((1, 1, ATT_U, d), lambda b, i: (b, i, 0, 0)),
        ],
        out_shape=[
            SDS((bsz, s, d), F32),
            SDS((bsz, nb, ts, d), BF16),
            SDS((bsz, nb, d, ts), BF16),
            SDS((bsz, nb, ATT_U, d), F32),
        ],
        compiler_params=_cparams(("parallel", "arbitrary")),
        name="kvq",
    )(x, mod, kvmod, n1g.reshape(1, d), kvg.reshape(1, d), w_q.astype(BF16), w_kv.astype(BF16),
      jnp.tile(q_g, N_HEADS).reshape(1, d), jnp.tile(k_g, N_HEADS).reshape(1, d))


def _bf16_pieces(v):
    hi = v.astype(BF16).astype(F32)
    mid = (v - hi).astype(BF16).astype(F32)
    lo = ((v - hi) - mid).astype(BF16).astype(F32)
    return hi, mid, lo


def _attn_body(slopes_ref, q_ref, k_ref, vt_ref, km_ref, o_ref, cm_scr, kaug_scr, qaug_scr, s_scr):
    hp = pl.program_id(1)
    own = pl.program_id(2)
    bs = MOBA_BLOCK
    nb = k_ref.shape[1] * ATT_U
    pw = 2 * HEAD_DIM
    nh = 2 * ATT_HP

    @pl.when(own == 0)
    def _():
        rel = (lax.broadcasted_iota(jnp.int32, (bs, bs), 1)
               - lax.broadcasted_iota(jnp.int32, (bs, bs), 0))
        cm_scr[0] = jnp.zeros((bs, bs), F32)
        cm_scr[1] = jnp.where(rel >= 0, 0.0, -jnp.inf)
        klane = lax.broadcasted_iota(jnp.int32, (ATT_U * bs, ATT_AUG), 1)
        koff = (lax.broadcasted_iota(jnp.int32, (ATT_U * bs, ATT_AUG), 0) % bs).astype(F32)
        kaug_scr[...] = jnp.where(klane < 3, 1.0, jnp.where(klane < 6, koff, 0.0)).astype(BF16)
        qlane = lax.broadcasted_iota(jnp.int32, (bs, ATT_AUG), 1)
        qoff = lax.broadcasted_iota(jnp.int32, (bs, ATT_AUG), 0).astype(F32)
        for hd in range(nh):
            s2 = jnp.full((bs, ATT_AUG), slopes_ref[nh * hp + hd], F32) * LOG2E
            pieces = _bf16_pieces(-(s2 * qoff)) + _bf16_pieces(s2)
            aug = jnp.zeros((bs, ATT_AUG), F32)
            for lane_ix, piece in enumerate(pieces):
                aug = jnp.where(qlane == lane_ix, piece, aug)
            qaug_scr[hd] = aug.astype(BF16)

    dcol = lax.broadcasted_iota(jnp.int32, (1, pw), 1)
    blk_row = lax.broadcasted_iota(jnp.int32, (nb, 1), 0)
    valid = blk_row < own
    in_head = [(dcol >= h * HEAD_DIM) & (dcol < (h + 1) * HEAD_DIM) for h in range(2)]
    pair_cols = lambda hd: slice((hd // 2) * pw, (hd // 2 + 1) * pw)

    gates = []
    for pair in range(nh // 2):
        qp = q_ref[0, :, pair_cols(2 * pair)]
        kmp = km_ref[0, :, pair_cols(2 * pair)]
        km2 = jnp.concatenate([jnp.where(in_head[h], kmp, 0.0) for h in range(2)], axis=0)
        gates.append(lax.dot_general(km2, qp, NT_DIMS, precision=HIGHEST, preferred_element_type=F32))

    qs, selbias = [], []
    for hd in range(nh):
        slope = slopes_ref[nh * hp + hd]
        qm = jnp.where(in_head[hd % 2], q_ref[0, :, pair_cols(hd)], 0.0)
        gate = jnp.where(valid, gates[hd // 2][(hd % 2) * nb:(hd % 2 + 1) * nb], -jnp.inf)
        rank = jnp.zeros(gate.shape, F32)
        for i in range(nb):
            gi = gate[i:i + 1, :]
            tie = jnp.where(blk_row > i, 1.0, 0.0)
            rank = rank + jnp.where(gi > gate, 1.0, jnp.where(gi == gate, tie, 0.0))
        chosen = jnp.where(valid, jnp.where(rank < MOBA_TOPK, 1.0, 0.0),
                           jnp.where(blk_row == own, 1.0, 0.0))
        off = (own - blk_row).astype(F32) * (bs * LOG2E * slope)
        selbias.append(jnp.where(chosen > 0.5, -off, -jnp.inf))
        qs.append(jnp.concatenate([(qm * (LOG2E * HEAD_DIM ** -0.5)).astype(BF16), qaug_scr[hd]], axis=1))

    last = k_ref.shape[1] - 1
    ones = jnp.ones((ATT_ONES, ATT_U * bs), BF16)

    def qk(it, hd):
        kc = jnp.concatenate([k_ref[0, it, :, pair_cols(hd)], kaug_scr[...]], axis=1)
        return lax.dot_general(kc, qs[hd], NT_DIMS, preferred_element_type=F32)

    def visit(it, slot, carry, next_it, has_own):
        out = []
        if next_it is not None:
            for hd in range(nh):
                s_scr[1 - slot, hd] = qk(jnp.minimum(next_it, last), hd)
        for hd in range(nh):
            m, acc = carry[hd]

            def tile(u):
                t = s_scr[slot, hd, u * bs:(u + 1) * bs]
                if has_own:
                    t = t + cm_scr[(it * ATT_U + u == own).astype(jnp.int32)]
                return t

            rows = [jnp.sum(jnp.where(blk_row == it * ATT_U + u, selbias[hd], 0.0), axis=0, keepdims=True)
                    for u in range(ATT_U)]
            mx = m
            for u in range(ATT_U):
                mx = jnp.maximum(mx, jnp.max(tile(u), axis=0, keepdims=True) + rows[u])
            m_safe = jnp.where(mx == -jnp.inf, 0.0, mx)
            alpha = jnp.exp2(m - m_safe)
            p = jnp.concatenate([jnp.exp2((tile(u) + (rows[u] - m_safe)).astype(BF16)) for u in range(ATT_U)],
                                axis=0)
            vj = jnp.concatenate([vt_ref[0, it, hd * HEAD_DIM:(hd + 1) * HEAD_DIM, :], ones], axis=0)
            acc = alpha * acc + _dot(vj, p)
            out.append((mx, acc))
        return tuple(out)

    def body(i2, carry):
        carry = visit(2 * i2, 1, carry, 2 * i2 + 1, False)
        return visit(2 * i2 + 1, 0, carry, 2 * i2 + 2, False)

    n_past = own // ATT_U
    for hd in range(nh):
        s_scr[0, hd] = qk(n_past, hd)
    init = tuple((jnp.full((1, bs), -jnp.inf, F32), jnp.zeros((HEAD_DIM + ATT_ONES, bs), F32))
                 for hd in range(nh))
    carry = visit(n_past, 0, init, 0, True)
    carry = lax.fori_loop(0, n_past // 2, body, carry)
    carry = lax.cond(n_past % 2 == 1, lambda c: visit(n_past - 1, 1, c, None, False), lambda c: c, carry)
    outs = [acc[:HEAD_DIM] / acc[HEAD_DIM:HEAD_DIM + 1] for (_, acc) in carry]
    o_ref[0] = jnp.concatenate(outs, axis=0).astype(o_ref.dtype)


def _attn(q, kb, vtb, kmean):
    bsz, s, d = q.shape
    nsb = s // (ATT_U * MOBA_BLOCK)
    assert nsb % 2 == 0, "the attention loop visits key super-blocks in pairs"
    nb = s // MOBA_BLOCK
    w = 2 * HEAD_DIM * ATT_HP
    slopes = (2.0 ** (-8.0 * jnp.arange(1, N_HEADS + 1, dtype=F32) / N_HEADS)).astype(F32)
    return pl.pallas_call(
        _attn_body,
        grid_spec=pltpu.PrefetchScalarGridSpec(
            num_scalar_prefetch=1,
            grid=(bsz, N_HEADS // (2 * ATT_HP), nb),
            in_specs=[
                pl.BlockSpec((1, MOBA_BLOCK, w), lambda b, h, c, sl: (b, c, h)),
                pl.BlockSpec((1, nsb, ATT_U * MOBA_BLOCK, w), lambda b, h, c, sl: (b, 0, 0, h)),
                pl.BlockSpec((1, nsb, w, ATT_U * MOBA_BLOCK), lambda b, h, c, sl: (b, 0, h, 0)),
                pl.BlockSpec((1, nb, w), lambda b, h, c, sl: (b, 0, h)),
            ],
            out_specs=pl.BlockSpec((1, w, MOBA_BLOCK), lambda b, h, c, sl: (b, h, c)),
            scratch_shapes=[pltpu.VMEM((2, MOBA_BLOCK, MOBA_BLOCK), F32),
                            pltpu.VMEM((ATT_U * MOBA_BLOCK, ATT_AUG), BF16),
                            pltpu.VMEM((2 * ATT_HP, MOBA_BLOCK, ATT_AUG), BF16),
                            pltpu.VMEM((2, 2 * ATT_HP, ATT_U * MOBA_BLOCK, MOBA_BLOCK), F32)],
        ),
        out_shape=SDS((bsz, d, s), BF16),
        compiler_params=_cparams(("parallel", "parallel", "arbitrary")),
        name="attn",
    )(slopes, q, kb, vtb, kmean)


OPROJ_TS = 512


def _oproj_body(at_ref, x_ref, mod_ref, wo_ref, o_ref):
    g1 = mod_ref[0][2:3]
    m = lax.dot_general(at_ref[0], wo_ref[...], (((0,), (0,)), ((), ())), preferred_element_type=F32)
    o_ref[0] = x_ref[0] + g1 * m


def _oproj(at, x, mod, w_o):
    bsz, s, d = x.shape
    ts = min(OPROJ_TS, s)
    tok = pl.BlockSpec((1, ts, d), lambda b, i: (b, i, 0))
    return pl.pallas_call(
        _oproj_body,
        grid=(bsz, s // ts),
        in_specs=[pl.BlockSpec((1, d, ts), lambda b, i: (b, 0, i)), tok,
                  pl.BlockSpec((1, 6, d), lambda b, i: (b, 0, 0)),
                  pl.BlockSpec((d, d), lambda b, i: (0, 0))],
        out_specs=tok,
        out_shape=SDS((bsz, s, d), F32),
        compiler_params=_cparams(("parallel", "arbitrary")),
        name="oproj",
    )(at, x, mod, w_o.astype(BF16))


def kernel(x, c, ada_w, ada_b, norm1_g, norm2_g, sgu_w_in, sgu_b_in, sgu_v_g, sgu_w_s, sgu_b_s, sgu_w_out,
           kv_ada_w, kv_ada_b, kv_norm_g, w_kv, k_norm_g, attn_w_q, q_norm_g, attn_w_o,
           moe_w_router, moe_b_router, moe_w_gate, moe_b_gate, moe_w_up, moe_b_up, moe_w_down, moe_b_down):
    bsz, s, d = x.shape
    moe = lambda l, xx, mod: _moe(xx, mod, l, norm2_g[l], moe_w_router[l], moe_b_router[l],
                                  moe_w_gate, moe_b_gate, moe_w_up, moe_b_up, moe_w_down, moe_b_down)
    mod0 = _ada(c, ada_w, ada_b[0], 0).reshape(bsz, 6, d)
    x = _sgu(x, mod0, norm1_g[0], sgu_w_in[0], sgu_b_in[0], sgu_v_g[0], sgu_w_s[0], sgu_b_s[0], sgu_w_out[0])
    x = moe(0, x, mod0)
    mod1 = _ada(c, ada_w, ada_b[1], 1).reshape(bsz, 6, d)
    kvmod = _ada(c, kv_ada_w, kv_ada_b).reshape(bsz, 2, d)
    q, kb, vtb, kmean = _kvq(x, mod1, kvmod, norm1_g[1], kv_norm_g, attn_w_q[0], w_kv, q_norm_g[0], k_norm_g)
    a = _attn(q, kb, vtb, kmean.reshape(bsz, s // MOBA_BLOCK, d))
    x = _oproj(a, x, mod1, attn_w_o[0])
    x = moe(1, x, mod1)
    return x
```

```python
import functools
import math

import jax
import jax.numpy as jnp
import numpy as np
from jax import lax
from jax.experimental import pallas as pl
from jax.experimental.pallas import tpu as pltpu

F32 = jnp.float32
BF16 = jnp.bfloat16
HIGHEST = lax.Precision.HIGHEST
SDS = jax.ShapeDtypeStruct

D_MODEL = 1024
SGU_CHUNK = 128
SGU_GROUPS = 8
D_SGU = 3 * D_MODEL
SGU_GROUP_DIM = D_SGU // SGU_GROUPS
HEAD_DIM = 64
N_HEADS = D_MODEL // HEAD_DIM
MOBA_BLOCK = 256
MOBA_TOPK = 3
ATT_U = 2
ATT_ONES = 16
ATT_AUG = 128
ATT_HP = 4
LOG2E = math.log2(math.e)
N_EXPERTS = 32
TOP_K = 4
SWIGLU_LIMIT = 7.0
SWIGLU_ALPHA = 1.702
EXPERT_BLOCK = 512
EPS = 1e-6

VMEM_LIMIT_BYTES = 56 * 1024 * 1024
NT_DIMS = (((1,), (1,)), ((), ()))


def _cparams(sem):
    return pltpu.CompilerParams(dimension_semantics=sem, vmem_limit_bytes=VMEM_LIMIT_BYTES)


def _dot(a, b, **kw):
    return jnp.dot(a, b, preferred_element_type=F32, **kw)


LANES = 128
ROW_TILE = D_MODEL // LANES


def _store_row_tiles(ref, val):
    rows = val.shape[0]
    for j in range(ROW_TILE):
        ref[pl.ds(j, rows, stride=ROW_TILE), :] = val[:, j * LANES:(j + 1) * LANES]


def _load_row_tiles(ref, rows):
    return jnp.concatenate([ref[pl.ds(j, rows, stride=ROW_TILE), :] for j in range(ROW_TILE)], axis=-1)


def _dot_3pass_nt(a, b):
    a_hi = a.astype(BF16)
    a_lo = (a - a_hi.astype(F32)).astype(BF16)
    b_hi = b.astype(BF16)
    b_lo = (b - b_hi.astype(F32)).astype(BF16)
    nt = lambda u, v: lax.dot_general(u, v, NT_DIMS, preferred_element_type=F32)
    return nt(a_hi, b_hi) + (nt(a_lo, b_hi) + nt(a_hi, b_lo))


def _norm_mod(x, g, sc, sh):
    ms = jnp.mean(x * x, axis=-1, keepdims=True)
    return (x * lax.rsqrt(ms + EPS) * g) * (1.0 + sc) + sh


def _ada_body(c_ref, w_ref, b_ref, o_ref):
    c = c_ref[...]
    ca = c * jax.nn.sigmoid(c)
    o_ref[...] = _dot(ca, w_ref[...], precision=HIGHEST) + b_ref[...]


def _ada(c, w, b, layer=0):
    bsz, d = c.shape
    n = w.shape[-1]
    tn = 1024
    w = w.reshape(-1, d, n)
    return pl.pallas_call(
        _ada_body,
        grid=(n // tn,),
        in_specs=[
            pl.BlockSpec((bsz, d), lambda j: (0, 0)),
            pl.BlockSpec((None, d, tn), lambda j: (layer, 0, j)),
            pl.BlockSpec((1, tn), lambda j: (0, j)),
        ],
        out_specs=pl.BlockSpec((bsz, tn), lambda j: (0, j)),
        out_shape=SDS((bsz, n), F32),
        compiler_params=_cparams(("arbitrary",)),
        name="ada",
    )(c, w, b.reshape(1, n))


SGU_TS = 512
SGU_CB = 768


def _gelu_tanh(x):
    c = np.float32(np.sqrt(2.0 / np.pi))
    return x * (0.5 * (1.0 + jnp.tanh(c * (x + 0.044715 * (x * x * x)))))


def _sgu_body(x_ref, mod_ref, n1g_ref, win_ref, bin_ref, vg_ref, ws_ref, bst_ref, wout_ref,
              o_ref, u_scr, v_scr, y_scr):
    ts = x_ref.shape[1]
    x = x_ref[0]
    mod = mod_ref[0]
    sh1, sc1, g1 = mod[0:1], mod[1:2], mod[2:3]
    h = _norm_mod(x, n1g_ref[...], sc1, sh1).astype(BF16)

    for j in range(D_SGU // SGU_CB):
        cs = slice(j * SGU_CB, (j + 1) * SGU_CB)
        u_scr[:, cs] = _gelu_tanh(_dot(h, win_ref[:, cs]) + bin_ref[:, cs])
    ssq = jnp.zeros((ts, 1), F32)
    for j in range(D_SGU // SGU_CB):
        cs = slice(j * SGU_CB, (j + 1) * SGU_CB)
        ws_cols = slice(D_SGU + j * SGU_CB, D_SGU + (j + 1) * SGU_CB)
        z = _gelu_tanh(_dot(h, win_ref[:, ws_cols]) + bin_ref[:, ws_cols])
        v_scr[:, cs] = z
        ssq = ssq + jnp.sum(z * z, axis=-1, keepdims=True)
    rs = lax.rsqrt(ssq / D_SGU + EPS)

    r_i = lax.broadcasted_iota(jnp.int32, (SGU_CHUNK, SGU_CHUNK), 0)
    c_i = lax.broadcasted_iota(jnp.int32, (SGU_CHUNK, SGU_CHUNK), 1)
    causal = c_i <= r_i
    for g in range(SGU_GROUPS):
        cs = slice(g * SGU_GROUP_DIM, (g + 1) * SGU_GROUP_DIM)
        wsg = jnp.where(causal, ws_ref[g], 0.0).astype(BF16)
        bsg = bst_ref[:, g:g + 1]
        for n in range(ts // SGU_CHUNK):
            rows = slice(n * SGU_CHUNK, (n + 1) * SGU_CHUNK)
            vn = (v_scr[rows, cs] * rs[rows] * vg_ref[:, cs]).astype(BF16)
            mixed = _dot(wsg, vn) + bsg
            y_scr[rows, cs] = (u_scr[rows, cs] * mixed).astype(BF16)

    m = _dot(y_scr[...], wout_ref[...])
    o_ref[0] = x + g1 * m


def _sgu(x, mod, n1g, w_in, b_in, v_g, w_s, b_s, w_out):
    bsz, s, d = x.shape
    ts = SGU_TS
    const = lambda *shape: pl.BlockSpec(shape, lambda b, i: (0,) * len(shape),
                                        pipeline_mode=pl.Buffered(1))
    return pl.pallas_call(
        _sgu_body,
        grid=(bsz, s // ts),
        in_specs=[
            pl.BlockSpec((1, ts, d), lambda b, i: (b, i, 0)),
            pl.BlockSpec((1, 6, d), lambda b, i: (b, 0, 0)),
            const(1, d),
            const(d, 2 * D_SGU),
            const(1, 2 * D_SGU),
            const(1, D_SGU),
            const(SGU_GROUPS, SGU_CHUNK, SGU_CHUNK),
            const(SGU_CHUNK, SGU_GROUPS),
            const(D_SGU, d),
        ],
        out_specs=pl.BlockSpec((1, ts, d), lambda b, i: (b, i, 0)),
        out_shape=SDS((bsz, s, d), F32),
        scratch_shapes=[
            pltpu.VMEM((ts, D_SGU), F32),
            pltpu.VMEM((ts, D_SGU), F32),
            pltpu.VMEM((ts, D_SGU), BF16),
        ],
        compiler_params=_cparams(("parallel", "arbitrary")),
        name="sgu",
    )(x, mod, n1g.reshape(1, d), w_in.astype(BF16), b_in.reshape(1, -1), v_g.reshape(1, -1),
      w_s, b_s.T, w_out.astype(BF16))


ROUTER_TS = 512


def _router_body(x_ref, mod_ref, n2g_ref, wrt_ref, brc_ref, h_ref, idx_ref, p_ref, rk_ref, cnt_ref, run_scr):
    @pl.when((pl.program_id(0) == 0) & (pl.program_id(1) == 0))
    def _():
        run_scr[...] = jnp.zeros(run_scr.shape, run_scr.dtype)

    x = x_ref[0]
    mod = mod_ref[0]
    sh2, sc2 = mod[3:4], mod[4:5]
    h2 = _norm_mod(x, n2g_ref[...], sc2, sh2)
    _store_row_tiles(h_ref, h2)
    ts = x.shape[0]
    logits = _dot_3pass_nt(wrt_ref[...], h2) + brc_ref[...]
    row = lax.broadcasted_iota(jnp.int32, logits.shape, 0)
    work = logits
    vals, idxs = [], []
    for _ in range(TOP_K):
        mx = jnp.max(work, axis=0, keepdims=True)
        ix = jnp.min(jnp.where(work == mx, row, N_EXPERTS), axis=0, keepdims=True)
        vals.append(mx)
        idxs.append(ix)
        work = jnp.where(row == ix, -jnp.inf, work)
    e = jnp.exp(jnp.concatenate(vals, axis=0) - vals[0])
    p_t = e / jnp.sum(e, axis=0, keepdims=True)
    idx_ref[0] = jnp.concatenate(idxs, axis=0)

    onehots = [row == ix for ix in idxs]
    mask = jnp.zeros(logits.shape, F32)
    for oh in onehots:
        mask = mask + jnp.where(oh, 1.0, 0.0)
    r_i = lax.broadcasted_iota(jnp.int32, (ts, ts), 0)
    c_i = lax.broadcasted_iota(jnp.int32, (ts, ts), 1)
    before = _dot(mask.astype(BF16), jnp.where(r_i < c_i, 1.0, 0.0).astype(BF16)) + run_scr[...]
    rk = [jnp.sum(jnp.where(oh, before, 0.0), axis=0, keepdims=True) for oh in onehots]
    rk_ref[0] = jnp.concatenate(rk, axis=0).astype(jnp.int32)
    run_scr[...] = run_scr[...] + jnp.sum(mask, axis=1, keepdims=True)
    cnt_ref[...] = run_scr[...].astype(jnp.int32)

    eye = jnp.where(r_i == c_i, 1.0, 0.0).astype(BF16)
    p_ref[0] = sum(lax.dot_general(eye, piece.astype(BF16), NT_DIMS, preferred_element_type=F32)
                   for piece in _bf16_pieces(p_t))


def _router(x, mod, n2g, w_r, b_r):
    bsz, s, d = x.shape
    ts = min(ROUTER_TS, s)
    kt = pl.BlockSpec((1, TOP_K, ts), lambda b, i: (b, 0, i))
    return pl.pallas_call(
        _router_body,
        grid=(bsz, s // ts),
        in_specs=[
            pl.BlockSpec((1, ts, d), lambda b, i: (b, i, 0)),
            pl.BlockSpec((1, 6, d), lambda b, i: (b, 0, 0)),
            pl.BlockSpec((1, d), lambda b, i: (0, 0)),
            pl.BlockSpec((N_EXPERTS, d), lambda b, i: (0, 0)),
            pl.BlockSpec((N_EXPERTS, 1), lambda b, i: (0, 0)),
        ],
        out_specs=[pl.BlockSpec((ts * ROW_TILE, LANES), lambda b, i: (b * (s // ts) + i, 0)),
                   kt, pl.BlockSpec((1, ts, TOP_K), lambda b, i: (b, i, 0)), kt,
                   pl.BlockSpec((N_EXPERTS, 1), lambda b, i: (0, 0))],
        out_shape=[SDS((bsz * s * ROW_TILE, LANES), F32), SDS((bsz, TOP_K, s), jnp.int32),
                   SDS((bsz, s, TOP_K), F32),
                   SDS((bsz, TOP_K, s), jnp.int32), SDS((N_EXPERTS, 1), jnp.int32)],
        scratch_shapes=[pltpu.VMEM((N_EXPERTS, 1), F32)],
        compiler_params=_cparams(("arbitrary", "arbitrary")),
        name="router",
    )(x, mod, n2g.reshape(1, d), w_r.T, b_r.reshape(-1, 1))


def _route_meta(top_idx, rank, counts):
    n = top_idx.size // TOP_K
    padded = (counts + EXPERT_BLOCK - 1) // EXPERT_BLOCK * EXPERT_BLOCK
    pend = jnp.cumsum(padded)
    pstart = pend - padded
    dest = (pstart[top_idx] + rank).astype(jnp.int32)
    n_rows = n * TOP_K + N_EXPERTS * EXPERT_BLOCK
    n_blk = n_rows // EXPERT_BLOCK
    blk_row0 = jnp.arange(n_blk) * EXPERT_BLOCK
    blk_exp = jnp.minimum(jnp.sum(pend[None, :] <= blk_row0[:, None], axis=1), N_EXPERTS - 1)
    n_used = (pend[-1] // EXPERT_BLOCK).astype(jnp.int32).reshape(1)
    eids = jnp.arange(N_EXPERTS)
    has_rows = counts > 0
    last_exp = jnp.max(jnp.where(has_rows, eids, 0))
    blk_exp = jnp.where(blk_row0 < pend[-1], blk_exp, last_exp).astype(jnp.int32)
    nxt_e = jnp.min(jnp.where((eids[None, :] > eids[:, None]) & has_rows[None, :], eids[None, :], N_EXPERTS), axis=1)
    nxt_e = jnp.where(nxt_e < N_EXPERTS, nxt_e, -1)
    slot_e = (jnp.cumsum(has_rows.astype(jnp.int32)) - 1) % 2
    blk_nxt = nxt_e[blk_exp].astype(jnp.int32)
    blk_slot = slot_e[blk_exp].astype(jnp.int32)
    pad_blk = jnp.where(padded > 0, pend - EXPERT_BLOCK, -1)
    tail = pend[-1] + jnp.arange(N_EXPERTS) * EXPERT_BLOCK
    tail_blk = jnp.where(tail < n_rows, tail, -1)
    pad_blk = jnp.concatenate([pad_blk, tail_blk]).astype(jnp.int32)
    return dest, (blk_exp, n_used, blk_nxt, blk_slot), pad_blk, n_rows


MOVE_TT = 1024
COMBINE_TT = 1024


def _per_step(dest, tt):
    bsz, k, s = dest.shape
    return dest.reshape(bsz, k, s // tt, tt).transpose(0, 2, 1, 3).reshape(bsz * (s // tt), 1, k * tt)


def _dispatch_body(pad_ref, dest_ref, h_ref, xs_ref, zero_scr, sem, zsem):
    tt = h_ref.shape[0] // ROW_TILE
    blk = EXPERT_BLOCK * ROW_TILE

    @pl.when(pl.program_id(0) == 0)
    def _():
        zero_scr[...] = jnp.zeros(zero_scr.shape, zero_scr.dtype)

        def zero_copy(e):
            row0 = pl.multiple_of(pad_ref[e], blk)
            return pltpu.make_async_copy(zero_scr, xs_ref.at[pl.ds(row0, blk)], zsem)

        for e in range(2 * N_EXPERTS):
            pl.when(pad_ref[e] >= 0)(lambda e=e: zero_copy(e).start())
        for e in range(2 * N_EXPERTS):
            pl.when(pad_ref[e] >= 0)(lambda e=e: zero_copy(e).wait())

    def issue(t, carry):
        src = h_ref.at[pl.ds(pl.multiple_of(t * ROW_TILE, ROW_TILE), ROW_TILE)]
        for k in range(TOP_K):
            d = pl.multiple_of(dest_ref[0, k * tt + t], ROW_TILE)
            pltpu.make_async_copy(src, xs_ref.at[pl.ds(d, ROW_TILE)], sem).start(priority=k % 2)
        return carry

    lax.fori_loop(0, tt, issue, 0)
    for k in range(TOP_K):
        pltpu.make_async_copy(h_ref, h_ref, sem).wait()


def _dispatch(h, dest8, pad_blk, n_rows):
    n = h.shape[0] // ROW_TILE
    tt = min(MOVE_TT, dest8.shape[-1])
    return pl.pallas_call(
        _dispatch_body,
        grid_spec=pltpu.PrefetchScalarGridSpec(
            num_scalar_prefetch=1,
            grid=(n // tt,),
            in_specs=[
                pl.BlockSpec((None, 1, TOP_K * tt), lambda i, pad: (i, 0, 0), memory_space=pltpu.SMEM),
                pl.BlockSpec((tt * ROW_TILE, LANES), lambda i, pad: (i, 0)),
            ],
            out_specs=pl.BlockSpec(memory_space=pl.ANY),
            scratch_shapes=[pltpu.VMEM((EXPERT_BLOCK * ROW_TILE, LANES), h.dtype), pltpu.SemaphoreType.DMA(()),
                            pltpu.SemaphoreType.DMA(())],
        ),
        out_shape=SDS((n_rows * ROW_TILE, LANES), h.dtype),
        compiler_params=_cparams(("arbitrary",)),
        name="dispatch",
    )(pad_blk * ROW_TILE, _per_step(dest8, tt), h)


def _ffn_body(layer, be_ref, nu_ref, nx_ref, sl_ref, xs_ref, wg_hbm, bg_ref, wu_hbm, bu_ref, wd_hbm, bd_ref,
              y_ref, w_f32, w_bf, sem):
    i = pl.program_id(0)
    slot = sl_ref[i]

    def fetch(e, s):
        return [pltpu.make_async_copy(w.at[layer, e], w_f32.at[s, n], sem.at[s])
                for n, w in enumerate((wg_hbm, wu_hbm, wd_hbm))]

    @pl.when(i == 0)
    def _():
        for c in fetch(be_ref[0], slot):
            c.start()

    @pl.when((i == 0) | (be_ref[i] != be_ref[jnp.maximum(i - 1, 0)]))
    def _():
        for c in fetch(be_ref[i], slot):
            c.wait()

        @pl.when(nx_ref[i] >= 0)
        def _():
            for c in fetch(nx_ref[i], 1 - slot):
                c.start()

        for n in range(3):
            w_bf[n] = w_f32[slot, n].astype(BF16)

    @pl.when(i < nu_ref[0])
    def _():
        x = _load_row_tiles(xs_ref, EXPERT_BLOCK).astype(BF16)
        g = jnp.minimum(_dot(x, w_bf[0]) + bg_ref[...], SWIGLU_LIMIT)
        u = jnp.clip(_dot(x, w_bf[1]) + bu_ref[...], -SWIGLU_LIMIT, SWIGLU_LIMIT)
        a = g * jax.nn.sigmoid(SWIGLU_ALPHA * g) * (u + 1.0)
        _store_row_tiles(y_ref, _dot(a.astype(BF16), w_bf[2]) + bd_ref[...])

    @pl.when(i >= nu_ref[0])
    def _():
        y_ref[...] = jnp.zeros(y_ref.shape, y_ref.dtype)


def _ffn(xs, blk_meta, layer, w_gate, b_gate, w_up, b_up, w_down, b_down):
    n_rows = xs.shape[0] // ROW_TILE
    _, e, d, de = w_gate.shape
    assert d == de == D_MODEL
    n_blk = n_rows // EXPERT_BLOCK
    blk = (EXPERT_BLOCK * ROW_TILE, LANES)
    bspec = lambda c: pl.BlockSpec((None, None, 1, c), lambda i, be, nu, nx, sl: (layer, be[i], 0, 0))
    hbm = pl.BlockSpec(memory_space=pl.ANY)
    xs_map = lambda i, be, nu, nx, sl: (jnp.minimum(i, jnp.maximum(nu[0] - 1, 0)), 0)
    return pl.pallas_call(
        functools.partial(_ffn_body, layer),
        grid_spec=pltpu.PrefetchScalarGridSpec(
            num_scalar_prefetch=4,
            grid=(n_blk,),
            in_specs=[
                pl.BlockSpec(blk, xs_map),
                hbm, bspec(de), hbm, bspec(de), hbm, bspec(d),
            ],
            out_specs=pl.BlockSpec(blk, lambda i, be, nu, nx, sl: (i, 0)),
            scratch_shapes=[pltpu.VMEM((2, 3, d, de), F32), pltpu.VMEM((3, d, de), BF16),
                            pltpu.SemaphoreType.DMA((2,))],
        ),
        out_shape=SDS((n_rows * ROW_TILE, LANES), F32),
        compiler_params=_cparams(("arbitrary",)),
        name="ffn",
    )(*blk_meta, xs, w_gate, b_gate.reshape(-1, e, 1, de), w_up, b_up.reshape(-1, e, 1, de),
      w_down, b_down.reshape(-1, e, 1, d))


COMBINE_RC = 128


def _combine_body(dest_ref, x_ref, mod_ref, p_ref, y_ref, o_ref, buf, sem):
    tt = x_ref.shape[0]

    def issue(t, carry):
        t8 = pl.multiple_of(t * ROW_TILE, ROW_TILE)
        for k in range(TOP_K):
            d = pl.multiple_of(dest_ref[0, k * tt + t], ROW_TILE)
            pltpu.make_async_copy(y_ref.at[pl.ds(d, ROW_TILE)], buf.at[k, pl.ds(t8, ROW_TILE)],
                                  sem).start(priority=k % 2)
        return carry

    lax.fori_loop(0, tt, issue, 0)
    pltpu.make_async_copy(buf, buf, sem).wait()

    g2 = mod_ref[0][5:6]
    for r0 in range(0, tt, COMBINE_RC):
        rows = slice(r0, r0 + COMBINE_RC)
        p = p_ref[rows, :]
        for j in range(ROW_TILE):
            cols = slice(j * LANES, (j + 1) * LANES)
            tile = lambda k: buf[k, pl.ds(r0 * ROW_TILE + j, COMBINE_RC, stride=ROW_TILE), :]
            acc = p[:, 0:1] * tile(0)
            for k in range(1, TOP_K):
                acc = acc + p[:, k:k + 1] * tile(k)
            o_ref[rows, cols] = x_ref[rows, cols] + g2[:, cols] * acc


def _combine(x, mod, p, y, dest8, seq):
    n, d = x.shape
    tt = min(COMBINE_TT, seq)
    n_steps = n // tt
    per_b = seq // tt
    return pl.pallas_call(
        _combine_body,
        grid=(n_steps,),
        in_specs=[
            pl.BlockSpec((None, 1, TOP_K * tt), lambda i: (i, 0, 0), memory_space=pltpu.SMEM),
            pl.BlockSpec((tt, d), lambda i: (i, 0)),
            pl.BlockSpec((1, 6, d), lambda i: (i // per_b, 0, 0)),
            pl.BlockSpec((tt, TOP_K), lambda i: (i, 0)),
            pl.BlockSpec(memory_space=pl.ANY),
        ],
        out_specs=pl.BlockSpec((tt, d), lambda i: (i, 0)),
        out_shape=SDS((n, d), F32),
        scratch_shapes=[pltpu.VMEM((TOP_K, tt * ROW_TILE, LANES), F32), pltpu.SemaphoreType.DMA(())],
        compiler_params=_cparams(("arbitrary",)),
        name="combine",
    )(_per_step(dest8, tt), x, mod, p, y)


def _moe(x, mod, layer, n2g, w_r, b_r, w_gate, b_gate, w_up, b_up, w_down, b_down):
    bsz, s, d = x.shape
    n = bsz * s
    h2, top_idx, probs, rank, counts = _router(x, mod, n2g, w_r, b_r)
    dest, blk_meta, pad_blk, n_rows = _route_meta(top_idx, rank, counts.reshape(N_EXPERTS))
    dest8 = dest * ROW_TILE
    xs = _dispatch(h2, dest8, pad_blk, n_rows)
    y = _ffn(xs, blk_meta, layer, w_gate, b_gate, w_up, b_up, w_down, b_down)
    out = _combine(x.reshape(n, d), mod, probs.reshape(n, TOP_K), y, dest8, s)
    return out.reshape(bsz, s, d)


def _seg_mats():
    r = lax.broadcasted_iota(jnp.int32, (D_MODEL, N_HEADS), 0) // HEAD_DIM
    c = lax.broadcasted_iota(jnp.int32, (D_MODEL, N_HEADS), 1)
    seg = jnp.where(r == c, 1.0, 0.0).astype(BF16)
    rt = lax.broadcasted_iota(jnp.int32, (N_HEADS, D_MODEL), 0)
    ct = lax.broadcasted_iota(jnp.int32, (N_HEADS, D_MODEL), 1) // HEAD_DIM
    seg_t = jnp.where(rt == ct, 1.0, 0.0).astype(BF16)
    return seg, seg_t


def _split_dot(a, b):
    hi = a.astype(BF16)
    lo = (a - hi.astype(F32)).astype(BF16)
    return _dot(hi, b) + _dot(lo, b)


def _head_rmsnorm(x, g, seg, seg_t):
    ms = _split_dot(x * x, seg) * (1.0 / HEAD_DIM)
    r = lax.rsqrt(ms + EPS)
    return x * _split_dot(r, seg_t) * g


def _kvq_body(x_ref, mod_ref, kvmod_ref, n1g_ref, kvg_ref, wq_ref, wkv_ref, qg_ref, kg_ref,
              q_ref, k_ref, vt_ref, km_ref):
    x = x_ref[0]
    mod = mod_ref[0]
    kvmod = kvmod_ref[0]
    ms = jnp.mean(x * x, axis=-1, keepdims=True)
    xn = x * lax.rsqrt(ms + EPS)
    hq = ((xn * n1g_ref[...]) * (1.0 + mod[1:2]) + mod[0:1]).astype(BF16)
    hk = ((xn * kvg_ref[...]) * (1.0 + kvmod[1:2]) + kvmod[0:1]).astype(BF16)
    seg, seg_t = _seg_mats()
    q = _head_rmsnorm(_dot(hq, wq_ref[...]), qg_ref[...], seg, seg_t)
    q_ref[0] = q
    kv = _dot(hk, wkv_ref[...])
    k = _head_rmsnorm(kv[:, :D_MODEL], kg_ref[...], seg, seg_t)
    k_ref[0, 0] = k.astype(BF16)
    for u in range(ATT_U):
        rows = slice(u * MOBA_BLOCK, (u + 1) * MOBA_BLOCK)
        km_ref[0, 0, u:u + 1, :] = jnp.mean(k[rows], axis=0, keepdims=True)
    vt_ref[0, 0] = kv[:, D_MODEL:].T.astype(BF16)


def _kvq(x, mod, kvmod, n1g, kvg, w_q, w_kv, q_g, k_g):
    bsz, s, d = x.shape
    ts = ATT_U * MOBA_BLOCK
    nb = s // ts
    const = lambda *shape: pl.BlockSpec(shape, lambda b, i: (0,) * len(shape))
    return pl.pallas_call(
        _kvq_body,
        grid=(bsz, nb),
        in_specs=[
            pl.BlockSpec((1, ts, d), lambda b, i: (b, i, 0)),
            pl.BlockSpec((1, 6, d), lambda b, i: (b, 0, 0)),
            pl.BlockSpec((1, 2, d), lambda b, i: (b, 0, 0)),
            const(1, d), const(1, d), const(d, d), const(d, 2 * d), const(1, d), const(1, d),
        ],
        out_specs=[
            pl.BlockSpec((1, ts, d), lambda b, i: (b, i, 0)),
            pl.BlockSpec((1, 1, ts, d), lambda b, i: (b, i, 0, 0)),
            pl.BlockSpec((1, 1, d, ts), lambda b, i: (b, i, 0, 0)),
            pl.BlockSpec((1, 1, ATT_U, d), lambda b, i: (b, i, 0, 0)),
        ],
        out_shape=[
            SDS((bsz, s, d), F32),
            SDS((bsz, nb, ts, d), BF16),
            SDS((bsz, nb, d, ts), BF16),
            SDS((bsz, nb, ATT_U, d), F32),
        ],
        compiler_params=_cparams(("parallel", "arbitrary")),
        name="kvq",
    )(x, mod, kvmod, n1g.reshape(1, d), kvg.reshape(1, d), w_q.astype(BF16), w_kv.astype(BF16),
      jnp.tile(q_g, N_HEADS).reshape(1, d), jnp.tile(k_g, N_HEADS).reshape(1, d))


def _bf16_pieces(v):
    hi = v.astype(BF16).astype(F32)
    mid = (v - hi).astype(BF16).astype(F32)
    lo = ((v - hi) - mid).astype(BF16).astype(F32)
    return hi, mid, lo


def _attn_body(slopes_ref, q_ref, k_ref, vt_ref, km_ref, o_ref, cm_scr, kaug_scr, qaug_scr, s_scr):
    hp = pl.program_id(1)
    own = pl.program_id(2)
    bs = MOBA_BLOCK
    nb = k_ref.shape[1] * ATT_U
    pw = 2 * HEAD_DIM
    nh = 2 * ATT_HP

    @pl.when(own == 0)
    def _():
        rel = (lax.broadcasted_iota(jnp.int32, (bs, bs), 1)
               - lax.broadcasted_iota(jnp.int32, (bs, bs), 0))
        cm_scr[0] = jnp.zeros((bs, bs), F32)
        cm_scr[1] = jnp.where(rel >= 0, 0.0, -jnp.inf)
        klane = lax.broadcasted_iota(jnp.int32, (ATT_U * bs, ATT_AUG), 1)
        koff = (lax.broadcasted_iota(jnp.int32, (ATT_U * bs, ATT_AUG), 0) % bs).astype(F32)
        kaug_scr[...] = jnp.where(klane < 3, 1.0, jnp.where(klane < 6, koff, 0.0)).astype(BF16)
        qlane = lax.broadcasted_iota(jnp.int32, (bs, ATT_AUG), 1)
        qoff = lax.broadcasted_iota(jnp.int32, (bs, ATT_AUG), 0).astype(F32)
        for hd in range(nh):
            s2 = jnp.full((bs, ATT_AUG), slopes_ref[nh * hp + hd], F32) * LOG2E
            pieces = _bf16_pieces(-(s2 * qoff)) + _bf16_pieces(s2)
            aug = jnp.zeros((bs, ATT_AUG), F32)
            for lane_ix, piece in enumerate(pieces):
                aug = jnp.where(qlane == lane_ix, piece, aug)
            qaug_scr[hd] = aug.astype(BF16)

    dcol = lax.broadcasted_iota(jnp.int32, (1, pw), 1)
    blk_row = lax.broadcasted_iota(jnp.int32, (nb, 1), 0)
    valid = blk_row < own
    in_head = [(dcol >= h * HEAD_DIM) & (dcol < (h + 1) * HEAD_DIM) for h in range(2)]
    pair_cols = lambda hd: slice((hd // 2) * pw, (hd // 2 + 1) * pw)

    gates = []
    for pair in range(nh // 2):
        qp = q_ref[0, :, pair_cols(2 * pair)]
        kmp = km_ref[0, :, pair_cols(2 * pair)]
        km2 = jnp.concatenate([jnp.where(in_head[h], kmp, 0.0) for h in range(2)], axis=0)
        gates.append(lax.dot_general(km2, qp, NT_DIMS, precision=HIGHEST, preferred_element_type=F32))

    qs, selbias = [], []
    for hd in range(nh):
        slope = slopes_ref[nh * hp + hd]
        qm = jnp.where(in_head[hd % 2], q_ref[0, :, pair_cols(hd)], 0.0)
        gate = jnp.where(valid, gates[hd // 2][(hd % 2) * nb:(hd % 2 + 1) * nb], -jnp.inf)
        rank = jnp.zeros(gate.shape, F32)
        for i in range(nb):
            gi = gate[i:i + 1, :]
            tie = jnp.where(blk_row > i, 1.0, 0.0)
            rank = rank + jnp.where(gi > gate, 1.0, jnp.where(gi == gate, tie, 0.0))
        chosen = jnp.where(valid, jnp.where(rank < MOBA_TOPK, 1.0, 0.0),
                           jnp.where(blk_row == own, 1.0, 0.0))
        off = (own - blk_row).astype(F32) * (bs * LOG2E * slope)
        selbias.append(jnp.where(chosen > 0.5, -off, -jnp.inf))
        qs.append(jnp.concatenate([(qm * (LOG2E * HEAD_DIM ** -0.5)).astype(BF16), qaug_scr[hd]], axis=1))

    last = k_ref.shape[1] - 1
    ones = jnp.ones((ATT_ONES, ATT_U * bs), BF16)

    def qk(it, hd):
        kc = jnp.concatenate([k_ref[0, it, :, pair_cols(hd)], kaug_scr[...]], axis=1)
        return lax.dot_general(kc, qs[hd], NT_DIMS, preferred_element_type=F32)

    def visit(it, slot, carry, next_it, has_own):
        out = []
        if next_it is not None:
            for hd in range(nh):
                s_scr[1 - slot, hd] = qk(jnp.minimum(next_it, last), hd)
        for hd in range(nh):
            m, acc = carry[hd]

            def tile(u):
                t = s_scr[slot, hd, u * bs:(u + 1) * bs]
                if has_own:
                    t = t + cm_scr[(it * ATT_U + u == own).astype(jnp.int32)]
                return t

            rows = [jnp.sum(jnp.where(blk_row == it * ATT_U + u, selbias[hd], 0.0), axis=0, keepdims=True)
                    for u in range(ATT_U)]
            mx = m
            for u in range(ATT_U):
                mx = jnp.maximum(mx, jnp.max(tile(u), axis=0, keepdims=True) + rows[u])
            m_safe = jnp.where(mx == -jnp.inf, 0.0, mx)
            alpha = jnp.exp2(m - m_safe)
            p = jnp.concatenate([jnp.exp2((tile(u) + (rows[u] - m_safe)).astype(BF16)) for u in range(ATT_U)],
                                axis=0)
            vj = jnp.concatenate([vt_ref[0, it, hd * HEAD_DIM:(hd + 1) * HEAD_DIM, :], ones], axis=0)
            acc = alpha * acc + _dot(vj, p)
            out.append((mx, acc))
        return tuple(out)

    def body(i2, carry):
        carry = visit(2 * i2, 1, carry, 2 * i2 + 1, False)
        return visit(2 * i2 + 1, 0, carry, 2 * i2 + 2, False)

    n_past = own // ATT_U
    for hd in range(nh):
        s_scr[0, hd] = qk(n_past, hd)
    init = tuple((jnp.full((1, bs), -jnp.inf, F32), jnp.zeros((HEAD_DIM + ATT_ONES, bs), F32))
                 for hd in range(nh))
    carry = visit(n_past, 0, init, 0, True)
    carry = lax.fori_loop(0, n_past // 2, body, carry)
    carry = lax.cond(n_past % 2 == 1, lambda c: visit(n_past - 1, 1, c, None, False), lambda c: c, carry)
    outs = [acc[:HEAD_DIM] / acc[HEAD_DIM:HEAD_DIM + 1] for (_, acc) in carry]
    o_ref[0] = jnp.concatenate(outs, axis=0).astype(o_ref.dtype)


def _attn(q, kb, vtb, kmean):
    bsz, s, d = q.shape
    nsb = s // (ATT_U * MOBA_BLOCK)
    assert nsb % 2 == 0, "the attention loop visits key super-blocks in pairs"
    nb = s // MOBA_BLOCK
    w = 2 * HEAD_DIM * ATT_HP
    slopes = (2.0 ** (-8.0 * jnp.arange(1, N_HEADS + 1, dtype=F32) / N_HEADS)).astype(F32)
    return pl.pallas_call(
        _attn_body,
        grid_spec=pltpu.PrefetchScalarGridSpec(
            num_scalar_prefetch=1,
            grid=(bsz, N_HEADS // (2 * ATT_HP), nb),
            in_specs=[
                pl.BlockSpec((1, MOBA_BLOCK, w), lambda b, h, c, sl: (b, c, h)),
                pl.BlockSpec((1, nsb, ATT_U * MOBA_BLOCK, w), lambda b, h, c, sl: (b, 0, 0, h)),
                pl.BlockSpec((1, nsb, w, ATT_U * MOBA_BLOCK), lambda b, h, c, sl: (b, 0, h, 0)),
                pl.BlockSpec((1, nb, w), lambda b, h, c, sl: (b, 0, h)),
            ],
            out_specs=pl.BlockSpec((1, w, MOBA_BLOCK), lambda b, h, c, sl: (b, h, c)),
            scratch_shapes=[pltpu.VMEM((2, MOBA_BLOCK, MOBA_BLOCK), F32),
                            pltpu.VMEM((ATT_U * MOBA_BLOCK, ATT_AUG), BF16),
                            pltpu.VMEM((2 * ATT_HP, MOBA_BLOCK, ATT_AUG), BF16),
                            pltpu.VMEM((2, 2 * ATT_HP, ATT_U * MOBA_BLOCK, MOBA_BLOCK), F32)],
        ),
        out_shape=SDS((bsz, d, s), BF16),
        compiler_params=_cparams(("parallel", "parallel", "arbitrary")),
        name="attn",
    )(slopes, q, kb, vtb, kmean)


OPROJ_TS = 512


def _oproj_body(at_ref, x_ref, mod_ref, wo_ref, o_ref):
    g1 = mod_ref[0][2:3]
    m = lax.dot_general(at_ref[0], wo_ref[...], (((0,), (0,)), ((), ())), preferred_element_type=F32)
    o_ref[0] = x_ref[0] + g1 * m


def _oproj(at, x, mod, w_o):
    bsz, s, d = x.shape
    ts = min(OPROJ_TS, s)
    tok = pl.BlockSpec((1, ts, d), lambda b, i: (b, i, 0))
    return pl.pallas_call(
        _oproj_body,
        grid=(bsz, s // ts),
        in_specs=[pl.BlockSpec((1, d, ts), lambda b, i: (b, 0, i)), tok,
                  pl.BlockSpec((1, 6, d), lambda b, i: (b, 0, 0)),
                  pl.BlockSpec((d, d), lambda b, i: (0, 0))],
        out_specs=tok,
        out_shape=SDS((bsz, s, d), F32),
        compiler_params=_cparams(("parallel", "arbitrary")),
        name="oproj",
    )(at, x, mod, w_o.astype(BF16))


def kernel(x, c, ada_w, ada_b, norm1_g, norm2_g, sgu_w_in, sgu_b_in, sgu_v_g, sgu_w_s, sgu_b_s, sgu_w_out,
           kv_ada_w, kv_ada_b, kv_norm_g, w_kv, k_norm_g, attn_w_q, q_norm_g, attn_w_o,
           moe_w_router, moe_b_router, moe_w_gate, moe_b_gate, moe_w_up, moe_b_up, moe_w_down, moe_b_down):
    bsz, s, d = x.shape
    moe = lambda l, xx, mod: _moe(xx, mod, l, norm2_g[l], moe_w_router[l], moe_b_router[l],
                                  moe_w_gate, moe_b_gate, moe_w_up, moe_b_up, moe_w_down, moe_b_down)
    mod0 = _ada(c, ada_w, ada_b[0], 0).reshape(bsz, 6, d)
    x = _sgu(x, mod0, norm1_g[0], sgu_w_in[0], sgu_b_in[0], sgu_v_g[0], sgu_w_s[0], sgu_b_s[0], sgu_w_out[0])
    x = moe(0, x, mod0)
    mod1 = _ada(c, ada_w, ada_b[1], 1).reshape(bsz, 6, d)
    kvmod = _ada(c, kv_ada_w, kv_ada_b).reshape(bsz, 2, d)
    q, kb, vtb, kmean = _kvq(x, mod1, kvmod, norm1_g[1], kv_norm_g, attn_w_q[0], w_kv, q_norm_g[0], k_norm_g)
    a = _attn(q, kb, vtb, kmean.reshape(bsz, s // MOBA_BLOCK, d))
    x = _oproj(a, x, mod1, attn_w_o[0])
    x = moe(1, x, mod1)
    return x
```

```python
import functools
import math

import jax
import jax.numpy as jnp
import numpy as np
from jax import lax
from jax.experimental import pallas as pl
from jax.experimental.pallas import tpu as pltpu

F32 = jnp.float32
BF16 = jnp.bfloat16
HIGHEST = lax.Precision.HIGHEST
SDS = jax.ShapeDtypeStruct

D_MODEL = 1024
SGU_CHUNK = 128
SGU_GROUPS = 8
D_SGU = 3 * D_MODEL
SGU_GROUP_DIM = D_SGU // SGU_GROUPS
HEAD_DIM = 64
N_HEADS = D_MODEL // HEAD_DIM
MOBA_BLOCK = 256
MOBA_TOPK = 3
ATT_U = 2
ATT_ONES = 16
ATT_AUG = 128
ATT_HP = 4
LOG2E = math.log2(math.e)
N_EXPERTS = 32
TOP_K = 4
SWIGLU_LIMIT = 7.0
SWIGLU_ALPHA = 1.702
EXPERT_BLOCK = 512
EPS = 1e-6

VMEM_LIMIT_BYTES = 56 * 1024 * 1024
NT_DIMS = (((1,), (1,)), ((), ()))


def _cparams(sem):
    return pltpu.CompilerParams(dimension_semantics=sem, vmem_limit_bytes=VMEM_LIMIT_BYTES)


def _dot(a, b, **kw):
    return jnp.dot(a, b, preferred_element_type=F32, **kw)


LANES = 128
ROW_TILE = D_MODEL // LANES


def _store_row_tiles(ref, val):
    rows = val.shape[0]
    for j in range(ROW_TILE):
        ref[pl.ds(j, rows, stride=ROW_TILE), :] = val[:, j * LANES:(j + 1) * LANES]


def _load_row_tiles(ref, rows):
    return jnp.concatenate([ref[pl.ds(j, rows, stride=ROW_TILE), :] for j in range(ROW_TILE)], axis=-1)


def _dot_3pass_nt(a, b):
    a_hi = a.astype(BF16)
    a_lo = (a - a_hi.astype(F32)).astype(BF16)
    b_hi = b.astype(BF16)
    b_lo = (b - b_hi.astype(F32)).astype(BF16)
    nt = lambda u, v: lax.dot_general(u, v, NT_DIMS, preferred_element_type=F32)
    return nt(a_hi, b_hi) + (nt(a_lo, b_hi) + nt(a_hi, b_lo))


def _norm_mod(x, g, sc, sh):
    ms = jnp.mean(x * x, axis=-1, keepdims=True)
    return (x * lax.rsqrt(ms + EPS) * g) * (1.0 + sc) + sh


def _ada_body(c_ref, w_ref, b_ref, o_ref):
    c = c_ref[...]
    ca = c * jax.nn.sigmoid(c)
    o_ref[...] = _dot(ca, w_ref[...], precision=HIGHEST) + b_ref[...]


def _ada(c, w, b, layer=0):
    bsz, d = c.shape
    n = w.shape[-1]
    tn = 1024
    w = w.reshape(-1, d, n)
    return pl.pallas_call(
        _ada_body,
        grid=(n // tn,),
        in_specs=[
            pl.BlockSpec((bsz, d), lambda j: (0, 0)),
            pl.BlockSpec((None, d, tn), lambda j: (layer, 0, j)),
            pl.BlockSpec((1, tn), lambda j: (0, j)),
        ],
        out_specs=pl.BlockSpec((bsz, tn), lambda j: (0, j)),
        out_shape=SDS((bsz, n), F32),
        compiler_params=_cparams(("arbitrary",)),
        name="ada",
    )(c, w, b.reshape(1, n))


SGU_TS = 512
SGU_CB = 768


def _gelu_tanh(x):
    c = np.float32(np.sqrt(2.0 / np.pi))
    return x * (0.5 * (1.0 + jnp.tanh(c * (x + 0.044715 * (x * x * x)))))


def _sgu_body(x_ref, mod_ref, n1g_ref, win_ref, bin_ref, vg_ref, ws_ref, bst_ref, wout_ref,
              o_ref, u_scr, v_scr, y_scr):
    ts = x_ref.shape[1]
    x = x_ref[0]
    mod = mod_ref[0]
    sh1, sc1, g1 = mod[0:1], mod[1:2], mod[2:3]
    h = _norm_mod(x, n1g_ref[...], sc1, sh1).astype(BF16)

    for j in range(D_SGU // SGU_CB):
        cs = slice(j * SGU_CB, (j + 1) * SGU_CB)
        u_scr[:, cs] = _gelu_tanh(_dot(h, win_ref[:, cs]) + bin_ref[:, cs])
    ssq = jnp.zeros((ts, 1), F32)
    for j in range(D_SGU // SGU_CB):
        cs = slice(j * SGU_CB, (j + 1) * SGU_CB)
        ws_cols = slice(D_SGU + j * SGU_CB, D_SGU + (j + 1) * SGU_CB)
        z = _gelu_tanh(_dot(h, win_ref[:, ws_cols]) + bin_ref[:, ws_cols])
        v_scr[:, cs] = z
        ssq = ssq + jnp.sum(z * z, axis=-1, keepdims=True)
    rs = lax.rsqrt(ssq / D_SGU + EPS)

    r_i = lax.broadcasted_iota(jnp.int32, (SGU_CHUNK, SGU_CHUNK), 0)
    c_i = lax.broadcasted_iota(jnp.int32, (SGU_CHUNK, SGU_CHUNK), 1)
    causal = c_i <= r_i
    for g in range(SGU_GROUPS):
        cs = slice(g * SGU_GROUP_DIM, (g + 1) * SGU_GROUP_DIM)
        wsg = jnp.where(causal, ws_ref[g], 0.0).astype(BF16)
        bsg = bst_ref[:, g:g + 1]
        for n in range(ts // SGU_CHUNK):
            rows = slice(n * SGU_CHUNK, (n + 1) * SGU_CHUNK)
            vn = (v_scr[rows, cs] * rs[rows] * vg_ref[:, cs]).astype(BF16)
            mixed = _dot(wsg, vn) + bsg
            y_scr[rows, cs] = (u_scr[rows, cs] * mixed).astype(BF16)

    m = _dot(y_scr[...], wout_ref[...])
    o_ref[0] = x + g1 * m


def _sgu(x, mod, n1g, w_in, b_in, v_g, w_s, b_s, w_out):
    bsz, s, d = x.shape
    ts = SGU_TS
    const = lambda *shape: pl.BlockSpec(shape, lambda b, i: (0,) * len(shape),
                                        pipeline_mode=pl.Buffered(1))
    return pl.pallas_call(
        _sgu_body,
        grid=(bsz, s // ts),
        in_specs=[
            pl.BlockSpec((1, ts, d), lambda b, i: (b, i, 0)),
            pl.BlockSpec((1, 6, d), lambda b, i: (b, 0, 0)),
            const(1, d),
            const(d, 2 * D_SGU),
            const(1, 2 * D_SGU),
            const(1, D_SGU),
            const(SGU_GROUPS, SGU_CHUNK, SGU_CHUNK),
            const(SGU_CHUNK, SGU_GROUPS),
            const(D_SGU, d),
        ],
        out_specs=pl.BlockSpec((1, ts, d), lambda b, i: (b, i, 0)),
        out_shape=SDS((bsz, s, d), F32),
        scratch_shapes=[
            pltpu.VMEM((ts, D_SGU), F32),
            pltpu.VMEM((ts, D_SGU), F32),
            pltpu.VMEM((ts, D_SGU), BF16),
        ],
        compiler_params=_cparams(("parallel", "arbitrary")),
        name="sgu",
    )(x, mod, n1g.reshape(1, d), w_in.astype(BF16), b_in.reshape(1, -1), v_g.reshape(1, -1),
      w_s, b_s.T, w_out.astype(BF16))


ROUTER_TS = 512


def _router_body(x_ref, mod_ref, n2g_ref, wrt_ref, brc_ref, h_ref, idx_ref, p_ref, rk_ref, cnt_ref, run_scr):
    @pl.when((pl.program_id(0) == 0) & (pl.program_id(1) == 0))
    def _():
        run_scr[...] = jnp.zeros(run_scr.shape, run_scr.dtype)

    x = x_ref[0]
    mod = mod_ref[0]
    sh2, sc2 = mod[3:4], mod[4:5]
    h2 = _norm_mod(x, n2g_ref[...], sc2, sh2)
    _store_row_tiles(h_ref, h2)
    ts = x.shape[0]
    logits = _dot_3pass_nt(wrt_ref[...], h2) + brc_ref[...]
    row = lax.broadcasted_iota(jnp.int32, logits.shape, 0)
    work = logits
    vals, idxs = [], []
    for _ in range(TOP_K):
        mx = jnp.max(work, axis=0, keepdims=True)
        ix = jnp.min(jnp.where(work == mx, row, N_EXPERTS), axis=0, keepdims=True)
        vals.append(mx)
        idxs.append(ix)
        work = jnp.where(row == ix, -jnp.inf, work)
    e = jnp.exp(jnp.concatenate(vals, axis=0) - vals[0])
    p_t = e / jnp.sum(e, axis=0, keepdims=True)
    idx_ref[0] = jnp.concatenate(idxs, axis=0)

    onehots = [row == ix for ix in idxs]
    mask = jnp.zeros(logits.shape, F32)
    for oh in onehots:
        mask = mask + jnp.where(oh, 1.0, 0.0)
    r_i = lax.broadcasted_iota(jnp.int32, (ts, ts), 0)
    c_i = lax.broadcasted_iota(jnp.int32, (ts, ts), 1)
    before = _dot(mask.astype(BF16), jnp.where(r_i < c_i, 1.0, 0.0).astype(BF16)) + run_scr[...]
    rk = [jnp.sum(jnp.where(oh, before, 0.0), axis=0, keepdims=True) for oh in onehots]
    rk_ref[0] = jnp.concatenate(rk, axis=0).astype(jnp.int32)
    run_scr[...] = run_scr[...] + jnp.sum(mask, axis=1, keepdims=True)
    cnt_ref[...] = run_scr[...].astype(jnp.int32)

    eye = jnp.where(r_i == c_i, 1.0, 0.0).astype(BF16)
    p_ref[0] = sum(lax.dot_general(eye, piece.astype(BF16), NT_DIMS, preferred_element_type=F32)
                   for piece in _bf16_pieces(p_t))


def _router(x, mod, n2g, w_r, b_r):
    bsz, s, d = x.shape
    ts = min(ROUTER_TS, s)
    kt = pl.BlockSpec((1, TOP_K, ts), lambda b, i: (b, 0, i))
    return pl.pallas_call(
        _router_body,
        grid=(bsz, s // ts),
        in_specs=[
            pl.BlockSpec((1, ts, d), lambda b, i: (b, i, 0)),
            pl.BlockSpec((1, 6, d), lambda b, i: (b, 0, 0)),
            pl.BlockSpec((1, d), lambda b, i: (0, 0)),
            pl.BlockSpec((N_EXPERTS, d), lambda b, i: (0, 0)),
            pl.BlockSpec((N_EXPERTS, 1), lambda b, i: (0, 0)),
        ],
        out_specs=[pl.BlockSpec((ts * ROW_TILE, LANES), lambda b, i: (b * (s // ts) + i, 0)),
                   kt, pl.BlockSpec((1, ts, TOP_K), lambda b, i: (b, i, 0)), kt,
                   pl.BlockSpec((N_EXPERTS, 1), lambda b, i: (0, 0))],
        out_shape=[SDS((bsz * s * ROW_TILE, LANES), F32), SDS((bsz, TOP_K, s), jnp.int32),
                   SDS((bsz, s, TOP_K), F32),
                   SDS((bsz, TOP_K, s), jnp.int32), SDS((N_EXPERTS, 1), jnp.int32)],
        scratch_shapes=[pltpu.VMEM((N_EXPERTS, 1), F32)],
        compiler_params=_cparams(("arbitrary", "arbitrary")),
        name="router",
    )(x, mod, n2g.reshape(1, d), w_r.T, b_r.reshape(-1, 1))


def _route_meta(top_idx, rank, counts):
    n = top_idx.size // TOP_K
    padded = (counts + EXPERT_BLOCK - 1) // EXPERT_BLOCK * EXPERT_BLOCK
    pend = jnp.cumsum(padded)
    pstart = pend - padded
    first_row = jnp.sum(jnp.where(top_idx[..., None] == jnp.arange(N_EXPERTS), pstart, 0), axis=-1)
    dest = (first_row + rank).astype(jnp.int32)
    n_rows = n * TOP_K + N_EXPERTS * EXPERT_BLOCK
    n_blk = n_rows // EXPERT_BLOCK
    blk_row0 = jnp.arange(n_blk) * EXPERT_BLOCK
    blk_exp = jnp.minimum(jnp.sum(pend[None, :] <= blk_row0[:, None], axis=1), N_EXPERTS - 1)
    n_used = (pend[-1] // EXPERT_BLOCK).astype(jnp.int32).reshape(1)
    eids = jnp.arange(N_EXPERTS)
    has_rows = counts > 0
    last_exp = jnp.max(jnp.where(has_rows, eids, 0))
    blk_exp = jnp.where(blk_row0 < pend[-1], blk_exp, last_exp).astype(jnp.int32)
    nxt_e = jnp.min(jnp.where((eids[None, :] > eids[:, None]) & has_rows[None, :], eids[None, :], N_EXPERTS), axis=1)
    nxt_e = jnp.where(nxt_e < N_EXPERTS, nxt_e, -1)
    slot_e = (jnp.cumsum(has_rows.astype(jnp.int32)) - 1) % 2
    blk_nxt = nxt_e[blk_exp].astype(jnp.int32)
    blk_slot = slot_e[blk_exp].astype(jnp.int32)
    pad_blk = jnp.where(padded > 0, pend - EXPERT_BLOCK, -1)
    tail = pend[-1] + jnp.arange(N_EXPERTS) * EXPERT_BLOCK
    tail_blk = jnp.where(tail < n_rows, tail, -1)
    pad_blk = jnp.concatenate([pad_blk, tail_blk]).astype(jnp.int32)
    return dest, (blk_exp, n_used, blk_nxt, blk_slot), pad_blk, n_rows


MOVE_TT = 1024
COMBINE_TT = 1024


def _per_step(dest, tt):
    bsz, k, s = dest.shape
    return dest.reshape(bsz, k, s // tt, tt).transpose(0, 2, 1, 3).reshape(bsz * (s // tt), 1, k * tt)


def _dispatch_body(pad_ref, dest_ref, h_ref, xs_ref, zero_scr, sem, zsem):
    tt = h_ref.shape[0] // ROW_TILE
    blk = EXPERT_BLOCK * ROW_TILE

    @pl.when(pl.program_id(0) == 0)
    def _():
        zero_scr[...] = jnp.zeros(zero_scr.shape, zero_scr.dtype)

        def zero_copy(e):
            row0 = pl.multiple_of(pad_ref[e], blk)
            return pltpu.make_async_copy(zero_scr, xs_ref.at[pl.ds(row0, blk)], zsem)

        for e in range(2 * N_EXPERTS):
            pl.when(pad_ref[e] >= 0)(lambda e=e: zero_copy(e).start())
        for e in range(2 * N_EXPERTS):
            pl.when(pad_ref[e] >= 0)(lambda e=e: zero_copy(e).wait())

    def issue(t, carry):
        src = h_ref.at[pl.ds(pl.multiple_of(t * ROW_TILE, ROW_TILE), ROW_TILE)]
        for k in range(TOP_K):
            d = pl.multiple_of(dest_ref[0, k * tt + t], ROW_TILE)
            pltpu.make_async_copy(src, xs_ref.at[pl.ds(d, ROW_TILE)], sem).start(priority=k % 2)
        return carry

    lax.fori_loop(0, tt, issue, 0)
    for k in range(TOP_K):
        pltpu.make_async_copy(h_ref, h_ref, sem).wait()


def _dispatch(h, dest8, pad_blk, n_rows):
    n = h.shape[0] // ROW_TILE
    tt = min(MOVE_TT, dest8.shape[-1])
    return pl.pallas_call(
        _dispatch_body,
        grid_spec=pltpu.PrefetchScalarGridSpec(
            num_scalar_prefetch=1,
            grid=(n // tt,),
            in_specs=[
                pl.BlockSpec((None, 1, TOP_K * tt), lambda i, pad: (i, 0, 0), memory_space=pltpu.SMEM),
                pl.BlockSpec((tt * ROW_TILE, LANES), lambda i, pad: (i, 0)),
            ],
            out_specs=pl.BlockSpec(memory_space=pl.ANY),
            scratch_shapes=[pltpu.VMEM((EXPERT_BLOCK * ROW_TILE, LANES), h.dtype), pltpu.SemaphoreType.DMA(()),
                            pltpu.SemaphoreType.DMA(())],
        ),
        out_shape=SDS((n_rows * ROW_TILE, LANES), h.dtype),
        compiler_params=_cparams(("arbitrary",)),
        name="dispatch",
    )(pad_blk * ROW_TILE, _per_step(dest8, tt), h)


def _ffn_body(layer, be_ref, nu_ref, nx_ref, sl_ref, xs_ref, wg_hbm, bg_ref, wu_hbm, bu_ref, wd_hbm, bd_ref,
              y_ref, w_f32, w_bf, sem):
    i = pl.program_id(0)
    slot = sl_ref[i]

    def fetch(e, s):
        return [pltpu.make_async_copy(w.at[layer, e], w_f32.at[s, n], sem.at[s])
                for n, w in enumerate((wg_hbm, wu_hbm, wd_hbm))]

    @pl.when(i == 0)
    def _():
        for c in fetch(be_ref[0], slot):
            c.start()

    @pl.when((i == 0) | (be_ref[i] != be_ref[jnp.maximum(i - 1, 0)]))
    def _():
        for c in fetch(be_ref[i], slot):
            c.wait()

        @pl.when(nx_ref[i] >= 0)
        def _():
            for c in fetch(nx_ref[i], 1 - slot):
                c.start()

        for n in range(3):
            w_bf[n] = w_f32[slot, n].astype(BF16)

    @pl.when(i < nu_ref[0])
    def _():
        x = _load_row_tiles(xs_ref, EXPERT_BLOCK).astype(BF16)
        g = jnp.minimum(_dot(x, w_bf[0]) + bg_ref[...], SWIGLU_LIMIT)
        u = jnp.clip(_dot(x, w_bf[1]) + bu_ref[...], -SWIGLU_LIMIT, SWIGLU_LIMIT)
        a = g * jax.nn.sigmoid(SWIGLU_ALPHA * g) * (u + 1.0)
        _store_row_tiles(y_ref, _dot(a.astype(BF16), w_bf[2]) + bd_ref[...])

    @pl.when(i >= nu_ref[0])
    def _():
        y_ref[...] = jnp.zeros(y_ref.shape, y_ref.dtype)


def _ffn(xs, blk_meta, layer, w_gate, b_gate, w_up, b_up, w_down, b_down):
    n_rows = xs.shape[0] // ROW_TILE
    _, e, d, de = w_gate.shape
    assert d == de == D_MODEL
    n_blk = n_rows // EXPERT_BLOCK
    blk = (EXPERT_BLOCK * ROW_TILE, LANES)
    bspec = lambda c: pl.BlockSpec((None, None, 1, c), lambda i, be, nu, nx, sl: (layer, be[i], 0, 0))
    hbm = pl.BlockSpec(memory_space=pl.ANY)
    xs_map = lambda i, be, nu, nx, sl: (jnp.minimum(i, jnp.maximum(nu[0] - 1, 0)), 0)
    return pl.pallas_call(
        functools.partial(_ffn_body, layer),
        grid_spec=pltpu.PrefetchScalarGridSpec(
            num_scalar_prefetch=4,
            grid=(n_blk,),
            in_specs=[
                pl.BlockSpec(blk, xs_map),
                hbm, bspec(de), hbm, bspec(de), hbm, bspec(d),
            ],
            out_specs=pl.BlockSpec(blk, lambda i, be, nu, nx, sl: (i, 0)),
            scratch_shapes=[pltpu.VMEM((2, 3, d, de), F32), pltpu.VMEM((3, d, de), BF16),
                            pltpu.SemaphoreType.DMA((2,))],
        ),
        out_shape=SDS((n_rows * ROW_TILE, LANES), F32),
        compiler_params=_cparams(("arbitrary",)),
        name="ffn",
    )(*blk_meta, xs, w_gate, b_gate.reshape(-1, e, 1, de), w_up, b_up.reshape(-1, e, 1, de),
      w_down, b_down.reshape(-1, e, 1, d))


COMBINE_RC = 128


def _combine_body(dest_ref, x_ref, mod_ref, p_ref, y_ref, o_ref, buf, sem):
    tt = x_ref.shape[0]

    def issue(t, carry):
        t8 = pl.multiple_of(t * ROW_TILE, ROW_TILE)
        for k in range(TOP_K):
            d = pl.multiple_of(dest_ref[0, k * tt + t], ROW_TILE)
            pltpu.make_async_copy(y_ref.at[pl.ds(d, ROW_TILE)], buf.at[k, pl.ds(t8, ROW_TILE)],
                                  sem).start(priority=k % 2)
        return carry

    lax.fori_loop(0, tt, issue, 0)
    pltpu.make_async_copy(buf, buf, sem).wait()

    g2 = mod_ref[0][5:6]
    for r0 in range(0, tt, COMBINE_RC):
        rows = slice(r0, r0 + COMBINE_RC)
        p = p_ref[rows, :]
        for j in range(ROW_TILE):
            cols = slice(j * LANES, (j + 1) * LANES)
            tile = lambda k: buf[k, pl.ds(r0 * ROW_TILE + j, COMBINE_RC, stride=ROW_TILE), :]
            acc = p[:, 0:1] * tile(0)
            for k in range(1, TOP_K):
                acc = acc + p[:, k:k + 1] * tile(k)
            o_ref[rows, cols] = x_ref[rows, cols] + g2[:, cols] * acc


def _combine(x, mod, p, y, dest8, seq):
    n, d = x.shape
    tt = min(COMBINE_TT, seq)
    n_steps = n // tt
    per_b = seq // tt
    return pl.pallas_call(
        _combine_body,
        grid=(n_steps,),
        in_specs=[
            pl.BlockSpec((None, 1, TOP_K * tt), lambda i: (i, 0, 0), memory_space=pltpu.SMEM),
            pl.BlockSpec((tt, d), lambda i: (i, 0)),
            pl.BlockSpec((1, 6, d), lambda i: (i // per_b, 0, 0)),
            pl.BlockSpec((tt, TOP_K), lambda i: (i, 0)),
            pl.BlockSpec(memory_space=pl.ANY),
        ],
        out_specs=pl.BlockSpec((tt, d), lambda i: (i, 0)),
        out_shape=SDS((n, d), F32),
        scratch_shapes=[pltpu.VMEM((TOP_K, tt * ROW_TILE, LANES), F32), pltpu.SemaphoreType.DMA(())],
        compiler_params=_cparams(("arbitrary",)),
        name="combine",
    )(_per_step(dest8, tt), x, mod, p, y)


def _moe(x, mod, layer, n2g, w_r, b_r, w_gate, b_gate, w_up, b_up, w_down, b_down):
    bsz, s, d = x.shape
    n = bsz * s
    h2, top_idx, probs, rank, counts = _router(x, mod, n2g, w_r, b_r)
    dest, blk_meta, pad_blk, n_rows = _route_meta(top_idx, rank, counts.reshape(N_EXPERTS))
    dest8 = dest * ROW_TILE
    xs = _dispatch(h2, dest8, pad_blk, n_rows)
    y = _ffn(xs, blk_meta, layer, w_gate, b_gate, w_up, b_up, w_down, b_down)
    out = _combine(x.reshape(n, d), mod, probs.reshape(n, TOP_K), y, dest8, s)
    return out.reshape(bsz, s, d)


def _seg_mats():
    r = lax.broadcasted_iota(jnp.int32, (D_MODEL, N_HEADS), 0) // HEAD_DIM
    c = lax.broadcasted_iota(jnp.int32, (D_MODEL, N_HEADS), 1)
    seg = jnp.where(r == c, 1.0, 0.0).astype(BF16)
    rt = lax.broadcasted_iota(jnp.int32, (N_HEADS, D_MODEL), 0)
    ct = lax.broadcasted_iota(jnp.int32, (N_HEADS, D_MODEL), 1) // HEAD_DIM
    seg_t = jnp.where(rt == ct, 1.0, 0.0).astype(BF16)
    return seg, seg_t


def _split_dot(a, b):
    hi = a.astype(BF16)
    lo = (a - hi.astype(F32)).astype(BF16)
    return _dot(hi, b) + _dot(lo, b)


def _head_rmsnorm(x, g, seg, seg_t):
    ms = _split_dot(x * x, seg) * (1.0 / HEAD_DIM)
    r = lax.rsqrt(ms + EPS)
    return x * _split_dot(r, seg_t) * g


def _kvq_body(x_ref, mod_ref, kvmod_ref, n1g_ref, kvg_ref, wq_ref, wkv_ref, qg_ref, kg_ref,
              q_ref, k_ref, vt_ref, km_ref):
    x = x_ref[0]
    mod = mod_ref[0]
    kvmod = kvmod_ref[0]
    ms = jnp.mean(x * x, axis=-1, keepdims=True)
    xn = x * lax.rsqrt(ms + EPS)
    hq = ((xn * n1g_ref[...]) * (1.0 + mod[1:2]) + mod[0:1]).astype(BF16)
    hk = ((xn * kvg_ref[...]) * (1.0 + kvmod[1:2]) + kvmod[0:1]).astype(BF16)
    seg, seg_t = _seg_mats()
    q = _head_rmsnorm(_dot(hq, wq_ref[...]), qg_ref[...], seg, seg_t)
    q_ref[0] = q
    kv = _dot(hk, wkv_ref[...])
    k = _head_rmsnorm(kv[:, :D_MODEL], kg_ref[...], seg, seg_t)
    k_ref[0, 0] = k.astype(BF16)
    for u in range(ATT_U):
        rows = slice(u * MOBA_BLOCK, (u + 1) * MOBA_BLOCK)
        km_ref[0, 0, u:u + 1, :] = jnp.mean(k[rows], axis=0, keepdims=True)
    vt_ref[0, 0] = kv[:, D_MODEL:].T.astype(BF16)


def _kvq(x, mod, kvmod, n1g, kvg, w_q, w_kv, q_g, k_g):
    bsz, s, d = x.shape
    ts = ATT_U * MOBA_BLOCK
    nb = s // ts
    const = lambda *shape: pl.BlockSpec(shape, lambda b, i: (0,) * len(shape))
    return pl.pallas_call(
        _kvq_body,
        grid=(bsz, nb),
        in_specs=[
            pl.BlockSpec((1, ts, d), lambda b, i: (b, i, 0)),
            pl.BlockSpec((1, 6, d), lambda b, i: (b, 0, 0)),
            pl.BlockSpec((1, 2, d), lambda b, i: (b, 0, 0)),
            const(1, d), const(1, d), const(d, d), const(d, 2 * d), const(1, d), const(1, d),
        ],
        out_specs=[
            pl.BlockSpec((1, ts, d), lambda b, i: (b, i, 0)),
            pl.BlockSpec((1, 1, ts, d), lambda b, i: (b, i, 0, 0)),
            pl.BlockSpec((1, 1, d, ts), lambda b, i: (b, i, 0, 0)),
            pl.BlockSpec((1, 1, ATT_U, d), lambda b, i: (b, i, 0, 0)),
        ],
        out_shape=[
            SDS((bsz, s, d), F32),
            SDS((bsz, nb, ts, d), BF16),
            SDS((bsz, nb, d, ts), BF16),
            SDS((bsz, nb, ATT_U, d), F32),
        ],
        compiler_params=_cparams(("parallel", "arbitrary")),
        name="kvq",
    )(x, mod, kvmod, n1g.reshape(1, d), kvg.reshape(1, d), w_q.astype(BF16), w_kv.astype(BF16),
      jnp.tile(q_g, N_HEADS).reshape(1, d), jnp.tile(k_g, N_HEADS).reshape(1, d))


def _bf16_pieces(v):
    hi = v.astype(BF16).astype(F32)
    mid = (v - hi).astype(BF16).astype(F32)
    lo = ((v - hi) - mid).astype(BF16).astype(F32)
    return hi, mid, lo


def _attn_body(slopes_ref, q_ref, k_ref, vt_ref, km_ref, o_ref, cm_scr, kaug_scr, qaug_scr, s_scr):
    hp = pl.program_id(1)
    own = pl.program_id(2)
    bs = MOBA_BLOCK
    nb = k_ref.shape[1] * ATT_U
    pw = 2 * HEAD_DIM
    nh = 2 * ATT_HP

    @pl.when(own == 0)
    def _():
        rel = (lax.broadcasted_iota(jnp.int32, (bs, bs), 1)
               - lax.broadcasted_iota(jnp.int32, (bs, bs), 0))
        cm_scr[0] = jnp.zeros((bs, bs), F32)
        cm_scr[1] = jnp.where(rel >= 0, 0.0, -jnp.inf)
        klane = lax.broadcasted_iota(jnp.int32, (ATT_U * bs, ATT_AUG), 1)
        koff = (lax.broadcasted_iota(jnp.int32, (ATT_U * bs, ATT_AUG), 0) % bs).astype(F32)
        kaug_scr[...] = jnp.where(klane < 3, 1.0, jnp.where(klane < 6, koff, 0.0)).astype(BF16)
        qlane = lax.broadcasted_iota(jnp.int32, (bs, ATT_AUG), 1)
        qoff = lax.broadcasted_iota(jnp.int32, (bs, ATT_AUG), 0).astype(F32)
        for hd in range(nh):
            s2 = jnp.full((bs, ATT_AUG), slopes_ref[nh * hp + hd], F32) * LOG2E
            pieces = _bf16_pieces(-(s2 * qoff)) + _bf16_pieces(s2)
            aug = jnp.zeros((bs, ATT_AUG), F32)
            for lane_ix, piece in enumerate(pieces):
                aug = jnp.where(qlane == lane_ix, piece, aug)
            qaug_scr[hd] = aug.astype(BF16)

    dcol = lax.broadcasted_iota(jnp.int32, (1, pw), 1)
    blk_row = lax.broadcasted_iota(jnp.int32, (nb, 1), 0)
    valid = blk_row < own
    in_head = [(dcol >= h * HEAD_DIM) & (dcol < (h + 1) * HEAD_DIM) for h in range(2)]
    pair_cols = lambda hd: slice((hd // 2) * pw, (hd // 2 + 1) * pw)

    gates = []
    for pair in range(nh // 2):
        qp = q_ref[0, :, pair_cols(2 * pair)]
        kmp = km_ref[0, :, pair_cols(2 * pair)]
        km2 = jnp.concatenate([jnp.where(in_head[h], kmp, 0.0) for h in range(2)], axis=0)
        gates.append(lax.dot_general(km2, qp, NT_DIMS, precision=HIGHEST, preferred_element_type=F32))

    qs, selbias = [], []
    for hd in range(nh):
        slope = slopes_ref[nh * hp + hd]
        qm = jnp.where(in_head[hd % 2], q_ref[0, :, pair_cols(hd)], 0.0)
        gate = jnp.where(valid, gates[hd // 2][(hd % 2) * nb:(hd % 2 + 1) * nb], -jnp.inf)
        rank = jnp.zeros(gate.shape, F32)
        for i in range(nb):
            gi = gate[i:i + 1, :]
            tie = jnp.where(blk_row > i, 1.0, 0.0)
            rank = rank + jnp.where(gi > gate, 1.0, jnp.where(gi == gate, tie, 0.0))
        chosen = jnp.where(valid, jnp.where(rank < MOBA_TOPK, 1.0, 0.0),
                           jnp.where(blk_row == own, 1.0, 0.0))
        off = (own - blk_row).astype(F32) * (bs * LOG2E * slope)
        selbias.append(jnp.where(chosen > 0.5, -off, -jnp.inf))
        qs.append(jnp.concatenate([(qm * (LOG2E * HEAD_DIM ** -0.5)).astype(BF16), qaug_scr[hd]], axis=1))

    last = k_ref.shape[1] - 1
    ones = jnp.ones((ATT_ONES, ATT_U * bs), BF16)

    def qk(it, hd):
        kc = jnp.concatenate([k_ref[0, it, :, pair_cols(hd)], kaug_scr[...]], axis=1)
        return lax.dot_general(kc, qs[hd], NT_DIMS, preferred_element_type=F32)

    def visit(it, slot, carry, next_it, has_own):
        out = []
        if next_it is not None:
            for hd in range(nh):
                s_scr[1 - slot, hd] = qk(jnp.minimum(next_it, last), hd)
        for hd in range(nh):
            m, acc = carry[hd]

            def tile(u):
                t = s_scr[slot, hd, u * bs:(u + 1) * bs]
                if has_own:
                    t = t + cm_scr[(it * ATT_U + u == own).astype(jnp.int32)]
                return t

            rows = [jnp.sum(jnp.where(blk_row == it * ATT_U + u, selbias[hd], 0.0), axis=0, keepdims=True)
                    for u in range(ATT_U)]
            mx = m
            for u in range(ATT_U):
                mx = jnp.maximum(mx, jnp.max(tile(u), axis=0, keepdims=True) + rows[u])
            m_safe = jnp.where(mx == -jnp.inf, 0.0, mx)
            alpha = jnp.exp2(m - m_safe)
            p = jnp.concatenate([jnp.exp2((tile(u) + (rows[u] - m_safe)).astype(BF16)) for u in range(ATT_U)],
                                axis=0)
            vj = jnp.concatenate([vt_ref[0, it, hd * HEAD_DIM:(hd + 1) * HEAD_DIM, :], ones], axis=0)
            acc = alpha * acc + _dot(vj, p)
            out.append((mx, acc))
        return tuple(out)

    def body(i2, carry):
        carry = visit(2 * i2, 1, carry, 2 * i2 + 1, False)
        return visit(2 * i2 + 1, 0, carry, 2 * i2 + 2, False)

    n_past = own // ATT_U
    for hd in range(nh):
        s_scr[0, hd] = qk(n_past, hd)
    init = tuple((jnp.full((1, bs), -jnp.inf, F32), jnp.zeros((HEAD_DIM + ATT_ONES, bs), F32))
                 for hd in range(nh))
    carry = visit(n_past, 0, init, 0, True)
    carry = lax.fori_loop(0, n_past // 2, body, carry)
    carry = lax.cond(n_past % 2 == 1, lambda c: visit(n_past - 1, 1, c, None, False), lambda c: c, carry)
    outs = [acc[:HEAD_DIM] / acc[HEAD_DIM:HEAD_DIM + 1] for (_, acc) in carry]
    o_ref[0] = jnp.concatenate(outs, axis=0).astype(o_ref.dtype)


def _attn(q, kb, vtb, kmean):
    bsz, s, d = q.shape
    nsb = s // (ATT_U * MOBA_BLOCK)
    assert nsb % 2 == 0, "the attention loop visits key super-blocks in pairs"
    nb = s // MOBA_BLOCK
    w = 2 * HEAD_DIM * ATT_HP
    slopes = (2.0 ** (-8.0 * jnp.arange(1, N_HEADS + 1, dtype=F32) / N_HEADS)).astype(F32)
    return pl.pallas_call(
        _attn_body,
        grid_spec=pltpu.PrefetchScalarGridSpec(
            num_scalar_prefetch=1,
            grid=(bsz, N_HEADS // (2 * ATT_HP), nb),
            in_specs=[
                pl.BlockSpec((1, MOBA_BLOCK, w), lambda b, h, c, sl: (b, c, h)),
                pl.BlockSpec((1, nsb, ATT_U * MOBA_BLOCK, w), lambda b, h, c, sl: (b, 0, 0, h)),
                pl.BlockSpec((1, nsb, w, ATT_U * MOBA_BLOCK), lambda b, h, c, sl: (b, 0, h, 0)),
                pl.BlockSpec((1, nb, w), lambda b, h, c, sl: (b, 0, h)),
            ],
            out_specs=pl.BlockSpec((1, w, MOBA_BLOCK), lambda b, h, c, sl: (b, h, c)),
            scratch_shapes=[pltpu.VMEM((2, MOBA_BLOCK, MOBA_BLOCK), F32),
                            pltpu.VMEM((ATT_U * MOBA_BLOCK, ATT_AUG), BF16),
                            pltpu.VMEM((2 * ATT_HP, MOBA_BLOCK, ATT_AUG), BF16),
                            pltpu.VMEM((2, 2 * ATT_HP, ATT_U * MOBA_BLOCK, MOBA_BLOCK), F32)],
        ),
        out_shape=SDS((bsz, d, s), BF16),
        compiler_params=_cparams(("parallel", "parallel", "arbitrary")),
        name="attn",
    )(slopes, q, kb, vtb, kmean)


OPROJ_TS = 512


def _oproj_body(at_ref, x_ref, mod_ref, wo_ref, o_ref):
    g1 = mod_ref[0][2:3]
    m = lax.dot_general(at_ref[0], wo_ref[...], (((0,), (0,)), ((), ())), preferred_element_type=F32)
    o_ref[0] = x_ref[0] + g1 * m


def _oproj(at, x, mod, w_o):
    bsz, s, d = x.shape
    ts = min(OPROJ_TS, s)
    tok = pl.BlockSpec((1, ts, d), lambda b, i: (b, i, 0))
    return pl.pallas_call(
        _oproj_body,
        grid=(bsz, s // ts),
        in_specs=[pl.BlockSpec((1, d, ts), lambda b, i: (b, 0, i)), tok,
                  pl.BlockSpec((1, 6, d), lambda b, i: (b, 0, 0)),
                  pl.BlockSpec((d, d), lambda b, i: (0, 0))],
        out_specs=tok,
        out_shape=SDS((bsz, s, d), F32),
        compiler_params=_cparams(("parallel", "arbitrary")),
        name="oproj",
    )(at, x, mod, w_o.astype(BF16))


def kernel(x, c, ada_w, ada_b, norm1_g, norm2_g, sgu_w_in, sgu_b_in, sgu_v_g, sgu_w_s, sgu_b_s, sgu_w_out,
           kv_ada_w, kv_ada_b, kv_norm_g, w_kv, k_norm_g, attn_w_q, q_norm_g, attn_w_o,
           moe_w_router, moe_b_router, moe_w_gate, moe_b_gate, moe_w_up, moe_b_up, moe_w_down, moe_b_down):
    bsz, s, d = x.shape
    moe = lambda l, xx, mod: _moe(xx, mod, l, norm2_g[l], moe_w_router[l], moe_b_router[l],
                                  moe_w_gate, moe_b_gate, moe_w_up, moe_b_up, moe_w_down, moe_b_down)
    mod0 = _ada(c, ada_w, ada_b[0], 0).reshape(bsz, 6, d)
    x = _sgu(x, mod0, norm1_g[0], sgu_w_in[0], sgu_b_in[0], sgu_v_g[0], sgu_w_s[0], sgu_b_s[0], sgu_w_out[0])
    x = moe(0, x, mod0)
    mod1 = _ada(c, ada_w, ada_b[1], 1).reshape(bsz, 6, d)
    kvmod = _ada(c, kv_ada_w, kv_ada_b).reshape(bsz, 2, d)
    q, kb, vtb, kmean = _kvq(x, mod1, kvmod, norm1_g[1], kv_norm_g, attn_w_q[0], w_kv, q_norm_g[0], k_norm_g)
    a = _attn(q, kb, vtb, kmean.reshape(bsz, s // MOBA_BLOCK, d))
    x = _oproj(a, x, mod1, attn_w_o[0])
    x = moe(1, x, mod1)
    return x
```

```python
import functools
import math

import jax
import jax.numpy as jnp
import numpy as np
from jax import lax
from jax.experimental import pallas as pl
from jax.experimental.pallas import tpu as pltpu

F32 = jnp.float32
BF16 = jnp.bfloat16
HIGHEST = lax.Precision.HIGHEST
SDS = jax.ShapeDtypeStruct

D_MODEL = 1024
SGU_CHUNK = 128
SGU_GROUPS = 8
D_SGU = 3 * D_MODEL
SGU_GROUP_DIM = D_SGU // SGU_GROUPS
HEAD_DIM = 64
N_HEADS = D_MODEL // HEAD_DIM
MOBA_BLOCK = 256
MOBA_TOPK = 3
ATT_U = 2
ATT_ONES = 16
ATT_AUG = 128
ATT_HP = 4
LOG2E = math.log2(math.e)
N_EXPERTS = 32
TOP_K = 4
SWIGLU_LIMIT = 7.0
SWIGLU_ALPHA = 1.702
EXPERT_BLOCK = 512
EPS = 1e-6

VMEM_LIMIT_BYTES = 56 * 1024 * 1024
NT_DIMS = (((1,), (1,)), ((), ()))


def _cparams(sem):
    return pltpu.CompilerParams(dimension_semantics=sem, vmem_limit_bytes=VMEM_LIMIT_BYTES)


def _dot(a, b, **kw):
    return jnp.dot(a, b, preferred_element_type=F32, **kw)


LANES = 128
ROW_TILE = D_MODEL // LANES


def _store_row_tiles(ref, val):
    rows = val.shape[0]
    for j in range(ROW_TILE):
        ref[pl.ds(j, rows, stride=ROW_TILE), :] = val[:, j * LANES:(j + 1) * LANES]


def _load_row_tiles(ref, rows):
    return jnp.concatenate([ref[pl.ds(j, rows, stride=ROW_TILE), :] for j in range(ROW_TILE)], axis=-1)


def _dot_3pass_nt(a, b):
    a_hi = a.astype(BF16)
    a_lo = (a - a_hi.astype(F32)).astype(BF16)
    b_hi = b.astype(BF16)
    b_lo = (b - b_hi.astype(F32)).astype(BF16)
    nt = lambda u, v: lax.dot_general(u, v, NT_DIMS, preferred_element_type=F32)
    return nt(a_hi, b_hi) + (nt(a_lo, b_hi) + nt(a_hi, b_lo))


def _norm_mod(x, g, sc, sh):
    ms = jnp.mean(x * x, axis=-1, keepdims=True)
    return (x * lax.rsqrt(ms + EPS) * g) * (1.0 + sc) + sh


def _ada_body(c_ref, w_ref, b_ref, o_ref):
    c = c_ref[...]
    ca = c * jax.nn.sigmoid(c)
    o_ref[...] = _dot(ca, w_ref[...], precision=HIGHEST) + b_ref[...]


def _ada(c, w, b, layer=0):
    bsz, d = c.shape
    n = w.shape[-1]
    tn = 1024
    w = w.reshape(-1, d, n)
    return pl.pallas_call(
        _ada_body,
        grid=(n // tn,),
        in_specs=[
            pl.BlockSpec((bsz, d), lambda j: (0, 0)),
            pl.BlockSpec((None, d, tn), lambda j: (layer, 0, j)),
            pl.BlockSpec((1, tn), lambda j: (0, j)),
        ],
        out_specs=pl.BlockSpec((bsz, tn), lambda j: (0, j)),
        out_shape=SDS((bsz, n), F32),
        compiler_params=_cparams(("arbitrary",)),
        name="ada",
    )(c, w, b.reshape(1, n))


SGU_TS = 512
SGU_CB = 768


def _gelu_tanh(x):
    c = np.float32(np.sqrt(2.0 / np.pi))
    return x * (0.5 * (1.0 + jnp.tanh(c * (x + 0.044715 * (x * x * x)))))


def _sgu_body(x_ref, mod_ref, n1g_ref, win_ref, bin_ref, vg_ref, ws_ref, bst_ref, wout_ref,
              o_ref, u_scr, v_scr, y_scr):
    ts = x_ref.shape[1]
    x = x_ref[0]
    mod = mod_ref[0]
    sh1, sc1, g1 = mod[0:1], mod[1:2], mod[2:3]
    h = _norm_mod(x, n1g_ref[...], sc1, sh1).astype(BF16)

    for j in range(D_SGU // SGU_CB):
        cs = slice(j * SGU_CB, (j + 1) * SGU_CB)
        u_scr[:, cs] = _gelu_tanh(_dot(h, win_ref[:, cs]) + bin_ref[:, cs])
    ssq = jnp.zeros((ts, 1), F32)
    for j in range(D_SGU // SGU_CB):
        cs = slice(j * SGU_CB, (j + 1) * SGU_CB)
        ws_cols = slice(D_SGU + j * SGU_CB, D_SGU + (j + 1) * SGU_CB)
        z = _gelu_tanh(_dot(h, win_ref[:, ws_cols]) + bin_ref[:, ws_cols])
        v_scr[:, cs] = z
        ssq = ssq + jnp.sum(z * z, axis=-1, keepdims=True)
    rs = lax.rsqrt(ssq / D_SGU + EPS)

    r_i = lax.broadcasted_iota(jnp.int32, (SGU_CHUNK, SGU_CHUNK), 0)
    c_i = lax.broadcasted_iota(jnp.int32, (SGU_CHUNK, SGU_CHUNK), 1)
    causal = c_i <= r_i
    for g in range(SGU_GROUPS):
        cs = slice(g * SGU_GROUP_DIM, (g + 1) * SGU_GROUP_DIM)
        wsg = jnp.where(causal, ws_ref[g], 0.0).astype(BF16)
        bsg = bst_ref[:, g:g + 1]
        for n in range(ts // SGU_CHUNK):
            rows = slice(n * SGU_CHUNK, (n + 1) * SGU_CHUNK)
            vn = (v_scr[rows, cs] * rs[rows] * vg_ref[:, cs]).astype(BF16)
            mixed = _dot(wsg, vn) + bsg
            y_scr[rows, cs] = (u_scr[rows, cs] * mixed).astype(BF16)

    m = _dot(y_scr[...], wout_ref[...])
    o_ref[0] = x + g1 * m


def _sgu(x, mod, n1g, w_in, b_in, v_g, w_s, b_s, w_out):
    bsz, s, d = x.shape
    ts = SGU_TS
    const = lambda *shape: pl.BlockSpec(shape, lambda b, i: (0,) * len(shape),
                                        pipeline_mode=pl.Buffered(1))
    return pl.pallas_call(
        _sgu_body,
        grid=(bsz, s // ts),
        in_specs=[
            pl.BlockSpec((1, ts, d), lambda b, i: (b, i, 0)),
            pl.BlockSpec((1, 6, d), lambda b, i: (b, 0, 0)),
            const(1, d),
            const(d, 2 * D_SGU),
            const(1, 2 * D_SGU),
            const(1, D_SGU),
            const(SGU_GROUPS, SGU_CHUNK, SGU_CHUNK),
            const(SGU_CHUNK, SGU_GROUPS),
            const(D_SGU, d),
        ],
        out_specs=pl.BlockSpec((1, ts, d), lambda b, i: (b, i, 0)),
        out_shape=SDS((bsz, s, d), F32),
        scratch_shapes=[
            pltpu.VMEM((ts, D_SGU), F32),
            pltpu.VMEM((ts, D_SGU), F32),
            pltpu.VMEM((ts, D_SGU), BF16),
        ],
        compiler_params=_cparams(("parallel", "arbitrary")),
        name="sgu",
    )(x, mod, n1g.reshape(1, d), w_in.astype(BF16), b_in.reshape(1, -1), v_g.reshape(1, -1),
      w_s, b_s.T, w_out.astype(BF16))


ROUTER_TS = 512


def _router_body(x_ref, mod_ref, n2g_ref, wrt_ref, brc_ref, h_ref, idx_ref, p_ref, rk_ref, cnt_ref, run_scr):
    @pl.when((pl.program_id(0) == 0) & (pl.program_id(1) == 0))
    def _():
        run_scr[...] = jnp.zeros(run_scr.shape, run_scr.dtype)

    x = x_ref[0]
    mod = mod_ref[0]
    sh2, sc2 = mod[3:4], mod[4:5]
    h2 = _norm_mod(x, n2g_ref[...], sc2, sh2)
    _store_row_tiles(h_ref, h2)
    ts = x.shape[0]
    logits = _dot_3pass_nt(wrt_ref[...], h2) + brc_ref[...]
    row = lax.broadcasted_iota(jnp.int32, logits.shape, 0)
    work = logits
    vals, idxs = [], []
    for _ in range(TOP_K):
        mx = jnp.max(work, axis=0, keepdims=True)
        ix = jnp.min(jnp.where(work == mx, row, N_EXPERTS), axis=0, keepdims=True)
        vals.append(mx)
        idxs.append(ix)
        work = jnp.where(row == ix, -jnp.inf, work)
    e = jnp.exp(jnp.concatenate(vals, axis=0) - vals[0])
    p_t = e / jnp.sum(e, axis=0, keepdims=True)
    idx_ref[0] = jnp.concatenate(idxs, axis=0)

    onehots = [row == ix for ix in idxs]
    mask = jnp.zeros(logits.shape, F32)
    for oh in onehots:
        mask = mask + jnp.where(oh, 1.0, 0.0)
    r_i = lax.broadcasted_iota(jnp.int32, (ts, ts), 0)
    c_i = lax.broadcasted_iota(jnp.int32, (ts, ts), 1)
    before = _dot(mask.astype(BF16), jnp.where(r_i < c_i, 1.0, 0.0).astype(BF16)) + run_scr[...]
    rk = [jnp.sum(jnp.where(oh, before, 0.0), axis=0, keepdims=True) for oh in onehots]
    rk_ref[0] = jnp.concatenate(rk, axis=0).astype(jnp.int32)
    run_scr[...] = run_scr[...] + jnp.sum(mask, axis=1, keepdims=True)
    cnt_ref[...] = run_scr[...].astype(jnp.int32)

    eye = jnp.where(r_i == c_i, 1.0, 0.0).astype(BF16)
    p_ref[0] = sum(lax.dot_general(eye, piece.astype(BF16), NT_DIMS, preferred_element_type=F32)
                   for piece in _bf16_pieces(p_t))


def _router(x, mod, n2g, w_r, b_r):
    bsz, s, d = x.shape
    ts = min(ROUTER_TS, s)
    kt = pl.BlockSpec((1, TOP_K, ts), lambda b, i: (b, 0, i))
    return pl.pallas_call(
        _router_body,
        grid=(bsz, s // ts),
        in_specs=[
            pl.BlockSpec((1, ts, d), lambda b, i: (b, i, 0)),
            pl.BlockSpec((1, 6, d), lambda b, i: (b, 0, 0)),
            pl.BlockSpec((1, d), lambda b, i: (0, 0)),
            pl.BlockSpec((N_EXPERTS, d), lambda b, i: (0, 0)),
            pl.BlockSpec((N_EXPERTS, 1), lambda b, i: (0, 0)),
        ],
        out_specs=[pl.BlockSpec((ts * ROW_TILE, LANES), lambda b, i: (b * (s // ts) + i, 0)),
                   kt, pl.BlockSpec((1, ts, TOP_K), lambda b, i: (b, i, 0)), kt,
                   pl.BlockSpec((N_EXPERTS, 1), lambda b, i: (0, 0))],
        out_shape=[SDS((bsz * s * ROW_TILE, LANES), F32), SDS((bsz, TOP_K, s), jnp.int32),
                   SDS((bsz, s, TOP_K), F32),
                   SDS((bsz, TOP_K, s), jnp.int32), SDS((N_EXPERTS, 1), jnp.int32)],
        scratch_shapes=[pltpu.VMEM((N_EXPERTS, 1), F32)],
        compiler_params=_cparams(("arbitrary", "arbitrary")),
        name="router",
    )(x, mod, n2g.reshape(1, d), w_r.T, b_r.reshape(-1, 1))


def _route_meta(top_idx, rank, counts):
    n = top_idx.size // TOP_K
    padded = (counts + EXPERT_BLOCK - 1) // EXPERT_BLOCK * EXPERT_BLOCK
    pend = jnp.cumsum(padded)
    pstart = pend - padded
    first_row = jnp.sum(jnp.where(top_idx[..., None] == jnp.arange(N_EXPERTS), pstart, 0), axis=-1)
    dest = (first_row + rank).astype(jnp.int32)
    n_rows = n * TOP_K + N_EXPERTS * EXPERT_BLOCK
    n_blk = n_rows // EXPERT_BLOCK
    blk_row0 = jnp.arange(n_blk) * EXPERT_BLOCK
    blk_exp = jnp.minimum(jnp.sum(pend[None, :] <= blk_row0[:, None], axis=1), N_EXPERTS - 1)
    n_used = (pend[-1] // EXPERT_BLOCK).astype(jnp.int32).reshape(1)
    eids = jnp.arange(N_EXPERTS)
    has_rows = counts > 0
    last_exp = jnp.max(jnp.where(has_rows, eids, 0))
    blk_exp = jnp.where(blk_row0 < pend[-1], blk_exp, last_exp).astype(jnp.int32)
    nxt_e = jnp.min(jnp.where((eids[None, :] > eids[:, None]) & has_rows[None, :], eids[None, :], N_EXPERTS), axis=1)
    nxt_e = jnp.where(nxt_e < N_EXPERTS, nxt_e, -1)
    slot_e = (jnp.cumsum(has_rows.astype(jnp.int32)) - 1) % 2
    blk_nxt = nxt_e[blk_exp].astype(jnp.int32)
    blk_slot = slot_e[blk_exp].astype(jnp.int32)
    pad_blk = jnp.where(padded > 0, pend - EXPERT_BLOCK, -1)
    tail = pend[-1] + jnp.arange(N_EXPERTS) * EXPERT_BLOCK
    tail_blk = jnp.where(tail < n_rows, tail, -1)
    pad_blk = jnp.concatenate([pad_blk, tail_blk]).astype(jnp.int32)
    return dest, (blk_exp, n_used, blk_nxt, blk_slot), pad_blk, n_rows


MOVE_TT = 1024
COMBINE_TT = 1024


def _per_step(dest, tt):
    bsz, k, s = dest.shape
    return dest.reshape(bsz, k, s // tt, tt).transpose(0, 2, 1, 3).reshape(bsz * (s // tt), 1, k * tt)


def _dispatch_body(pad_ref, dest_ref, h_ref, xs_ref, zero_scr, sem, zsem):
    tt = h_ref.shape[0] // ROW_TILE
    blk = EXPERT_BLOCK * ROW_TILE

    @pl.when(pl.program_id(0) == 0)
    def _():
        zero_scr[...] = jnp.zeros(zero_scr.shape, zero_scr.dtype)

        def zero_copy(e):
            row0 = pl.multiple_of(pad_ref[e], blk)
            return pltpu.make_async_copy(zero_scr, xs_ref.at[pl.ds(row0, blk)], zsem)

        for e in range(2 * N_EXPERTS):
            pl.when(pad_ref[e] >= 0)(lambda e=e: zero_copy(e).start())
        for e in range(2 * N_EXPERTS):
            pl.when(pad_ref[e] >= 0)(lambda e=e: zero_copy(e).wait())

    def issue(t, carry):
        src = h_ref.at[pl.ds(pl.multiple_of(t * ROW_TILE, ROW_TILE), ROW_TILE)]
        for k in range(TOP_K):
            d = pl.multiple_of(dest_ref[0, k * tt + t], ROW_TILE)
            pltpu.make_async_copy(src, xs_ref.at[pl.ds(d, ROW_TILE)], sem).start(priority=k % 2)
        return carry

    lax.fori_loop(0, tt, issue, 0)
    for k in range(TOP_K):
        pltpu.make_async_copy(h_ref, h_ref, sem).wait()


def _dispatch(h, dest8, pad_blk, n_rows):
    n = h.shape[0] // ROW_TILE
    tt = min(MOVE_TT, dest8.shape[-1])
    return pl.pallas_call(
        _dispatch_body,
        grid_spec=pltpu.PrefetchScalarGridSpec(
            num_scalar_prefetch=1,
            grid=(n // tt,),
            in_specs=[
                pl.BlockSpec((None, 1, TOP_K * tt), lambda i, pad: (i, 0, 0), memory_space=pltpu.SMEM),
                pl.BlockSpec((tt * ROW_TILE, LANES), lambda i, pad: (i, 0)),
            ],
            out_specs=pl.BlockSpec(memory_space=pl.ANY),
            scratch_shapes=[pltpu.VMEM((EXPERT_BLOCK * ROW_TILE, LANES), h.dtype), pltpu.SemaphoreType.DMA(()),
                            pltpu.SemaphoreType.DMA(())],
        ),
        out_shape=SDS((n_rows * ROW_TILE, LANES), h.dtype),
        compiler_params=_cparams(("arbitrary",)),
        name="dispatch",
    )(pad_blk * ROW_TILE, _per_step(dest8, tt), h)


def _ffn_body(layer, be_ref, nu_ref, nx_ref, sl_ref, xs_ref, wg_hbm, bg_ref, wu_hbm, bu_ref, wd_hbm, bd_ref,
              y_ref, w_f32, w_bf, sem):
    i = pl.program_id(0)
    slot = sl_ref[i]

    def fetch(e, s):
        return [pltpu.make_async_copy(w.at[layer, e], w_f32.at[s, n], sem.at[s])
                for n, w in enumerate((wg_hbm, wu_hbm, wd_hbm))]

    @pl.when(i == 0)
    def _():
        for c in fetch(be_ref[0], slot):
            c.start()

    @pl.when((i == 0) | (be_ref[i] != be_ref[jnp.maximum(i - 1, 0)]))
    def _():
        for c in fetch(be_ref[i], slot):
            c.wait()

        @pl.when(nx_ref[i] >= 0)
        def _():
            for c in fetch(nx_ref[i], 1 - slot):
                c.start()

        for n in range(3):
            w_bf[n] = w_f32[slot, n].astype(BF16)

    @pl.when(i < nu_ref[0])
    def _():
        x = _load_row_tiles(xs_ref, EXPERT_BLOCK).astype(BF16)
        g = jnp.minimum(_dot(x, w_bf[0]) + bg_ref[...], SWIGLU_LIMIT)
        u = jnp.clip(_dot(x, w_bf[1]) + bu_ref[...], -SWIGLU_LIMIT, SWIGLU_LIMIT)
        a = g * jax.nn.sigmoid(SWIGLU_ALPHA * g) * (u + 1.0)
        _store_row_tiles(y_ref, _dot(a.astype(BF16), w_bf[2]) + bd_ref[...])

    @pl.when(i >= nu_ref[0])
    def _():
        y_ref[...] = jnp.zeros(y_ref.shape, y_ref.dtype)


def _ffn(xs, blk_meta, layer, w_gate, b_gate, w_up, b_up, w_down, b_down):
    n_rows = xs.shape[0] // ROW_TILE
    _, e, d, de = w_gate.shape
    assert d == de == D_MODEL
    n_blk = n_rows // EXPERT_BLOCK
    blk = (EXPERT_BLOCK * ROW_TILE, LANES)
    bspec = lambda c: pl.BlockSpec((None, None, 1, c), lambda i, be, nu, nx, sl: (layer, be[i], 0, 0))
    hbm = pl.BlockSpec(memory_space=pl.ANY)
    xs_map = lambda i, be, nu, nx, sl: (jnp.minimum(i, jnp.maximum(nu[0] - 1, 0)), 0)
    return pl.pallas_call(
        functools.partial(_ffn_body, layer),
        grid_spec=pltpu.PrefetchScalarGridSpec(
            num_scalar_prefetch=4,
            grid=(n_blk,),
            in_specs=[
                pl.BlockSpec(blk, xs_map),
                hbm, bspec(de), hbm, bspec(de), hbm, bspec(d),
            ],
            out_specs=pl.BlockSpec(blk, lambda i, be, nu, nx, sl: (i, 0)),
            scratch_shapes=[pltpu.VMEM((2, 3, d, de), F32), pltpu.VMEM((3, d, de), BF16),
                            pltpu.SemaphoreType.DMA((2,))],
        ),
        out_shape=SDS((n_rows * ROW_TILE, LANES), F32),
        compiler_params=_cparams(("arbitrary",)),
        name="ffn",
    )(*blk_meta, xs, w_gate, b_gate.reshape(-1, e, 1, de), w_up, b_up.reshape(-1, e, 1, de),
      w_down, b_down.reshape(-1, e, 1, d))


COMBINE_RC = 128


def _combine_body(dest_ref, x_ref, mod_ref, p_ref, y_ref, o_ref, buf, sem):
    tt = x_ref.shape[0]

    def issue(t, carry):
        t8 = pl.multiple_of(t * ROW_TILE, ROW_TILE)
        for k in range(TOP_K):
            d = pl.multiple_of(dest_ref[0, k * tt + t], ROW_TILE)
            pltpu.make_async_copy(y_ref.at[pl.ds(d, ROW_TILE)], buf.at[k, pl.ds(t8, ROW_TILE)],
                                  sem).start(priority=k % 2)
        return carry

    lax.fori_loop(0, tt, issue, 0)
    pltpu.make_async_copy(buf, buf, sem).wait()

    g2 = mod_ref[0][5:6]
    for r0 in range(0, tt, COMBINE_RC):
        rows = slice(r0, r0 + COMBINE_RC)
        p = p_ref[rows, :]
        for j in range(ROW_TILE):
            cols = slice(j * LANES, (j + 1) * LANES)
            tile = lambda k: buf[k, pl.ds(r0 * ROW_TILE + j, COMBINE_RC, stride=ROW_TILE), :]
            acc = p[:, 0:1] * tile(0)
            for k in range(1, TOP_K):
                acc = acc + p[:, k:k + 1] * tile(k)
            o_ref[rows, cols] = x_ref[rows, cols] + g2[:, cols] * acc


def _combine(x, mod, p, y, dest8, seq):
    n, d = x.shape
    tt = min(COMBINE_TT, seq)
    n_steps = n // tt
    per_b = seq // tt
    return pl.pallas_call(
        _combine_body,
        grid=(n_steps,),
        in_specs=[
            pl.BlockSpec((None, 1, TOP_K * tt), lambda i: (i, 0, 0), memory_space=pltpu.SMEM),
            pl.BlockSpec((tt, d), lambda i: (i, 0)),
            pl.BlockSpec((1, 6, d), lambda i: (i // per_b, 0, 0)),
            pl.BlockSpec((tt, TOP_K), lambda i: (i, 0)),
            pl.BlockSpec(memory_space=pl.ANY),
        ],
        out_specs=pl.BlockSpec((tt, d), lambda i: (i, 0)),
        out_shape=SDS((n, d), F32),
        scratch_shapes=[pltpu.VMEM((TOP_K, tt * ROW_TILE, LANES), F32), pltpu.SemaphoreType.DMA(())],
        compiler_params=_cparams(("arbitrary",)),
        name="combine",
    )(_per_step(dest8, tt), x, mod, p, y)


def _moe(x, mod, layer, n2g, w_r, b_r, w_gate, b_gate, w_up, b_up, w_down, b_down):
    bsz, s, d = x.shape
    n = bsz * s
    h2, top_idx, probs, rank, counts = _router(x, mod, n2g, w_r, b_r)
    dest, blk_meta, pad_blk, n_rows = _route_meta(top_idx, rank, counts.reshape(N_EXPERTS))
    dest8 = dest * ROW_TILE
    xs = _dispatch(h2, dest8, pad_blk, n_rows)
    y = _ffn(xs, blk_meta, layer, w_gate, b_gate, w_up, b_up, w_down, b_down)
    out = _combine(x.reshape(n, d), mod, probs.reshape(n, TOP_K), y, dest8, s)
    return out.reshape(bsz, s, d)


def _seg_mats():
    r = lax.broadcasted_iota(jnp.int32, (D_MODEL, N_HEADS), 0) // HEAD_DIM
    c = lax.broadcasted_iota(jnp.int32, (D_MODEL, N_HEADS), 1)
    seg = jnp.where(r == c, 1.0, 0.0).astype(BF16)
    rt = lax.broadcasted_iota(jnp.int32, (N_HEADS, D_MODEL), 0)
    ct = lax.broadcasted_iota(jnp.int32, (N_HEADS, D_MODEL), 1) // HEAD_DIM
    seg_t = jnp.where(rt == ct, 1.0, 0.0).astype(BF16)
    return seg, seg_t


def _split_dot(a, b):
    hi = a.astype(BF16)
    lo = (a - hi.astype(F32)).astype(BF16)
    return _dot(hi, b) + _dot(lo, b)


def _head_rmsnorm(x, g, seg, seg_t):
    ms = _dot((x * x).astype(BF16), seg) * (1.0 / HEAD_DIM)
    r = lax.rsqrt(ms + EPS)
    return x * _split_dot(r, seg_t) * g


def _kvq_body(x_ref, mod_ref, kvmod_ref, n1g_ref, kvg_ref, wq_ref, wkv_ref, qg_ref, kg_ref,
              q_ref, k_ref, vt_ref, km_ref):
    x = x_ref[0]
    mod = mod_ref[0]
    kvmod = kvmod_ref[0]
    ms = jnp.mean(x * x, axis=-1, keepdims=True)
    xn = x * lax.rsqrt(ms + EPS)
    hq = ((xn * n1g_ref[...]) * (1.0 + mod[1:2]) + mod[0:1]).astype(BF16)
    hk = ((xn * kvg_ref[...]) * (1.0 + kvmod[1:2]) + kvmod[0:1]).astype(BF16)
    seg, seg_t = _seg_mats()
    q = _head_rmsnorm(_dot(hq, wq_ref[...]), qg_ref[...], seg, seg_t)
    q_ref[0] = q
    kv = _dot(hk, wkv_ref[...])
    k = _head_rmsnorm(kv[:, :D_MODEL], kg_ref[...], seg, seg_t)
    k_ref[0, 0] = k.astype(BF16)
    for u in range(ATT_U):
        rows = slice(u * MOBA_BLOCK, (u + 1) * MOBA_BLOCK)
        km_ref[0, 0, u:u + 1, :] = jnp.mean(k[rows], axis=0, keepdims=True)
    vt_ref[0, 0] = kv[:, D_MODEL:].T.astype(BF16)


def _kvq(x, mod, kvmod, n1g, kvg, w_q, w_kv, q_g, k_g):
    bsz, s, d = x.shape
    ts = ATT_U * MOBA_BLOCK
    nb = s // ts
    const = lambda *shape: pl.BlockSpec(shape, lambda b, i: (0,) * len(shape))
    return pl.pallas_call(
        _kvq_body,
        grid=(bsz, nb),
        in_specs=[
            pl.BlockSpec((1, ts, d), lambda b, i: (b, i, 0)),
            pl.BlockSpec((1, 6, d), lambda b, i: (b, 0, 0)),
            pl.BlockSpec((1, 2, d), lambda b, i: (b, 0, 0)),
            const(1, d), const(1, d), const(d, d), const(d, 2 * d), const(1, d), const(1, d),
        ],
        out_specs=[
            pl.BlockSpec((1, ts, d), lambda b, i: (b, i, 0)),
            pl.BlockSpec((1, 1, ts, d), lambda b, i: (b, i, 0, 0)),
            pl.BlockSpec((1, 1, d, ts), lambda b, i: (b, i, 0, 0)),
            pl.BlockSpec((1, 1, ATT_U, d), lambda b, i: (b, i, 0, 0)),
        ],
        out_shape=[
            SDS((bsz, s, d), F32),
            SDS((bsz, nb, ts, d), BF16),
            SDS((bsz, nb, d, ts), BF16),
            SDS((bsz, nb, ATT_U, d), F32),
        ],
        compiler_params=_cparams(("parallel", "arbitrary")),
        name="kvq",
    )(x, mod, kvmod, n1g.reshape(1, d), kvg.reshape(1, d), w_q.astype(BF16), w_kv.astype(BF16),
      jnp.tile(q_g, N_HEADS).reshape(1, d), jnp.tile(k_g, N_HEADS).reshape(1, d))


def _bf16_pieces(v):
    hi = v.astype(BF16).astype(F32)
    mid = (v - hi).astype(BF16).astype(F32)
    lo = ((v - hi) - mid).astype(BF16).astype(F32)
    return hi, mid, lo


def _attn_body(slopes_ref, q_ref, k_ref, vt_ref, km_ref, o_ref, cm_scr, kaug_scr, qaug_scr, s_scr):
    hp = pl.program_id(1)
    own = pl.program_id(2)
    bs = MOBA_BLOCK
    nb = k_ref.shape[1] * ATT_U
    pw = 2 * HEAD_DIM
    nh = 2 * ATT_HP

    @pl.when(own == 0)
    def _():
        rel = (lax.broadcasted_iota(jnp.int32, (bs, bs), 1)
               - lax.broadcasted_iota(jnp.int32, (bs, bs), 0))
        cm_scr[0] = jnp.zeros((bs, bs), F32)
        cm_scr[1] = jnp.where(rel >= 0, 0.0, -jnp.inf)
        klane = lax.broadcasted_iota(jnp.int32, (ATT_U * bs, ATT_AUG), 1)
        koff = (lax.broadcasted_iota(jnp.int32, (ATT_U * bs, ATT_AUG), 0) % bs).astype(F32)
        kaug_scr[...] = jnp.where(klane < 3, 1.0, jnp.where(klane < 6, koff, 0.0)).astype(BF16)
        qlane = lax.broadcasted_iota(jnp.int32, (bs, ATT_AUG), 1)
        qoff = lax.broadcasted_iota(jnp.int32, (bs, ATT_AUG), 0).astype(F32)
        for hd in range(nh):
            s2 = jnp.full((bs, ATT_AUG), slopes_ref[nh * hp + hd], F32) * LOG2E
            pieces = _bf16_pieces(-(s2 * qoff)) + _bf16_pieces(s2)
            aug = jnp.zeros((bs, ATT_AUG), F32)
            for lane_ix, piece in enumerate(pieces):
                aug = jnp.where(qlane == lane_ix, piece, aug)
            qaug_scr[hd] = aug.astype(BF16)

    dcol = lax.broadcasted_iota(jnp.int32, (1, pw), 1)
    blk_row = lax.broadcasted_iota(jnp.int32, (nb, 1), 0)
    valid = blk_row < own
    in_head = [(dcol >= h * HEAD_DIM) & (dcol < (h + 1) * HEAD_DIM) for h in range(2)]
    pair_cols = lambda hd: slice((hd // 2) * pw, (hd // 2 + 1) * pw)

    gates = []
    for pair in range(nh // 2):
        qp = q_ref[0, :, pair_cols(2 * pair)]
        kmp = km_ref[0, :, pair_cols(2 * pair)]
        km2 = jnp.concatenate([jnp.where(in_head[h], kmp, 0.0) for h in range(2)], axis=0)
        gates.append(lax.dot_general(km2, qp, NT_DIMS, precision=HIGHEST, preferred_element_type=F32))

    qs, selbias = [], []
    for hd in range(nh):
        slope = slopes_ref[nh * hp + hd]
        qm = jnp.where(in_head[hd % 2], q_ref[0, :, pair_cols(hd)], 0.0)
        gate = jnp.where(valid, gates[hd // 2][(hd % 2) * nb:(hd % 2 + 1) * nb], -jnp.inf)
        rank = jnp.zeros(gate.shape, F32)
        for i in range(nb):
            gi = gate[i:i + 1, :]
            tie = jnp.where(blk_row > i, 1.0, 0.0)
            rank = rank + jnp.where(gi > gate, 1.0, jnp.where(gi == gate, tie, 0.0))
        chosen = jnp.where(valid, jnp.where(rank < MOBA_TOPK, 1.0, 0.0),
                           jnp.where(blk_row == own, 1.0, 0.0))
        off = (own - blk_row).astype(F32) * (bs * LOG2E * slope)
        selbias.append(jnp.where(chosen > 0.5, -off, -jnp.inf))
        qs.append(jnp.concatenate([(qm * (LOG2E * HEAD_DIM ** -0.5)).astype(BF16), qaug_scr[hd]], axis=1))

    last = k_ref.shape[1] - 1
    ones = jnp.ones((ATT_ONES, ATT_U * bs), BF16)

    def qk(it, hd):
        kc = jnp.concatenate([k_ref[0, it, :, pair_cols(hd)], kaug_scr[...]], axis=1)
        return lax.dot_general(kc, qs[hd], NT_DIMS, preferred_element_type=F32)

    def visit(it, slot, carry, next_it, has_own):
        out = []
        if next_it is not None:
            for hd in range(nh):
                s_scr[1 - slot, hd] = qk(jnp.minimum(next_it, last), hd)
        for hd in range(nh):
            m, acc = carry[hd]

            def tile(u):
                t = s_scr[slot, hd, u * bs:(u + 1) * bs]
                if has_own:
                    t = t + cm_scr[(it * ATT_U + u == own).astype(jnp.int32)]
                return t

            rows = [jnp.sum(jnp.where(blk_row == it * ATT_U + u, selbias[hd], 0.0), axis=0, keepdims=True)
                    for u in range(ATT_U)]
            mx = m
            for u in range(ATT_U):
                mx = jnp.maximum(mx, jnp.max(tile(u), axis=0, keepdims=True) + rows[u])
            m_safe = jnp.where(mx == -jnp.inf, 0.0, mx)
            alpha = jnp.exp2(m - m_safe)
            p = jnp.concatenate([jnp.exp2((tile(u) + (rows[u] - m_safe)).astype(BF16)) for u in range(ATT_U)],
                                axis=0)
            vj = jnp.concatenate([vt_ref[0, it, hd * HEAD_DIM:(hd + 1) * HEAD_DIM, :], ones], axis=0)
            acc = alpha * acc + _dot(vj, p)
            out.append((mx, acc))
        return tuple(out)

    def body(i2, carry):
        carry = visit(2 * i2, 1, carry, 2 * i2 + 1, False)
        return visit(2 * i2 + 1, 0, carry, 2 * i2 + 2, False)

    n_past = own // ATT_U
    for hd in range(nh):
        s_scr[0, hd] = qk(n_past, hd)
    init = tuple((jnp.full((1, bs), -jnp.inf, F32), jnp.zeros((HEAD_DIM + ATT_ONES, bs), F32))
                 for hd in range(nh))
    carry = visit(n_past, 0, init, 0, True)
    carry = lax.fori_loop(0, n_past // 2, body, carry)
    carry = lax.cond(n_past % 2 == 1, lambda c: visit(n_past - 1, 1, c, None, False), lambda c: c, carry)
    outs = [acc[:HEAD_DIM] / acc[HEAD_DIM:HEAD_DIM + 1] for (_, acc) in carry]
    o_ref[0] = jnp.concatenate(outs, axis=0).astype(o_ref.dtype)


def _attn(q, kb, vtb, kmean):
    bsz, s, d = q.shape
    nsb = s // (ATT_U * MOBA_BLOCK)
    assert nsb % 2 == 0, "the attention loop visits key super-blocks in pairs"
    nb = s // MOBA_BLOCK
    w = 2 * HEAD_DIM * ATT_HP
    slopes = (2.0 ** (-8.0 * jnp.arange(1, N_HEADS + 1, dtype=F32) / N_HEADS)).astype(F32)
    return pl.pallas_call(
        _attn_body,
        grid_spec=pltpu.PrefetchScalarGridSpec(
            num_scalar_prefetch=1,
            grid=(bsz, N_HEADS // (2 * ATT_HP), nb),
            in_specs=[
                pl.BlockSpec((1, MOBA_BLOCK, w), lambda b, h, c, sl: (b, c, h)),
                pl.BlockSpec((1, nsb, ATT_U * MOBA_BLOCK, w), lambda b, h, c, sl: (b, 0, 0, h)),
                pl.BlockSpec((1, nsb, w, ATT_U * MOBA_BLOCK), lambda b, h, c, sl: (b, 0, h, 0)),
                pl.BlockSpec((1, nb, w), lambda b, h, c, sl: (b, 0, h)),
            ],
            out_specs=pl.BlockSpec((1, w, MOBA_BLOCK), lambda b, h, c, sl: (b, h, c)),
            scratch_shapes=[pltpu.VMEM((2, MOBA_BLOCK, MOBA_BLOCK), F32),
                            pltpu.VMEM((ATT_U * MOBA_BLOCK, ATT_AUG), BF16),
                            pltpu.VMEM((2 * ATT_HP, MOBA_BLOCK, ATT_AUG), BF16),
                            pltpu.VMEM((2, 2 * ATT_HP, ATT_U * MOBA_BLOCK, MOBA_BLOCK), F32)],
        ),
        out_shape=SDS((bsz, d, s), BF16),
        compiler_params=_cparams(("parallel", "parallel", "arbitrary")),
        name="attn",
    )(slopes, q, kb, vtb, kmean)


OPROJ_TS = 512


def _oproj_body(at_ref, x_ref, mod_ref, wo_ref, o_ref):
    g1 = mod_ref[0][2:3]
    m = lax.dot_general(at_ref[0], wo_ref[...], (((0,), (0,)), ((), ())), preferred_element_type=F32)
    o_ref[0] = x_ref[0] + g1 * m


def _oproj(at, x, mod, w_o):
    bsz, s, d = x.shape
    ts = min(OPROJ_TS, s)
    tok = pl.BlockSpec((1, ts, d), lambda b, i: (b, i, 0))
    return pl.pallas_call(
        _oproj_body,
        grid=(bsz, s // ts),
        in_specs=[pl.BlockSpec((1, d, ts), lambda b, i: (b, 0, i)), tok,
                  pl.BlockSpec((1, 6, d), lambda b, i: (b, 0, 0)),
                  pl.BlockSpec((d, d), lambda b, i: (0, 0))],
        out_specs=tok,
        out_shape=SDS((bsz, s, d), F32),
        compiler_params=_cparams(("parallel", "arbitrary")),
        name="oproj",
    )(at, x, mod, w_o.astype(BF16))


def kernel(x, c, ada_w, ada_b, norm1_g, norm2_g, sgu_w_in, sgu_b_in, sgu_v_g, sgu_w_s, sgu_b_s, sgu_w_out,
           kv_ada_w, kv_ada_b, kv_norm_g, w_kv, k_norm_g, attn_w_q, q_norm_g, attn_w_o,
           moe_w_router, moe_b_router, moe_w_gate, moe_b_gate, moe_w_up, moe_b_up, moe_w_down, moe_b_down):
    bsz, s, d = x.shape
    moe = lambda l, xx, mod: _moe(xx, mod, l, norm2_g[l], moe_w_router[l], moe_b_router[l],
                                  moe_w_gate, moe_b_gate, moe_w_up, moe_b_up, moe_w_down, moe_b_down)
    mod0 = _ada(c, ada_w, ada_b[0], 0).reshape(bsz, 6, d)
    x = _sgu(x, mod0, norm1_g[0], sgu_w_in[0], sgu_b_in[0], sgu_v_g[0], sgu_w_s[0], sgu_b_s[0], sgu_w_out[0])
    x = moe(0, x, mod0)
    mod1 = _ada(c, ada_w, ada_b[1], 1).reshape(bsz, 6, d)
    kvmod = _ada(c, kv_ada_w, kv_ada_b).reshape(bsz, 2, d)
    q, kb, vtb, kmean = _kvq(x, mod1, kvmod, norm1_g[1], kv_norm_g, attn_w_q[0], w_kv, q_norm_g[0], k_norm_g)
    a = _attn(q, kb, vtb, kmean.reshape(bsz, s // MOBA_BLOCK, d))
    x = _oproj(a, x, mod1, attn_w_o[0])
    x = moe(1, x, mod1)
    return x
```

```python
import functools
import math

import jax
import jax.numpy as jnp
import numpy as np
from jax import lax
from jax.experimental import pallas as pl
from jax.experimental.pallas import tpu as pltpu

F32 = jnp.float32
BF16 = jnp.bfloat16
HIGHEST = lax.Precision.HIGHEST
SDS = jax.ShapeDtypeStruct

D_MODEL = 1024
SGU_CHUNK = 128
SGU_GROUPS = 8
D_SGU = 3 * D_MODEL
SGU_GROUP_DIM = D_SGU // SGU_GROUPS
HEAD_DIM = 64
N_HEADS = D_MODEL // HEAD_DIM
MOBA_BLOCK = 256
MOBA_TOPK = 3
ATT_U = 2
ATT_ONES = 16
ATT_AUG = 128
ATT_HP = 4
LOG2E = math.log2(math.e)
N_EXPERTS = 32
TOP_K = 4
SWIGLU_LIMIT = 7.0
SWIGLU_ALPHA = 1.702
EXPERT_BLOCK = 512
EPS = 1e-6

VMEM_LIMIT_BYTES = 56 * 1024 * 1024
NT_DIMS = (((1,), (1,)), ((), ()))


def _cparams(sem):
    return pltpu.CompilerParams(dimension_semantics=sem, vmem_limit_bytes=VMEM_LIMIT_BYTES)


def _dot(a, b, **kw):
    return jnp.dot(a, b, preferred_element_type=F32, **kw)


LANES = 128
ROW_TILE = D_MODEL // LANES


def _store_row_tiles(ref, val):
    rows = val.shape[0]
    for j in range(ROW_TILE):
        ref[pl.ds(j, rows, stride=ROW_TILE), :] = val[:, j * LANES:(j + 1) * LANES]


def _load_row_tiles(ref, rows):
    return jnp.concatenate([ref[pl.ds(j, rows, stride=ROW_TILE), :] for j in range(ROW_TILE)], axis=-1)


def _dot_3pass_nt(a, b):
    a_hi = a.astype(BF16)
    a_lo = (a - a_hi.astype(F32)).astype(BF16)
    b_hi = b.astype(BF16)
    b_lo = (b - b_hi.astype(F32)).astype(BF16)
    nt = lambda u, v: lax.dot_general(u, v, NT_DIMS, preferred_element_type=F32)
    return nt(a_hi, b_hi) + (nt(a_lo, b_hi) + nt(a_hi, b_lo))


def _norm_mod(x, g, sc, sh):
    ms = jnp.mean(x * x, axis=-1, keepdims=True)
    return (x * lax.rsqrt(ms + EPS) * g) * (1.0 + sc) + sh


def _ada_body(c_ref, w_ref, b_ref, o_ref):
    c = c_ref[...]
    ca = c * jax.nn.sigmoid(c)
    o_ref[...] = _dot(ca, w_ref[...], precision=HIGHEST) + b_ref[...]


def _ada(c, w, b, layer=0):
    bsz, d = c.shape
    n = w.shape[-1]
    tn = 1024
    w = w.reshape(-1, d, n)
    return pl.pallas_call(
        _ada_body,
        grid=(n // tn,),
        in_specs=[
            pl.BlockSpec((bsz, d), lambda j: (0, 0)),
            pl.BlockSpec((None, d, tn), lambda j: (layer, 0, j)),
            pl.BlockSpec((1, tn), lambda j: (0, j)),
        ],
        out_specs=pl.BlockSpec((bsz, tn), lambda j: (0, j)),
        out_shape=SDS((bsz, n), F32),
        compiler_params=_cparams(("arbitrary",)),
        name="ada",
    )(c, w, b.reshape(1, n))


SGU_TS = 512
SGU_CB = 768


def _gelu_tanh(x):
    c = np.float32(np.sqrt(2.0 / np.pi))
    return x * (0.5 * (1.0 + jnp.tanh(c * (x + 0.044715 * (x * x * x)))))


def _sgu_body(x_ref, mod_ref, n1g_ref, win_ref, bin_ref, vg_ref, ws_ref, bst_ref, wout_ref,
              o_ref, u_scr, v_scr, y_scr):
    ts = x_ref.shape[1]
    x = x_ref[0]
    mod = mod_ref[0]
    sh1, sc1, g1 = mod[0:1], mod[1:2], mod[2:3]
    h = _norm_mod(x, n1g_ref[...], sc1, sh1).astype(BF16)

    for j in range(D_SGU // SGU_CB):
        cs = slice(j * SGU_CB, (j + 1) * SGU_CB)
        u_scr[:, cs] = _gelu_tanh(_dot(h, win_ref[:, cs]) + bin_ref[:, cs])
    ssq = jnp.zeros((ts, 1), F32)
    for j in range(D_SGU // SGU_CB):
        cs = slice(j * SGU_CB, (j + 1) * SGU_CB)
        ws_cols = slice(D_SGU + j * SGU_CB, D_SGU + (j + 1) * SGU_CB)
        z = _gelu_tanh(_dot(h, win_ref[:, ws_cols]) + bin_ref[:, ws_cols])
        v_scr[:, cs] = z
        ssq = ssq + jnp.sum(z * z, axis=-1, keepdims=True)
    rs = lax.rsqrt(ssq / D_SGU + EPS)

    r_i = lax.broadcasted_iota(jnp.int32, (SGU_CHUNK, SGU_CHUNK), 0)
    c_i = lax.broadcasted_iota(jnp.int32, (SGU_CHUNK, SGU_CHUNK), 1)
    causal = c_i <= r_i
    for g in range(SGU_GROUPS):
        cs = slice(g * SGU_GROUP_DIM, (g + 1) * SGU_GROUP_DIM)
        wsg = jnp.where(causal, ws_ref[g], 0.0).astype(BF16)
        bsg = bst_ref[:, g:g + 1]
        for n in range(ts // SGU_CHUNK):
            rows = slice(n * SGU_CHUNK, (n + 1) * SGU_CHUNK)
            vn = (v_scr[rows, cs] * rs[rows] * vg_ref[:, cs]).astype(BF16)
            mixed = _dot(wsg, vn) + bsg
            y_scr[rows, cs] = (u_scr[rows, cs] * mixed).astype(BF16)

    m = _dot(y_scr[...], wout_ref[...])
    o_ref[0] = x + g1 * m


def _sgu(x, mod, n1g, w_in, b_in, v_g, w_s, b_s, w_out):
    bsz, s, d = x.shape
    ts = SGU_TS
    const = lambda *shape: pl.BlockSpec(shape, lambda b, i: (0,) * len(shape),
                                        pipeline_mode=pl.Buffered(1))
    return pl.pallas_call(
        _sgu_body,
        grid=(bsz, s // ts),
        in_specs=[
            pl.BlockSpec((1, ts, d), lambda b, i: (b, i, 0)),
            pl.BlockSpec((1, 6, d), lambda b, i: (b, 0, 0)),
            const(1, d),
            const(d, 2 * D_SGU),
            const(1, 2 * D_SGU),
            const(1, D_SGU),
            const(SGU_GROUPS, SGU_CHUNK, SGU_CHUNK),
            const(SGU_CHUNK, SGU_GROUPS),
            const(D_SGU, d),
        ],
        out_specs=pl.BlockSpec((1, ts, d), lambda b, i: (b, i, 0)),
        out_shape=SDS((bsz, s, d), F32),
        scratch_shapes=[
            pltpu.VMEM((ts, D_SGU), F32),
            pltpu.VMEM((ts, D_SGU), F32),
            pltpu.VMEM((ts, D_SGU), BF16),
        ],
        compiler_params=_cparams(("parallel", "arbitrary")),
        name="sgu",
    )(x, mod, n1g.reshape(1, d), w_in.astype(BF16), b_in.reshape(1, -1), v_g.reshape(1, -1),
      w_s, b_s.T, w_out.astype(BF16))


ROUTER_TS = 512


def _router_body(x_ref, mod_ref, n2g_ref, wrt_ref, brc_ref, h_ref, idx_ref, p_ref, rk_ref, cnt_ref, run_scr):
    @pl.when((pl.program_id(0) == 0) & (pl.program_id(1) == 0))
    def _():
        run_scr[...] = jnp.zeros(run_scr.shape, run_scr.dtype)

    x = x_ref[0]
    mod = mod_ref[0]
    sh2, sc2 = mod[3:4], mod[4:5]
    h2 = _norm_mod(x, n2g_ref[...], sc2, sh2)
    _store_row_tiles(h_ref, h2)
    ts = x.shape[0]
    logits = _dot_3pass_nt(wrt_ref[...], h2) + brc_ref[...]
    row = lax.broadcasted_iota(jnp.int32, logits.shape, 0)
    work = logits
    vals, idxs = [], []
    for _ in range(TOP_K):
        mx = jnp.max(work, axis=0, keepdims=True)
        ix = jnp.min(jnp.where(work == mx, row, N_EXPERTS), axis=0, keepdims=True)
        vals.append(mx)
        idxs.append(ix)
        work = jnp.where(row == ix, -jnp.inf, work)
    e = jnp.exp(jnp.concatenate(vals, axis=0) - vals[0])
    p_t = e / jnp.sum(e, axis=0, keepdims=True)
    idx_ref[0] = jnp.concatenate(idxs, axis=0)

    onehots = [row == ix for ix in idxs]
    mask = jnp.zeros(logits.shape, F32)
    for oh in onehots:
        mask = mask + jnp.where(oh, 1.0, 0.0)
    r_i = lax.broadcasted_iota(jnp.int32, (ts, ts), 0)
    c_i = lax.broadcasted_iota(jnp.int32, (ts, ts), 1)
    before = _dot(mask.astype(BF16), jnp.where(r_i < c_i, 1.0, 0.0).astype(BF16)) + run_scr[...]
    rk = [jnp.sum(jnp.where(oh, before, 0.0), axis=0, keepdims=True) for oh in onehots]
    rk_ref[0] = jnp.concatenate(rk, axis=0).astype(jnp.int32)
    run_scr[...] = run_scr[...] + jnp.sum(mask, axis=1, keepdims=True)
    cnt_ref[...] = run_scr[...].astype(jnp.int32)

    eye = jnp.where(r_i == c_i, 1.0, 0.0).astype(BF16)
    p_ref[0] = sum(lax.dot_general(eye, piece.astype(BF16), NT_DIMS, preferred_element_type=F32)
                   for piece in _bf16_pieces(p_t))


def _router(x, mod, n2g, w_r, b_r):
    bsz, s, d = x.shape
    ts = min(ROUTER_TS, s)
    kt = pl.BlockSpec((1, TOP_K, ts), lambda b, i: (b, 0, i))
    return pl.pallas_call(
        _router_body,
        grid=(bsz, s // ts),
        in_specs=[
            pl.BlockSpec((1, ts, d), lambda b, i: (b, i, 0)),
            pl.BlockSpec((1, 6, d), lambda b, i: (b, 0, 0)),
            pl.BlockSpec((1, d), lambda b, i: (0, 0)),
            pl.BlockSpec((N_EXPERTS, d), lambda b, i: (0, 0)),
            pl.BlockSpec((N_EXPERTS, 1), lambda b, i: (0, 0)),
        ],
        out_specs=[pl.BlockSpec((ts * ROW_TILE, LANES), lambda b, i: (b * (s // ts) + i, 0)),
                   kt, pl.BlockSpec((1, ts, TOP_K), lambda b, i: (b, i, 0)), kt,
                   pl.BlockSpec((N_EXPERTS, 1), lambda b, i: (0, 0))],
        out_shape=[SDS((bsz * s * ROW_TILE, LANES), F32), SDS((bsz, TOP_K, s), jnp.int32),
                   SDS((bsz, s, TOP_K), F32),
                   SDS((bsz, TOP_K, s), jnp.int32), SDS((N_EXPERTS, 1), jnp.int32)],
        scratch_shapes=[pltpu.VMEM((N_EXPERTS, 1), F32)],
        compiler_params=_cparams(("arbitrary", "arbitrary")),
        name="router",
    )(x, mod, n2g.reshape(1, d), w_r.T, b_r.reshape(-1, 1))


def _route_meta(top_idx, rank, counts):
    n = top_idx.size // TOP_K
    padded = (counts + EXPERT_BLOCK - 1) // EXPERT_BLOCK * EXPERT_BLOCK
    pend = jnp.cumsum(padded)
    pstart = pend - padded
    first_row = jnp.sum(jnp.where(top_idx[..., None] == jnp.arange(N_EXPERTS), pstart, 0), axis=-1)
    dest = (first_row + rank).astype(jnp.int32)
    n_rows = n * TOP_K + N_EXPERTS * EXPERT_BLOCK
    n_blk = n_rows // EXPERT_BLOCK
    blk_row0 = jnp.arange(n_blk) * EXPERT_BLOCK
    blk_exp = jnp.minimum(jnp.sum(pend[None, :] <= blk_row0[:, None], axis=1), N_EXPERTS - 1)
    n_used = (pend[-1] // EXPERT_BLOCK).astype(jnp.int32).reshape(1)
    eids = jnp.arange(N_EXPERTS)
    has_rows = counts > 0
    last_exp = jnp.max(jnp.where(has_rows, eids, 0))
    blk_exp = jnp.where(blk_row0 < pend[-1], blk_exp, last_exp).astype(jnp.int32)
    nxt_e = jnp.min(jnp.where((eids[None, :] > eids[:, None]) & has_rows[None, :], eids[None, :], N_EXPERTS), axis=1)
    nxt_e = jnp.where(nxt_e < N_EXPERTS, nxt_e, -1)
    slot_e = (jnp.cumsum(has_rows.astype(jnp.int32)) - 1) % 2
    blk_nxt = nxt_e[blk_exp].astype(jnp.int32)
    blk_slot = slot_e[blk_exp].astype(jnp.int32)
    pad_blk = jnp.where(padded > 0, pend - EXPERT_BLOCK, -1)
    tail = pend[-1] + jnp.arange(N_EXPERTS) * EXPERT_BLOCK
    tail_blk = jnp.where(tail < n_rows, tail, -1)
    pad_blk = jnp.concatenate([pad_blk, tail_blk]).astype(jnp.int32)
    return dest, (blk_exp, n_used, blk_nxt, blk_slot), pad_blk, n_rows


MOVE_TT = 1024
COMBINE_TT = 1024


def _per_step(dest, tt):
    bsz, k, s = dest.shape
    return dest.reshape(bsz, k, s // tt, tt).transpose(0, 2, 1, 3).reshape(bsz * (s // tt), 1, k * tt)


def _dispatch_body(pad_ref, dest_ref, h_ref, xs_ref, zero_scr, sem, zsem):
    tt = h_ref.shape[0] // ROW_TILE
    blk = EXPERT_BLOCK * ROW_TILE

    @pl.when(pl.program_id(0) == 0)
    def _():
        zero_scr[...] = jnp.zeros(zero_scr.shape, zero_scr.dtype)

        def zero_copy(e):
            row0 = pl.multiple_of(pad_ref[e], blk)
            return pltpu.make_async_copy(zero_scr, xs_ref.at[pl.ds(row0, blk)], zsem)

        for e in range(2 * N_EXPERTS):
            pl.when(pad_ref[e] >= 0)(lambda e=e: zero_copy(e).start())
        for e in range(2 * N_EXPERTS):
            pl.when(pad_ref[e] >= 0)(lambda e=e: zero_copy(e).wait())

    def issue(t, carry):
        src = h_ref.at[pl.ds(pl.multiple_of(t * ROW_TILE, ROW_TILE), ROW_TILE)]
        for k in range(TOP_K):
            d = pl.multiple_of(dest_ref[0, k * tt + t], ROW_TILE)
            pltpu.make_async_copy(src, xs_ref.at[pl.ds(d, ROW_TILE)], sem).start(priority=k % 2)
        return carry

    lax.fori_loop(0, tt, issue, 0)
    for k in range(TOP_K):
        pltpu.make_async_copy(h_ref, h_ref, sem).wait()


def _dispatch(h, dest8, pad_blk, n_rows):
    n = h.shape[0] // ROW_TILE
    tt = min(MOVE_TT, dest8.shape[-1])
    return pl.pallas_call(
        _dispatch_body,
        grid_spec=pltpu.PrefetchScalarGridSpec(
            num_scalar_prefetch=1,
            grid=(n // tt,),
            in_specs=[
                pl.BlockSpec((None, 1, TOP_K * tt), lambda i, pad: (i, 0, 0), memory_space=pltpu.SMEM),
                pl.BlockSpec((tt * ROW_TILE, LANES), lambda i, pad: (i, 0)),
            ],
            out_specs=pl.BlockSpec(memory_space=pl.ANY),
            scratch_shapes=[pltpu.VMEM((EXPERT_BLOCK * ROW_TILE, LANES), h.dtype), pltpu.SemaphoreType.DMA(()),
                            pltpu.SemaphoreType.DMA(())],
        ),
        out_shape=SDS((n_rows * ROW_TILE, LANES), h.dtype),
        compiler_params=_cparams(("arbitrary",)),
        name="dispatch",
    )(pad_blk * ROW_TILE, _per_step(dest8, tt), h)


def _ffn_body(layer, be_ref, nu_ref, nx_ref, sl_ref, xs_ref, wg_hbm, bg_ref, wu_hbm, bu_ref, wd_hbm, bd_ref,
              y_ref, w_f32, w_bf, sem):
    i = pl.program_id(0)
    slot = sl_ref[i]

    def fetch(e, s):
        return [pltpu.make_async_copy(w.at[layer, e], w_f32.at[s, n], sem.at[s])
                for n, w in enumerate((wg_hbm, wu_hbm, wd_hbm))]

    @pl.when(i == 0)
    def _():
        for c in fetch(be_ref[0], slot):
            c.start()

    @pl.when((i == 0) | (be_ref[i] != be_ref[jnp.maximum(i - 1, 0)]))
    def _():
        for c in fetch(be_ref[i], slot):
            c.wait()

        @pl.when(nx_ref[i] >= 0)
        def _():
            for c in fetch(nx_ref[i], 1 - slot):
                c.start()

        for n in range(3):
            w_bf[n] = w_f32[slot, n].astype(BF16)

    @pl.when(i < nu_ref[0])
    def _():
        x = _load_row_tiles(xs_ref, EXPERT_BLOCK).astype(BF16)
        g = jnp.minimum(_dot(x, w_bf[0]) + bg_ref[...], SWIGLU_LIMIT)
        u = jnp.clip(_dot(x, w_bf[1]) + bu_ref[...], -SWIGLU_LIMIT, SWIGLU_LIMIT)
        a = g * jax.nn.sigmoid(SWIGLU_ALPHA * g) * (u + 1.0)
        _store_row_tiles(y_ref, _dot(a.astype(BF16), w_bf[2]) + bd_ref[...])

    @pl.when(i >= nu_ref[0])
    def _():
        y_ref[...] = jnp.zeros(y_ref.shape, y_ref.dtype)


def _ffn(xs, blk_meta, layer, w_gate, b_gate, w_up, b_up, w_down, b_down):
    n_rows = xs.shape[0] // ROW_TILE
    _, e, d, de = w_gate.shape
    assert d == de == D_MODEL
    n_blk = n_rows // EXPERT_BLOCK
    blk = (EXPERT_BLOCK * ROW_TILE, LANES)
    bspec = lambda c: pl.BlockSpec((None, None, 1, c), lambda i, be, nu, nx, sl: (layer, be[i], 0, 0))
    hbm = pl.BlockSpec(memory_space=pl.ANY)
    xs_map = lambda i, be, nu, nx, sl: (jnp.minimum(i, jnp.maximum(nu[0] - 1, 0)), 0)
    return pl.pallas_call(
        functools.partial(_ffn_body, layer),
        grid_spec=pltpu.PrefetchScalarGridSpec(
            num_scalar_prefetch=4,
            grid=(n_blk,),
            in_specs=[
                pl.BlockSpec(blk, xs_map),
                hbm, bspec(de), hbm, bspec(de), hbm, bspec(d),
            ],
            out_specs=pl.BlockSpec(blk, lambda i, be, nu, nx, sl: (i, 0)),
            scratch_shapes=[pltpu.VMEM((2, 3, d, de), F32), pltpu.VMEM((3, d, de), BF16),
                            pltpu.SemaphoreType.DMA((2,))],
        ),
        out_shape=SDS((n_rows * ROW_TILE, LANES), F32),
        compiler_params=_cparams(("arbitrary",)),
        name="ffn",
    )(*blk_meta, xs, w_gate, b_gate.reshape(-1, e, 1, de), w_up, b_up.reshape(-1, e, 1, de),
      w_down, b_down.reshape(-1, e, 1, d))


COMBINE_RC = 128
COMBINE_PARTS = 4


def _combine_body(dest_ref, x_ref, mod_ref, p_ref, y_ref, o_ref, buf, sem):
    tt = x_ref.shape[0]
    tp = tt // COMBINE_PARTS

    def part_rows(part):
        return pl.ds(part * tp * ROW_TILE, tp * ROW_TILE)

    for part in range(COMBINE_PARTS):
        def issue(t, carry, part=part):
            t8 = pl.multiple_of(t * ROW_TILE, ROW_TILE)
            for k in range(TOP_K):
                d = pl.multiple_of(dest_ref[0, k * tt + t], ROW_TILE)
                pltpu.make_async_copy(y_ref.at[pl.ds(d, ROW_TILE)], buf.at[k, pl.ds(t8, ROW_TILE)],
                                      sem.at[part]).start(priority=k % 2)
            return carry

        lax.fori_loop(part * tp, (part + 1) * tp, issue, 0)

    g2 = mod_ref[0][5:6]
    for part in range(COMBINE_PARTS):
        pltpu.make_async_copy(buf.at[:, part_rows(part)], buf.at[:, part_rows(part)], sem.at[part]).wait()
        for r0 in range(part * tp, (part + 1) * tp, COMBINE_RC):
            rows = slice(r0, r0 + COMBINE_RC)
            p = p_ref[rows, :]
            for j in range(ROW_TILE):
                cols = slice(j * LANES, (j + 1) * LANES)
                tile = lambda k: buf[k, pl.ds(r0 * ROW_TILE + j, COMBINE_RC, stride=ROW_TILE), :]
                acc = p[:, 0:1] * tile(0)
                for k in range(1, TOP_K):
                    acc = acc + p[:, k:k + 1] * tile(k)
                o_ref[rows, cols] = x_ref[rows, cols] + g2[:, cols] * acc


def _combine(x, mod, p, y, dest8, seq):
    n, d = x.shape
    tt = min(COMBINE_TT, seq)
    n_steps = n // tt
    per_b = seq // tt
    return pl.pallas_call(
        _combine_body,
        grid=(n_steps,),
        in_specs=[
            pl.BlockSpec((None, 1, TOP_K * tt), lambda i: (i, 0, 0), memory_space=pltpu.SMEM),
            pl.BlockSpec((tt, d), lambda i: (i, 0)),
            pl.BlockSpec((1, 6, d), lambda i: (i // per_b, 0, 0)),
            pl.BlockSpec((tt, TOP_K), lambda i: (i, 0)),
            pl.BlockSpec(memory_space=pl.ANY),
        ],
        out_specs=pl.BlockSpec((tt, d), lambda i: (i, 0)),
        out_shape=SDS((n, d), F32),
        scratch_shapes=[pltpu.VMEM((TOP_K, tt * ROW_TILE, LANES), F32),
                        pltpu.SemaphoreType.DMA((COMBINE_PARTS,))],
        compiler_params=_cparams(("arbitrary",)),
        name="combine",
    )(_per_step(dest8, tt), x, mod, p, y)


def _moe(x, mod, layer, n2g, w_r, b_r, w_gate, b_gate, w_up, b_up, w_down, b_down):
    bsz, s, d = x.shape
    n = bsz * s
    h2, top_idx, probs, rank, counts = _router(x, mod, n2g, w_r, b_r)
    dest, blk_meta, pad_blk, n_rows = _route_meta(top_idx, rank, counts.reshape(N_EXPERTS))
    dest8 = dest * ROW_TILE
    xs = _dispatch(h2, dest8, pad_blk, n_rows)
    y = _ffn(xs, blk_meta, layer, w_gate, b_gate, w_up, b_up, w_down, b_down)
    out = _combine(x.reshape(n, d), mod, probs.reshape(n, TOP_K), y, dest8, s)
    return out.reshape(bsz, s, d)


def _seg_mats():
    r = lax.broadcasted_iota(jnp.int32, (D_MODEL, N_HEADS), 0) // HEAD_DIM
    c = lax.broadcasted_iota(jnp.int32, (D_MODEL, N_HEADS), 1)
    seg = jnp.where(r == c, 1.0, 0.0).astype(BF16)
    rt = lax.broadcasted_iota(jnp.int32, (N_HEADS, D_MODEL), 0)
    ct = lax.broadcasted_iota(jnp.int32, (N_HEADS, D_MODEL), 1) // HEAD_DIM
    seg_t = jnp.where(rt == ct, 1.0, 0.0).astype(BF16)
    return seg, seg_t


def _split_dot(a, b):
    hi = a.astype(BF16)
    lo = (a - hi.astype(F32)).astype(BF16)
    return _dot(hi, b) + _dot(lo, b)


def _head_rmsnorm(x, g, seg, seg_t):
    ms = _dot((x * x).astype(BF16), seg) * (1.0 / HEAD_DIM)
    r = lax.rsqrt(ms + EPS)
    return x * _split_dot(r, seg_t) * g


def _kvq_body(x_ref, mod_ref, kvmod_ref, n1g_ref, kvg_ref, wq_ref, wkv_ref, qg_ref, kg_ref,
              q_ref, k_ref, vt_ref, km_ref):
    x = x_ref[0]
    mod = mod_ref[0]
    kvmod = kvmod_ref[0]
    ms = jnp.mean(x * x, axis=-1, keepdims=True)
    xn = x * lax.rsqrt(ms + EPS)
    hq = ((xn * n1g_ref[...]) * (1.0 + mod[1:2]) + mod[0:1]).astype(BF16)
    hk = ((xn * kvg_ref[...]) * (1.0 + kvmod[1:2]) + kvmod[0:1]).astype(BF16)
    seg, seg_t = _seg_mats()
    q = _head_rmsnorm(_dot(hq, wq_ref[...]), qg_ref[...], seg, seg_t)
    q_ref[0] = q
    kv = _dot(hk, wkv_ref[...])
    k = _head_rmsnorm(kv[:, :D_MODEL], kg_ref[...], seg, seg_t)
    k_ref[0, 0] = k.astype(BF16)
    for u in range(ATT_U):
        rows = slice(u * MOBA_BLOCK, (u + 1) * MOBA_BLOCK)
        km_ref[0, 0, u:u + 1, :] = jnp.mean(k[rows], axis=0, keepdims=True)
    vt_ref[0, 0] = kv[:, D_MODEL:].T.astype(BF16)


def _kvq(x, mod, kvmod, n1g, kvg, w_q, w_kv, q_g, k_g):
    bsz, s, d = x.shape
    ts = ATT_U * MOBA_BLOCK
    nb = s // ts
    const = lambda *shape: pl.BlockSpec(shape, lambda b, i: (0,) * len(shape))
    return pl.pallas_call(
        _kvq_body,
        grid=(bsz, nb),
        in_specs=[
            pl.BlockSpec((1, ts, d), lambda b, i: (b, i, 0)),
            pl.BlockSpec((1, 6, d), lambda b, i: (b, 0, 0)),
            pl.BlockSpec((1, 2, d), lambda b, i: (b, 0, 0)),
            const(1, d), const(1, d), const(d, d), const(d, 2 * d), const(1, d), const(1, d),
        ],
        out_specs=[
            pl.BlockSpec((1, ts, d), lambda b, i: (b, i, 0)),
            pl.BlockSpec((1, 1, ts, d), lambda b, i: (b, i, 0, 0)),
            pl.BlockSpec((1, 1, d, ts), lambda b, i: (b, i, 0, 0)),
            pl.BlockSpec((1, 1, ATT_U, d), lambda b, i: (b, i, 0, 0)),
        ],
        out_shape=[
            SDS((bsz, s, d), F32),
            SDS((bsz, nb, ts, d), BF16),
            SDS((bsz, nb, d, ts), BF16),
            SDS((bsz, nb, ATT_U, d), F32),
        ],
        compiler_params=_cparams(("parallel", "arbitrary")),
        name="kvq",
    )(x, mod, kvmod, n1g.reshape(1, d), kvg.reshape(1, d), w_q.astype(BF16), w_kv.astype(BF16),
      jnp.tile(q_g, N_HEADS).reshape(1, d), jnp.tile(k_g, N_HEADS).reshape(1, d))


def _bf16_pieces(v):
    hi = v.astype(BF16).astype(F32)
    mid = (v - hi).astype(BF16).astype(F32)
    lo = ((v - hi) - mid).astype(BF16).astype(F32)
    return hi, mid, lo


def _attn_body(slopes_ref, q_ref, k_ref, vt_ref, km_ref, o_ref, cm_scr, kaug_scr, qaug_scr, s_scr):
    hp = pl.program_id(1)
    own = pl.program_id(2)
    bs = MOBA_BLOCK
    nb = k_ref.shape[1] * ATT_U
    pw = 2 * HEAD_DIM
    nh = 2 * ATT_HP

    @pl.when(own == 0)
    def _():
        rel = (lax.broadcasted_iota(jnp.int32, (bs, bs), 1)
               - lax.broadcasted_iota(jnp.int32, (bs, bs), 0))
        cm_scr[0] = jnp.zeros((bs, bs), F32)
        cm_scr[1] = jnp.where(rel >= 0, 0.0, -jnp.inf)
        klane = lax.broadcasted_iota(jnp.int32, (ATT_U * bs, ATT_AUG), 1)
        koff = (lax.broadcasted_iota(jnp.int32, (ATT_U * bs, ATT_AUG), 0) % bs).astype(F32)
        kaug_scr[...] = jnp.where(klane < 3, 1.0, jnp.where(klane < 6, koff, 0.0)).astype(BF16)
        qlane = lax.broadcasted_iota(jnp.int32, (bs, ATT_AUG), 1)
        qoff = lax.broadcasted_iota(jnp.int32, (bs, ATT_AUG), 0).astype(F32)
        for hd in range(nh):
            s2 = jnp.full((bs, ATT_AUG), slopes_ref[nh * hp + hd], F32) * LOG2E
            pieces = _bf16_pieces(-(s2 * qoff)) + _bf16_pieces(s2)
            aug = jnp.zeros((bs, ATT_AUG), F32)
            for lane_ix, piece in enumerate(pieces):
                aug = jnp.where(qlane == lane_ix, piece, aug)
            qaug_scr[hd] = aug.astype(BF16)

    dcol = lax.broadcasted_iota(jnp.int32, (1, pw), 1)
    blk_row = lax.broadcasted_iota(jnp.int32, (nb, 1), 0)
    valid = blk_row < own
    in_head = [(dcol >= h * HEAD_DIM) & (dcol < (h + 1) * HEAD_DIM) for h in range(2)]
    pair_cols = lambda hd: slice((hd // 2) * pw, (hd // 2 + 1) * pw)

    gates = []
    for pair in range(nh // 2):
        qp = q_ref[0, :, pair_cols(2 * pair)]
        kmp = km_ref[0, :, pair_cols(2 * pair)]
        km2 = jnp.concatenate([jnp.where(in_head[h], kmp, 0.0) for h in range(2)], axis=0)
        gates.append(lax.dot_general(km2, qp, NT_DIMS, precision=HIGHEST, preferred_element_type=F32))

    qs, selbias = [], []
    for hd in range(nh):
        slope = slopes_ref[nh * hp + hd]
        qm = jnp.where(in_head[hd % 2], q_ref[0, :, pair_cols(hd)], 0.0)
        gate = jnp.where(valid, gates[hd // 2][(hd % 2) * nb:(hd % 2 + 1) * nb], -jnp.inf)
        rank = jnp.zeros(gate.shape, F32)
        for i in range(nb):
            gi = gate[i:i + 1, :]
            tie = jnp.where(blk_row > i, 1.0, 0.0)
            rank = rank + jnp.where(gi > gate, 1.0, jnp.where(gi == gate, tie, 0.0))
        chosen = jnp.where(valid, jnp.where(rank < MOBA_TOPK, 1.0, 0.0),
                           jnp.where(blk_row == own, 1.0, 0.0))
        off = (own - blk_row).astype(F32) * (bs * LOG2E * slope)
        selbias.append(jnp.where(chosen > 0.5, -off, -jnp.inf))
        qs.append(jnp.concatenate([(qm * (LOG2E * HEAD_DIM ** -0.5)).astype(BF16), qaug_scr[hd]], axis=1))

    last = k_ref.shape[1] - 1
    ones = jnp.ones((ATT_ONES, ATT_U * bs), BF16)

    def qk(it, hd):
        kc = jnp.concatenate([k_ref[0, it, :, pair_cols(hd)], kaug_scr[...]], axis=1)
        return lax.dot_general(kc, qs[hd], NT_DIMS, preferred_element_type=F32)

    def visit(it, slot, carry, next_it, has_own):
        out = []
        if next_it is not None:
            for hd in range(nh):
                s_scr[1 - slot, hd] = qk(jnp.minimum(next_it, last), hd)
        for hd in range(nh):
            m, acc = carry[hd]

            def tile(u):
                t = s_scr[slot, hd, u * bs:(u + 1) * bs]
                if has_own:
                    t = t + cm_scr[(it * ATT_U + u == own).astype(jnp.int32)]
                return t

            rows = [jnp.sum(jnp.where(blk_row == it * ATT_U + u, selbias[hd], 0.0), axis=0, keepdims=True)
                    for u in range(ATT_U)]
            mx = m
            for u in range(ATT_U):
                mx = jnp.maximum(mx, jnp.max(tile(u), axis=0, keepdims=True) + rows[u])
            m_safe = jnp.where(mx == -jnp.inf, 0.0, mx)
            alpha = jnp.exp2(m - m_safe)
            p = jnp.concatenate([jnp.exp2((tile(u) + (rows[u] - m_safe)).astype(BF16)) for u in range(ATT_U)],
                                axis=0)
            vj = jnp.concatenate([vt_ref[0, it, hd * HEAD_DIM:(hd + 1) * HEAD_DIM, :], ones], axis=0)
            acc = alpha * acc + _dot(vj, p)
            out.append((mx, acc))
        return tuple(out)

    def body(i2, carry):
        carry = visit(2 * i2, 1, carry, 2 * i2 + 1, False)
        return visit(2 * i2 + 1, 0, carry, 2 * i2 + 2, False)

    n_past = own // ATT_U
    for hd in range(nh):
        s_scr[0, hd] = qk(n_past, hd)
    init = tuple((jnp.full((1, bs), -jnp.inf, F32), jnp.zeros((HEAD_DIM + ATT_ONES, bs), F32))
                 for hd in range(nh))
    carry = visit(n_past, 0, init, 0, True)
    carry = lax.fori_loop(0, n_past // 2, body, carry)
    carry = lax.cond(n_past % 2 == 1, lambda c: visit(n_past - 1, 1, c, None, False), lambda c: c, carry)
    outs = [acc[:HEAD_DIM] / acc[HEAD_DIM:HEAD_DIM + 1] for (_, acc) in carry]
    o_ref[0] = jnp.concatenate(outs, axis=0).astype(o_ref.dtype)


def _attn(q, kb, vtb, kmean):
    bsz, s, d = q.shape
    nsb = s // (ATT_U * MOBA_BLOCK)
    assert nsb % 2 == 0, "the attention loop visits key super-blocks in pairs"
    nb = s // MOBA_BLOCK
    w = 2 * HEAD_DIM * ATT_HP
    slopes = (2.0 ** (-8.0 * jnp.arange(1, N_HEADS + 1, dtype=F32) / N_HEADS)).astype(F32)
    return pl.pallas_call(
        _attn_body,
        grid_spec=pltpu.PrefetchScalarGridSpec(
            num_scalar_prefetch=1,
            grid=(bsz, N_HEADS // (2 * ATT_HP), nb),
            in_specs=[
                pl.BlockSpec((1, MOBA_BLOCK, w), lambda b, h, c, sl: (b, c, h)),
                pl.BlockSpec((1, nsb, ATT_U * MOBA_BLOCK, w), lambda b, h, c, sl: (b, 0, 0, h)),
                pl.BlockSpec((1, nsb, w, ATT_U * MOBA_BLOCK), lambda b, h, c, sl: (b, 0, h, 0)),
                pl.BlockSpec((1, nb, w), lambda b, h, c, sl: (b, 0, h)),
            ],
            out_specs=pl.BlockSpec((1, w, MOBA_BLOCK), lambda b, h, c, sl: (b, h, c)),
            scratch_shapes=[pltpu.VMEM((2, MOBA_BLOCK, MOBA_BLOCK), F32),
                            pltpu.VMEM((ATT_U * MOBA_BLOCK, ATT_AUG), BF16),
                            pltpu.VMEM((2 * ATT_HP, MOBA_BLOCK, ATT_AUG), BF16),
                            pltpu.VMEM((2, 2 * ATT_HP, ATT_U * MOBA_BLOCK, MOBA_BLOCK), F32)],
        ),
        out_shape=SDS((bsz, d, s), BF16),
        compiler_params=_cparams(("parallel", "parallel", "arbitrary")),
        name="attn",
    )(slopes, q, kb, vtb, kmean)


OPROJ_TS = 512


def _oproj_body(at_ref, x_ref, mod_ref, wo_ref, o_ref):
    g1 = mod_ref[0][2:3]
    m = lax.dot_general(at_ref[0], wo_ref[...], (((0,), (0,)), ((), ())), preferred_element_type=F32)
    o_ref[0] = x_ref[0] + g1 * m


def _oproj(at, x, mod, w_o):
    bsz, s, d = x.shape
    ts = min(OPROJ_TS, s)
    tok = pl.BlockSpec((1, ts, d), lambda b, i: (b, i, 0))
    return pl.pallas_call(
        _oproj_body,
        grid=(bsz, s // ts),
        in_specs=[pl.BlockSpec((1, d, ts), lambda b, i: (b, 0, i)), tok,
                  pl.BlockSpec((1, 6, d), lambda b, i: (b, 0, 0)),
                  pl.BlockSpec((d, d), lambda b, i: (0, 0))],
        out_specs=tok,
        out_shape=SDS((bsz, s, d), F32),
        compiler_params=_cparams(("parallel", "arbitrary")),
        name="oproj",
    )(at, x, mod, w_o.astype(BF16))


def kernel(x, c, ada_w, ada_b, norm1_g, norm2_g, sgu_w_in, sgu_b_in, sgu_v_g, sgu_w_s, sgu_b_s, sgu_w_out,
           kv_ada_w, kv_ada_b, kv_norm_g, w_kv, k_norm_g, attn_w_q, q_norm_g, attn_w_o,
           moe_w_router, moe_b_router, moe_w_gate, moe_b_gate, moe_w_up, moe_b_up, moe_w_down, moe_b_down):
    bsz, s, d = x.shape
    moe = lambda l, xx, mod: _moe(xx, mod, l, norm2_g[l], moe_w_router[l], moe_b_router[l],
                                  moe_w_gate, moe_b_gate, moe_w_up, moe_b_up, moe_w_down, moe_b_down)
    mod0 = _ada(c, ada_w, ada_b[0], 0).reshape(bsz, 6, d)
    x = _sgu(x, mod0, norm1_g[0], sgu_w_in[0], sgu_b_in[0], sgu_v_g[0], sgu_w_s[0], sgu_b_s[0], sgu_w_out[0])
    x = moe(0, x, mod0)
    mod1 = _ada(c, ada_w, ada_b[1], 1).reshape(bsz, 6, d)
    kvmod = _ada(c, kv_ada_w, kv_ada_b).reshape(bsz, 2, d)
    q, kb, vtb, kmean = _kvq(x, mod1, kvmod, norm1_g[1], kv_norm_g, attn_w_q[0], w_kv, q_norm_g[0], k_norm_g)
    a = _attn(q, kb, vtb, kmean.reshape(bsz, s // MOBA_BLOCK, d))
    x = _oproj(a, x, mod1, attn_w_o[0])
    x = moe(1, x, mod1)
    return x
```

```python
import functools
import math

import jax
import jax.numpy as jnp
import numpy as np
from jax import lax
from jax.experimental import pallas as pl
from jax.experimental.pallas import tpu as pltpu

F32 = jnp.float32
BF16 = jnp.bfloat16
HIGHEST = lax.Precision.HIGHEST
SDS = jax.ShapeDtypeStruct

D_MODEL = 1024
SGU_CHUNK = 128
SGU_GROUPS = 8
D_SGU = 3 * D_MODEL
SGU_GROUP_DIM = D_SGU // SGU_GROUPS
HEAD_DIM = 64
N_HEADS = D_MODEL // HEAD_DIM
MOBA_BLOCK = 256
MOBA_TOPK = 3
ATT_U = 2
ATT_ONES = 16
ATT_AUG = 128
ATT_HP = 4
LOG2E = math.log2(math.e)
N_EXPERTS = 32
TOP_K = 4
SWIGLU_LIMIT = 7.0
SWIGLU_ALPHA = 1.702
EXPERT_BLOCK = 512
EPS = 1e-6

VMEM_LIMIT_BYTES = 56 * 1024 * 1024
NT_DIMS = (((1,), (1,)), ((), ()))


def _cparams(sem):
    return pltpu.CompilerParams(dimension_semantics=sem, vmem_limit_bytes=VMEM_LIMIT_BYTES)


def _dot(a, b, **kw):
    return jnp.dot(a, b, preferred_element_type=F32, **kw)


LANES = 128
ROW_TILE = D_MODEL // LANES


def _store_row_tiles(ref, val):
    rows = val.shape[0]
    for j in range(ROW_TILE):
        ref[pl.ds(j, rows, stride=ROW_TILE), :] = val[:, j * LANES:(j + 1) * LANES]


def _load_row_tiles(ref, rows):
    return jnp.concatenate([ref[pl.ds(j, rows, stride=ROW_TILE), :] for j in range(ROW_TILE)], axis=-1)


def _dot_3pass_nt(a, b):
    a_hi = a.astype(BF16)
    a_lo = (a - a_hi.astype(F32)).astype(BF16)
    b_hi = b.astype(BF16)
    b_lo = (b - b_hi.astype(F32)).astype(BF16)
    nt = lambda u, v: lax.dot_general(u, v, NT_DIMS, preferred_element_type=F32)
    return nt(a_hi, b_hi) + (nt(a_lo, b_hi) + nt(a_hi, b_lo))


def _norm_mod(x, g, sc, sh):
    ms = jnp.mean(x * x, axis=-1, keepdims=True)
    return (x * lax.rsqrt(ms + EPS) * g) * (1.0 + sc) + sh


def _ada_body(c_ref, w_ref, b_ref, o_ref):
    c = c_ref[...]
    ca = c * jax.nn.sigmoid(c)
    o_ref[...] = _dot(ca, w_ref[...], precision=HIGHEST) + b_ref[...]


def _ada(c, w, b, layer=0):
    bsz, d = c.shape
    n = w.shape[-1]
    tn = 1024
    w = w.reshape(-1, d, n)
    return pl.pallas_call(
        _ada_body,
        grid=(n // tn,),
        in_specs=[
            pl.BlockSpec((bsz, d), lambda j: (0, 0)),
            pl.BlockSpec((None, d, tn), lambda j: (layer, 0, j)),
            pl.BlockSpec((1, tn), lambda j: (0, j)),
        ],
        out_specs=pl.BlockSpec((bsz, tn), lambda j: (0, j)),
        out_shape=SDS((bsz, n), F32),
        compiler_params=_cparams(("arbitrary",)),
        name="ada",
    )(c, w, b.reshape(1, n))


SGU_TS = 512
SGU_CB = 768


def _gelu_tanh(x):
    c = np.float32(np.sqrt(2.0 / np.pi))
    return x * (0.5 * (1.0 + jnp.tanh(c * (x + 0.044715 * (x * x * x)))))


def _sgu_body(x_ref, mod_ref, n1g_ref, win_ref, bin_ref, vg_ref, ws_ref, bst_ref, wout_ref,
              o_ref, u_scr, v_scr, y_scr):
    ts = x_ref.shape[1]
    x = x_ref[0]
    mod = mod_ref[0]
    sh1, sc1, g1 = mod[0:1], mod[1:2], mod[2:3]
    h = _norm_mod(x, n1g_ref[...], sc1, sh1).astype(BF16)

    ssq = jnp.zeros((ts, 1), F32)
    for j in range(D_SGU // SGU_CB):
        cs = slice(j * SGU_CB, (j + 1) * SGU_CB)
        ws_cols = slice(D_SGU + j * SGU_CB, D_SGU + (j + 1) * SGU_CB)
        z = _gelu_tanh(_dot(h, win_ref[:, ws_cols]) + bin_ref[:, ws_cols])
        v_scr[:, cs] = z
        ssq = ssq + jnp.sum(z * z, axis=-1, keepdims=True)
    rs = lax.rsqrt(ssq / D_SGU + EPS)

    r_i = lax.broadcasted_iota(jnp.int32, (SGU_CHUNK, SGU_CHUNK), 0)
    c_i = lax.broadcasted_iota(jnp.int32, (SGU_CHUNK, SGU_CHUNK), 1)
    causal = c_i <= r_i
    m = jnp.zeros((ts, x.shape[1]), F32)
    for j in range(D_SGU // SGU_CB):
        cb = slice(j * SGU_CB, (j + 1) * SGU_CB)
        u_scr[:, cb] = _gelu_tanh(_dot(h, win_ref[:, cb]) + bin_ref[:, cb])
        for g in range(j * SGU_CB // SGU_GROUP_DIM, (j + 1) * SGU_CB // SGU_GROUP_DIM):
            cs = slice(g * SGU_GROUP_DIM, (g + 1) * SGU_GROUP_DIM)
            wsg = jnp.where(causal, ws_ref[g], 0.0).astype(BF16)
            bsg = bst_ref[:, g:g + 1]
            for n in range(ts // SGU_CHUNK):
                rows = slice(n * SGU_CHUNK, (n + 1) * SGU_CHUNK)
                vn = (v_scr[rows, cs] * rs[rows] * vg_ref[:, cs]).astype(BF16)
                mixed = _dot(wsg, vn) + bsg
                y_scr[rows, cs] = (u_scr[rows, cs] * mixed).astype(BF16)
        m = m + _dot(y_scr[:, cb], wout_ref[cb, :])
    o_ref[0] = x + g1 * m


def _sgu(x, mod, n1g, w_in, b_in, v_g, w_s, b_s, w_out):
    bsz, s, d = x.shape
    ts = SGU_TS
    const = lambda *shape: pl.BlockSpec(shape, lambda b, i: (0,) * len(shape),
                                        pipeline_mode=pl.Buffered(1))
    return pl.pallas_call(
        _sgu_body,
        grid=(bsz, s // ts),
        in_specs=[
            pl.BlockSpec((1, ts, d), lambda b, i: (b, i, 0)),
            pl.BlockSpec((1, 6, d), lambda b, i: (b, 0, 0)),
            const(1, d),
            const(d, 2 * D_SGU),
            const(1, 2 * D_SGU),
            const(1, D_SGU),
            const(SGU_GROUPS, SGU_CHUNK, SGU_CHUNK),
            const(SGU_CHUNK, SGU_GROUPS),
            const(D_SGU, d),
        ],
        out_specs=pl.BlockSpec((1, ts, d), lambda b, i: (b, i, 0)),
        out_shape=SDS((bsz, s, d), F32),
        scratch_shapes=[
            pltpu.VMEM((ts, D_SGU), F32),
            pltpu.VMEM((ts, D_SGU), F32),
            pltpu.VMEM((ts, D_SGU), BF16),
        ],
        compiler_params=_cparams(("parallel", "arbitrary")),
        name="sgu",
    )(x, mod, n1g.reshape(1, d), w_in.astype(BF16), b_in.reshape(1, -1), v_g.reshape(1, -1),
      w_s, b_s.T, w_out.astype(BF16))


ROUTER_TS = 512


def _router_body(x_ref, mod_ref, n2g_ref, wrt_ref, brc_ref, h_ref, idx_ref, p_ref, rk_ref, cnt_ref, run_scr):
    @pl.when((pl.program_id(0) == 0) & (pl.program_id(1) == 0))
    def _():
        run_scr[...] = jnp.zeros(run_scr.shape, run_scr.dtype)

    x = x_ref[0]
    mod = mod_ref[0]
    sh2, sc2 = mod[3:4], mod[4:5]
    h2 = _norm_mod(x, n2g_ref[...], sc2, sh2)
    _store_row_tiles(h_ref, h2)
    ts = x.shape[0]
    logits = _dot_3pass_nt(wrt_ref[...], h2) + brc_ref[...]
    row = lax.broadcasted_iota(jnp.int32, logits.shape, 0)
    work = logits
    vals, idxs = [], []
    for _ in range(TOP_K):
        mx = jnp.max(work, axis=0, keepdims=True)
        ix = jnp.min(jnp.where(work == mx, row, N_EXPERTS), axis=0, keepdims=True)
        vals.append(mx)
        idxs.append(ix)
        work = jnp.where(row == ix, -jnp.inf, work)
    e = jnp.exp(jnp.concatenate(vals, axis=0) - vals[0])
    p_t = e / jnp.sum(e, axis=0, keepdims=True)
    idx_ref[0] = jnp.concatenate(idxs, axis=0)

    onehots = [row == ix for ix in idxs]
    mask = jnp.zeros(logits.shape, F32)
    for oh in onehots:
        mask = mask + jnp.where(oh, 1.0, 0.0)
    r_i = lax.broadcasted_iota(jnp.int32, (ts, ts), 0)
    c_i = lax.broadcasted_iota(jnp.int32, (ts, ts), 1)
    before = _dot(mask.astype(BF16), jnp.where(r_i < c_i, 1.0, 0.0).astype(BF16)) + run_scr[...]
    rk = [jnp.sum(jnp.where(oh, before, 0.0), axis=0, keepdims=True) for oh in onehots]
    rk_ref[0] = jnp.concatenate(rk, axis=0).astype(jnp.int32)
    run_scr[...] = run_scr[...] + jnp.sum(mask, axis=1, keepdims=True)
    cnt_ref[...] = run_scr[...].astype(jnp.int32)

    eye = jnp.where(r_i == c_i, 1.0, 0.0).astype(BF16)
    p_ref[0] = sum(lax.dot_general(eye, piece.astype(BF16), NT_DIMS, preferred_element_type=F32)
                   for piece in _bf16_pieces(p_t))


def _router(x, mod, n2g, w_r, b_r):
    bsz, s, d = x.shape
    ts = min(ROUTER_TS, s)
    kt = pl.BlockSpec((1, TOP_K, ts), lambda b, i: (b, 0, i))
    return pl.pallas_call(
        _router_body,
        grid=(bsz, s // ts),
        in_specs=[
            pl.BlockSpec((1, ts, d), lambda b, i: (b, i, 0)),
            pl.BlockSpec((1, 6, d), lambda b, i: (b, 0, 0)),
            pl.BlockSpec((1, d), lambda b, i: (0, 0)),
            pl.BlockSpec((N_EXPERTS, d), lambda b, i: (0, 0)),
            pl.BlockSpec((N_EXPERTS, 1), lambda b, i: (0, 0)),
        ],
        out_specs=[pl.BlockSpec((ts * ROW_TILE, LANES), lambda b, i: (b * (s // ts) + i, 0)),
                   kt, pl.BlockSpec((1, ts, TOP_K), lambda b, i: (b, i, 0)), kt,
                   pl.BlockSpec((N_EXPERTS, 1), lambda b, i: (0, 0))],
        out_shape=[SDS((bsz * s * ROW_TILE, LANES), F32), SDS((bsz, TOP_K, s), jnp.int32),
                   SDS((bsz, s, TOP_K), F32),
                   SDS((bsz, TOP_K, s), jnp.int32), SDS((N_EXPERTS, 1), jnp.int32)],
        scratch_shapes=[pltpu.VMEM((N_EXPERTS, 1), F32)],
        compiler_params=_cparams(("arbitrary", "arbitrary")),
        name="router",
    )(x, mod, n2g.reshape(1, d), w_r.T, b_r.reshape(-1, 1))


def _route_meta(top_idx, rank, counts):
    n = top_idx.size // TOP_K
    padded = (counts + EXPERT_BLOCK - 1) // EXPERT_BLOCK * EXPERT_BLOCK
    pend = jnp.cumsum(padded)
    pstart = pend - padded
    first_row = jnp.sum(jnp.where(top_idx[..., None] == jnp.arange(N_EXPERTS), pstart, 0), axis=-1)
    dest = (first_row + rank).astype(jnp.int32)
    n_rows = n * TOP_K + N_EXPERTS * EXPERT_BLOCK
    n_blk = n_rows // EXPERT_BLOCK
    blk_row0 = jnp.arange(n_blk) * EXPERT_BLOCK
    blk_exp = jnp.minimum(jnp.sum(pend[None, :] <= blk_row0[:, None], axis=1), N_EXPERTS - 1)
    n_used = (pend[-1] // EXPERT_BLOCK).astype(jnp.int32).reshape(1)
    eids = jnp.arange(N_EXPERTS)
    has_rows = counts > 0
    last_exp = jnp.max(jnp.where(has_rows, eids, 0))
    blk_exp = jnp.where(blk_row0 < pend[-1], blk_exp, last_exp).astype(jnp.int32)
    nxt_e = jnp.min(jnp.where((eids[None, :] > eids[:, None]) & has_rows[None, :], eids[None, :], N_EXPERTS), axis=1)
    nxt_e = jnp.where(nxt_e < N_EXPERTS, nxt_e, -1)
    slot_e = (jnp.cumsum(has_rows.astype(jnp.int32)) - 1) % 2
    blk_nxt = nxt_e[blk_exp].astype(jnp.int32)
    blk_slot = slot_e[blk_exp].astype(jnp.int32)
    pad_blk = jnp.where(padded > 0, pend - EXPERT_BLOCK, -1)
    tail = pend[-1] + jnp.arange(N_EXPERTS) * EXPERT_BLOCK
    tail_blk = jnp.where(tail < n_rows, tail, -1)
    pad_blk = jnp.concatenate([pad_blk, tail_blk]).astype(jnp.int32)
    return dest, (blk_exp, n_used, blk_nxt, blk_slot), pad_blk, n_rows


MOVE_TT = 2048
COMBINE_TT = 1024


def _per_step(dest, tt):
    bsz, k, s = dest.shape
    return dest.reshape(bsz, k, s // tt, tt).transpose(0, 2, 1, 3).reshape(bsz * (s // tt), 1, k * tt)


def _dispatch_body(pad_ref, dest_ref, h_ref, xs_ref, zero_scr, sem, zsem):
    tt = h_ref.shape[0] // ROW_TILE
    blk = EXPERT_BLOCK * ROW_TILE

    @pl.when(pl.program_id(0) == 0)
    def _():
        zero_scr[...] = jnp.zeros(zero_scr.shape, zero_scr.dtype)

        def zero_copy(e):
            row0 = pl.multiple_of(pad_ref[e], blk)
            return pltpu.make_async_copy(zero_scr, xs_ref.at[pl.ds(row0, blk)], zsem)

        for e in range(2 * N_EXPERTS):
            pl.when(pad_ref[e] >= 0)(lambda e=e: zero_copy(e).start())
        for e in range(2 * N_EXPERTS):
            pl.when(pad_ref[e] >= 0)(lambda e=e: zero_copy(e).wait())

    def issue(t, carry):
        src = h_ref.at[pl.ds(pl.multiple_of(t * ROW_TILE, ROW_TILE), ROW_TILE)]
        for k in range(TOP_K):
            d = pl.multiple_of(dest_ref[0, k * tt + t], ROW_TILE)
            pltpu.make_async_copy(src, xs_ref.at[pl.ds(d, ROW_TILE)], sem).start(priority=k % 2)
        return carry

    lax.fori_loop(0, tt, issue, 0)
    for k in range(TOP_K):
        pltpu.make_async_copy(h_ref, h_ref, sem).wait()


def _dispatch(h, dest8, pad_blk, n_rows):
    n = h.shape[0] // ROW_TILE
    tt = min(MOVE_TT, dest8.shape[-1])
    return pl.pallas_call(
        _dispatch_body,
        grid_spec=pltpu.PrefetchScalarGridSpec(
            num_scalar_prefetch=1,
            grid=(n // tt,),
            in_specs=[
                pl.BlockSpec((None, 1, TOP_K * tt), lambda i, pad: (i, 0, 0), memory_space=pltpu.SMEM),
                pl.BlockSpec((tt * ROW_TILE, LANES), lambda i, pad: (i, 0)),
            ],
            out_specs=pl.BlockSpec(memory_space=pl.ANY),
            scratch_shapes=[pltpu.VMEM((EXPERT_BLOCK * ROW_TILE, LANES), h.dtype), pltpu.SemaphoreType.DMA(()),
                            pltpu.SemaphoreType.DMA(())],
        ),
        out_shape=SDS((n_rows * ROW_TILE, LANES), h.dtype),
        compiler_params=_cparams(("arbitrary",)),
        name="dispatch",
    )(pad_blk * ROW_TILE, _per_step(dest8, tt), h)


def _ffn_body(layer, be_ref, nu_ref, nx_ref, sl_ref, xs_ref, wg_hbm, bg_ref, wu_hbm, bu_ref, wd_hbm, bd_ref,
              y_ref, w_f32, w_bf, sem):
    i = pl.program_id(0)
    slot = sl_ref[i]

    def fetch(e, s):
        return [pltpu.make_async_copy(w.at[layer, e], w_f32.at[s, n], sem.at[s])
                for n, w in enumerate((wg_hbm, wu_hbm, wd_hbm))]

    @pl.when(i == 0)
    def _():
        for c in fetch(be_ref[0], slot):
            c.start()

    @pl.when((i == 0) | (be_ref[i] != be_ref[jnp.maximum(i - 1, 0)]))
    def _():
        for c in fetch(be_ref[i], slot):
            c.wait()

        @pl.when(nx_ref[i] >= 0)
        def _():
            for c in fetch(nx_ref[i], 1 - slot):
                c.start()

        for n in range(3):
            w_bf[n] = w_f32[slot, n].astype(BF16)

    @pl.when(i < nu_ref[0])
    def _():
        x = _load_row_tiles(xs_ref, EXPERT_BLOCK).astype(BF16)
        g = jnp.minimum(_dot(x, w_bf[0]) + bg_ref[...], SWIGLU_LIMIT)
        u = jnp.clip(_dot(x, w_bf[1]) + bu_ref[...], -SWIGLU_LIMIT, SWIGLU_LIMIT)
        a = g * jax.nn.sigmoid(SWIGLU_ALPHA * g) * (u + 1.0)
        _store_row_tiles(y_ref, _dot(a.astype(BF16), w_bf[2]) + bd_ref[...])

    @pl.when(i >= nu_ref[0])
    def _():
        y_ref[...] = jnp.zeros(y_ref.shape, y_ref.dtype)


def _ffn(xs, blk_meta, layer, w_gate, b_gate, w_up, b_up, w_down, b_down):
    n_rows = xs.shape[0] // ROW_TILE
    _, e, d, de = w_gate.shape
    assert d == de == D_MODEL
    n_blk = n_rows // EXPERT_BLOCK
    blk = (EXPERT_BLOCK * ROW_TILE, LANES)
    bspec = lambda c: pl.BlockSpec((None, None, 1, c), lambda i, be, nu, nx, sl: (layer, be[i], 0, 0))
    hbm = pl.BlockSpec(memory_space=pl.ANY)
    xs_map = lambda i, be, nu, nx, sl: (jnp.minimum(i, jnp.maximum(nu[0] - 1, 0)), 0)
    return pl.pallas_call(
        functools.partial(_ffn_body, layer),
        grid_spec=pltpu.PrefetchScalarGridSpec(
            num_scalar_prefetch=4,
            grid=(n_blk,),
            in_specs=[
                pl.BlockSpec(blk, xs_map),
                hbm, bspec(de), hbm, bspec(de), hbm, bspec(d),
            ],
            out_specs=pl.BlockSpec(blk, lambda i, be, nu, nx, sl: (i, 0)),
            scratch_shapes=[pltpu.VMEM((2, 3, d, de), F32), pltpu.VMEM((3, d, de), BF16),
                            pltpu.SemaphoreType.DMA((2,))],
        ),
        out_shape=SDS((n_rows * ROW_TILE, LANES), F32),
        compiler_params=_cparams(("arbitrary",)),
        name="ffn",
    )(*blk_meta, xs, w_gate, b_gate.reshape(-1, e, 1, de), w_up, b_up.reshape(-1, e, 1, de),
      w_down, b_down.reshape(-1, e, 1, d))


COMBINE_RC = 128
COMBINE_PARTS = 4


def _combine_body(dest_ref, x_ref, mod_ref, p_ref, y_ref, o_ref, buf, sem):
    tt = x_ref.shape[0]
    tp = tt // COMBINE_PARTS

    def part_rows(part):
        return pl.ds(part * tp * ROW_TILE, tp * ROW_TILE)

    for part in range(COMBINE_PARTS):
        def issue(t, carry, part=part):
            t8 = pl.multiple_of(t * ROW_TILE, ROW_TILE)
            for k in range(TOP_K):
                d = pl.multiple_of(dest_ref[0, k * tt + t], ROW_TILE)
                pltpu.make_async_copy(y_ref.at[pl.ds(d, ROW_TILE)], buf.at[k, pl.ds(t8, ROW_TILE)],
                                      sem.at[part]).start(priority=k % 2)
            return carry

        lax.fori_loop(part * tp, (part + 1) * tp, issue, 0)

    g2 = mod_ref[0][5:6]
    for part in range(COMBINE_PARTS):
        pltpu.make_async_copy(buf.at[:, part_rows(part)], buf.at[:, part_rows(part)], sem.at[part]).wait()
        for r0 in range(part * tp, (part + 1) * tp, COMBINE_RC):
            rows = slice(r0, r0 + COMBINE_RC)
            p = p_ref[rows, :]
            for j in range(ROW_TILE):
                cols = slice(j * LANES, (j + 1) * LANES)
                tile = lambda k: buf[k, pl.ds(r0 * ROW_TILE + j, COMBINE_RC, stride=ROW_TILE), :]
                acc = p[:, 0:1] * tile(0)
                for k in range(1, TOP_K):
                    acc = acc + p[:, k:k + 1] * tile(k)
                o_ref[rows, cols] = x_ref[rows, cols] + g2[:, cols] * acc


def _combine(x, mod, p, y, dest8, seq):
    n, d = x.shape
    tt = min(COMBINE_TT, seq)
    n_steps = n // tt
    per_b = seq // tt
    return pl.pallas_call(
        _combine_body,
        grid=(n_steps,),
        in_specs=[
            pl.BlockSpec((None, 1, TOP_K * tt), lambda i: (i, 0, 0), memory_space=pltpu.SMEM),
            pl.BlockSpec((tt, d), lambda i: (i, 0)),
            pl.BlockSpec((1, 6, d), lambda i: (i // per_b, 0, 0)),
            pl.BlockSpec((tt, TOP_K), lambda i: (i, 0)),
            pl.BlockSpec(memory_space=pl.ANY),
        ],
        out_specs=pl.BlockSpec((tt, d), lambda i: (i, 0)),
        out_shape=SDS((n, d), F32),
        scratch_shapes=[pltpu.VMEM((TOP_K, tt * ROW_TILE, LANES), F32),
                        pltpu.SemaphoreType.DMA((COMBINE_PARTS,))],
        compiler_params=_cparams(("arbitrary",)),
        name="combine",
    )(_per_step(dest8, tt), x, mod, p, y)


def _moe(x, mod, layer, n2g, w_r, b_r, w_gate, b_gate, w_up, b_up, w_down, b_down):
    bsz, s, d = x.shape
    n = bsz * s
    h2, top_idx, probs, rank, counts = _router(x, mod, n2g, w_r, b_r)
    dest, blk_meta, pad_blk, n_rows = _route_meta(top_idx, rank, counts.reshape(N_EXPERTS))
    dest8 = dest * ROW_TILE
    xs = _dispatch(h2, dest8, pad_blk, n_rows)
    y = _ffn(xs, blk_meta, layer, w_gate, b_gate, w_up, b_up, w_down, b_down)
    out = _combine(x.reshape(n, d), mod, probs.reshape(n, TOP_K), y, dest8, s)
    return out.reshape(bsz, s, d)


def _seg_mats():
    r = lax.broadcasted_iota(jnp.int32, (D_MODEL, N_HEADS), 0) // HEAD_DIM
    c = lax.broadcasted_iota(jnp.int32, (D_MODEL, N_HEADS), 1)
    seg = jnp.where(r == c, 1.0, 0.0).astype(BF16)
    rt = lax.broadcasted_iota(jnp.int32, (N_HEADS, D_MODEL), 0)
    ct = lax.broadcasted_iota(jnp.int32, (N_HEADS, D_MODEL), 1) // HEAD_DIM
    seg_t = jnp.where(rt == ct, 1.0, 0.0).astype(BF16)
    return seg, seg_t


def _split_dot(a, b):
    hi = a.astype(BF16)
    lo = (a - hi.astype(F32)).astype(BF16)
    return _dot(hi, b) + _dot(lo, b)


def _head_rmsnorm(x, g, seg, seg_t):
    ms = _dot((x * x).astype(BF16), seg) * (1.0 / HEAD_DIM)
    r = lax.rsqrt(ms + EPS)
    return x * _split_dot(r, seg_t) * g


def _kvq_body(x_ref, mod_ref, kvmod_ref, n1g_ref, kvg_ref, wq_ref, wkv_ref, qg_ref, kg_ref,
              q_ref, k_ref, vt_ref, km_ref):
    x = x_ref[0]
    mod = mod_ref[0]
    kvmod = kvmod_ref[0]
    ms = jnp.mean(x * x, axis=-1, keepdims=True)
    xn = x * lax.rsqrt(ms + EPS)
    hq = ((xn * n1g_ref[...]) * (1.0 + mod[1:2]) + mod[0:1]).astype(BF16)
    hk = ((xn * kvg_ref[...]) * (1.0 + kvmod[1:2]) + kvmod[0:1]).astype(BF16)
    seg, seg_t = _seg_mats()
    q = _head_rmsnorm(_dot(hq, wq_ref[...]), qg_ref[...], seg, seg_t)
    q_ref[0] = q
    kv = _dot(hk, wkv_ref[...])
    k = _head_rmsnorm(kv[:, :D_MODEL], kg_ref[...], seg, seg_t)
    k_ref[0, 0] = k.astype(BF16)
    for u in range(ATT_U):
        rows = slice(u * MOBA_BLOCK, (u + 1) * MOBA_BLOCK)
        km_ref[0, 0, u:u + 1, :] = jnp.mean(k[rows], axis=0, keepdims=True)
    vt_ref[0, 0] = kv[:, D_MODEL:].T.astype(BF16)


def _kvq(x, mod, kvmod, n1g, kvg, w_q, w_kv, q_g, k_g):
    bsz, s, d = x.shape
    ts = ATT_U * MOBA_BLOCK
    nb = s // ts
    const = lambda *shape: pl.BlockSpec(shape, lambda b, i: (0,) * len(shape))
    return pl.pallas_call(
        _kvq_body,
        grid=(bsz, nb),
        in_specs=[
            pl.BlockSpec((1, ts, d), lambda b, i: (b, i, 0)),
            pl.BlockSpec((1, 6, d), lambda b, i: (b, 0, 0)),
            pl.BlockSpec((1, 2, d), lambda b, i: (b, 0, 0)),
            const(1, d), const(1, d), const(d, d), const(d, 2 * d), const(1, d), const(1, d),
        ],
        out_specs=[
            pl.BlockSpec((1, ts, d), lambda b, i: (b, i, 0)),
            pl.BlockSpec((1, 1, ts, d), lambda b, i: (b, i, 0, 0)),
            pl.BlockSpec((1, 1, d, ts), lambda b, i: (b, i, 0, 0)),
            pl.BlockSpec((1, 1, ATT_U, d), lambda b, i: (b, i, 0, 0)),
        ],
        out_shape=[
            SDS((bsz, s, d), F32),
            SDS((bsz, nb, ts, d), BF16),
            SDS((bsz, nb, d, ts), BF16),
            SDS((bsz, nb, ATT_U, d), F32),
        ],
        compiler_params=_cparams(("parallel", "arbitrary")),
        name="kvq",
    )(x, mod, kvmod, n1g.reshape(1, d), kvg.reshape(1, d), w_q.astype(BF16), w_kv.astype(BF16),
      jnp.tile(q_g, N_HEADS).reshape(1, d), jnp.tile(k_g, N_HEADS).reshape(1, d))


def _bf16_pieces(v):
    hi = v.astype(BF16).astype(F32)
    mid = (v - hi).astype(BF16).astype(F32)
    lo = ((v - hi) - mid).astype(BF16).astype(F32)
    return hi, mid, lo


def _attn_body(slopes_ref, q_ref, k_ref, vt_ref, km_ref, o_ref, cm_scr, kaug_scr, qaug_scr, s_scr):
    hp = pl.program_id(1)
    own = pl.program_id(2)
    bs = MOBA_BLOCK
    nb = k_ref.shape[1] * ATT_U
    pw = 2 * HEAD_DIM
    nh = 2 * ATT_HP

    @pl.when(own == 0)
    def _():
        rel = (lax.broadcasted_iota(jnp.int32, (bs, bs), 1)
               - lax.broadcasted_iota(jnp.int32, (bs, bs), 0))
        cm_scr[0] = jnp.zeros((bs, bs), F32)
        cm_scr[1] = jnp.where(rel >= 0, 0.0, -jnp.inf)
        klane = lax.broadcasted_iota(jnp.int32, (ATT_U * bs, ATT_AUG), 1)
        koff = (lax.broadcasted_iota(jnp.int32, (ATT_U * bs, ATT_AUG), 0) % bs).astype(F32)
        kaug_scr[...] = jnp.where(klane < 3, 1.0, jnp.where(klane < 6, koff, 0.0)).astype(BF16)
        qlane = lax.broadcasted_iota(jnp.int32, (bs, ATT_AUG), 1)
        qoff = lax.broadcasted_iota(jnp.int32, (bs, ATT_AUG), 0).astype(F32)
        for hd in range(nh):
            s2 = jnp.full((bs, ATT_AUG), slopes_ref[nh * hp + hd], F32) * LOG2E
            pieces = _bf16_pieces(-(s2 * qoff)) + _bf16_pieces(s2)
            aug = jnp.zeros((bs, ATT_AUG), F32)
            for lane_ix, piece in enumerate(pieces):
                aug = jnp.where(qlane == lane_ix, piece, aug)
            qaug_scr[hd] = aug.astype(BF16)

    dcol = lax.broadcasted_iota(jnp.int32, (1, pw), 1)
    blk_row = lax.broadcasted_iota(jnp.int32, (nb, 1), 0)
    valid = blk_row < own
    in_head = [(dcol >= h * HEAD_DIM) & (dcol < (h + 1) * HEAD_DIM) for h in range(2)]
    pair_cols = lambda hd: slice((hd // 2) * pw, (hd // 2 + 1) * pw)

    gates = []
    for pair in range(nh // 2):
        qp = q_ref[0, :, pair_cols(2 * pair)]
        kmp = km_ref[0, :, pair_cols(2 * pair)]
        km2 = jnp.concatenate([jnp.where(in_head[h], kmp, 0.0) for h in range(2)], axis=0)
        gates.append(lax.dot_general(km2, qp, NT_DIMS, precision=HIGHEST, preferred_element_type=F32))

    qs, selbias = [], []
    for hd in range(nh):
        slope = slopes_ref[nh * hp + hd]
        qm = jnp.where(in_head[hd % 2], q_ref[0, :, pair_cols(hd)], 0.0)
        gate = jnp.where(valid, gates[hd // 2][(hd % 2) * nb:(hd % 2 + 1) * nb], -jnp.inf)
        rank = jnp.zeros(gate.shape, F32)
        for i in range(nb):
            gi = gate[i:i + 1, :]
            tie = jnp.where(blk_row > i, 1.0, 0.0)
            rank = rank + jnp.where(gi > gate, 1.0, jnp.where(gi == gate, tie, 0.0))
        chosen = jnp.where(valid, jnp.where(rank < MOBA_TOPK, 1.0, 0.0),
                           jnp.where(blk_row == own, 1.0, 0.0))
        off = (own - blk_row).astype(F32) * (bs * LOG2E * slope)
        selbias.append(jnp.where(chosen > 0.5, -off, -jnp.inf))
        qs.append(jnp.concatenate([(qm * (LOG2E * HEAD_DIM ** -0.5)).astype(BF16), qaug_scr[hd]], axis=1))

    last = k_ref.shape[1] - 1
    ones = jnp.ones((ATT_ONES, ATT_U * bs), BF16)

    def qk(it, hd):
        kc = jnp.concatenate([k_ref[0, it, :, pair_cols(hd)], kaug_scr[...]], axis=1)
        return lax.dot_general(kc, qs[hd], NT_DIMS, preferred_element_type=F32)

    def visit(it, slot, carry, next_it, has_own):
        out = []
        if next_it is not None:
            for hd in range(nh):
                s_scr[1 - slot, hd] = qk(jnp.minimum(next_it, last), hd)
        for hd in range(nh):
            m, acc = carry[hd]

            def tile(u):
                t = s_scr[slot, hd, u * bs:(u + 1) * bs]
                if has_own:
                    t = t + cm_scr[(it * ATT_U + u == own).astype(jnp.int32)]
                return t

            rows = [jnp.sum(jnp.where(blk_row == it * ATT_U + u, selbias[hd], 0.0), axis=0, keepdims=True)
                    for u in range(ATT_U)]
            mx = m
            for u in range(ATT_U):
                mx = jnp.maximum(mx, jnp.max(tile(u), axis=0, keepdims=True) + rows[u])
            m_safe = jnp.where(mx == -jnp.inf, 0.0, mx)
            alpha = jnp.exp2(m - m_safe)
            p = jnp.concatenate([jnp.exp2((tile(u) + (rows[u] - m_safe)).astype(BF16)) for u in range(ATT_U)],
                                axis=0)
            vj = jnp.concatenate([vt_ref[0, it, hd * HEAD_DIM:(hd + 1) * HEAD_DIM, :], ones], axis=0)
            acc = alpha * acc + _dot(vj, p)
            out.append((mx, acc))
        return tuple(out)

    def body(i2, carry):
        carry = visit(2 * i2, 1, carry, 2 * i2 + 1, False)
        return visit(2 * i2 + 1, 0, carry, 2 * i2 + 2, False)

    n_past = own // ATT_U
    for hd in range(nh):
        s_scr[0, hd] = qk(n_past, hd)
    init = tuple((jnp.full((1, bs), -jnp.inf, F32), jnp.zeros((HEAD_DIM + ATT_ONES, bs), F32))
                 for hd in range(nh))
    carry = visit(n_past, 0, init, 0, True)
    carry = lax.fori_loop(0, n_past // 2, body, carry)
    carry = lax.cond(n_past % 2 == 1, lambda c: visit(n_past - 1, 1, c, None, False), lambda c: c, carry)
    outs = [acc[:HEAD_DIM] / acc[HEAD_DIM:HEAD_DIM + 1] for (_, acc) in carry]
    o_ref[0] = jnp.concatenate(outs, axis=0).astype(o_ref.dtype)


def _attn(q, kb, vtb, kmean):
    bsz, s, d = q.shape
    nsb = s // (ATT_U * MOBA_BLOCK)
    assert nsb % 2 == 0, "the attention loop visits key super-blocks in pairs"
    nb = s // MOBA_BLOCK
    w = 2 * HEAD_DIM * ATT_HP
    slopes = (2.0 ** (-8.0 * jnp.arange(1, N_HEADS + 1, dtype=F32) / N_HEADS)).astype(F32)
    return pl.pallas_call(
        _attn_body,
        grid_spec=pltpu.PrefetchScalarGridSpec(
            num_scalar_prefetch=1,
            grid=(bsz, N_HEADS // (2 * ATT_HP), nb),
            in_specs=[
                pl.BlockSpec((1, MOBA_BLOCK, w), lambda b, h, c, sl: (b, c, h)),
                pl.BlockSpec((1, nsb, ATT_U * MOBA_BLOCK, w), lambda b, h, c, sl: (b, 0, 0, h)),
                pl.BlockSpec((1, nsb, w, ATT_U * MOBA_BLOCK), lambda b, h, c, sl: (b, 0, h, 0)),
                pl.BlockSpec((1, nb, w), lambda b, h, c, sl: (b, 0, h)),
            ],
            out_specs=pl.BlockSpec((1, w, MOBA_BLOCK), lambda b, h, c, sl: (b, h, c)),
            scratch_shapes=[pltpu.VMEM((2, MOBA_BLOCK, MOBA_BLOCK), F32),
                            pltpu.VMEM((ATT_U * MOBA_BLOCK, ATT_AUG), BF16),
                            pltpu.VMEM((2 * ATT_HP, MOBA_BLOCK, ATT_AUG), BF16),
                            pltpu.VMEM((2, 2 * ATT_HP, ATT_U * MOBA_BLOCK, MOBA_BLOCK), F32)],
        ),
        out_shape=SDS((bsz, d, s), BF16),
        compiler_params=_cparams(("parallel", "parallel", "arbitrary")),
        name="attn",
    )(slopes, q, kb, vtb, kmean)


OPROJ_TS = 512


def _oproj_body(at_ref, x_ref, mod_ref, wo_ref, o_ref):
    g1 = mod_ref[0][2:3]
    m = lax.dot_general(at_ref[0], wo_ref[...], (((0,), (0,)), ((), ())), preferred_element_type=F32)
    o_ref[0] = x_ref[0] + g1 * m


def _oproj(at, x, mod, w_o):
    bsz, s, d = x.shape
    ts = min(OPROJ_TS, s)
    tok = pl.BlockSpec((1, ts, d), lambda b, i: (b, i, 0))
    return pl.pallas_call(
        _oproj_body,
        grid=(bsz, s // ts),
        in_specs=[pl.BlockSpec((1, d, ts), lambda b, i: (b, 0, i)), tok,
                  pl.BlockSpec((1, 6, d), lambda b, i: (b, 0, 0)),
                  pl.BlockSpec((d, d), lambda b, i: (0, 0))],
        out_specs=tok,
        out_shape=SDS((bsz, s, d), F32),
        compiler_params=_cparams(("parallel", "arbitrary")),
        name="oproj",
    )(at, x, mod, w_o.astype(BF16))


def kernel(x, c, ada_w, ada_b, norm1_g, norm2_g, sgu_w_in, sgu_b_in, sgu_v_g, sgu_w_s, sgu_b_s, sgu_w_out,
           kv_ada_w, kv_ada_b, kv_norm_g, w_kv, k_norm_g, attn_w_q, q_norm_g, attn_w_o,
           moe_w_router, moe_b_router, moe_w_gate, moe_b_gate, moe_w_up, moe_b_up, moe_w_down, moe_b_down):
    bsz, s, d = x.shape
    moe = lambda l, xx, mod: _moe(xx, mod, l, norm2_g[l], moe_w_router[l], moe_b_router[l],
                                  moe_w_gate, moe_b_gate, moe_w_up, moe_b_up, moe_w_down, moe_b_down)
    mod0 = _ada(c, ada_w, ada_b[0], 0).reshape(bsz, 6, d)
    x = _sgu(x, mod0, norm1_g[0], sgu_w_in[0], sgu_b_in[0], sgu_v_g[0], sgu_w_s[0], sgu_b_s[0], sgu_w_out[0])
    x = moe(0, x, mod0)
    mod1 = _ada(c, ada_w, ada_b[1], 1).reshape(bsz, 6, d)
    kvmod = _ada(c, kv_ada_w, kv_ada_b).reshape(bsz, 2, d)
    q, kb, vtb, kmean = _kvq(x, mod1, kvmod, norm1_g[1], kv_norm_g, attn_w_q[0], w_kv, q_norm_g[0], k_norm_g)
    a = _attn(q, kb, vtb, kmean.reshape(bsz, s // MOBA_BLOCK, d))
    x = _oproj(a, x, mod1, attn_w_o[0])
    x = moe(1, x, mod1)
    return x
```

```python
import functools
import math

import jax
import jax.numpy as jnp
import numpy as np
from jax import lax
from jax.experimental import pallas as pl
from jax.experimental.pallas import tpu as pltpu

F32 = jnp.float32
BF16 = jnp.bfloat16
HIGHEST = lax.Precision.HIGHEST
SDS = jax.ShapeDtypeStruct

D_MODEL = 1024
SGU_CHUNK = 128
SGU_GROUPS = 8
D_SGU = 3 * D_MODEL
SGU_GROUP_DIM = D_SGU // SGU_GROUPS
HEAD_DIM = 64
N_HEADS = D_MODEL // HEAD_DIM
MOBA_BLOCK = 256
MOBA_TOPK = 3
ATT_U = 2
ATT_ONES = 16
ATT_AUG = 128
ATT_HP = 4
LOG2E = math.log2(math.e)
N_EXPERTS = 32
TOP_K = 4
SWIGLU_LIMIT = 7.0
SWIGLU_ALPHA = 1.702
EXPERT_BLOCK = 512
EPS = 1e-6

VMEM_LIMIT_BYTES = 56 * 1024 * 1024
NT_DIMS = (((1,), (1,)), ((), ()))


def _cparams(sem):
    return pltpu.CompilerParams(dimension_semantics=sem, vmem_limit_bytes=VMEM_LIMIT_BYTES)


def _dot(a, b, **kw):
    return jnp.dot(a, b, preferred_element_type=F32, **kw)


LANES = 128
ROW_TILE = D_MODEL // LANES


def _store_row_tiles(ref, val):
    rows = val.shape[0]
    for j in range(ROW_TILE):
        ref[pl.ds(j, rows, stride=ROW_TILE), :] = val[:, j * LANES:(j + 1) * LANES]


def _load_row_tiles(ref, rows):
    return jnp.concatenate([ref[pl.ds(j, rows, stride=ROW_TILE), :] for j in range(ROW_TILE)], axis=-1)


def _dot_3pass_nt(a, b):
    a_hi = a.astype(BF16)
    a_lo = (a - a_hi.astype(F32)).astype(BF16)
    b_hi = b.astype(BF16)
    b_lo = (b - b_hi.astype(F32)).astype(BF16)
    nt = lambda u, v: lax.dot_general(u, v, NT_DIMS, preferred_element_type=F32)
    return nt(a_hi, b_hi) + (nt(a_lo, b_hi) + nt(a_hi, b_lo))


def _norm_mod(x, g, sc, sh):
    ms = jnp.mean(x * x, axis=-1, keepdims=True)
    return (x * lax.rsqrt(ms + EPS) * g) * (1.0 + sc) + sh


def _ada_body(c_ref, w_ref, b_ref, o_ref):
    c = c_ref[...]
    ca = c * jax.nn.sigmoid(c)
    o_ref[...] = _dot(ca, w_ref[...], precision=HIGHEST) + b_ref[...]


def _ada(c, w, b, layer=0):
    bsz, d = c.shape
    n = w.shape[-1]
    tn = 1024
    w = w.reshape(-1, d, n)
    return pl.pallas_call(
        _ada_body,
        grid=(n // tn,),
        in_specs=[
            pl.BlockSpec((bsz, d), lambda j: (0, 0)),
            pl.BlockSpec((None, d, tn), lambda j: (layer, 0, j)),
            pl.BlockSpec((1, tn), lambda j: (0, j)),
        ],
        out_specs=pl.BlockSpec((bsz, tn), lambda j: (0, j)),
        out_shape=SDS((bsz, n), F32),
        compiler_params=_cparams(("arbitrary",)),
        name="ada",
    )(c, w, b.reshape(1, n))


SGU_TS = 512
SGU_CB = 768


def _gelu_tanh(x):
    c = np.float32(np.sqrt(2.0 / np.pi))
    return x * (0.5 * (1.0 + jnp.tanh(c * (x + 0.044715 * (x * x * x)))))


def _sgu_body(x_ref, mod_ref, n1g_ref, win_ref, bin_ref, vg_ref, ws_ref, bst_ref, wout_ref,
              o_ref, u_scr, v_scr, y_scr):
    ts = x_ref.shape[1]
    x = x_ref[0]
    mod = mod_ref[0]
    sh1, sc1, g1 = mod[0:1], mod[1:2], mod[2:3]
    h = _norm_mod(x, n1g_ref[...], sc1, sh1).astype(BF16)

    ssq = jnp.zeros((ts, 1), F32)
    for j in range(D_SGU // SGU_CB):
        cs = slice(j * SGU_CB, (j + 1) * SGU_CB)
        ws_cols = slice(D_SGU + j * SGU_CB, D_SGU + (j + 1) * SGU_CB)
        z = _gelu_tanh(_dot(h, win_ref[:, ws_cols]) + bin_ref[:, ws_cols])
        v_scr[:, cs] = z
        ssq = ssq + jnp.sum(z * z, axis=-1, keepdims=True)
    rs = lax.rsqrt(ssq / D_SGU + EPS)

    r_i = lax.broadcasted_iota(jnp.int32, (SGU_CHUNK, SGU_CHUNK), 0)
    c_i = lax.broadcasted_iota(jnp.int32, (SGU_CHUNK, SGU_CHUNK), 1)
    causal = c_i <= r_i
    m = jnp.zeros((ts, x.shape[1]), F32)
    for j in range(D_SGU // SGU_CB):
        cb = slice(j * SGU_CB, (j + 1) * SGU_CB)
        u_scr[:, cb] = _gelu_tanh(_dot(h, win_ref[:, cb]) + bin_ref[:, cb])
        for g in range(j * SGU_CB // SGU_GROUP_DIM, (j + 1) * SGU_CB // SGU_GROUP_DIM):
            cs = slice(g * SGU_GROUP_DIM, (g + 1) * SGU_GROUP_DIM)
            wsg = jnp.where(causal, ws_ref[g], 0.0).astype(BF16)
            bsg = bst_ref[:, g:g + 1]
            for n in range(ts // SGU_CHUNK):
                rows = slice(n * SGU_CHUNK, (n + 1) * SGU_CHUNK)
                vn = (v_scr[rows, cs] * rs[rows] * vg_ref[:, cs]).astype(BF16)
                mixed = _dot(wsg, vn) + bsg
                y_scr[rows, cs] = (u_scr[rows, cs] * mixed).astype(BF16)
        m = m + _dot(y_scr[:, cb], wout_ref[cb, :])
    o_ref[0] = x + g1 * m


def _sgu(x, mod, n1g, w_in, b_in, v_g, w_s, b_s, w_out):
    bsz, s, d = x.shape
    ts = SGU_TS
    const = lambda *shape: pl.BlockSpec(shape, lambda b, i: (0,) * len(shape),
                                        pipeline_mode=pl.Buffered(1))
    return pl.pallas_call(
        _sgu_body,
        grid=(bsz, s // ts),
        in_specs=[
            pl.BlockSpec((1, ts, d), lambda b, i: (b, i, 0)),
            pl.BlockSpec((1, 6, d), lambda b, i: (b, 0, 0)),
            const(1, d),
            const(d, 2 * D_SGU),
            const(1, 2 * D_SGU),
            const(1, D_SGU),
            const(SGU_GROUPS, SGU_CHUNK, SGU_CHUNK),
            const(SGU_CHUNK, SGU_GROUPS),
            const(D_SGU, d),
        ],
        out_specs=pl.BlockSpec((1, ts, d), lambda b, i: (b, i, 0)),
        out_shape=SDS((bsz, s, d), F32),
        scratch_shapes=[
            pltpu.VMEM((ts, D_SGU), F32),
            pltpu.VMEM((ts, D_SGU), F32),
            pltpu.VMEM((ts, D_SGU), BF16),
        ],
        compiler_params=_cparams(("parallel", "arbitrary")),
        name="sgu",
    )(x, mod, n1g.reshape(1, d), w_in.astype(BF16), b_in.reshape(1, -1), v_g.reshape(1, -1),
      w_s, b_s.T, w_out.astype(BF16))


ROUTER_TS = 512


def _router_body(x_ref, mod_ref, n2g_ref, wrt_ref, brc_ref, h_ref, idx_ref, p_ref, rk_ref, cnt_ref, run_scr):
    @pl.when((pl.program_id(0) == 0) & (pl.program_id(1) == 0))
    def _():
        run_scr[...] = jnp.zeros(run_scr.shape, run_scr.dtype)

    x = x_ref[0]
    mod = mod_ref[0]
    sh2, sc2 = mod[3:4], mod[4:5]
    h2 = _norm_mod(x, n2g_ref[...], sc2, sh2)
    _store_row_tiles(h_ref, h2)
    ts = x.shape[0]
    logits = _dot_3pass_nt(wrt_ref[...], h2) + brc_ref[...]
    row = lax.broadcasted_iota(jnp.int32, logits.shape, 0)
    work = logits
    vals, idxs = [], []
    for _ in range(TOP_K):
        mx = jnp.max(work, axis=0, keepdims=True)
        ix = jnp.min(jnp.where(work == mx, row, N_EXPERTS), axis=0, keepdims=True)
        vals.append(mx)
        idxs.append(ix)
        work = jnp.where(row == ix, -jnp.inf, work)
    e = jnp.exp(jnp.concatenate(vals, axis=0) - vals[0])
    p_t = e / jnp.sum(e, axis=0, keepdims=True)
    idx_ref[0] = jnp.concatenate(idxs, axis=0)

    onehots = [row == ix for ix in idxs]
    mask = jnp.zeros(logits.shape, F32)
    for oh in onehots:
        mask = mask + jnp.where(oh, 1.0, 0.0)
    r_i = lax.broadcasted_iota(jnp.int32, (ts, ts), 0)
    c_i = lax.broadcasted_iota(jnp.int32, (ts, ts), 1)
    before = _dot(mask.astype(BF16), jnp.where(r_i < c_i, 1.0, 0.0).astype(BF16)) + run_scr[...]
    rk = [jnp.sum(jnp.where(oh, before, 0.0), axis=0, keepdims=True) for oh in onehots]
    rk_ref[0] = jnp.concatenate(rk, axis=0).astype(jnp.int32)
    run_scr[...] = run_scr[...] + jnp.sum(mask, axis=1, keepdims=True)
    cnt_ref[...] = run_scr[...].astype(jnp.int32)

    eye = jnp.where(r_i == c_i, 1.0, 0.0).astype(BF16)
    p_ref[0] = sum(lax.dot_general(eye, piece.astype(BF16), NT_DIMS, preferred_element_type=F32)
                   for piece in _bf16_pieces(p_t))


def _router(x, mod, n2g, w_r, b_r):
    bsz, s, d = x.shape
    ts = min(ROUTER_TS, s)
    kt = pl.BlockSpec((1, TOP_K, ts), lambda b, i: (b, 0, i))
    return pl.pallas_call(
        _router_body,
        grid=(bsz, s // ts),
        in_specs=[
            pl.BlockSpec((1, ts, d), lambda b, i: (b, i, 0)),
            pl.BlockSpec((1, 6, d), lambda b, i: (b, 0, 0)),
            pl.BlockSpec((1, d), lambda b, i: (0, 0)),
            pl.BlockSpec((N_EXPERTS, d), lambda b, i: (0, 0)),
            pl.BlockSpec((N_EXPERTS, 1), lambda b, i: (0, 0)),
        ],
        out_specs=[pl.BlockSpec((ts * ROW_TILE, LANES), lambda b, i: (b * (s // ts) + i, 0)),
                   kt, pl.BlockSpec((1, ts, TOP_K), lambda b, i: (b, i, 0)), kt,
                   pl.BlockSpec((N_EXPERTS, 1), lambda b, i: (0, 0))],
        out_shape=[SDS((bsz * s * ROW_TILE, LANES), F32), SDS((bsz, TOP_K, s), jnp.int32),
                   SDS((bsz, s, TOP_K), F32),
                   SDS((bsz, TOP_K, s), jnp.int32), SDS((N_EXPERTS, 1), jnp.int32)],
        scratch_shapes=[pltpu.VMEM((N_EXPERTS, 1), F32)],
        compiler_params=_cparams(("arbitrary", "arbitrary")),
        name="router",
    )(x, mod, n2g.reshape(1, d), w_r.T, b_r.reshape(-1, 1))


def _route_meta(top_idx, rank, counts):
    n = top_idx.size // TOP_K
    padded = (counts + EXPERT_BLOCK - 1) // EXPERT_BLOCK * EXPERT_BLOCK
    pend = jnp.cumsum(padded)
    pstart = pend - padded
    first_row = jnp.sum(jnp.where(top_idx[..., None] == jnp.arange(N_EXPERTS), pstart, 0), axis=-1)
    dest = (first_row + rank).astype(jnp.int32)
    n_rows = n * TOP_K + N_EXPERTS * EXPERT_BLOCK
    n_blk = n_rows // EXPERT_BLOCK
    blk_row0 = jnp.arange(n_blk) * EXPERT_BLOCK
    blk_exp = jnp.minimum(jnp.sum(pend[None, :] <= blk_row0[:, None], axis=1), N_EXPERTS - 1)
    n_used = (pend[-1] // EXPERT_BLOCK).astype(jnp.int32).reshape(1)
    eids = jnp.arange(N_EXPERTS)
    has_rows = counts > 0
    last_exp = jnp.max(jnp.where(has_rows, eids, 0))
    blk_exp = jnp.where(blk_row0 < pend[-1], blk_exp, last_exp).astype(jnp.int32)
    nxt_e = jnp.min(jnp.where((eids[None, :] > eids[:, None]) & has_rows[None, :], eids[None, :], N_EXPERTS), axis=1)
    nxt_e = jnp.where(nxt_e < N_EXPERTS, nxt_e, -1)
    slot_e = (jnp.cumsum(has_rows.astype(jnp.int32)) - 1) % 2
    blk_nxt = nxt_e[blk_exp].astype(jnp.int32)
    blk_slot = slot_e[blk_exp].astype(jnp.int32)
    blk_valid = jnp.clip(pstart[blk_exp] + counts[blk_exp] - blk_row0, 0, EXPERT_BLOCK)
    blk_valid = jnp.where(blk_row0 < pend[-1], blk_valid, 0).astype(jnp.int32)
    pad_blk = jnp.where(padded > 0, pend - EXPERT_BLOCK, -1)
    tail = pend[-1] + jnp.arange(N_EXPERTS) * EXPERT_BLOCK
    tail_blk = jnp.where(tail < n_rows, tail, -1)
    pad_blk = jnp.concatenate([pad_blk, tail_blk]).astype(jnp.int32)
    return dest, (blk_exp, n_used, blk_nxt, blk_slot, blk_valid), pad_blk, n_rows


MOVE_TT = 2048
COMBINE_TT = 1024


def _per_step(dest, tt):
    bsz, k, s = dest.shape
    return dest.reshape(bsz, k, s // tt, tt).transpose(0, 2, 1, 3).reshape(bsz * (s // tt), 1, k * tt)


def _dispatch_body(pad_ref, dest_ref, h_ref, xs_ref, zero_scr, sem, zsem):
    tt = h_ref.shape[0] // ROW_TILE
    blk = EXPERT_BLOCK * ROW_TILE

    @pl.when(pl.program_id(0) == 0)
    def _():
        zero_scr[...] = jnp.zeros(zero_scr.shape, zero_scr.dtype)

        def zero_copy(e):
            row0 = pl.multiple_of(pad_ref[e], blk)
            return pltpu.make_async_copy(zero_scr, xs_ref.at[pl.ds(row0, blk)], zsem)

        for e in range(2 * N_EXPERTS):
            pl.when(pad_ref[e] >= 0)(lambda e=e: zero_copy(e).start())
        for e in range(2 * N_EXPERTS):
            pl.when(pad_ref[e] >= 0)(lambda e=e: zero_copy(e).wait())

    def issue(t, carry):
        src = h_ref.at[pl.ds(pl.multiple_of(t * ROW_TILE, ROW_TILE), ROW_TILE)]
        for k in range(TOP_K):
            d = pl.multiple_of(dest_ref[0, k * tt + t], ROW_TILE)
            pltpu.make_async_copy(src, xs_ref.at[pl.ds(d, ROW_TILE)], sem).start(priority=k % 2)
        return carry

    lax.fori_loop(0, tt, issue, 0)
    for k in range(TOP_K):
        pltpu.make_async_copy(h_ref, h_ref, sem).wait()


def _dispatch(h, dest8, pad_blk, n_rows):
    n = h.shape[0] // ROW_TILE
    tt = min(MOVE_TT, dest8.shape[-1])
    return pl.pallas_call(
        _dispatch_body,
        grid_spec=pltpu.PrefetchScalarGridSpec(
            num_scalar_prefetch=1,
            grid=(n // tt,),
            in_specs=[
                pl.BlockSpec((None, 1, TOP_K * tt), lambda i, pad: (i, 0, 0), memory_space=pltpu.SMEM),
                pl.BlockSpec((tt * ROW_TILE, LANES), lambda i, pad: (i, 0)),
            ],
            out_specs=pl.BlockSpec(memory_space=pl.ANY),
            scratch_shapes=[pltpu.VMEM((EXPERT_BLOCK * ROW_TILE, LANES), h.dtype), pltpu.SemaphoreType.DMA(()),
                            pltpu.SemaphoreType.DMA(())],
        ),
        out_shape=SDS((n_rows * ROW_TILE, LANES), h.dtype),
        compiler_params=_cparams(("arbitrary",)),
        name="dispatch",
    )(pad_blk * ROW_TILE, _per_step(dest8, tt), h)


def _ffn_body(layer, be_ref, nu_ref, nx_ref, sl_ref, br_ref, xs_ref, wg_hbm, bg_ref, wu_hbm, bu_ref, wd_hbm,
              bd_ref, y_ref, w_f32, w_bf, sem):
    i = pl.program_id(0)
    slot = sl_ref[i]

    def fetch(e, s):
        return [pltpu.make_async_copy(w.at[layer, e], w_f32.at[s, n], sem.at[s])
                for n, w in enumerate((wg_hbm, wu_hbm, wd_hbm))]

    @pl.when(i == 0)
    def _():
        for c in fetch(be_ref[0], slot):
            c.start()

    @pl.when((i == 0) | (be_ref[i] != be_ref[jnp.maximum(i - 1, 0)]))
    def _():
        for c in fetch(be_ref[i], slot):
            c.wait()

        @pl.when(nx_ref[i] >= 0)
        def _():
            for c in fetch(nx_ref[i], 1 - slot):
                c.start()

        for n in range(3):
            w_bf[n] = w_f32[slot, n].astype(BF16)

    def expert_rows(rows):
        x = _load_row_tiles(xs_ref, rows).astype(BF16)
        g = jnp.minimum(_dot(x, w_bf[0]) + bg_ref[...], SWIGLU_LIMIT)
        u = jnp.clip(_dot(x, w_bf[1]) + bu_ref[...], -SWIGLU_LIMIT, SWIGLU_LIMIT)
        a = g * jax.nn.sigmoid(SWIGLU_ALPHA * g) * (u + 1.0)
        _store_row_tiles(y_ref, _dot(a.astype(BF16), w_bf[2]) + bd_ref[...])

    half = EXPERT_BLOCK // 2
    valid = br_ref[i]

    @pl.when(valid > half)
    def _():
        expert_rows(EXPERT_BLOCK)

    @pl.when((valid > 0) & (valid <= half))
    def _():
        expert_rows(half)
        y_ref[half * ROW_TILE:, :] = jnp.zeros((half * ROW_TILE, LANES), y_ref.dtype)

    @pl.when(valid == 0)
    def _():
        y_ref[...] = jnp.zeros(y_ref.shape, y_ref.dtype)


def _ffn(xs, blk_meta, layer, w_gate, b_gate, w_up, b_up, w_down, b_down):
    n_rows = xs.shape[0] // ROW_TILE
    _, e, d, de = w_gate.shape
    assert d == de == D_MODEL
    n_blk = n_rows // EXPERT_BLOCK
    blk = (EXPERT_BLOCK * ROW_TILE, LANES)
    bspec = lambda c: pl.BlockSpec((None, None, 1, c), lambda i, be, *_: (layer, be[i], 0, 0))
    hbm = pl.BlockSpec(memory_space=pl.ANY)
    xs_map = lambda i, be, nu, *_: (jnp.minimum(i, jnp.maximum(nu[0] - 1, 0)), 0)
    return pl.pallas_call(
        functools.partial(_ffn_body, layer),
        grid_spec=pltpu.PrefetchScalarGridSpec(
            num_scalar_prefetch=len(blk_meta),
            grid=(n_blk,),
            in_specs=[
                pl.BlockSpec(blk, xs_map),
                hbm, bspec(de), hbm, bspec(de), hbm, bspec(d),
            ],
            out_specs=pl.BlockSpec(blk, lambda i, *_: (i, 0)),
            scratch_shapes=[pltpu.VMEM((2, 3, d, de), F32), pltpu.VMEM((3, d, de), BF16),
                            pltpu.SemaphoreType.DMA((2,))],
        ),
        out_shape=SDS((n_rows * ROW_TILE, LANES), F32),
        compiler_params=_cparams(("arbitrary",)),
        name="ffn",
    )(*blk_meta, xs, w_gate, b_gate.reshape(-1, e, 1, de), w_up, b_up.reshape(-1, e, 1, de),
      w_down, b_down.reshape(-1, e, 1, d))


COMBINE_RC = 128
COMBINE_PARTS = 4


def _combine_body(dest_ref, x_ref, mod_ref, p_ref, y_ref, o_ref, buf, sem):
    tt = x_ref.shape[0]
    tp = tt // COMBINE_PARTS

    def part_rows(part):
        return pl.ds(part * tp * ROW_TILE, tp * ROW_TILE)

    for part in range(COMBINE_PARTS):
        def issue(t, carry, part=part):
            t8 = pl.multiple_of(t * ROW_TILE, ROW_TILE)
            for k in range(TOP_K):
                d = pl.multiple_of(dest_ref[0, k * tt + t], ROW_TILE)
                pltpu.make_async_copy(y_ref.at[pl.ds(d, ROW_TILE)], buf.at[k, pl.ds(t8, ROW_TILE)],
                                      sem.at[part]).start(priority=k % 2)
            return carry

        lax.fori_loop(part * tp, (part + 1) * tp, issue, 0)

    g2 = mod_ref[0][5:6]
    for part in range(COMBINE_PARTS):
        pltpu.make_async_copy(buf.at[:, part_rows(part)], buf.at[:, part_rows(part)], sem.at[part]).wait()
        for r0 in range(part * tp, (part + 1) * tp, COMBINE_RC):
            rows = slice(r0, r0 + COMBINE_RC)
            p = p_ref[rows, :]
            for j in range(ROW_TILE):
                cols = slice(j * LANES, (j + 1) * LANES)
                tile = lambda k: buf[k, pl.ds(r0 * ROW_TILE + j, COMBINE_RC, stride=ROW_TILE), :]
                acc = p[:, 0:1] * tile(0)
                for k in range(1, TOP_K):
                    acc = acc + p[:, k:k + 1] * tile(k)
                o_ref[rows, cols] = x_ref[rows, cols] + g2[:, cols] * acc


def _combine(x, mod, p, y, dest8, seq):
    n, d = x.shape
    tt = min(COMBINE_TT, seq)
    n_steps = n // tt
    per_b = seq // tt
    return pl.pallas_call(
        _combine_body,
        grid=(n_steps,),
        in_specs=[
            pl.BlockSpec((None, 1, TOP_K * tt), lambda i: (i, 0, 0), memory_space=pltpu.SMEM),
            pl.BlockSpec((tt, d), lambda i: (i, 0)),
            pl.BlockSpec((1, 6, d), lambda i: (i // per_b, 0, 0)),
            pl.BlockSpec((tt, TOP_K), lambda i: (i, 0)),
            pl.BlockSpec(memory_space=pl.ANY),
        ],
        out_specs=pl.BlockSpec((tt, d), lambda i: (i, 0)),
        out_shape=SDS((n, d), F32),
        scratch_shapes=[pltpu.VMEM((TOP_K, tt * ROW_TILE, LANES), F32),
                        pltpu.SemaphoreType.DMA((COMBINE_PARTS,))],
        compiler_params=_cparams(("arbitrary",)),
        name="combine",
    )(_per_step(dest8, tt), x, mod, p, y)


def _moe(x, mod, layer, n2g, w_r, b_r, w_gate, b_gate, w_up, b_up, w_down, b_down):
    bsz, s, d = x.shape
    n = bsz * s
    h2, top_idx, probs, rank, counts = _router(x, mod, n2g, w_r, b_r)
    dest, blk_meta, pad_blk, n_rows = _route_meta(top_idx, rank, counts.reshape(N_EXPERTS))
    dest8 = dest * ROW_TILE
    xs = _dispatch(h2, dest8, pad_blk, n_rows)
    y = _ffn(xs, blk_meta, layer, w_gate, b_gate, w_up, b_up, w_down, b_down)
    out = _combine(x.reshape(n, d), mod, probs.reshape(n, TOP_K), y, dest8, s)
    return out.reshape(bsz, s, d)


def _seg_mats():
    r = lax.broadcasted_iota(jnp.int32, (D_MODEL, N_HEADS), 0) // HEAD_DIM
    c = lax.broadcasted_iota(jnp.int32, (D_MODEL, N_HEADS), 1)
    seg = jnp.where(r == c, 1.0, 0.0).astype(BF16)
    rt = lax.broadcasted_iota(jnp.int32, (N_HEADS, D_MODEL), 0)
    ct = lax.broadcasted_iota(jnp.int32, (N_HEADS, D_MODEL), 1) // HEAD_DIM
    seg_t = jnp.where(rt == ct, 1.0, 0.0).astype(BF16)
    return seg, seg_t


def _split_dot(a, b):
    hi = a.astype(BF16)
    lo = (a - hi.astype(F32)).astype(BF16)
    return _dot(hi, b) + _dot(lo, b)


def _head_rmsnorm(x, g, seg, seg_t):
    ms = _dot((x * x).astype(BF16), seg) * (1.0 / HEAD_DIM)
    r = lax.rsqrt(ms + EPS)
    return x * _split_dot(r, seg_t) * g


def _kvq_body(x_ref, mod_ref, kvmod_ref, n1g_ref, kvg_ref, wq_ref, wkv_ref, qg_ref, kg_ref,
              q_ref, k_ref, vt_ref, km_ref):
    x = x_ref[0]
    mod = mod_ref[0]
    kvmod = kvmod_ref[0]
    ms = jnp.mean(x * x, axis=-1, keepdims=True)
    xn = x * lax.rsqrt(ms + EPS)
    hq = ((xn * n1g_ref[...]) * (1.0 + mod[1:2]) + mod[0:1]).astype(BF16)
    hk = ((xn * kvg_ref[...]) * (1.0 + kvmod[1:2]) + kvmod[0:1]).astype(BF16)
    seg, seg_t = _seg_mats()
    q = _head_rmsnorm(_dot(hq, wq_ref[...]), qg_ref[...], seg, seg_t)
    q_ref[0] = q
    kv = _dot(hk, wkv_ref[...])
    k = _head_rmsnorm(kv[:, :D_MODEL], kg_ref[...], seg, seg_t)
    k_ref[0, 0] = k.astype(BF16)
    for u in range(ATT_U):
        rows = slice(u * MOBA_BLOCK, (u + 1) * MOBA_BLOCK)
        km_ref[0, 0, u:u + 1, :] = jnp.mean(k[rows], axis=0, keepdims=True)
    vt_ref[0, 0] = kv[:, D_MODEL:].T.astype(BF16)


def _kvq(x, mod, kvmod, n1g, kvg, w_q, w_kv, q_g, k_g):
    bsz, s, d = x.shape
    ts = ATT_U * MOBA_BLOCK
    nb = s // ts
    const = lambda *shape: pl.BlockSpec(shape, lambda b, i: (0,) * len(shape))
    return pl.pallas_call(
        _kvq_body,
        grid=(bsz, nb),
        in_specs=[
            pl.BlockSpec((1, ts, d), lambda b, i: (b, i, 0)),
            pl.BlockSpec((1, 6, d), lambda b, i: (b, 0, 0)),
            pl.BlockSpec((1, 2, d), lambda b, i: (b, 0, 0)),
            const(1, d), const(1, d), const(d, d), const(d, 2 * d), const(1, d), const(1, d),
        ],
        out_specs=[
            pl.BlockSpec((1, ts, d), lambda b, i: (b, i, 0)),
            pl.BlockSpec((1, 1, ts, d), lambda b, i: (b, i, 0, 0)),
            pl.BlockSpec((1, 1, d, ts), lambda b, i: (b, i, 0, 0)),
            pl.BlockSpec((1, 1, ATT_U, d), lambda b, i: (b, i, 0, 0)),
        ],
        out_shape=[
            SDS((bsz, s, d), F32),
            SDS((bsz, nb, ts, d), BF16),
            SDS((bsz, nb, d, ts), BF16),
            SDS((bsz, nb, ATT_U, d), F32),
        ],
        compiler_params=_cparams(("parallel", "arbitrary")),
        name="kvq",
    )(x, mod, kvmod, n1g.reshape(1, d), kvg.reshape(1, d), w_q.astype(BF16), w_kv.astype(BF16),
      jnp.tile(q_g, N_HEADS).reshape(1, d), jnp.tile(k_g, N_HEADS).reshape(1, d))


def _bf16_pieces(v):
    hi = v.astype(BF16).astype(F32)
    mid = (v - hi).astype(BF16).astype(F32)
    lo = ((v - hi) - mid).astype(BF16).astype(F32)
    return hi, mid, lo


def _attn_body(slopes_ref, q_ref, k_ref, vt_ref, km_ref, o_ref, cm_scr, kaug_scr, qaug_scr, s_scr):
    hp = pl.program_id(1)
    own = pl.program_id(2)
    bs = MOBA_BLOCK
    nb = k_ref.shape[1] * ATT_U
    pw = 2 * HEAD_DIM
    nh = 2 * ATT_HP

    @pl.when(own == 0)
    def _():
        rel = (lax.broadcasted_iota(jnp.int32, (bs, bs), 1)
               - lax.broadcasted_iota(jnp.int32, (bs, bs), 0))
        cm_scr[0] = jnp.zeros((bs, bs), F32)
        cm_scr[1] = jnp.where(rel >= 0, 0.0, -jnp.inf)
        klane = lax.broadcasted_iota(jnp.int32, (ATT_U * bs, ATT_AUG), 1)
        koff = (lax.broadcasted_iota(jnp.int32, (ATT_U * bs, ATT_AUG), 0) % bs).astype(F32)
        kaug_scr[...] = jnp.where(klane < 3, 1.0, jnp.where(klane < 6, koff, 0.0)).astype(BF16)
        qlane = lax.broadcasted_iota(jnp.int32, (bs, ATT_AUG), 1)
        qoff = lax.broadcasted_iota(jnp.int32, (bs, ATT_AUG), 0).astype(F32)
        for hd in range(nh):
            s2 = jnp.full((bs, ATT_AUG), slopes_ref[nh * hp + hd], F32) * LOG2E
            pieces = _bf16_pieces(-(s2 * qoff)) + _bf16_pieces(s2)
            aug = jnp.zeros((bs, ATT_AUG), F32)
            for lane_ix, piece in enumerate(pieces):
                aug = jnp.where(qlane == lane_ix, piece, aug)
            qaug_scr[hd] = aug.astype(BF16)

    dcol = lax.broadcasted_iota(jnp.int32, (1, pw), 1)
    blk_row = lax.broadcasted_iota(jnp.int32, (nb, 1), 0)
    valid = blk_row < own
    in_head = [(dcol >= h * HEAD_DIM) & (dcol < (h + 1) * HEAD_DIM) for h in range(2)]
    pair_cols = lambda hd: slice((hd // 2) * pw, (hd // 2 + 1) * pw)

    gates = []
    for pair in range(nh // 2):
        qp = q_ref[0, :, pair_cols(2 * pair)]
        kmp = km_ref[0, :, pair_cols(2 * pair)]
        km2 = jnp.concatenate([jnp.where(in_head[h], kmp, 0.0) for h in range(2)], axis=0)
        gates.append(lax.dot_general(km2, qp, NT_DIMS, precision=HIGHEST, preferred_element_type=F32))

    qs, selbias = [], []
    for hd in range(nh):
        slope = slopes_ref[nh * hp + hd]
        qm = jnp.where(in_head[hd % 2], q_ref[0, :, pair_cols(hd)], 0.0)
        gate = jnp.where(valid, gates[hd // 2][(hd % 2) * nb:(hd % 2 + 1) * nb], -jnp.inf)
        rank = jnp.zeros(gate.shape, F32)
        for i in range(nb):
            gi = gate[i:i + 1, :]
            tie = jnp.where(blk_row > i, 1.0, 0.0)
            rank = rank + jnp.where(gi > gate, 1.0, jnp.where(gi == gate, tie, 0.0))
        chosen = jnp.where(valid, jnp.where(rank < MOBA_TOPK, 1.0, 0.0),
                           jnp.where(blk_row == own, 1.0, 0.0))
        off = (own - blk_row).astype(F32) * (bs * LOG2E * slope)
        selbias.append(jnp.where(chosen > 0.5, -off, -jnp.inf))
        qs.append(jnp.concatenate([(qm * (LOG2E * HEAD_DIM ** -0.5)).astype(BF16), qaug_scr[hd]], axis=1))

    last = k_ref.shape[1] - 1
    ones = jnp.ones((ATT_ONES, ATT_U * bs), BF16)

    def qk(it, hd):
        kc = jnp.concatenate([k_ref[0, it, :, pair_cols(hd)], kaug_scr[...]], axis=1)
        return lax.dot_general(kc, qs[hd], NT_DIMS, preferred_element_type=F32)

    def visit(it, slot, carry, next_it, has_own):
        out = []
        if next_it is not None:
            for hd in range(nh):
                s_scr[1 - slot, hd] = qk(jnp.minimum(next_it, last), hd)
        for hd in range(nh):
            m, acc = carry[hd]

            def tile(u):
                t = s_scr[slot, hd, u * bs:(u + 1) * bs]
                if has_own:
                    t = t + cm_scr[(it * ATT_U + u == own).astype(jnp.int32)]
                return t

            rows = [jnp.sum(jnp.where(blk_row == it * ATT_U + u, selbias[hd], 0.0), axis=0, keepdims=True)
                    for u in range(ATT_U)]
            mx = m
            for u in range(ATT_U):
                mx = jnp.maximum(mx, jnp.max(tile(u), axis=0, keepdims=True) + rows[u])
            m_safe = jnp.where(mx == -jnp.inf, 0.0, mx)
            alpha = jnp.exp2(m - m_safe)
            p = jnp.concatenate([jnp.exp2((tile(u) + (rows[u] - m_safe)).astype(BF16)) for u in range(ATT_U)],
                                axis=0)
            vj = jnp.concatenate([vt_ref[0, it, hd * HEAD_DIM:(hd + 1) * HEAD_DIM, :], ones], axis=0)
            acc = alpha * acc + _dot(vj, p)
            out.append((mx, acc))
        return tuple(out)

    def body(i2, carry):
        carry = visit(2 * i2, 1, carry, 2 * i2 + 1, False)
        return visit(2 * i2 + 1, 0, carry, 2 * i2 + 2, False)

    n_past = own // ATT_U
    for hd in range(nh):
        s_scr[0, hd] = qk(n_past, hd)
    init = tuple((jnp.full((1, bs), -jnp.inf, F32), jnp.zeros((HEAD_DIM + ATT_ONES, bs), F32))
                 for hd in range(nh))
    carry = visit(n_past, 0, init, 0, True)
    carry = lax.fori_loop(0, n_past // 2, body, carry)
    carry = lax.cond(n_past % 2 == 1, lambda c: visit(n_past - 1, 1, c, None, False), lambda c: c, carry)
    outs = [acc[:HEAD_DIM] / acc[HEAD_DIM:HEAD_DIM + 1] for (_, acc) in carry]
    o_ref[0] = jnp.concatenate(outs, axis=0).astype(o_ref.dtype)


def _attn(q, kb, vtb, kmean):
    bsz, s, d = q.shape
    nsb = s // (ATT_U * MOBA_BLOCK)
    assert nsb % 2 == 0, "the attention loop visits key super-blocks in pairs"
    nb = s // MOBA_BLOCK
    w = 2 * HEAD_DIM * ATT_HP
    slopes = (2.0 ** (-8.0 * jnp.arange(1, N_HEADS + 1, dtype=F32) / N_HEADS)).astype(F32)
    return pl.pallas_call(
        _attn_body,
        grid_spec=pltpu.PrefetchScalarGridSpec(
            num_scalar_prefetch=1,
            grid=(bsz, N_HEADS // (2 * ATT_HP), nb),
            in_specs=[
                pl.BlockSpec((1, MOBA_BLOCK, w), lambda b, h, c, sl: (b, c, h)),
                pl.BlockSpec((1, nsb, ATT_U * MOBA_BLOCK, w), lambda b, h, c, sl: (b, 0, 0, h)),
                pl.BlockSpec((1, nsb, w, ATT_U * MOBA_BLOCK), lambda b, h, c, sl: (b, 0, h, 0)),
                pl.BlockSpec((1, nb, w), lambda b, h, c, sl: (b, 0, h)),
            ],
            out_specs=pl.BlockSpec((1, w, MOBA_BLOCK), lambda b, h, c, sl: (b, h, c)),
            scratch_shapes=[pltpu.VMEM((2, MOBA_BLOCK, MOBA_BLOCK), F32),
                            pltpu.VMEM((ATT_U * MOBA_BLOCK, ATT_AUG), BF16),
                            pltpu.VMEM((2 * ATT_HP, MOBA_BLOCK, ATT_AUG), BF16),
                            pltpu.VMEM((2, 2 * ATT_HP, ATT_U * MOBA_BLOCK, MOBA_BLOCK), F32)],
        ),
        out_shape=SDS((bsz, d, s), BF16),
        compiler_params=_cparams(("parallel", "parallel", "arbitrary")),
        name="attn",
    )(slopes, q, kb, vtb, kmean)


OPROJ_TS = 512


def _oproj_body(at_ref, x_ref, mod_ref, wo_ref, o_ref):
    g1 = mod_ref[0][2:3]
    m = lax.dot_general(at_ref[0], wo_ref[...], (((0,), (0,)), ((), ())), preferred_element_type=F32)
    o_ref[0] = x_ref[0] + g1 * m


def _oproj(at, x, mod, w_o):
    bsz, s, d = x.shape
    ts = min(OPROJ_TS, s)
    tok = pl.BlockSpec((1, ts, d), lambda b, i: (b, i, 0))
    return pl.pallas_call(
        _oproj_body,
        grid=(bsz, s // ts),
        in_specs=[pl.BlockSpec((1, d, ts), lambda b, i: (b, 0, i)), tok,
                  pl.BlockSpec((1, 6, d), lambda b, i: (b, 0, 0)),
                  pl.BlockSpec((d, d), lambda b, i: (0, 0))],
        out_specs=tok,
        out_shape=SDS((bsz, s, d), F32),
        compiler_params=_cparams(("parallel", "arbitrary")),
        name="oproj",
    )(at, x, mod, w_o.astype(BF16))


def kernel(x, c, ada_w, ada_b, norm1_g, norm2_g, sgu_w_in, sgu_b_in, sgu_v_g, sgu_w_s, sgu_b_s, sgu_w_out,
           kv_ada_w, kv_ada_b, kv_norm_g, w_kv, k_norm_g, attn_w_q, q_norm_g, attn_w_o,
           moe_w_router, moe_b_router, moe_w_gate, moe_b_gate, moe_w_up, moe_b_up, moe_w_down, moe_b_down):
    bsz, s, d = x.shape
    moe = lambda l, xx, mod: _moe(xx, mod, l, norm2_g[l], moe_w_router[l], moe_b_router[l],
                                  moe_w_gate, moe_b_gate, moe_w_up, moe_b_up, moe_w_down, moe_b_down)
    mod0 = _ada(c, ada_w, ada_b[0], 0).reshape(bsz, 6, d)
    x = _sgu(x, mod0, norm1_g[0], sgu_w_in[0], sgu_b_in[0], sgu_v_g[0], sgu_w_s[0], sgu_b_s[0], sgu_w_out[0])
    x = moe(0, x, mod0)
    mod1 = _ada(c, ada_w, ada_b[1], 1).reshape(bsz, 6, d)
    kvmod = _ada(c, kv_ada_w, kv_ada_b).reshape(bsz, 2, d)
    q, kb, vtb, kmean = _kvq(x, mod1, kvmod, norm1_g[1], kv_norm_g, attn_w_q[0], w_kv, q_norm_g[0], k_norm_g)
    a = _attn(q, kb, vtb, kmean.reshape(bsz, s // MOBA_BLOCK, d))
    x = _oproj(a, x, mod1, attn_w_o[0])
    x = moe(1, x, mod1)
    return x
```

```python
import functools
import math

import jax
import jax.numpy as jnp
import numpy as np
from jax import lax
from jax.experimental import pallas as pl
from jax.experimental.pallas import tpu as pltpu

F32 = jnp.float32
BF16 = jnp.bfloat16
HIGHEST = lax.Precision.HIGHEST
SDS = jax.ShapeDtypeStruct

D_MODEL = 1024
SGU_CHUNK = 128
SGU_GROUPS = 8
D_SGU = 3 * D_MODEL
SGU_GROUP_DIM = D_SGU // SGU_GROUPS
HEAD_DIM = 64
N_HEADS = D_MODEL // HEAD_DIM
MOBA_BLOCK = 256
MOBA_TOPK = 3
ATT_U = 2
ATT_ONES = 16
ATT_AUG = 128
ATT_HP = 4
LOG2E = math.log2(math.e)
N_EXPERTS = 32
TOP_K = 4
SWIGLU_LIMIT = 7.0
SWIGLU_ALPHA = 1.702
EXPERT_BLOCK = 512
EPS = 1e-6

VMEM_LIMIT_BYTES = 56 * 1024 * 1024
NT_DIMS = (((1,), (1,)), ((), ()))


def _cparams(sem):
    return pltpu.CompilerParams(dimension_semantics=sem, vmem_limit_bytes=VMEM_LIMIT_BYTES)


def _dot(a, b, **kw):
    return jnp.dot(a, b, preferred_element_type=F32, **kw)


LANES = 128
ROW_TILE = D_MODEL // LANES


def _store_row_tiles(ref, val):
    rows = val.shape[0]
    for j in range(ROW_TILE):
        ref[pl.ds(j, rows, stride=ROW_TILE), :] = val[:, j * LANES:(j + 1) * LANES]


def _load_row_tiles(ref, rows):
    return jnp.concatenate([ref[pl.ds(j, rows, stride=ROW_TILE), :] for j in range(ROW_TILE)], axis=-1)


def _dot_3pass_nt(a, b):
    a_hi = a.astype(BF16)
    a_lo = (a - a_hi.astype(F32)).astype(BF16)
    b_hi = b.astype(BF16)
    b_lo = (b - b_hi.astype(F32)).astype(BF16)
    nt = lambda u, v: lax.dot_general(u, v, NT_DIMS, preferred_element_type=F32)
    return nt(a_hi, b_hi) + (nt(a_lo, b_hi) + nt(a_hi, b_lo))


def _norm_mod(x, g, sc, sh):
    ms = jnp.mean(x * x, axis=-1, keepdims=True)
    return (x * lax.rsqrt(ms + EPS) * g) * (1.0 + sc) + sh


def _ada_body(c_ref, w_ref, b_ref, o_ref):
    c = c_ref[...]
    ca = c * jax.nn.sigmoid(c)
    o_ref[...] = _dot(ca, w_ref[...], precision=HIGHEST) + b_ref[...]


def _ada(c, w, b, layer=0):
    bsz, d = c.shape
    n = w.shape[-1]
    tn = 1024
    w = w.reshape(-1, d, n)
    return pl.pallas_call(
        _ada_body,
        grid=(n // tn,),
        in_specs=[
            pl.BlockSpec((bsz, d), lambda j: (0, 0)),
            pl.BlockSpec((None, d, tn), lambda j: (layer, 0, j)),
            pl.BlockSpec((1, tn), lambda j: (0, j)),
        ],
        out_specs=pl.BlockSpec((bsz, tn), lambda j: (0, j)),
        out_shape=SDS((bsz, n), F32),
        compiler_params=_cparams(("arbitrary",)),
        name="ada",
    )(c, w, b.reshape(1, n))


SGU_TS = 512
SGU_CB = 768


def _gelu_tanh(x):
    c = np.float32(np.sqrt(2.0 / np.pi))
    return x * (0.5 * (1.0 + jnp.tanh(c * (x + 0.044715 * (x * x * x)))))


def _sgu_body(x_ref, mod_ref, n1g_ref, win_ref, bin_ref, vg_ref, ws_ref, bst_ref, wout_ref,
              o_ref, u_scr, v_scr, y_scr):
    ts = x_ref.shape[1]
    x = x_ref[0]
    mod = mod_ref[0]
    sh1, sc1, g1 = mod[0:1], mod[1:2], mod[2:3]
    h = _norm_mod(x, n1g_ref[...], sc1, sh1).astype(BF16)

    ssq = jnp.zeros((ts, 1), F32)
    for j in range(D_SGU // SGU_CB):
        cs = slice(j * SGU_CB, (j + 1) * SGU_CB)
        ws_cols = slice(D_SGU + j * SGU_CB, D_SGU + (j + 1) * SGU_CB)
        z = _gelu_tanh(_dot(h, win_ref[:, ws_cols]) + bin_ref[:, ws_cols])
        v_scr[:, cs] = z
        ssq = ssq + jnp.sum(z * z, axis=-1, keepdims=True)
    rs = lax.rsqrt(ssq / D_SGU + EPS)

    r_i = lax.broadcasted_iota(jnp.int32, (SGU_CHUNK, SGU_CHUNK), 0)
    c_i = lax.broadcasted_iota(jnp.int32, (SGU_CHUNK, SGU_CHUNK), 1)
    causal = c_i <= r_i
    m = jnp.zeros((ts, x.shape[1]), F32)
    for j in range(D_SGU // SGU_CB):
        cb = slice(j * SGU_CB, (j + 1) * SGU_CB)
        u_scr[:, cb] = _gelu_tanh(_dot(h, win_ref[:, cb]) + bin_ref[:, cb])
        for g in range(j * SGU_CB // SGU_GROUP_DIM, (j + 1) * SGU_CB // SGU_GROUP_DIM):
            cs = slice(g * SGU_GROUP_DIM, (g + 1) * SGU_GROUP_DIM)
            wsg = jnp.where(causal, ws_ref[g], 0.0).astype(BF16)
            bsg = bst_ref[:, g:g + 1]
            for n in range(ts // SGU_CHUNK):
                rows = slice(n * SGU_CHUNK, (n + 1) * SGU_CHUNK)
                vn = (v_scr[rows, cs] * rs[rows] * vg_ref[:, cs]).astype(BF16)
                mixed = _dot(wsg, vn) + bsg
                y_scr[rows, cs] = (u_scr[rows, cs] * mixed).astype(BF16)
        m = m + _dot(y_scr[:, cb], wout_ref[cb, :])
    o_ref[0] = x + g1 * m


def _sgu(x, mod, n1g, w_in, b_in, v_g, w_s, b_s, w_out):
    bsz, s, d = x.shape
    ts = SGU_TS
    const = lambda *shape: pl.BlockSpec(shape, lambda b, i: (0,) * len(shape),
                                        pipeline_mode=pl.Buffered(1))
    return pl.pallas_call(
        _sgu_body,
        grid=(bsz, s // ts),
        in_specs=[
            pl.BlockSpec((1, ts, d), lambda b, i: (b, i, 0)),
            pl.BlockSpec((1, 6, d), lambda b, i: (b, 0, 0)),
            const(1, d),
            const(d, 2 * D_SGU),
            const(1, 2 * D_SGU),
            const(1, D_SGU),
            const(SGU_GROUPS, SGU_CHUNK, SGU_CHUNK),
            const(SGU_CHUNK, SGU_GROUPS),
            const(D_SGU, d),
        ],
        out_specs=pl.BlockSpec((1, ts, d), lambda b, i: (b, i, 0)),
        out_shape=SDS((bsz, s, d), F32),
        scratch_shapes=[
            pltpu.VMEM((ts, D_SGU), F32),
            pltpu.VMEM((ts, D_SGU), F32),
            pltpu.VMEM((ts, D_SGU), BF16),
        ],
        compiler_params=_cparams(("parallel", "arbitrary")),
        name="sgu",
    )(x, mod, n1g.reshape(1, d), w_in.astype(BF16), b_in.reshape(1, -1), v_g.reshape(1, -1),
      w_s, b_s.T, w_out.astype(BF16))


ROUTER_TS = 512


def _router_body(x_ref, mod_ref, n2g_ref, wrt_ref, brc_ref, h_ref, idx_ref, p_ref, rk_ref, cnt_ref, run_scr):
    @pl.when((pl.program_id(0) == 0) & (pl.program_id(1) == 0))
    def _():
        run_scr[...] = jnp.zeros(run_scr.shape, run_scr.dtype)

    x = x_ref[0]
    mod = mod_ref[0]
    sh2, sc2 = mod[3:4], mod[4:5]
    h2 = _norm_mod(x, n2g_ref[...], sc2, sh2)
    _store_row_tiles(h_ref, h2)
    ts = x.shape[0]
    logits = _dot_3pass_nt(wrt_ref[...], h2) + brc_ref[...]
    row = lax.broadcasted_iota(jnp.int32, logits.shape, 0)
    work = logits
    vals, idxs = [], []
    for _ in range(TOP_K):
        mx = jnp.max(work, axis=0, keepdims=True)
        ix = jnp.min(jnp.where(work == mx, row, N_EXPERTS), axis=0, keepdims=True)
        vals.append(mx)
        idxs.append(ix)
        work = jnp.where(row == ix, -jnp.inf, work)
    e = jnp.exp(jnp.concatenate(vals, axis=0) - vals[0])
    p_t = e / jnp.sum(e, axis=0, keepdims=True)
    idx_ref[0] = jnp.concatenate(idxs, axis=0)

    onehots = [row == ix for ix in idxs]
    mask = jnp.zeros(logits.shape, F32)
    for oh in onehots:
        mask = mask + jnp.where(oh, 1.0, 0.0)
    r_i = lax.broadcasted_iota(jnp.int32, (ts, ts), 0)
    c_i = lax.broadcasted_iota(jnp.int32, (ts, ts), 1)
    before = _dot(mask.astype(BF16), jnp.where(r_i < c_i, 1.0, 0.0).astype(BF16)) + run_scr[...]
    rk = [jnp.sum(jnp.where(oh, before, 0.0), axis=0, keepdims=True) for oh in onehots]
    rk_ref[0] = jnp.concatenate(rk, axis=0).astype(jnp.int32)
    run_scr[...] = run_scr[...] + jnp.sum(mask, axis=1, keepdims=True)
    cnt_ref[...] = run_scr[...].astype(jnp.int32)

    eye = jnp.where(r_i == c_i, 1.0, 0.0).astype(BF16)
    p_ref[0] = sum(lax.dot_general(eye, piece.astype(BF16), NT_DIMS, preferred_element_type=F32)
                   for piece in _bf16_pieces(p_t))


def _router(x, mod, n2g, w_r, b_r):
    bsz, s, d = x.shape
    ts = min(ROUTER_TS, s)
    kt = pl.BlockSpec((1, TOP_K, ts), lambda b, i: (b, 0, i))
    return pl.pallas_call(
        _router_body,
        grid=(bsz, s // ts),
        in_specs=[
            pl.BlockSpec((1, ts, d), lambda b, i: (b, i, 0)),
            pl.BlockSpec((1, 6, d), lambda b, i: (b, 0, 0)),
            pl.BlockSpec((1, d), lambda b, i: (0, 0)),
            pl.BlockSpec((N_EXPERTS, d), lambda b, i: (0, 0)),
            pl.BlockSpec((N_EXPERTS, 1), lambda b, i: (0, 0)),
        ],
        out_specs=[pl.BlockSpec((ts * ROW_TILE, LANES), lambda b, i: (b * (s // ts) + i, 0)),
                   kt, pl.BlockSpec((1, ts, TOP_K), lambda b, i: (b, i, 0)), kt,
                   pl.BlockSpec((N_EXPERTS, 1), lambda b, i: (0, 0))],
        out_shape=[SDS((bsz * s * ROW_TILE, LANES), F32), SDS((bsz, TOP_K, s), jnp.int32),
                   SDS((bsz, s, TOP_K), F32),
                   SDS((bsz, TOP_K, s), jnp.int32), SDS((N_EXPERTS, 1), jnp.int32)],
        scratch_shapes=[pltpu.VMEM((N_EXPERTS, 1), F32)],
        compiler_params=_cparams(("arbitrary", "arbitrary")),
        name="router",
    )(x, mod, n2g.reshape(1, d), w_r.T, b_r.reshape(-1, 1))


def _route_meta(top_idx, rank, counts):
    n = top_idx.size // TOP_K
    padded = (counts + EXPERT_BLOCK - 1) // EXPERT_BLOCK * EXPERT_BLOCK
    pend = jnp.cumsum(padded)
    pstart = pend - padded
    first_row = jnp.sum(jnp.where(top_idx[..., None] == jnp.arange(N_EXPERTS), pstart, 0), axis=-1)
    dest = (first_row + rank).astype(jnp.int32)
    n_rows = n * TOP_K + N_EXPERTS * EXPERT_BLOCK
    n_blk = n_rows // EXPERT_BLOCK
    blk_row0 = jnp.arange(n_blk) * EXPERT_BLOCK
    blk_exp = jnp.minimum(jnp.sum(pend[None, :] <= blk_row0[:, None], axis=1), N_EXPERTS - 1)
    n_used = (pend[-1] // EXPERT_BLOCK).astype(jnp.int32).reshape(1)
    eids = jnp.arange(N_EXPERTS)
    has_rows = counts > 0
    last_exp = jnp.max(jnp.where(has_rows, eids, 0))
    blk_exp = jnp.where(blk_row0 < pend[-1], blk_exp, last_exp).astype(jnp.int32)
    nxt_e = jnp.min(jnp.where((eids[None, :] > eids[:, None]) & has_rows[None, :], eids[None, :], N_EXPERTS), axis=1)
    nxt_e = jnp.where(nxt_e < N_EXPERTS, nxt_e, -1)
    slot_e = (jnp.cumsum(has_rows.astype(jnp.int32)) - 1) % 2
    of_blk = lambda tab: jnp.sum(jnp.where(blk_exp[:, None] == eids[None, :], tab[None, :], 0), axis=1)
    blk_nxt = of_blk(nxt_e).astype(jnp.int32)
    blk_slot = of_blk(slot_e).astype(jnp.int32)
    blk_valid = jnp.clip(of_blk(pstart + counts) - blk_row0, 0, EXPERT_BLOCK)
    blk_valid = jnp.where(blk_row0 < pend[-1], blk_valid, 0).astype(jnp.int32)
    pad_blk = jnp.where(padded > 0, pend - EXPERT_BLOCK, -1)
    tail = pend[-1] + jnp.arange(N_EXPERTS) * EXPERT_BLOCK
    tail_blk = jnp.where(tail < n_rows, tail, -1)
    pad_blk = jnp.concatenate([pad_blk, tail_blk]).astype(jnp.int32)
    return dest, (blk_exp, n_used, blk_nxt, blk_slot, blk_valid), pad_blk, n_rows


MOVE_TT = 2048
COMBINE_TT = 1024


def _per_step(dest, tt):
    bsz, k, s = dest.shape
    return dest.reshape(bsz, k, s // tt, tt).transpose(0, 2, 1, 3).reshape(bsz * (s // tt), 1, k * tt)


def _dispatch_body(pad_ref, dest_ref, h_ref, xs_ref, zero_scr, sem, zsem):
    tt = h_ref.shape[0] // ROW_TILE
    blk = EXPERT_BLOCK * ROW_TILE

    @pl.when(pl.program_id(0) == 0)
    def _():
        zero_scr[...] = jnp.zeros(zero_scr.shape, zero_scr.dtype)

        def zero_copy(e):
            row0 = pl.multiple_of(pad_ref[e], blk)
            return pltpu.make_async_copy(zero_scr, xs_ref.at[pl.ds(row0, blk)], zsem)

        for e in range(2 * N_EXPERTS):
            pl.when(pad_ref[e] >= 0)(lambda e=e: zero_copy(e).start())
        for e in range(2 * N_EXPERTS):
            pl.when(pad_ref[e] >= 0)(lambda e=e: zero_copy(e).wait())

    def issue(t, carry):
        src = h_ref.at[pl.ds(pl.multiple_of(t * ROW_TILE, ROW_TILE), ROW_TILE)]
        for k in range(TOP_K):
            d = pl.multiple_of(dest_ref[0, k * tt + t], ROW_TILE)
            pltpu.make_async_copy(src, xs_ref.at[pl.ds(d, ROW_TILE)], sem).start(priority=k % 2)
        return carry

    lax.fori_loop(0, tt, issue, 0)
    for k in range(TOP_K):
        pltpu.make_async_copy(h_ref, h_ref, sem).wait()


def _dispatch(h, dest8, pad_blk, n_rows):
    n = h.shape[0] // ROW_TILE
    tt = min(MOVE_TT, dest8.shape[-1])
    return pl.pallas_call(
        _dispatch_body,
        grid_spec=pltpu.PrefetchScalarGridSpec(
            num_scalar_prefetch=1,
            grid=(n // tt,),
            in_specs=[
                pl.BlockSpec((None, 1, TOP_K * tt), lambda i, pad: (i, 0, 0), memory_space=pltpu.SMEM),
                pl.BlockSpec((tt * ROW_TILE, LANES), lambda i, pad: (i, 0)),
            ],
            out_specs=pl.BlockSpec(memory_space=pl.ANY),
            scratch_shapes=[pltpu.VMEM((EXPERT_BLOCK * ROW_TILE, LANES), h.dtype), pltpu.SemaphoreType.DMA(()),
                            pltpu.SemaphoreType.DMA(())],
        ),
        out_shape=SDS((n_rows * ROW_TILE, LANES), h.dtype),
        compiler_params=_cparams(("arbitrary",)),
        name="dispatch",
    )(pad_blk * ROW_TILE, _per_step(dest8, tt), h)


def _ffn_body(layer, be_ref, nu_ref, nx_ref, sl_ref, br_ref, xs_ref, wg_hbm, bg_ref, wu_hbm, bu_ref, wd_hbm,
              bd_ref, y_ref, w_f32, w_bf, sem):
    i = pl.program_id(0)
    slot = sl_ref[i]

    def fetch(e, s):
        return [pltpu.make_async_copy(w.at[layer, e], w_f32.at[s, n], sem.at[s])
                for n, w in enumerate((wg_hbm, wu_hbm, wd_hbm))]

    @pl.when(i == 0)
    def _():
        for c in fetch(be_ref[0], slot):
            c.start()

    @pl.when((i == 0) | (be_ref[i] != be_ref[jnp.maximum(i - 1, 0)]))
    def _():
        for c in fetch(be_ref[i], slot):
            c.wait()

        @pl.when(nx_ref[i] >= 0)
        def _():
            for c in fetch(nx_ref[i], 1 - slot):
                c.start()

        for n in range(3):
            w_bf[n] = w_f32[slot, n].astype(BF16)

    def expert_rows(rows):
        x = _load_row_tiles(xs_ref, rows).astype(BF16)
        g = jnp.minimum(_dot(x, w_bf[0]) + bg_ref[...], SWIGLU_LIMIT)
        u = jnp.clip(_dot(x, w_bf[1]) + bu_ref[...], -SWIGLU_LIMIT, SWIGLU_LIMIT)
        a = g * jax.nn.sigmoid(SWIGLU_ALPHA * g) * (u + 1.0)
        _store_row_tiles(y_ref, _dot(a.astype(BF16), w_bf[2]) + bd_ref[...])

    half = EXPERT_BLOCK // 2
    valid = br_ref[i]

    @pl.when(valid > half)
    def _():
        expert_rows(EXPERT_BLOCK)

    @pl.when((valid > 0) & (valid <= half))
    def _():
        expert_rows(half)
        y_ref[half * ROW_TILE:, :] = jnp.zeros((half * ROW_TILE, LANES), y_ref.dtype)

    @pl.when(valid == 0)
    def _():
        y_ref[...] = jnp.zeros(y_ref.shape, y_ref.dtype)


def _ffn(xs, blk_meta, layer, w_gate, b_gate, w_up, b_up, w_down, b_down):
    n_rows = xs.shape[0] // ROW_TILE
    _, e, d, de = w_gate.shape
    assert d == de == D_MODEL
    n_blk = n_rows // EXPERT_BLOCK
    blk = (EXPERT_BLOCK * ROW_TILE, LANES)
    bspec = lambda c: pl.BlockSpec((None, None, 1, c), lambda i, be, *_: (layer, be[i], 0, 0))
    hbm = pl.BlockSpec(memory_space=pl.ANY)
    xs_map = lambda i, be, nu, *_: (jnp.minimum(i, jnp.maximum(nu[0] - 1, 0)), 0)
    return pl.pallas_call(
        functools.partial(_ffn_body, layer),
        grid_spec=pltpu.PrefetchScalarGridSpec(
            num_scalar_prefetch=len(blk_meta),
            grid=(n_blk,),
            in_specs=[
                pl.BlockSpec(blk, xs_map),
                hbm, bspec(de), hbm, bspec(de), hbm, bspec(d),
            ],
            out_specs=pl.BlockSpec(blk, lambda i, *_: (i, 0)),
            scratch_shapes=[pltpu.VMEM((2, 3, d, de), F32), pltpu.VMEM((3, d, de), BF16),
                            pltpu.SemaphoreType.DMA((2,))],
        ),
        out_shape=SDS((n_rows * ROW_TILE, LANES), F32),
        compiler_params=_cparams(("arbitrary",)),
        name="ffn",
    )(*blk_meta, xs, w_gate, b_gate.reshape(-1, e, 1, de), w_up, b_up.reshape(-1, e, 1, de),
      w_down, b_down.reshape(-1, e, 1, d))


COMBINE_RC = 128
COMBINE_PARTS = 4


def _combine_body(dest_ref, x_ref, mod_ref, p_ref, y_ref, o_ref, buf, sem):
    tt = x_ref.shape[0]
    tp = tt // COMBINE_PARTS

    def part_rows(part):
        return pl.ds(part * tp * ROW_TILE, tp * ROW_TILE)

    for part in range(COMBINE_PARTS):
        def issue(t, carry, part=part):
            t8 = pl.multiple_of(t * ROW_TILE, ROW_TILE)
            for k in range(TOP_K):
                d = pl.multiple_of(dest_ref[0, k * tt + t], ROW_TILE)
                pltpu.make_async_copy(y_ref.at[pl.ds(d, ROW_TILE)], buf.at[k, pl.ds(t8, ROW_TILE)],
                                      sem.at[part]).start(priority=k % 2)
            return carry

        lax.fori_loop(part * tp, (part + 1) * tp, issue, 0)

    g2 = mod_ref[0][5:6]
    for part in range(COMBINE_PARTS):
        pltpu.make_async_copy(buf.at[:, part_rows(part)], buf.at[:, part_rows(part)], sem.at[part]).wait()
        for r0 in range(part * tp, (part + 1) * tp, COMBINE_RC):
            rows = slice(r0, r0 + COMBINE_RC)
            p = p_ref[rows, :]
            for j in range(ROW_TILE):
                cols = slice(j * LANES, (j + 1) * LANES)
                tile = lambda k: buf[k, pl.ds(r0 * ROW_TILE + j, COMBINE_RC, stride=ROW_TILE), :]
                acc = p[:, 0:1] * tile(0)
                for k in range(1, TOP_K):
                    acc = acc + p[:, k:k + 1] * tile(k)
                o_ref[rows, cols] = x_ref[rows, cols] + g2[:, cols] * acc


def _combine(x, mod, p, y, dest8, seq):
    n, d = x.shape
    tt = min(COMBINE_TT, seq)
    n_steps = n // tt
    per_b = seq // tt
    return pl.pallas_call(
        _combine_body,
        grid=(n_steps,),
        in_specs=[
            pl.BlockSpec((None, 1, TOP_K * tt), lambda i: (i, 0, 0), memory_space=pltpu.SMEM),
            pl.BlockSpec((tt, d), lambda i: (i, 0)),
            pl.BlockSpec((1, 6, d), lambda i: (i // per_b, 0, 0)),
            pl.BlockSpec((tt, TOP_K), lambda i: (i, 0)),
            pl.BlockSpec(memory_space=pl.ANY),
        ],
        out_specs=pl.BlockSpec((tt, d), lambda i: (i, 0)),
        out_shape=SDS((n, d), F32),
        scratch_shapes=[pltpu.VMEM((TOP_K, tt * ROW_TILE, LANES), F32),
                        pltpu.SemaphoreType.DMA((COMBINE_PARTS,))],
        compiler_params=_cparams(("arbitrary",)),
        name="combine",
    )(_per_step(dest8, tt), x, mod, p, y)


def _moe(x, mod, layer, n2g, w_r, b_r, w_gate, b_gate, w_up, b_up, w_down, b_down):
    bsz, s, d = x.shape
    n = bsz * s
    h2, top_idx, probs, rank, counts = _router(x, mod, n2g, w_r, b_r)
    dest, blk_meta, pad_blk, n_rows = _route_meta(top_idx, rank, counts.reshape(N_EXPERTS))
    dest8 = dest * ROW_TILE
    xs = _dispatch(h2, dest8, pad_blk, n_rows)
    y = _ffn(xs, blk_meta, layer, w_gate, b_gate, w_up, b_up, w_down, b_down)
    out = _combine(x.reshape(n, d), mod, probs.reshape(n, TOP_K), y, dest8, s)
    return out.reshape(bsz, s, d)


def _seg_mats():
    r = lax.broadcasted_iota(jnp.int32, (D_MODEL, N_HEADS), 0) // HEAD_DIM
    c = lax.broadcasted_iota(jnp.int32, (D_MODEL, N_HEADS), 1)
    seg = jnp.where(r == c, 1.0, 0.0).astype(BF16)
    rt = lax.broadcasted_iota(jnp.int32, (N_HEADS, D_MODEL), 0)
    ct = lax.broadcasted_iota(jnp.int32, (N_HEADS, D_MODEL), 1) // HEAD_DIM
    seg_t = jnp.where(rt == ct, 1.0, 0.0).astype(BF16)
    return seg, seg_t


def _split_dot(a, b):
    hi = a.astype(BF16)
    lo = (a - hi.astype(F32)).astype(BF16)
    return _dot(hi, b) + _dot(lo, b)


def _head_rmsnorm(x, g, seg, seg_t):
    ms = _dot((x * x).astype(BF16), seg) * (1.0 / HEAD_DIM)
    r = lax.rsqrt(ms + EPS)
    return x * _split_dot(r, seg_t) * g


def _kvq_body(x_ref, mod_ref, kvmod_ref, n1g_ref, kvg_ref, wq_ref, wkv_ref, qg_ref, kg_ref,
              q_ref, k_ref, vt_ref, km_ref):
    x = x_ref[0]
    mod = mod_ref[0]
    kvmod = kvmod_ref[0]
    ms = jnp.mean(x * x, axis=-1, keepdims=True)
    xn = x * lax.rsqrt(ms + EPS)
    hq = ((xn * n1g_ref[...]) * (1.0 + mod[1:2]) + mod[0:1]).astype(BF16)
    hk = ((xn * kvg_ref[...]) * (1.0 + kvmod[1:2]) + kvmod[0:1]).astype(BF16)
    seg, seg_t = _seg_mats()
    q = _head_rmsnorm(_dot(hq, wq_ref[...]), qg_ref[...], seg, seg_t)
    q_ref[0] = q
    kv = _dot(hk, wkv_ref[...])
    k = _head_rmsnorm(kv[:, :D_MODEL], kg_ref[...], seg, seg_t)
    k_ref[0, 0] = k.astype(BF16)
    for u in range(ATT_U):
        rows = slice(u * MOBA_BLOCK, (u + 1) * MOBA_BLOCK)
        km_ref[0, 0, u:u + 1, :] = jnp.mean(k[rows], axis=0, keepdims=True)
    vt_ref[0, 0] = kv[:, D_MODEL:].T.astype(BF16)


def _kvq(x, mod, kvmod, n1g, kvg, w_q, w_kv, q_g, k_g):
    bsz, s, d = x.shape
    ts = ATT_U * MOBA_BLOCK
    nb = s // ts
    const = lambda *shape: pl.BlockSpec(shape, lambda b, i: (0,) * len(shape))
    return pl.pallas_call(
        _kvq_body,
        grid=(bsz, nb),
        in_specs=[
            pl.BlockSpec((1, ts, d), lambda b, i: (b, i, 0)),
            pl.BlockSpec((1, 6, d), lambda b, i: (b, 0, 0)),
            pl.BlockSpec((1, 2, d), lambda b, i: (b, 0, 0)),
            const(1, d), const(1, d), const(d, d), const(d, 2 * d), const(1, d), const(1, d),
        ],
        out_specs=[
            pl.BlockSpec((1, ts, d), lambda b, i: (b, i, 0)),
            pl.BlockSpec((1, 1, ts, d), lambda b, i: (b, i, 0, 0)),
            pl.BlockSpec((1, 1, d, ts), lambda b, i: (b, i, 0, 0)),
            pl.BlockSpec((1, 1, ATT_U, d), lambda b, i: (b, i, 0, 0)),
        ],
        out_shape=[
            SDS((bsz, s, d), F32),
            SDS((bsz, nb, ts, d), BF16),
            SDS((bsz, nb, d, ts), BF16),
            SDS((bsz, nb, ATT_U, d), F32),
        ],
        compiler_params=_cparams(("parallel", "arbitrary")),
        name="kvq",
    )(x, mod, kvmod, n1g.reshape(1, d), kvg.reshape(1, d), w_q.astype(BF16), w_kv.astype(BF16),
      jnp.tile(q_g, N_HEADS).reshape(1, d), jnp.tile(k_g, N_HEADS).reshape(1, d))


def _bf16_pieces(v):
    hi = v.astype(BF16).astype(F32)
    mid = (v - hi).astype(BF16).astype(F32)
    lo = ((v - hi) - mid).astype(BF16).astype(F32)
    return hi, mid, lo


def _attn_body(slopes_ref, q_ref, k_ref, vt_ref, km_ref, o_ref, cm_scr, kaug_scr, qaug_scr, s_scr):
    hp = pl.program_id(1)
    own = pl.program_id(2)
    bs = MOBA_BLOCK
    nb = k_ref.shape[1] * ATT_U
    pw = 2 * HEAD_DIM
    nh = 2 * ATT_HP

    @pl.when(own == 0)
    def _():
        rel = (lax.broadcasted_iota(jnp.int32, (bs, bs), 1)
               - lax.broadcasted_iota(jnp.int32, (bs, bs), 0))
        cm_scr[0] = jnp.zeros((bs, bs), F32)
        cm_scr[1] = jnp.where(rel >= 0, 0.0, -jnp.inf)
        klane = lax.broadcasted_iota(jnp.int32, (ATT_U * bs, ATT_AUG), 1)
        koff = (lax.broadcasted_iota(jnp.int32, (ATT_U * bs, ATT_AUG), 0) % bs).astype(F32)
        kaug_scr[...] = jnp.where(klane < 3, 1.0, jnp.where(klane < 6, koff, 0.0)).astype(BF16)
        qlane = lax.broadcasted_iota(jnp.int32, (bs, ATT_AUG), 1)
        qoff = lax.broadcasted_iota(jnp.int32, (bs, ATT_AUG), 0).astype(F32)
        for hd in range(nh):
            s2 = jnp.full((bs, ATT_AUG), slopes_ref[nh * hp + hd], F32) * LOG2E
            pieces = _bf16_pieces(-(s2 * qoff)) + _bf16_pieces(s2)
            aug = jnp.zeros((bs, ATT_AUG), F32)
            for lane_ix, piece in enumerate(pieces):
                aug = jnp.where(qlane == lane_ix, piece, aug)
            qaug_scr[hd] = aug.astype(BF16)

    dcol = lax.broadcasted_iota(jnp.int32, (1, pw), 1)
    blk_row = lax.broadcasted_iota(jnp.int32, (nb, 1), 0)
    valid = blk_row < own
    in_head = [(dcol >= h * HEAD_DIM) & (dcol < (h + 1) * HEAD_DIM) for h in range(2)]
    pair_cols = lambda hd: slice((hd // 2) * pw, (hd // 2 + 1) * pw)

    gates = []
    for pair in range(nh // 2):
        qp = q_ref[0, :, pair_cols(2 * pair)]
        kmp = km_ref[0, :, pair_cols(2 * pair)]
        km2 = jnp.concatenate([jnp.where(in_head[h], kmp, 0.0) for h in range(2)], axis=0)
        gates.append(lax.dot_general(km2, qp, NT_DIMS, precision=HIGHEST, preferred_element_type=F32))

    qs, selbias = [], []
    for hd in range(nh):
        slope = slopes_ref[nh * hp + hd]
        qm = jnp.where(in_head[hd % 2], q_ref[0, :, pair_cols(hd)], 0.0)
        gate = jnp.where(valid, gates[hd // 2][(hd % 2) * nb:(hd % 2 + 1) * nb], -jnp.inf)
        rank = jnp.zeros(gate.shape, F32)
        for i in range(nb):
            gi = gate[i:i + 1, :]
            tie = jnp.where(blk_row > i, 1.0, 0.0)
            rank = rank + jnp.where(gi > gate, 1.0, jnp.where(gi == gate, tie, 0.0))
        chosen = jnp.where(valid, jnp.where(rank < MOBA_TOPK, 1.0, 0.0),
                           jnp.where(blk_row == own, 1.0, 0.0))
        off = (own - blk_row).astype(F32) * (bs * LOG2E * slope)
        selbias.append(jnp.where(chosen > 0.5, -off, -jnp.inf))
        qs.append(jnp.concatenate([(qm * (LOG2E * HEAD_DIM ** -0.5)).astype(BF16), qaug_scr[hd]], axis=1))

    last = k_ref.shape[1] - 1
    ones = jnp.ones((ATT_ONES, ATT_U * bs), BF16)

    def qk(it, hd):
        kc = jnp.concatenate([k_ref[0, it, :, pair_cols(hd)], kaug_scr[...]], axis=1)
        return lax.dot_general(kc, qs[hd], NT_DIMS, preferred_element_type=F32)

    def visit(it, slot, carry, next_it, has_own):
        out = []
        if next_it is not None:
            for hd in range(nh):
                s_scr[1 - slot, hd] = qk(jnp.minimum(next_it, last), hd)
        for hd in range(nh):
            m, acc = carry[hd]

            def tile(u):
                t = s_scr[slot, hd, u * bs:(u + 1) * bs]
                if has_own:
                    t = t + cm_scr[(it * ATT_U + u == own).astype(jnp.int32)]
                return t

            rows = [jnp.sum(jnp.where(blk_row == it * ATT_U + u, selbias[hd], 0.0), axis=0, keepdims=True)
                    for u in range(ATT_U)]
            mx = m
            for u in range(ATT_U):
                mx = jnp.maximum(mx, jnp.max(tile(u), axis=0, keepdims=True) + rows[u])
            m_safe = jnp.where(mx == -jnp.inf, 0.0, mx)
            alpha = jnp.exp2(m - m_safe)
            p = jnp.concatenate([jnp.exp2((tile(u) + (rows[u] - m_safe)).astype(BF16)) for u in range(ATT_U)],
                                axis=0)
            vj = jnp.concatenate([vt_ref[0, it, hd * HEAD_DIM:(hd + 1) * HEAD_DIM, :], ones], axis=0)
            acc = alpha * acc + _dot(vj, p)
            out.append((mx, acc))
        return tuple(out)

    def body(i2, carry):
        carry = visit(2 * i2, 1, carry, 2 * i2 + 1, False)
        return visit(2 * i2 + 1, 0, carry, 2 * i2 + 2, False)

    n_past = own // ATT_U
    for hd in range(nh):
        s_scr[0, hd] = qk(n_past, hd)
    init = tuple((jnp.full((1, bs), -jnp.inf, F32), jnp.zeros((HEAD_DIM + ATT_ONES, bs), F32))
                 for hd in range(nh))
    carry = visit(n_past, 0, init, 0, True)
    carry = lax.fori_loop(0, n_past // 2, body, carry)
    carry = lax.cond(n_past % 2 == 1, lambda c: visit(n_past - 1, 1, c, None, False), lambda c: c, carry)
    outs = [acc[:HEAD_DIM] / acc[HEAD_DIM:HEAD_DIM + 1] for (_, acc) in carry]
    o_ref[0] = jnp.concatenate(outs, axis=0).astype(o_ref.dtype)


def _attn(q, kb, vtb, kmean):
    bsz, s, d = q.shape
    nsb = s // (ATT_U * MOBA_BLOCK)
    assert nsb % 2 == 0, "the attention loop visits key super-blocks in pairs"
    nb = s // MOBA_BLOCK
    w = 2 * HEAD_DIM * ATT_HP
    slopes = (2.0 ** (-8.0 * jnp.arange(1, N_HEADS + 1, dtype=F32) / N_HEADS)).astype(F32)
    return pl.pallas_call(
        _attn_body,
        grid_spec=pltpu.PrefetchScalarGridSpec(
            num_scalar_prefetch=1,
            grid=(bsz, N_HEADS // (2 * ATT_HP), nb),
            in_specs=[
                pl.BlockSpec((1, MOBA_BLOCK, w), lambda b, h, c, sl: (b, c, h)),
                pl.BlockSpec((1, nsb, ATT_U * MOBA_BLOCK, w), lambda b, h, c, sl: (b, 0, 0, h)),
                pl.BlockSpec((1, nsb, w, ATT_U * MOBA_BLOCK), lambda b, h, c, sl: (b, 0, h, 0)),
                pl.BlockSpec((1, nb, w), lambda b, h, c, sl: (b, 0, h)),
            ],
            out_specs=pl.BlockSpec((1, w, MOBA_BLOCK), lambda b, h, c, sl: (b, h, c)),
            scratch_shapes=[pltpu.VMEM((2, MOBA_BLOCK, MOBA_BLOCK), F32),
                            pltpu.VMEM((ATT_U * MOBA_BLOCK, ATT_AUG), BF16),
                            pltpu.VMEM((2 * ATT_HP, MOBA_BLOCK, ATT_AUG), BF16),
                            pltpu.VMEM((2, 2 * ATT_HP, ATT_U * MOBA_BLOCK, MOBA_BLOCK), F32)],
        ),
        out_shape=SDS((bsz, d, s), BF16),
        compiler_params=_cparams(("parallel", "parallel", "arbitrary")),
        name="attn",
    )(slopes, q, kb, vtb, kmean)


OPROJ_TS = 512


def _oproj_body(at_ref, x_ref, mod_ref, wo_ref, o_ref):
    g1 = mod_ref[0][2:3]
    m = lax.dot_general(at_ref[0], wo_ref[...], (((0,), (0,)), ((), ())), preferred_element_type=F32)
    o_ref[0] = x_ref[0] + g1 * m


def _oproj(at, x, mod, w_o):
    bsz, s, d = x.shape
    ts = min(OPROJ_TS, s)
    tok = pl.BlockSpec((1, ts, d), lambda b, i: (b, i, 0))
    return pl.pallas_call(
        _oproj_body,
        grid=(bsz, s // ts),
        in_specs=[pl.BlockSpec((1, d, ts), lambda b, i: (b, 0, i)), tok,
                  pl.BlockSpec((1, 6, d), lambda b, i: (b, 0, 0)),
                  pl.BlockSpec((d, d), lambda b, i: (0, 0))],
        out_specs=tok,
        out_shape=SDS((bsz, s, d), F32),
        compiler_params=_cparams(("parallel", "arbitrary")),
        name="oproj",
    )(at, x, mod, w_o.astype(BF16))


def kernel(x, c, ada_w, ada_b, norm1_g, norm2_g, sgu_w_in, sgu_b_in, sgu_v_g, sgu_w_s, sgu_b_s, sgu_w_out,
           kv_ada_w, kv_ada_b, kv_norm_g, w_kv, k_norm_g, attn_w_q, q_norm_g, attn_w_o,
           moe_w_router, moe_b_router, moe_w_gate, moe_b_gate, moe_w_up, moe_b_up, moe_w_down, moe_b_down):
    bsz, s, d = x.shape
    moe = lambda l, xx, mod: _moe(xx, mod, l, norm2_g[l], moe_w_router[l], moe_b_router[l],
                                  moe_w_gate, moe_b_gate, moe_w_up, moe_b_up, moe_w_down, moe_b_down)
    mod0 = _ada(c, ada_w, ada_b[0], 0).reshape(bsz, 6, d)
    x = _sgu(x, mod0, norm1_g[0], sgu_w_in[0], sgu_b_in[0], sgu_v_g[0], sgu_w_s[0], sgu_b_s[0], sgu_w_out[0])
    x = moe(0, x, mod0)
    mod1 = _ada(c, ada_w, ada_b[1], 1).reshape(bsz, 6, d)
    kvmod = _ada(c, kv_ada_w, kv_ada_b).reshape(bsz, 2, d)
    q, kb, vtb, kmean = _kvq(x, mod1, kvmod, norm1_g[1], kv_norm_g, attn_w_q[0], w_kv, q_norm_g[0], k_norm_g)
    a = _attn(q, kb, vtb, kmean.reshape(bsz, s // MOBA_BLOCK, d))
    x = _oproj(a, x, mod1, attn_w_o[0])
    x = moe(1, x, mod1)
    return x
```

```python
import functools
import math

import jax
import jax.numpy as jnp
import numpy as np
from jax import lax
from jax.experimental import pallas as pl
from jax.experimental.pallas import tpu as pltpu

F32 = jnp.float32
BF16 = jnp.bfloat16
HIGHEST = lax.Precision.HIGHEST
SDS = jax.ShapeDtypeStruct

D_MODEL = 1024
SGU_CHUNK = 128
SGU_GROUPS = 8
D_SGU = 3 * D_MODEL
SGU_GROUP_DIM = D_SGU // SGU_GROUPS
HEAD_DIM = 64
N_HEADS = D_MODEL // HEAD_DIM
MOBA_BLOCK = 256
MOBA_TOPK = 3
ATT_U = 2
ATT_ONES = 16
ATT_AUG = 128
ATT_HP = 4
LOG2E = math.log2(math.e)
N_EXPERTS = 32
TOP_K = 4
SWIGLU_LIMIT = 7.0
SWIGLU_ALPHA = 1.702
EXPERT_BLOCK = 512
EPS = 1e-6

VMEM_LIMIT_BYTES = 56 * 1024 * 1024
NT_DIMS = (((1,), (1,)), ((), ()))


def _cparams(sem):
    return pltpu.CompilerParams(dimension_semantics=sem, vmem_limit_bytes=VMEM_LIMIT_BYTES)


def _dot(a, b, **kw):
    return jnp.dot(a, b, preferred_element_type=F32, **kw)


LANES = 128
ROW_TILE = D_MODEL // LANES


def _store_row_tiles(ref, val):
    rows = val.shape[0]
    for j in range(ROW_TILE):
        ref[pl.ds(j, rows, stride=ROW_TILE), :] = val[:, j * LANES:(j + 1) * LANES]


def _load_row_tiles(ref, rows):
    return jnp.concatenate([ref[pl.ds(j, rows, stride=ROW_TILE), :] for j in range(ROW_TILE)], axis=-1)


def _dot_3pass_nt(a, b):
    a_hi = a.astype(BF16)
    a_lo = (a - a_hi.astype(F32)).astype(BF16)
    b_hi = b.astype(BF16)
    b_lo = (b - b_hi.astype(F32)).astype(BF16)
    nt = lambda u, v: lax.dot_general(u, v, NT_DIMS, preferred_element_type=F32)
    return nt(a_hi, b_hi) + (nt(a_lo, b_hi) + nt(a_hi, b_lo))


def _norm_mod(x, g, sc, sh):
    ms = jnp.mean(x * x, axis=-1, keepdims=True)
    return (x * lax.rsqrt(ms + EPS) * g) * (1.0 + sc) + sh


def _ada_body(c_ref, w_ref, b_ref, o_ref):
    c = c_ref[...]
    ca = c * jax.nn.sigmoid(c)
    o_ref[...] = _dot(ca, w_ref[...], precision=HIGHEST) + b_ref[...]


def _ada(c, w, b, layer=0):
    bsz, d = c.shape
    n = w.shape[-1]
    tn = 1024
    w = w.reshape(-1, d, n)
    return pl.pallas_call(
        _ada_body,
        grid=(n // tn,),
        in_specs=[
            pl.BlockSpec((bsz, d), lambda j: (0, 0)),
            pl.BlockSpec((None, d, tn), lambda j: (layer, 0, j)),
            pl.BlockSpec((1, tn), lambda j: (0, j)),
        ],
        out_specs=pl.BlockSpec((bsz, tn), lambda j: (0, j)),
        out_shape=SDS((bsz, n), F32),
        compiler_params=_cparams(("arbitrary",)),
        name="ada",
    )(c, w, b.reshape(1, n))


SGU_TS = 512
SGU_CB = 768


def _gelu_tanh(x):
    c = np.float32(np.sqrt(2.0 / np.pi))
    return x * (0.5 * (1.0 + jnp.tanh(c * (x + 0.044715 * (x * x * x)))))


def _sgu_body(x_ref, mod_ref, n1g_ref, win_ref, bin_ref, vg_ref, ws_ref, bst_ref, wout_ref,
              o_ref, u_scr, v_scr, y_scr):
    ts = x_ref.shape[1]
    x = x_ref[0]
    mod = mod_ref[0]
    sh1, sc1, g1 = mod[0:1], mod[1:2], mod[2:3]
    h = _norm_mod(x, n1g_ref[...], sc1, sh1).astype(BF16)

    ssq = jnp.zeros((ts, 1), F32)
    for j in range(D_SGU // SGU_CB):
        cs = slice(j * SGU_CB, (j + 1) * SGU_CB)
        ws_cols = slice(D_SGU + j * SGU_CB, D_SGU + (j + 1) * SGU_CB)
        z = _gelu_tanh(_dot(h, win_ref[:, ws_cols]) + bin_ref[:, ws_cols])
        v_scr[:, cs] = z
        ssq = ssq + jnp.sum(z * z, axis=-1, keepdims=True)
    rs = lax.rsqrt(ssq / D_SGU + EPS)

    r_i = lax.broadcasted_iota(jnp.int32, (SGU_CHUNK, SGU_CHUNK), 0)
    c_i = lax.broadcasted_iota(jnp.int32, (SGU_CHUNK, SGU_CHUNK), 1)
    causal = c_i <= r_i
    m = jnp.zeros((ts, x.shape[1]), F32)
    for j in range(D_SGU // SGU_CB):
        cb = slice(j * SGU_CB, (j + 1) * SGU_CB)
        u_scr[:, cb] = _gelu_tanh(_dot(h, win_ref[:, cb]) + bin_ref[:, cb])
        for g in range(j * SGU_CB // SGU_GROUP_DIM, (j + 1) * SGU_CB // SGU_GROUP_DIM):
            cs = slice(g * SGU_GROUP_DIM, (g + 1) * SGU_GROUP_DIM)
            wsg = jnp.where(causal, ws_ref[g], 0.0).astype(BF16)
            bsg = bst_ref[:, g:g + 1]
            for n in range(ts // SGU_CHUNK):
                rows = slice(n * SGU_CHUNK, (n + 1) * SGU_CHUNK)
                vn = (v_scr[rows, cs] * rs[rows] * vg_ref[:, cs]).astype(BF16)
                mixed = _dot(wsg, vn) + bsg
                y_scr[rows, cs] = (u_scr[rows, cs] * mixed).astype(BF16)
        m = m + _dot(y_scr[:, cb], wout_ref[cb, :])
    o_ref[0] = x + g1 * m


def _sgu(x, mod, n1g, w_in, b_in, v_g, w_s, b_s, w_out):
    bsz, s, d = x.shape
    ts = SGU_TS
    const = lambda *shape: pl.BlockSpec(shape, lambda b, i: (0,) * len(shape),
                                        pipeline_mode=pl.Buffered(1))
    return pl.pallas_call(
        _sgu_body,
        grid=(bsz, s // ts),
        in_specs=[
            pl.BlockSpec((1, ts, d), lambda b, i: (b, i, 0)),
            pl.BlockSpec((1, 6, d), lambda b, i: (b, 0, 0)),
            const(1, d),
            const(d, 2 * D_SGU),
            const(1, 2 * D_SGU),
            const(1, D_SGU),
            const(SGU_GROUPS, SGU_CHUNK, SGU_CHUNK),
            const(SGU_CHUNK, SGU_GROUPS),
            const(D_SGU, d),
        ],
        out_specs=pl.BlockSpec((1, ts, d), lambda b, i: (b, i, 0)),
        out_shape=SDS((bsz, s, d), F32),
        scratch_shapes=[
            pltpu.VMEM((ts, D_SGU), F32),
            pltpu.VMEM((ts, D_SGU), F32),
            pltpu.VMEM((ts, D_SGU), BF16),
        ],
        compiler_params=_cparams(("parallel", "arbitrary")),
        name="sgu",
    )(x, mod, n1g.reshape(1, d), w_in.astype(BF16), b_in.reshape(1, -1), v_g.reshape(1, -1),
      w_s, b_s.T, w_out.astype(BF16))


ROUTER_TS = 512


def _router_body(x_ref, mod_ref, n2g_ref, wrt_ref, brc_ref, h_ref, idx_ref, p_ref, rk_ref, cnt_ref, run_scr):
    @pl.when((pl.program_id(0) == 0) & (pl.program_id(1) == 0))
    def _():
        run_scr[...] = jnp.zeros(run_scr.shape, run_scr.dtype)

    x = x_ref[0]
    mod = mod_ref[0]
    sh2, sc2 = mod[3:4], mod[4:5]
    h2 = _norm_mod(x, n2g_ref[...], sc2, sh2)
    _store_row_tiles(h_ref, h2)
    ts = x.shape[0]
    logits = _dot_3pass_nt(wrt_ref[...], h2) + brc_ref[...]
    row = lax.broadcasted_iota(jnp.int32, logits.shape, 0)
    work = logits
    vals, idxs = [], []
    for _ in range(TOP_K):
        mx = jnp.max(work, axis=0, keepdims=True)
        ix = jnp.min(jnp.where(work == mx, row, N_EXPERTS), axis=0, keepdims=True)
        vals.append(mx)
        idxs.append(ix)
        work = jnp.where(row == ix, -jnp.inf, work)
    e = jnp.exp(jnp.concatenate(vals, axis=0) - vals[0])
    p_t = e / jnp.sum(e, axis=0, keepdims=True)
    idx_ref[0] = jnp.concatenate(idxs, axis=0)

    onehots = [row == ix for ix in idxs]
    mask = jnp.zeros(logits.shape, F32)
    for oh in onehots:
        mask = mask + jnp.where(oh, 1.0, 0.0)
    r_i = lax.broadcasted_iota(jnp.int32, (ts, ts), 0)
    c_i = lax.broadcasted_iota(jnp.int32, (ts, ts), 1)
    before = _dot(mask.astype(BF16), jnp.where(r_i < c_i, 1.0, 0.0).astype(BF16)) + run_scr[...]
    rk = [jnp.sum(jnp.where(oh, before, 0.0), axis=0, keepdims=True) for oh in onehots]
    rk_ref[0] = jnp.concatenate(rk, axis=0).astype(jnp.int32)
    run_scr[...] = run_scr[...] + jnp.sum(mask, axis=1, keepdims=True)
    cnt_ref[...] = run_scr[...].astype(jnp.int32)

    eye = jnp.where(r_i == c_i, 1.0, 0.0).astype(BF16)
    p_ref[0] = sum(lax.dot_general(eye, piece.astype(BF16), NT_DIMS, preferred_element_type=F32)
                   for piece in _bf16_pieces(p_t))


def _router(x, mod, n2g, w_r, b_r):
    bsz, s, d = x.shape
    ts = min(ROUTER_TS, s)
    kt = pl.BlockSpec((1, TOP_K, ts), lambda b, i: (b, 0, i))
    return pl.pallas_call(
        _router_body,
        grid=(bsz, s // ts),
        in_specs=[
            pl.BlockSpec((1, ts, d), lambda b, i: (b, i, 0)),
            pl.BlockSpec((1, 6, d), lambda b, i: (b, 0, 0)),
            pl.BlockSpec((1, d), lambda b, i: (0, 0)),
            pl.BlockSpec((N_EXPERTS, d), lambda b, i: (0, 0)),
            pl.BlockSpec((N_EXPERTS, 1), lambda b, i: (0, 0)),
        ],
        out_specs=[pl.BlockSpec((ts * ROW_TILE, LANES), lambda b, i: (b * (s // ts) + i, 0)),
                   kt, pl.BlockSpec((1, ts, TOP_K), lambda b, i: (b, i, 0)), kt,
                   pl.BlockSpec((N_EXPERTS, 1), lambda b, i: (0, 0))],
        out_shape=[SDS((bsz * s * ROW_TILE, LANES), F32), SDS((bsz, TOP_K, s), jnp.int32),
                   SDS((bsz, s, TOP_K), F32),
                   SDS((bsz, TOP_K, s), jnp.int32), SDS((N_EXPERTS, 1), jnp.int32)],
        scratch_shapes=[pltpu.VMEM((N_EXPERTS, 1), F32)],
        compiler_params=_cparams(("arbitrary", "arbitrary")),
        name="router",
    )(x, mod, n2g.reshape(1, d), w_r.T, b_r.reshape(-1, 1))


def _route_meta(top_idx, rank, counts):
    n = top_idx.size // TOP_K
    padded = (counts + EXPERT_BLOCK - 1) // EXPERT_BLOCK * EXPERT_BLOCK
    pend = jnp.cumsum(padded)
    pstart = pend - padded
    first_row = jnp.sum(jnp.where(top_idx[..., None] == jnp.arange(N_EXPERTS), pstart, 0), axis=-1)
    dest = (first_row + rank).astype(jnp.int32)
    n_rows = n * TOP_K + N_EXPERTS * EXPERT_BLOCK
    n_blk = n_rows // EXPERT_BLOCK
    blk_row0 = jnp.arange(n_blk) * EXPERT_BLOCK
    blk_exp = jnp.minimum(jnp.sum(pend[None, :] <= blk_row0[:, None], axis=1), N_EXPERTS - 1)
    n_used = (pend[-1] // EXPERT_BLOCK).astype(jnp.int32).reshape(1)
    eids = jnp.arange(N_EXPERTS)
    has_rows = counts > 0
    last_exp = jnp.max(jnp.where(has_rows, eids, 0))
    blk_exp = jnp.where(blk_row0 < pend[-1], blk_exp, last_exp).astype(jnp.int32)
    nxt_e = jnp.min(jnp.where((eids[None, :] > eids[:, None]) & has_rows[None, :], eids[None, :], N_EXPERTS), axis=1)
    nxt_e = jnp.where(nxt_e < N_EXPERTS, nxt_e, -1)
    slot_e = (jnp.cumsum(has_rows.astype(jnp.int32)) - 1) % 2
    of_blk = lambda tab: jnp.sum(jnp.where(blk_exp[:, None] == eids[None, :], tab[None, :], 0), axis=1)
    blk_nxt = of_blk(nxt_e).astype(jnp.int32)
    blk_slot = of_blk(slot_e).astype(jnp.int32)
    blk_valid = jnp.clip(of_blk(pstart + counts) - blk_row0, 0, EXPERT_BLOCK)
    blk_valid = jnp.where(blk_row0 < pend[-1], blk_valid, 0).astype(jnp.int32)
    pad_blk = jnp.where(padded > 0, pend - EXPERT_BLOCK, -1)
    tail = pend[-1] + jnp.arange(N_EXPERTS) * EXPERT_BLOCK
    tail_blk = jnp.where(tail < n_rows, tail, -1)
    pad_blk = jnp.concatenate([pad_blk, tail_blk]).astype(jnp.int32)
    return dest, (blk_exp, n_used, blk_nxt, blk_slot, blk_valid), pad_blk, n_rows


MOVE_TT = 2048
COMBINE_TT = 1024


def _per_step(dest, tt):
    bsz, k, s = dest.shape
    return dest.reshape(bsz, k, s // tt, tt).transpose(0, 2, 1, 3).reshape(bsz * (s // tt), 1, k * tt)


def _dispatch_body(pad_ref, dest_ref, h_ref, xs_ref, zero_scr, sem, zsem):
    tt = h_ref.shape[0] // ROW_TILE
    blk = EXPERT_BLOCK * ROW_TILE

    @pl.when(pl.program_id(0) == 0)
    def _():
        zero_scr[...] = jnp.zeros(zero_scr.shape, zero_scr.dtype)

        def zero_copy(e):
            row0 = pl.multiple_of(pad_ref[e], blk)
            return pltpu.make_async_copy(zero_scr, xs_ref.at[pl.ds(row0, blk)], zsem)

        for e in range(2 * N_EXPERTS):
            pl.when(pad_ref[e] >= 0)(lambda e=e: zero_copy(e).start())
        for e in range(2 * N_EXPERTS):
            pl.when(pad_ref[e] >= 0)(lambda e=e: zero_copy(e).wait())

    def issue(t, carry):
        src = h_ref.at[pl.ds(pl.multiple_of(t * ROW_TILE, ROW_TILE), ROW_TILE)]
        for k in range(TOP_K):
            d = pl.multiple_of(dest_ref[0, k * tt + t], ROW_TILE)
            pltpu.make_async_copy(src, xs_ref.at[pl.ds(d, ROW_TILE)], sem).start(priority=k % 2)
        return carry

    lax.fori_loop(0, tt, issue, 0)
    for k in range(TOP_K):
        pltpu.make_async_copy(h_ref, h_ref, sem).wait()


def _dispatch(h, dest8, pad_blk, n_rows):
    n = h.shape[0] // ROW_TILE
    tt = min(MOVE_TT, dest8.shape[-1])
    return pl.pallas_call(
        _dispatch_body,
        grid_spec=pltpu.PrefetchScalarGridSpec(
            num_scalar_prefetch=1,
            grid=(n // tt,),
            in_specs=[
                pl.BlockSpec((None, 1, TOP_K * tt), lambda i, pad: (i, 0, 0), memory_space=pltpu.SMEM),
                pl.BlockSpec((tt * ROW_TILE, LANES), lambda i, pad: (i, 0)),
            ],
            out_specs=pl.BlockSpec(memory_space=pl.ANY),
            scratch_shapes=[pltpu.VMEM((EXPERT_BLOCK * ROW_TILE, LANES), h.dtype), pltpu.SemaphoreType.DMA(()),
                            pltpu.SemaphoreType.DMA(())],
        ),
        out_shape=SDS((n_rows * ROW_TILE, LANES), h.dtype),
        compiler_params=_cparams(("arbitrary",)),
        name="dispatch",
    )(pad_blk * ROW_TILE, _per_step(dest8, tt), h)


def _ffn_body(layer, be_ref, nu_ref, nx_ref, sl_ref, br_ref, xs_ref, wg_hbm, bg_ref, wu_hbm, bu_ref, wd_hbm,
              bd_ref, y_ref, w_f32, w_bf, sem):
    i = pl.program_id(0)
    slot = sl_ref[i]

    def fetch(e, s):
        return [pltpu.make_async_copy(w.at[layer, e], w_f32.at[s, n], sem.at[s])
                for n, w in enumerate((wg_hbm, wu_hbm, wd_hbm))]

    @pl.when(i == 0)
    def _():
        for c in fetch(be_ref[0], slot):
            c.start()

    @pl.when((i == 0) | (be_ref[i] != be_ref[jnp.maximum(i - 1, 0)]))
    def _():
        for c in fetch(be_ref[i], slot):
            c.wait()

        @pl.when(nx_ref[i] >= 0)
        def _():
            for c in fetch(nx_ref[i], 1 - slot):
                c.start()

        for n in range(3):
            w_bf[n] = w_f32[slot, n].astype(BF16)

    def expert_rows(rows):
        x = _load_row_tiles(xs_ref, rows).astype(BF16)
        g = jnp.minimum(_dot(x, w_bf[0]) + bg_ref[...], SWIGLU_LIMIT)
        u = jnp.clip(_dot(x, w_bf[1]) + bu_ref[...], -SWIGLU_LIMIT, SWIGLU_LIMIT)
        a = g * jax.nn.sigmoid(SWIGLU_ALPHA * g) * (u + 1.0)
        _store_row_tiles(y_ref, _dot(a.astype(BF16), w_bf[2]) + bd_ref[...])

    half = EXPERT_BLOCK // 2
    valid = br_ref[i]

    @pl.when(valid > half)
    def _():
        expert_rows(EXPERT_BLOCK)

    @pl.when((valid > 0) & (valid <= half))
    def _():
        expert_rows(half)
        y_ref[half * ROW_TILE:, :] = jnp.zeros((half * ROW_TILE, LANES), y_ref.dtype)

    @pl.when(valid == 0)
    def _():
        y_ref[...] = jnp.zeros(y_ref.shape, y_ref.dtype)


def _ffn(xs, blk_meta, layer, w_gate, b_gate, w_up, b_up, w_down, b_down):
    n_rows = xs.shape[0] // ROW_TILE
    _, e, d, de = w_gate.shape
    assert d == de == D_MODEL
    n_blk = n_rows // EXPERT_BLOCK
    blk = (EXPERT_BLOCK * ROW_TILE, LANES)
    bspec = lambda c: pl.BlockSpec((None, None, 1, c), lambda i, be, *_: (layer, be[i], 0, 0))
    hbm = pl.BlockSpec(memory_space=pl.ANY)
    xs_map = lambda i, be, nu, *_: (jnp.minimum(i, jnp.maximum(nu[0] - 1, 0)), 0)
    return pl.pallas_call(
        functools.partial(_ffn_body, layer),
        grid_spec=pltpu.PrefetchScalarGridSpec(
            num_scalar_prefetch=len(blk_meta),
            grid=(n_blk,),
            in_specs=[
                pl.BlockSpec(blk, xs_map),
                hbm, bspec(de), hbm, bspec(de), hbm, bspec(d),
            ],
            out_specs=pl.BlockSpec(blk, lambda i, *_: (i, 0)),
            scratch_shapes=[pltpu.VMEM((2, 3, d, de), F32), pltpu.VMEM((3, d, de), BF16),
                            pltpu.SemaphoreType.DMA((2,))],
        ),
        out_shape=SDS((n_rows * ROW_TILE, LANES), F32),
        compiler_params=_cparams(("arbitrary",)),
        name="ffn",
    )(*blk_meta, xs, w_gate, b_gate.reshape(-1, e, 1, de), w_up, b_up.reshape(-1, e, 1, de),
      w_down, b_down.reshape(-1, e, 1, d))


COMBINE_RC = 128
COMBINE_PARTS = 8


def _combine_body(dest_ref, x_ref, mod_ref, p_ref, y_ref, o_ref, buf, sem):
    tt = x_ref.shape[0]
    tp = tt // COMBINE_PARTS

    def part_rows(part):
        return pl.ds(part * tp * ROW_TILE, tp * ROW_TILE)

    for part in range(COMBINE_PARTS):
        def issue(t, carry, part=part):
            t8 = pl.multiple_of(t * ROW_TILE, ROW_TILE)
            for k in range(TOP_K):
                d = pl.multiple_of(dest_ref[0, k * tt + t], ROW_TILE)
                pltpu.make_async_copy(y_ref.at[pl.ds(d, ROW_TILE)], buf.at[k, pl.ds(t8, ROW_TILE)],
                                      sem.at[part]).start(priority=k % 2)
            return carry

        lax.fori_loop(part * tp, (part + 1) * tp, issue, 0)

    g2 = mod_ref[0][5:6]
    for part in range(COMBINE_PARTS):
        pltpu.make_async_copy(buf.at[:, part_rows(part)], buf.at[:, part_rows(part)], sem.at[part]).wait()
        for r0 in range(part * tp, (part + 1) * tp, COMBINE_RC):
            rows = slice(r0, r0 + COMBINE_RC)
            p = p_ref[rows, :]
            for j in range(ROW_TILE):
                cols = slice(j * LANES, (j + 1) * LANES)
                tile = lambda k: buf[k, pl.ds(r0 * ROW_TILE + j, COMBINE_RC, stride=ROW_TILE), :]
                acc = p[:, 0:1] * tile(0)
                for k in range(1, TOP_K):
                    acc = acc + p[:, k:k + 1] * tile(k)
                o_ref[rows, cols] = x_ref[rows, cols] + g2[:, cols] * acc


def _combine(x, mod, p, y, dest8, seq):
    n, d = x.shape
    tt = min(COMBINE_TT, seq)
    n_steps = n // tt
    per_b = seq // tt
    return pl.pallas_call(
        _combine_body,
        grid=(n_steps,),
        in_specs=[
            pl.BlockSpec((None, 1, TOP_K * tt), lambda i: (i, 0, 0), memory_space=pltpu.SMEM),
            pl.BlockSpec((tt, d), lambda i: (i, 0)),
            pl.BlockSpec((1, 6, d), lambda i: (i // per_b, 0, 0)),
            pl.BlockSpec((tt, TOP_K), lambda i: (i, 0)),
            pl.BlockSpec(memory_space=pl.ANY),
        ],
        out_specs=pl.BlockSpec((tt, d), lambda i: (i, 0)),
        out_shape=SDS((n, d), F32),
        scratch_shapes=[pltpu.VMEM((TOP_K, tt * ROW_TILE, LANES), F32),
                        pltpu.SemaphoreType.DMA((COMBINE_PARTS,))],
        compiler_params=_cparams(("arbitrary",)),
        name="combine",
    )(_per_step(dest8, tt), x, mod, p, y)


def _moe(x, mod, layer, n2g, w_r, b_r, w_gate, b_gate, w_up, b_up, w_down, b_down):
    bsz, s, d = x.shape
    n = bsz * s
    h2, top_idx, probs, rank, counts = _router(x, mod, n2g, w_r, b_r)
    dest, blk_meta, pad_blk, n_rows = _route_meta(top_idx, rank, counts.reshape(N_EXPERTS))
    dest8 = dest * ROW_TILE
    xs = _dispatch(h2, dest8, pad_blk, n_rows)
    y = _ffn(xs, blk_meta, layer, w_gate, b_gate, w_up, b_up, w_down, b_down)
    out = _combine(x.reshape(n, d), mod, probs.reshape(n, TOP_K), y, dest8, s)
    return out.reshape(bsz, s, d)


def _seg_mats():
    r = lax.broadcasted_iota(jnp.int32, (D_MODEL, N_HEADS), 0) // HEAD_DIM
    c = lax.broadcasted_iota(jnp.int32, (D_MODEL, N_HEADS), 1)
    seg = jnp.where(r == c, 1.0, 0.0).astype(BF16)
    rt = lax.broadcasted_iota(jnp.int32, (N_HEADS, D_MODEL), 0)
    ct = lax.broadcasted_iota(jnp.int32, (N_HEADS, D_MODEL), 1) // HEAD_DIM
    seg_t = jnp.where(rt == ct, 1.0, 0.0).astype(BF16)
    return seg, seg_t


def _split_dot(a, b):
    hi = a.astype(BF16)
    lo = (a - hi.astype(F32)).astype(BF16)
    return _dot(hi, b) + _dot(lo, b)


def _head_rmsnorm(x, g, seg, seg_t):
    ms = _dot((x * x).astype(BF16), seg) * (1.0 / HEAD_DIM)
    r = lax.rsqrt(ms + EPS)
    return x * _split_dot(r, seg_t) * g


def _kvq_body(x_ref, mod_ref, kvmod_ref, n1g_ref, kvg_ref, wq_ref, wkv_ref, qg_ref, kg_ref,
              q_ref, k_ref, vt_ref, km_ref):
    x = x_ref[0]
    mod = mod_ref[0]
    kvmod = kvmod_ref[0]
    ms = jnp.mean(x * x, axis=-1, keepdims=True)
    xn = x * lax.rsqrt(ms + EPS)
    hq = ((xn * n1g_ref[...]) * (1.0 + mod[1:2]) + mod[0:1]).astype(BF16)
    hk = ((xn * kvg_ref[...]) * (1.0 + kvmod[1:2]) + kvmod[0:1]).astype(BF16)
    seg, seg_t = _seg_mats()
    q = _head_rmsnorm(_dot(hq, wq_ref[...]), qg_ref[...], seg, seg_t)
    q_ref[0] = q
    kv = _dot(hk, wkv_ref[...])
    k = _head_rmsnorm(kv[:, :D_MODEL], kg_ref[...], seg, seg_t)
    k_ref[0, 0] = k.astype(BF16)
    for u in range(ATT_U):
        rows = slice(u * MOBA_BLOCK, (u + 1) * MOBA_BLOCK)
        km_ref[0, 0, u:u + 1, :] = jnp.mean(k[rows], axis=0, keepdims=True)
    vt_ref[0, 0] = kv[:, D_MODEL:].T.astype(BF16)


def _kvq(x, mod, kvmod, n1g, kvg, w_q, w_kv, q_g, k_g):
    bsz, s, d = x.shape
    ts = ATT_U * MOBA_BLOCK
    nb = s // ts
    const = lambda *shape: pl.BlockSpec(shape, lambda b, i: (0,) * len(shape))
    return pl.pallas_call(
        _kvq_body,
        grid=(bsz, nb),
        in_specs=[
            pl.BlockSpec((1, ts, d), lambda b, i: (b, i, 0)),
            pl.BlockSpec((1, 6, d), lambda b, i: (b, 0, 0)),
            pl.BlockSpec((1, 2, d), lambda b, i: (b, 0, 0)),
            const(1, d), const(1, d), const(d, d), const(d, 2 * d), const(1, d), const(1, d),
        ],
        out_specs=[
            pl.BlockSpec((1, ts, d), lambda b, i: (b, i, 0)),
            pl.BlockSpec((1, 1, ts, d), lambda b, i: (b, i, 0, 0)),
            pl.BlockSpec((1, 1, d, ts), lambda b, i: (b, i, 0, 0)),
            pl.BlockSpec((1, 1, ATT_U, d), lambda b, i: (b, i, 0, 0)),
        ],
        out_shape=[
            SDS((bsz, s, d), F32),
            SDS((bsz, nb, ts, d), BF16),
            SDS((bsz, nb, d, ts), BF16),
            SDS((bsz, nb, ATT_U, d), F32),
        ],
        compiler_params=_cparams(("parallel", "arbitrary")),
        name="kvq",
    )(x, mod, kvmod, n1g.reshape(1, d), kvg.reshape(1, d), w_q.astype(BF16), w_kv.astype(BF16),
      jnp.tile(q_g, N_HEADS).reshape(1, d), jnp.tile(k_g, N_HEADS).reshape(1, d))


def _bf16_pieces(v):
    hi = v.astype(BF16).astype(F32)
    mid = (v - hi).astype(BF16).astype(F32)
    lo = ((v - hi) - mid).astype(BF16).astype(F32)
    return hi, mid, lo


def _attn_body(slopes_ref, q_ref, k_ref, vt_ref, km_ref, o_ref, cm_scr, kaug_scr, qaug_scr, s_scr):
    hp = pl.program_id(1)
    own = pl.program_id(2)
    bs = MOBA_BLOCK
    nb = k_ref.shape[1] * ATT_U
    pw = 2 * HEAD_DIM
    nh = 2 * ATT_HP

    @pl.when(own == 0)
    def _():
        rel = (lax.broadcasted_iota(jnp.int32, (bs, bs), 1)
               - lax.broadcasted_iota(jnp.int32, (bs, bs), 0))
        cm_scr[0] = jnp.zeros((bs, bs), F32)
        cm_scr[1] = jnp.where(rel >= 0, 0.0, -jnp.inf)
        klane = lax.broadcasted_iota(jnp.int32, (ATT_U * bs, ATT_AUG), 1)
        koff = (lax.broadcasted_iota(jnp.int32, (ATT_U * bs, ATT_AUG), 0) % bs).astype(F32)
        kaug_scr[...] = jnp.where(klane < 3, 1.0, jnp.where(klane < 6, koff, 0.0)).astype(BF16)
        qlane = lax.broadcasted_iota(jnp.int32, (bs, ATT_AUG), 1)
        qoff = lax.broadcasted_iota(jnp.int32, (bs, ATT_AUG), 0).astype(F32)
        for hd in range(nh):
            s2 = jnp.full((bs, ATT_AUG), slopes_ref[nh * hp + hd], F32) * LOG2E
            pieces = _bf16_pieces(-(s2 * qoff)) + _bf16_pieces(s2)
            aug = jnp.zeros((bs, ATT_AUG), F32)
            for lane_ix, piece in enumerate(pieces):
                aug = jnp.where(qlane == lane_ix, piece, aug)
            qaug_scr[hd] = aug.astype(BF16)

    dcol = lax.broadcasted_iota(jnp.int32, (1, pw), 1)
    blk_row = lax.broadcasted_iota(jnp.int32, (nb, 1), 0)
    valid = blk_row < own
    in_head = [(dcol >= h * HEAD_DIM) & (dcol < (h + 1) * HEAD_DIM) for h in range(2)]
    pair_cols = lambda hd: slice((hd // 2) * pw, (hd // 2 + 1) * pw)

    gates = []
    for pair in range(nh // 2):
        qp = q_ref[0, :, pair_cols(2 * pair)]
        kmp = km_ref[0, :, pair_cols(2 * pair)]
        km2 = jnp.concatenate([jnp.where(in_head[h], kmp, 0.0) for h in range(2)], axis=0)
        gates.append(lax.dot_general(km2, qp, NT_DIMS, precision=HIGHEST, preferred_element_type=F32))

    qs, selbias = [], []
    for hd in range(nh):
        slope = slopes_ref[nh * hp + hd]
        qm = jnp.where(in_head[hd % 2], q_ref[0, :, pair_cols(hd)], 0.0)
        gate = jnp.where(valid, gates[hd // 2][(hd % 2) * nb:(hd % 2 + 1) * nb], -jnp.inf)
        rank = jnp.zeros(gate.shape, F32)
        for i in range(nb):
            gi = gate[i:i + 1, :]
            tie = jnp.where(blk_row > i, 1.0, 0.0)
            rank = rank + jnp.where(gi > gate, 1.0, jnp.where(gi == gate, tie, 0.0))
        chosen = jnp.where(valid, jnp.where(rank < MOBA_TOPK, 1.0, 0.0),
                           jnp.where(blk_row == own, 1.0, 0.0))
        off = (own - blk_row).astype(F32) * (bs * LOG2E * slope)
        selbias.append(jnp.where(chosen > 0.5, -off, -jnp.inf))
        qs.append(jnp.concatenate([(qm * (LOG2E * HEAD_DIM ** -0.5)).astype(BF16), qaug_scr[hd]], axis=1))

    last = k_ref.shape[1] - 1
    ones = jnp.ones((ATT_ONES, ATT_U * bs), BF16)

    def qk(it, hd):
        kc = jnp.concatenate([k_ref[0, it, :, pair_cols(hd)], kaug_scr[...]], axis=1)
        return lax.dot_general(kc, qs[hd], NT_DIMS, preferred_element_type=F32)

    def visit(it, slot, carry, next_it, has_own):
        out = []
        if next_it is not None:
            for hd in range(nh):
                s_scr[1 - slot, hd] = qk(jnp.minimum(next_it, last), hd)
        for hd in range(nh):
            m, acc = carry[hd]

            def tile(u):
                t = s_scr[slot, hd, u * bs:(u + 1) * bs]
                if has_own:
                    t = t + cm_scr[(it * ATT_U + u == own).astype(jnp.int32)]
                return t

            rows = [jnp.sum(jnp.where(blk_row == it * ATT_U + u, selbias[hd], 0.0), axis=0, keepdims=True)
                    for u in range(ATT_U)]
            mx = m
            for u in range(ATT_U):
                mx = jnp.maximum(mx, jnp.max(tile(u), axis=0, keepdims=True) + rows[u])
            m_safe = jnp.where(mx == -jnp.inf, 0.0, mx)
            alpha = jnp.exp2(m - m_safe)
            p = jnp.concatenate([jnp.exp2((tile(u) + (rows[u] - m_safe)).astype(BF16)) for u in range(ATT_U)],
                                axis=0)
            vj = jnp.concatenate([vt_ref[0, it, hd * HEAD_DIM:(hd + 1) * HEAD_DIM, :], ones], axis=0)
            acc = alpha * acc + _dot(vj, p)
            out.append((mx, acc))
        return tuple(out)

    def body(i2, carry):
        carry = visit(2 * i2, 1, carry, 2 * i2 + 1, False)
        return visit(2 * i2 + 1, 0, carry, 2 * i2 + 2, False)

    n_past = own // ATT_U
    for hd in range(nh):
        s_scr[0, hd] = qk(n_past, hd)
    init = tuple((jnp.full((1, bs), -jnp.inf, F32), jnp.zeros((HEAD_DIM + ATT_ONES, bs), F32))
                 for hd in range(nh))
    carry = visit(n_past, 0, init, 0, True)
    carry = lax.fori_loop(0, n_past // 2, body, carry)
    carry = lax.cond(n_past % 2 == 1, lambda c: visit(n_past - 1, 1, c, None, False), lambda c: c, carry)
    outs = [acc[:HEAD_DIM] / acc[HEAD_DIM:HEAD_DIM + 1] for (_, acc) in carry]
    o_ref[0] = jnp.concatenate(outs, axis=0).astype(o_ref.dtype)


def _attn(q, kb, vtb, kmean):
    bsz, s, d = q.shape
    nsb = s // (ATT_U * MOBA_BLOCK)
    assert nsb % 2 == 0, "the attention loop visits key super-blocks in pairs"
    nb = s // MOBA_BLOCK
    w = 2 * HEAD_DIM * ATT_HP
    slopes = (2.0 ** (-8.0 * jnp.arange(1, N_HEADS + 1, dtype=F32) / N_HEADS)).astype(F32)
    return pl.pallas_call(
        _attn_body,
        grid_spec=pltpu.PrefetchScalarGridSpec(
            num_scalar_prefetch=1,
            grid=(bsz, N_HEADS // (2 * ATT_HP), nb),
            in_specs=[
                pl.BlockSpec((1, MOBA_BLOCK, w), lambda b, h, c, sl: (b, c, h)),
                pl.BlockSpec((1, nsb, ATT_U * MOBA_BLOCK, w), lambda b, h, c, sl: (b, 0, 0, h)),
                pl.BlockSpec((1, nsb, w, ATT_U * MOBA_BLOCK), lambda b, h, c, sl: (b, 0, h, 0)),
                pl.BlockSpec((1, nb, w), lambda b, h, c, sl: (b, 0, h)),
            ],
            out_specs=pl.BlockSpec((1, w, MOBA_BLOCK), lambda b, h, c, sl: (b, h, c)),
            scratch_shapes=[pltpu.VMEM((2, MOBA_BLOCK, MOBA_BLOCK), F32),
                            pltpu.VMEM((ATT_U * MOBA_BLOCK, ATT_AUG), BF16),
                            pltpu.VMEM((2 * ATT_HP, MOBA_BLOCK, ATT_AUG), BF16),
                            pltpu.VMEM((2, 2 * ATT_HP, ATT_U * MOBA_BLOCK, MOBA_BLOCK), F32)],
        ),
        out_shape=SDS((bsz, d, s), BF16),
        compiler_params=_cparams(("parallel", "parallel", "arbitrary")),
        name="attn",
    )(slopes, q, kb, vtb, kmean)


OPROJ_TS = 1024


def _oproj_body(at_ref, x_ref, mod_ref, wo_ref, o_ref):
    g1 = mod_ref[0][2:3]
    m = lax.dot_general(at_ref[0], wo_ref[...], (((0,), (0,)), ((), ())), preferred_element_type=F32)
    o_ref[0] = x_ref[0] + g1 * m


def _oproj(at, x, mod, w_o):
    bsz, s, d = x.shape
    ts = min(OPROJ_TS, s)
    tok = pl.BlockSpec((1, ts, d), lambda b, i: (b, i, 0))
    return pl.pallas_call(
        _oproj_body,
        grid=(bsz, s // ts),
        in_specs=[pl.BlockSpec((1, d, ts), lambda b, i: (b, 0, i)), tok,
                  pl.BlockSpec((1, 6, d), lambda b, i: (b, 0, 0)),
                  pl.BlockSpec((d, d), lambda b, i: (0, 0))],
        out_specs=tok,
        out_shape=SDS((bsz, s, d), F32),
        compiler_params=_cparams(("parallel", "arbitrary")),
        name="oproj",
    )(at, x, mod, w_o.astype(BF16))


def kernel(x, c, ada_w, ada_b, norm1_g, norm2_g, sgu_w_in, sgu_b_in, sgu_v_g, sgu_w_s, sgu_b_s, sgu_w_out,
           kv_ada_w, kv_ada_b, kv_norm_g, w_kv, k_norm_g, attn_w_q, q_norm_g, attn_w_o,
           moe_w_router, moe_b_router, moe_w_gate, moe_b_gate, moe_w_up, moe_b_up, moe_w_down, moe_b_down):
    bsz, s, d = x.shape
    moe = lambda l, xx, mod: _moe(xx, mod, l, norm2_g[l], moe_w_router[l], moe_b_router[l],
                                  moe_w_gate, moe_b_gate, moe_w_up, moe_b_up, moe_w_down, moe_b_down)
    mod0 = _ada(c, ada_w, ada_b[0], 0).reshape(bsz, 6, d)
    x = _sgu(x, mod0, norm1_g[0], sgu_w_in[0], sgu_b_in[0], sgu_v_g[0], sgu_w_s[0], sgu_b_s[0], sgu_w_out[0])
    x = moe(0, x, mod0)
    mod1 = _ada(c, ada_w, ada_b[1], 1).reshape(bsz, 6, d)
    kvmod = _ada(c, kv_ada_w, kv_ada_b).reshape(bsz, 2, d)
    q, kb, vtb, kmean = _kvq(x, mod1, kvmod, norm1_g[1], kv_norm_g, attn_w_q[0], w_kv, q_norm_g[0], k_norm_g)
    a = _attn(q, kb, vtb, kmean.reshape(bsz, s // MOBA_BLOCK, d))
    x = _oproj(a, x, mod1, attn_w_o[0])
    x = moe(1, x, mod1)
    return x
```

```python
import functools
import math

import jax
import jax.numpy as jnp
import numpy as np
from jax import lax
from jax.experimental import pallas as pl
from jax.experimental.pallas import tpu as pltpu

F32 = jnp.float32
BF16 = jnp.bfloat16
HIGHEST = lax.Precision.HIGHEST
SDS = jax.ShapeDtypeStruct

D_MODEL = 1024
SGU_CHUNK = 128
SGU_GROUPS = 8
D_SGU = 3 * D_MODEL
SGU_GROUP_DIM = D_SGU // SGU_GROUPS
HEAD_DIM = 64
N_HEADS = D_MODEL // HEAD_DIM
MOBA_BLOCK = 256
MOBA_TOPK = 3
ATT_U = 2
ATT_ONES = 16
ATT_AUG = 128
ATT_HP = 4
LOG2E = math.log2(math.e)
N_EXPERTS = 32
TOP_K = 4
SWIGLU_LIMIT = 7.0
SWIGLU_ALPHA = 1.702
EXPERT_BLOCK = 512
EPS = 1e-6

VMEM_LIMIT_BYTES = 56 * 1024 * 1024
NT_DIMS = (((1,), (1,)), ((), ()))


def _cparams(sem):
    return pltpu.CompilerParams(dimension_semantics=sem, vmem_limit_bytes=VMEM_LIMIT_BYTES)


def _dot(a, b, **kw):
    return jnp.dot(a, b, preferred_element_type=F32, **kw)


LANES = 128
ROW_TILE = D_MODEL // LANES


def _store_row_tiles(ref, val):
    rows = val.shape[0]
    for j in range(ROW_TILE):
        ref[pl.ds(j, rows, stride=ROW_TILE), :] = val[:, j * LANES:(j + 1) * LANES]


def _load_row_tiles(ref, rows):
    return jnp.concatenate([ref[pl.ds(j, rows, stride=ROW_TILE), :] for j in range(ROW_TILE)], axis=-1)


def _dot_3pass_nt(a, b):
    a_hi = a.astype(BF16)
    a_lo = (a - a_hi.astype(F32)).astype(BF16)
    b_hi = b.astype(BF16)
    b_lo = (b - b_hi.astype(F32)).astype(BF16)
    nt = lambda u, v: lax.dot_general(u, v, NT_DIMS, preferred_element_type=F32)
    return nt(a_hi, b_hi) + (nt(a_lo, b_hi) + nt(a_hi, b_lo))


def _norm_mod(x, g, sc, sh):
    ms = jnp.mean(x * x, axis=-1, keepdims=True)
    return (x * lax.rsqrt(ms + EPS) * g) * (1.0 + sc) + sh


def _ada_body(c_ref, w_ref, b_ref, o_ref):
    c = c_ref[...]
    ca = c * jax.nn.sigmoid(c)
    o_ref[...] = _dot(ca, w_ref[...], precision=HIGHEST) + b_ref[...]


def _ada(c, w, b, layer=0):
    bsz, d = c.shape
    n = w.shape[-1]
    tn = 1024
    w = w.reshape(-1, d, n)
    return pl.pallas_call(
        _ada_body,
        grid=(n // tn,),
        in_specs=[
            pl.BlockSpec((bsz, d), lambda j: (0, 0)),
            pl.BlockSpec((None, d, tn), lambda j: (layer, 0, j)),
            pl.BlockSpec((1, tn), lambda j: (0, j)),
        ],
        out_specs=pl.BlockSpec((bsz, tn), lambda j: (0, j)),
        out_shape=SDS((bsz, n), F32),
        compiler_params=_cparams(("arbitrary",)),
        name="ada",
    )(c, w, b.reshape(1, n))


SGU_TS = 512
SGU_CB = 768


def _gelu_tanh(x):
    c = np.float32(np.sqrt(2.0 / np.pi))
    return x * (0.5 * (1.0 + jnp.tanh(c * (x + 0.044715 * (x * x * x)))))


def _sgu_body(x_ref, mod_ref, n1g_ref, win_ref, bin_ref, vg_ref, ws_ref, bst_ref, wout_ref,
              o_ref, u_scr, v_scr, y_scr):
    ts = x_ref.shape[1]
    x = x_ref[0]
    mod = mod_ref[0]
    sh1, sc1, g1 = mod[0:1], mod[1:2], mod[2:3]
    h = _norm_mod(x, n1g_ref[...], sc1, sh1).astype(BF16)

    ssq = jnp.zeros((ts, 1), F32)
    for j in range(D_SGU // SGU_CB):
        cs = slice(j * SGU_CB, (j + 1) * SGU_CB)
        ws_cols = slice(D_SGU + j * SGU_CB, D_SGU + (j + 1) * SGU_CB)
        z = _gelu_tanh(_dot(h, win_ref[:, ws_cols]) + bin_ref[:, ws_cols])
        v_scr[:, cs] = z
        ssq = ssq + jnp.sum(z * z, axis=-1, keepdims=True)
    rs = lax.rsqrt(ssq / D_SGU + EPS)

    r_i = lax.broadcasted_iota(jnp.int32, (SGU_CHUNK, SGU_CHUNK), 0)
    c_i = lax.broadcasted_iota(jnp.int32, (SGU_CHUNK, SGU_CHUNK), 1)
    causal = c_i <= r_i
    m = jnp.zeros((ts, x.shape[1]), F32)
    for j in range(D_SGU // SGU_CB):
        cb = slice(j * SGU_CB, (j + 1) * SGU_CB)
        u_scr[:, cb] = _gelu_tanh(_dot(h, win_ref[:, cb]) + bin_ref[:, cb])
        for g in range(j * SGU_CB // SGU_GROUP_DIM, (j + 1) * SGU_CB // SGU_GROUP_DIM):
            cs = slice(g * SGU_GROUP_DIM, (g + 1) * SGU_GROUP_DIM)
            wsg = jnp.where(causal, ws_ref[g], 0.0).astype(BF16)
            bsg = bst_ref[:, g:g + 1]
            for n in range(ts // SGU_CHUNK):
                rows = slice(n * SGU_CHUNK, (n + 1) * SGU_CHUNK)
                vn = (v_scr[rows, cs] * rs[rows] * vg_ref[:, cs]).astype(BF16)
                mixed = _dot(wsg, vn) + bsg
                y_scr[rows, cs] = (u_scr[rows, cs] * mixed).astype(BF16)
        m = m + _dot(y_scr[:, cb], wout_ref[cb, :])
    o_ref[0] = x + g1 * m


def _sgu(x, mod, n1g, w_in, b_in, v_g, w_s, b_s, w_out):
    bsz, s, d = x.shape
    ts = SGU_TS
    const = lambda *shape: pl.BlockSpec(shape, lambda b, i: (0,) * len(shape),
                                        pipeline_mode=pl.Buffered(1))
    return pl.pallas_call(
        _sgu_body,
        grid=(bsz, s // ts),
        in_specs=[
            pl.BlockSpec((1, ts, d), lambda b, i: (b, i, 0)),
            pl.BlockSpec((1, 6, d), lambda b, i: (b, 0, 0)),
            const(1, d),
            const(d, 2 * D_SGU),
            const(1, 2 * D_SGU),
            const(1, D_SGU),
            const(SGU_GROUPS, SGU_CHUNK, SGU_CHUNK),
            const(SGU_CHUNK, SGU_GROUPS),
            const(D_SGU, d),
        ],
        out_specs=pl.BlockSpec((1, ts, d), lambda b, i: (b, i, 0)),
        out_shape=SDS((bsz, s, d), F32),
        scratch_shapes=[
            pltpu.VMEM((ts, D_SGU), F32),
            pltpu.VMEM((ts, D_SGU), F32),
            pltpu.VMEM((ts, D_SGU), BF16),
        ],
        compiler_params=_cparams(("parallel", "arbitrary")),
        name="sgu",
    )(x, mod, n1g.reshape(1, d), w_in.astype(BF16), b_in.reshape(1, -1), v_g.reshape(1, -1),
      w_s, b_s.T, w_out.astype(BF16))


ROUTER_TS = 512


def _router_body(x_ref, mod_ref, n2g_ref, wrt_ref, brc_ref, h_ref, idx_ref, p_ref, rk_ref, cnt_ref, run_scr):
    @pl.when((pl.program_id(0) == 0) & (pl.program_id(1) == 0))
    def _():
        run_scr[...] = jnp.zeros(run_scr.shape, run_scr.dtype)

    x = x_ref[0]
    mod = mod_ref[0]
    sh2, sc2 = mod[3:4], mod[4:5]
    h2 = _norm_mod(x, n2g_ref[...], sc2, sh2)
    _store_row_tiles(h_ref, h2)
    ts = x.shape[0]
    logits = _dot_3pass_nt(wrt_ref[...], h2) + brc_ref[...]
    row = lax.broadcasted_iota(jnp.int32, logits.shape, 0)
    work = logits
    vals, idxs = [], []
    for _ in range(TOP_K):
        mx = jnp.max(work, axis=0, keepdims=True)
        ix = jnp.min(jnp.where(work == mx, row, N_EXPERTS), axis=0, keepdims=True)
        vals.append(mx)
        idxs.append(ix)
        work = jnp.where(row == ix, -jnp.inf, work)
    e = jnp.exp(jnp.concatenate(vals, axis=0) - vals[0])
    p_t = e / jnp.sum(e, axis=0, keepdims=True)
    idx_ref[0] = jnp.concatenate(idxs, axis=0)

    onehots = [row == ix for ix in idxs]
    mask = jnp.zeros(logits.shape, F32)
    for oh in onehots:
        mask = mask + jnp.where(oh, 1.0, 0.0)
    r_i = lax.broadcasted_iota(jnp.int32, (ts, ts), 0)
    c_i = lax.broadcasted_iota(jnp.int32, (ts, ts), 1)
    before = _dot(mask.astype(BF16), jnp.where(r_i < c_i, 1.0, 0.0).astype(BF16)) + run_scr[...]
    rk = [jnp.sum(jnp.where(oh, before, 0.0), axis=0, keepdims=True) for oh in onehots]
    rk_ref[0] = jnp.concatenate(rk, axis=0).astype(jnp.int32)
    run_scr[...] = run_scr[...] + jnp.sum(mask, axis=1, keepdims=True)
    cnt_ref[...] = run_scr[...].astype(jnp.int32)

    eye = jnp.where(r_i == c_i, 1.0, 0.0).astype(BF16)
    p_ref[0] = sum(lax.dot_general(eye, piece.astype(BF16), NT_DIMS, preferred_element_type=F32)
                   for piece in _bf16_pieces(p_t))


def _router(x, mod, n2g, w_r, b_r):
    bsz, s, d = x.shape
    ts = min(ROUTER_TS, s)
    kt = pl.BlockSpec((1, TOP_K, ts), lambda b, i: (b, 0, i))
    return pl.pallas_call(
        _router_body,
        grid=(bsz, s // ts),
        in_specs=[
            pl.BlockSpec((1, ts, d), lambda b, i: (b, i, 0)),
            pl.BlockSpec((1, 6, d), lambda b, i: (b, 0, 0)),
            pl.BlockSpec((1, d), lambda b, i: (0, 0)),
            pl.BlockSpec((N_EXPERTS, d), lambda b, i: (0, 0)),
            pl.BlockSpec((N_EXPERTS, 1), lambda b, i: (0, 0)),
        ],
        out_specs=[pl.BlockSpec((ts * ROW_TILE, LANES), lambda b, i: (b * (s // ts) + i, 0)),
                   kt, pl.BlockSpec((1, ts, TOP_K), lambda b, i: (b, i, 0)), kt,
                   pl.BlockSpec((N_EXPERTS, 1), lambda b, i: (0, 0))],
        out_shape=[SDS((bsz * s * ROW_TILE, LANES), F32), SDS((bsz, TOP_K, s), jnp.int32),
                   SDS((bsz, s, TOP_K), F32),
                   SDS((bsz, TOP_K, s), jnp.int32), SDS((N_EXPERTS, 1), jnp.int32)],
        scratch_shapes=[pltpu.VMEM((N_EXPERTS, 1), F32)],
        compiler_params=_cparams(("arbitrary", "arbitrary")),
        name="router",
    )(x, mod, n2g.reshape(1, d), w_r.T, b_r.reshape(-1, 1))


def _route_meta(top_idx, rank, counts):
    n = top_idx.size // TOP_K
    padded = (counts + EXPERT_BLOCK - 1) // EXPERT_BLOCK * EXPERT_BLOCK
    pend = jnp.cumsum(padded)
    pstart = pend - padded
    first_row = jnp.sum(jnp.where(top_idx[..., None] == jnp.arange(N_EXPERTS), pstart, 0), axis=-1)
    dest = (first_row + rank).astype(jnp.int32)
    n_rows = n * TOP_K + N_EXPERTS * EXPERT_BLOCK
    n_blk = n_rows // EXPERT_BLOCK
    blk_row0 = jnp.arange(n_blk) * EXPERT_BLOCK
    blk_exp = jnp.minimum(jnp.sum(pend[None, :] <= blk_row0[:, None], axis=1), N_EXPERTS - 1)
    n_used = (pend[-1] // EXPERT_BLOCK).astype(jnp.int32).reshape(1)
    eids = jnp.arange(N_EXPERTS)
    has_rows = counts > 0
    last_exp = jnp.max(jnp.where(has_rows, eids, 0))
    blk_exp = jnp.where(blk_row0 < pend[-1], blk_exp, last_exp).astype(jnp.int32)
    nxt_e = jnp.min(jnp.where((eids[None, :] > eids[:, None]) & has_rows[None, :], eids[None, :], N_EXPERTS), axis=1)
    nxt_e = jnp.where(nxt_e < N_EXPERTS, nxt_e, -1)
    slot_e = (jnp.cumsum(has_rows.astype(jnp.int32)) - 1) % 2
    of_blk = lambda tab: jnp.sum(jnp.where(blk_exp[:, None] == eids[None, :], tab[None, :], 0), axis=1)
    blk_nxt = of_blk(nxt_e).astype(jnp.int32)
    blk_slot = of_blk(slot_e).astype(jnp.int32)
    blk_valid = jnp.clip(of_blk(pstart + counts) - blk_row0, 0, EXPERT_BLOCK)
    blk_valid = jnp.where(blk_row0 < pend[-1], blk_valid, 0).astype(jnp.int32)
    pad_blk = jnp.where(padded > 0, pend - EXPERT_BLOCK, -1)
    tail = pend[-1] + jnp.arange(N_EXPERTS) * EXPERT_BLOCK
    tail_blk = jnp.where(tail < n_rows, tail, -1)
    pad_blk = jnp.concatenate([pad_blk, tail_blk]).astype(jnp.int32)
    return dest, (blk_exp, n_used, blk_nxt, blk_slot, blk_valid), pad_blk, n_rows


MOVE_TT = 2048
COMBINE_TT = 1024


def _per_step(dest, tt):
    bsz, k, s = dest.shape
    return dest.reshape(bsz, k, s // tt, tt).transpose(0, 2, 1, 3).reshape(bsz * (s // tt), 1, k * tt)


def _dispatch_body(pad_ref, dest_ref, h_ref, xs_ref, zero_scr, sem, zsem):
    tt = h_ref.shape[0] // ROW_TILE
    blk = EXPERT_BLOCK * ROW_TILE

    @pl.when(pl.program_id(0) == 0)
    def _():
        zero_scr[...] = jnp.zeros(zero_scr.shape, zero_scr.dtype)

        def zero_copy(e):
            row0 = pl.multiple_of(pad_ref[e], blk)
            return pltpu.make_async_copy(zero_scr, xs_ref.at[pl.ds(row0, blk)], zsem)

        for e in range(2 * N_EXPERTS):
            pl.when(pad_ref[e] >= 0)(lambda e=e: zero_copy(e).start())
        for e in range(2 * N_EXPERTS):
            pl.when(pad_ref[e] >= 0)(lambda e=e: zero_copy(e).wait())

    def issue(t, carry):
        src = h_ref.at[pl.ds(pl.multiple_of(t * ROW_TILE, ROW_TILE), ROW_TILE)]
        for k in range(TOP_K):
            d = pl.multiple_of(dest_ref[0, k * tt + t], ROW_TILE)
            pltpu.make_async_copy(src, xs_ref.at[pl.ds(d, ROW_TILE)], sem).start(priority=k % 2)
        return carry

    lax.fori_loop(0, tt, issue, 0)
    for k in range(TOP_K):
        pltpu.make_async_copy(h_ref, h_ref, sem).wait()


def _dispatch(h, dest8, pad_blk, n_rows):
    n = h.shape[0] // ROW_TILE
    tt = min(MOVE_TT, dest8.shape[-1])
    return pl.pallas_call(
        _dispatch_body,
        grid_spec=pltpu.PrefetchScalarGridSpec(
            num_scalar_prefetch=1,
            grid=(n // tt,),
            in_specs=[
                pl.BlockSpec((None, 1, TOP_K * tt), lambda i, pad: (i, 0, 0), memory_space=pltpu.SMEM),
                pl.BlockSpec((tt * ROW_TILE, LANES), lambda i, pad: (i, 0)),
            ],
            out_specs=pl.BlockSpec(memory_space=pl.ANY),
            scratch_shapes=[pltpu.VMEM((EXPERT_BLOCK * ROW_TILE, LANES), h.dtype), pltpu.SemaphoreType.DMA(()),
                            pltpu.SemaphoreType.DMA(())],
        ),
        out_shape=SDS((n_rows * ROW_TILE, LANES), h.dtype),
        compiler_params=_cparams(("arbitrary",)),
        name="dispatch",
    )(pad_blk * ROW_TILE, _per_step(dest8, tt), h)


def _ffn_body(layer, be_ref, nu_ref, nx_ref, sl_ref, br_ref, xs_ref, wg_hbm, bg_ref, wu_hbm, bu_ref, wd_hbm,
              bd_ref, y_ref, w_f32, w_bf, sem):
    i = pl.program_id(0)
    slot = sl_ref[i]

    def fetch(e, s):
        return [pltpu.make_async_copy(w.at[layer, e], w_f32.at[s, n], sem.at[s])
                for n, w in enumerate((wg_hbm, wu_hbm, wd_hbm))]

    @pl.when(i == 0)
    def _():
        for c in fetch(be_ref[0], slot):
            c.start()

    @pl.when((i == 0) | (be_ref[i] != be_ref[jnp.maximum(i - 1, 0)]))
    def _():
        for c in fetch(be_ref[i], slot):
            c.wait()

        @pl.when(nx_ref[i] >= 0)
        def _():
            for c in fetch(nx_ref[i], 1 - slot):
                c.start()

        for n in range(3):
            w_bf[n] = w_f32[slot, n].astype(BF16)

    def expert_rows(rows):
        x = _load_row_tiles(xs_ref, rows).astype(BF16)
        g = jnp.minimum(_dot(x, w_bf[0]) + bg_ref[...], SWIGLU_LIMIT)
        u = jnp.clip(_dot(x, w_bf[1]) + bu_ref[...], -SWIGLU_LIMIT, SWIGLU_LIMIT)
        a = g * jax.nn.sigmoid(SWIGLU_ALPHA * g) * (u + 1.0)
        _store_row_tiles(y_ref, _dot(a.astype(BF16), w_bf[2]) + bd_ref[...])

    half = EXPERT_BLOCK // 2
    valid = br_ref[i]

    @pl.when(valid > half)
    def _():
        expert_rows(EXPERT_BLOCK)

    @pl.when((valid > 0) & (valid <= half))
    def _():
        expert_rows(half)
        y_ref[half * ROW_TILE:, :] = jnp.zeros((half * ROW_TILE, LANES), y_ref.dtype)

    @pl.when(valid == 0)
    def _():
        y_ref[...] = jnp.zeros(y_ref.shape, y_ref.dtype)


def _ffn(xs, blk_meta, layer, w_gate, b_gate, w_up, b_up, w_down, b_down):
    n_rows = xs.shape[0] // ROW_TILE
    _, e, d, de = w_gate.shape
    assert d == de == D_MODEL
    n_blk = n_rows // EXPERT_BLOCK
    blk = (EXPERT_BLOCK * ROW_TILE, LANES)
    bspec = lambda c: pl.BlockSpec((None, None, 1, c), lambda i, be, *_: (layer, be[i], 0, 0))
    hbm = pl.BlockSpec(memory_space=pl.ANY)
    xs_map = lambda i, be, nu, *_: (jnp.minimum(i, jnp.maximum(nu[0] - 1, 0)), 0)
    return pl.pallas_call(
        functools.partial(_ffn_body, layer),
        grid_spec=pltpu.PrefetchScalarGridSpec(
            num_scalar_prefetch=len(blk_meta),
            grid=(n_blk,),
            in_specs=[
                pl.BlockSpec(blk, xs_map),
                hbm, bspec(de), hbm, bspec(de), hbm, bspec(d),
            ],
            out_specs=pl.BlockSpec(blk, lambda i, *_: (i, 0)),
            scratch_shapes=[pltpu.VMEM((2, 3, d, de), F32), pltpu.VMEM((3, d, de), BF16),
                            pltpu.SemaphoreType.DMA((2,))],
        ),
        out_shape=SDS((n_rows * ROW_TILE, LANES), F32),
        compiler_params=_cparams(("arbitrary",)),
        name="ffn",
    )(*blk_meta, xs, w_gate, b_gate.reshape(-1, e, 1, de), w_up, b_up.reshape(-1, e, 1, de),
      w_down, b_down.reshape(-1, e, 1, d))


COMBINE_RC = 128
COMBINE_PARTS = 8


def _combine_body(dest_ref, x_ref, mod_ref, p_ref, y_ref, o_ref, buf, sem):
    tt = x_ref.shape[0]
    tp = tt // COMBINE_PARTS

    def part_rows(part):
        return pl.ds(part * tp * ROW_TILE, tp * ROW_TILE)

    for part in range(COMBINE_PARTS):
        def issue(t, carry, part=part):
            t8 = pl.multiple_of(t * ROW_TILE, ROW_TILE)
            for k in range(TOP_K):
                d = pl.multiple_of(dest_ref[0, k * tt + t], ROW_TILE)
                pltpu.make_async_copy(y_ref.at[pl.ds(d, ROW_TILE)], buf.at[k, pl.ds(t8, ROW_TILE)],
                                      sem.at[part]).start(priority=k % 2)
            return carry

        lax.fori_loop(part * tp, (part + 1) * tp, issue, 0)

    g2 = mod_ref[0][5:6]
    for part in range(COMBINE_PARTS):
        pltpu.make_async_copy(buf.at[:, part_rows(part)], buf.at[:, part_rows(part)], sem.at[part]).wait()
        for r0 in range(part * tp, (part + 1) * tp, COMBINE_RC):
            rows = slice(r0, r0 + COMBINE_RC)
            p = p_ref[rows, :]
            for j in range(ROW_TILE):
                cols = slice(j * LANES, (j + 1) * LANES)
                tile = lambda k: buf[k, pl.ds(r0 * ROW_TILE + j, COMBINE_RC, stride=ROW_TILE), :]
                acc = p[:, 0:1] * tile(0)
                for k in range(1, TOP_K):
                    acc = acc + p[:, k:k + 1] * tile(k)
                o_ref[rows, cols] = x_ref[rows, cols] + g2[:, cols] * acc


def _combine(x, mod, p, y, dest8, seq):
    n, d = x.shape
    tt = min(COMBINE_TT, seq)
    n_steps = n // tt
    per_b = seq // tt
    return pl.pallas_call(
        _combine_body,
        grid=(n_steps,),
        in_specs=[
            pl.BlockSpec((None, 1, TOP_K * tt), lambda i: (i, 0, 0), memory_space=pltpu.SMEM),
            pl.BlockSpec((tt, d), lambda i: (i, 0)),
            pl.BlockSpec((1, 6, d), lambda i: (i // per_b, 0, 0)),
            pl.BlockSpec((tt, TOP_K), lambda i: (i, 0)),
            pl.BlockSpec(memory_space=pl.ANY),
        ],
        out_specs=pl.BlockSpec((tt, d), lambda i: (i, 0)),
        out_shape=SDS((n, d), F32),
        scratch_shapes=[pltpu.VMEM((TOP_K, tt * ROW_TILE, LANES), F32),
                        pltpu.SemaphoreType.DMA((COMBINE_PARTS,))],
        compiler_params=_cparams(("arbitrary",)),
        name="combine",
    )(_per_step(dest8, tt), x, mod, p, y)


def _moe(x, mod, layer, n2g, w_r, b_r, w_gate, b_gate, w_up, b_up, w_down, b_down):
    bsz, s, d = x.shape
    n = bsz * s
    h2, top_idx, probs, rank, counts = _router(x, mod, n2g, w_r, b_r)
    dest, blk_meta, pad_blk, n_rows = _route_meta(top_idx, rank, counts.reshape(N_EXPERTS))
    dest8 = dest * ROW_TILE
    xs = _dispatch(h2, dest8, pad_blk, n_rows)
    y = _ffn(xs, blk_meta, layer, w_gate, b_gate, w_up, b_up, w_down, b_down)
    out = _combine(x.reshape(n, d), mod, probs.reshape(n, TOP_K), y, dest8, s)
    return out.reshape(bsz, s, d)


def _seg_mats():
    r = lax.broadcasted_iota(jnp.int32, (D_MODEL, N_HEADS), 0) // HEAD_DIM
    c = lax.broadcasted_iota(jnp.int32, (D_MODEL, N_HEADS), 1)
    seg = jnp.where(r == c, 1.0, 0.0).astype(BF16)
    rt = lax.broadcasted_iota(jnp.int32, (N_HEADS, D_MODEL), 0)
    ct = lax.broadcasted_iota(jnp.int32, (N_HEADS, D_MODEL), 1) // HEAD_DIM
    seg_t = jnp.where(rt == ct, 1.0, 0.0).astype(BF16)
    return seg, seg_t


def _split_dot(a, b):
    hi = a.astype(BF16)
    lo = (a - hi.astype(F32)).astype(BF16)
    return _dot(hi, b) + _dot(lo, b)


def _head_rmsnorm(x, g, seg, seg_t):
    ms = _dot((x * x).astype(BF16), seg) * (1.0 / HEAD_DIM)
    r = lax.rsqrt(ms + EPS)
    return x * _split_dot(r, seg_t) * g


def _kvq_body(x_ref, mod_ref, kvmod_ref, n1g_ref, kvg_ref, wq_ref, wkv_ref, qg_ref, kg_ref,
              q_ref, k_ref, vt_ref, km_ref):
    x = x_ref[0]
    mod = mod_ref[0]
    kvmod = kvmod_ref[0]
    ms = jnp.mean(x * x, axis=-1, keepdims=True)
    xn = x * lax.rsqrt(ms + EPS)
    hq = ((xn * n1g_ref[...]) * (1.0 + mod[1:2]) + mod[0:1]).astype(BF16)
    hk = ((xn * kvg_ref[...]) * (1.0 + kvmod[1:2]) + kvmod[0:1]).astype(BF16)
    seg, seg_t = _seg_mats()
    q = _head_rmsnorm(_dot(hq, wq_ref[...]), qg_ref[...], seg, seg_t)
    q_ref[0] = q
    kv = _dot(hk, wkv_ref[...])
    k = _head_rmsnorm(kv[:, :D_MODEL], kg_ref[...], seg, seg_t)
    k_ref[0, 0] = k.astype(BF16)
    for u in range(ATT_U):
        rows = slice(u * MOBA_BLOCK, (u + 1) * MOBA_BLOCK)
        km_ref[0, 0, u:u + 1, :] = jnp.mean(k[rows], axis=0, keepdims=True)
    vt_ref[0, 0] = kv[:, D_MODEL:].T.astype(BF16)


def _kvq(x, mod, kvmod, n1g, kvg, w_q, w_kv, q_g, k_g):
    bsz, s, d = x.shape
    ts = ATT_U * MOBA_BLOCK
    nb = s // ts
    const = lambda *shape: pl.BlockSpec(shape, lambda b, i: (0,) * len(shape))
    return pl.pallas_call(
        _kvq_body,
        grid=(bsz, nb),
        in_specs=[
            pl.BlockSpec((1, ts, d), lambda b, i: (b, i, 0)),
            pl.BlockSpec((1, 6, d), lambda b, i: (b, 0, 0)),
            pl.BlockSpec((1, 2, d), lambda b, i: (b, 0, 0)),
            const(1, d), const(1, d), const(d, d), const(d, 2 * d), const(1, d), const(1, d),
        ],
        out_specs=[
            pl.BlockSpec((1, ts, d), lambda b, i: (b, i, 0)),
            pl.BlockSpec((1, 1, ts, d), lambda b, i: (b, i, 0, 0)),
            pl.BlockSpec((1, 1, d, ts), lambda b, i: (b, i, 0, 0)),
            pl.BlockSpec((1, 1, ATT_U, d), lambda b, i: (b, i, 0, 0)),
        ],
        out_shape=[
            SDS((bsz, s, d), F32),
            SDS((bsz, nb, ts, d), BF16),
            SDS((bsz, nb, d, ts), BF16),
            SDS((bsz, nb, ATT_U, d), F32),
        ],
        compiler_params=_cparams(("parallel", "arbitrary")),
        name="kvq",
    )(x, mod, kvmod, n1g.reshape(1, d), kvg.reshape(1, d), w_q.astype(BF16), w_kv.astype(BF16),
      jnp.tile(q_g, N_HEADS).reshape(1, d), jnp.tile(k_g, N_HEADS).reshape(1, d))


def _bf16_pieces(v):
    hi = v.astype(BF16).astype(F32)
    mid = (v - hi).astype(BF16).astype(F32)
    lo = ((v - hi) - mid).astype(BF16).astype(F32)
    return hi, mid, lo


def _attn_body(slopes_ref, q_ref, k_ref, vt_ref, km_ref, o_ref, cm_scr, kaug_scr, qaug_scr, s_scr):
    hp = pl.program_id(1)
    own = pl.program_id(2)
    bs = MOBA_BLOCK
    nb = k_ref.shape[1] * ATT_U
    pw = 2 * HEAD_DIM
    nh = 2 * ATT_HP

    @pl.when(own == 0)
    def _():
        rel = (lax.broadcasted_iota(jnp.int32, (bs, bs), 1)
               - lax.broadcasted_iota(jnp.int32, (bs, bs), 0))
        cm_scr[0] = jnp.zeros((bs, bs), F32)
        cm_scr[1] = jnp.where(rel >= 0, 0.0, -jnp.inf)
        klane = lax.broadcasted_iota(jnp.int32, (ATT_U * bs, ATT_AUG), 1)
        koff = (lax.broadcasted_iota(jnp.int32, (ATT_U * bs, ATT_AUG), 0) % bs).astype(F32)
        kaug_scr[...] = jnp.where(klane < 3, 1.0, jnp.where(klane < 6, koff, 0.0)).astype(BF16)
        qlane = lax.broadcasted_iota(jnp.int32, (bs, ATT_AUG), 1)
        qoff = lax.broadcasted_iota(jnp.int32, (bs, ATT_AUG), 0).astype(F32)
        for hd in range(nh):
            s2 = jnp.full((bs, ATT_AUG), slopes_ref[nh * hp + hd], F32) * LOG2E
            pieces = _bf16_pieces(-(s2 * qoff)) + _bf16_pieces(s2)
            aug = jnp.zeros((bs, ATT_AUG), F32)
            for lane_ix, piece in enumerate(pieces):
                aug = jnp.where(qlane == lane_ix, piece, aug)
            qaug_scr[hd] = aug.astype(BF16)

    dcol = lax.broadcasted_iota(jnp.int32, (1, pw), 1)
    blk_row = lax.broadcasted_iota(jnp.int32, (nb, 1), 0)
    valid = blk_row < own
    in_head = [(dcol >= h * HEAD_DIM) & (dcol < (h + 1) * HEAD_DIM) for h in range(2)]
    pair_cols = lambda hd: slice((hd // 2) * pw, (hd // 2 + 1) * pw)

    gates = []
    for pair in range(nh // 2):
        qp = q_ref[0, :, pair_cols(2 * pair)]
        kmp = km_ref[0, :, pair_cols(2 * pair)]
        km2 = jnp.concatenate([jnp.where(in_head[h], kmp, 0.0) for h in range(2)], axis=0)
        gates.append(_dot_3pass_nt(km2, qp))

    qs, selbias = [], []
    for hd in range(nh):
        slope = slopes_ref[nh * hp + hd]
        qm = jnp.where(in_head[hd % 2], q_ref[0, :, pair_cols(hd)], 0.0)
        gate = jnp.where(valid, gates[hd // 2][(hd % 2) * nb:(hd % 2 + 1) * nb], -jnp.inf)
        rank = jnp.zeros(gate.shape, F32)
        for i in range(nb):
            gi = gate[i:i + 1, :]
            tie = jnp.where(blk_row > i, 1.0, 0.0)
            rank = rank + jnp.where(gi > gate, 1.0, jnp.where(gi == gate, tie, 0.0))
        chosen = jnp.where(valid, jnp.where(rank < MOBA_TOPK, 1.0, 0.0),
                           jnp.where(blk_row == own, 1.0, 0.0))
        off = (own - blk_row).astype(F32) * (bs * LOG2E * slope)
        selbias.append(jnp.where(chosen > 0.5, -off, -jnp.inf))
        qs.append(jnp.concatenate([(qm * (LOG2E * HEAD_DIM ** -0.5)).astype(BF16), qaug_scr[hd]], axis=1))

    last = k_ref.shape[1] - 1
    ones = jnp.ones((ATT_ONES, ATT_U * bs), BF16)

    def qk(it, hd):
        kc = jnp.concatenate([k_ref[0, it, :, pair_cols(hd)], kaug_scr[...]], axis=1)
        return lax.dot_general(kc, qs[hd], NT_DIMS, preferred_element_type=F32)

    def visit(it, slot, carry, next_it, has_own):
        out = []
        if next_it is not None:
            for hd in range(nh):
                s_scr[1 - slot, hd] = qk(jnp.minimum(next_it, last), hd)
        for hd in range(nh):
            m, acc = carry[hd]

            def tile(u):
                t = s_scr[slot, hd, u * bs:(u + 1) * bs]
                if has_own:
                    t = t + cm_scr[(it * ATT_U + u == own).astype(jnp.int32)]
                return t

            rows = [jnp.sum(jnp.where(blk_row == it * ATT_U + u, selbias[hd], 0.0), axis=0, keepdims=True)
                    for u in range(ATT_U)]
            mx = m
            for u in range(ATT_U):
                mx = jnp.maximum(mx, jnp.max(tile(u), axis=0, keepdims=True) + rows[u])
            m_safe = jnp.where(mx == -jnp.inf, 0.0, mx)
            alpha = jnp.exp2(m - m_safe)
            p = jnp.concatenate([jnp.exp2((tile(u) + (rows[u] - m_safe)).astype(BF16)) for u in range(ATT_U)],
                                axis=0)
            vj = jnp.concatenate([vt_ref[0, it, hd * HEAD_DIM:(hd + 1) * HEAD_DIM, :], ones], axis=0)
            acc = alpha * acc + _dot(vj, p)
            out.append((mx, acc))
        return tuple(out)

    def body(i2, carry):
        carry = visit(2 * i2, 1, carry, 2 * i2 + 1, False)
        return visit(2 * i2 + 1, 0, carry, 2 * i2 + 2, False)

    n_past = own // ATT_U
    for hd in range(nh):
        s_scr[0, hd] = qk(n_past, hd)
    init = tuple((jnp.full((1, bs), -jnp.inf, F32), jnp.zeros((HEAD_DIM + ATT_ONES, bs), F32))
                 for hd in range(nh))
    carry = visit(n_past, 0, init, 0, True)
    carry = lax.fori_loop(0, n_past // 2, body, carry)
    carry = lax.cond(n_past % 2 == 1, lambda c: visit(n_past - 1, 1, c, None, False), lambda c: c, carry)
    outs = [acc[:HEAD_DIM] / acc[HEAD_DIM:HEAD_DIM + 1] for (_, acc) in carry]
    o_ref[0] = jnp.concatenate(outs, axis=0).astype(o_ref.dtype)


def _attn(q, kb, vtb, kmean):
    bsz, s, d = q.shape
    nsb = s // (ATT_U * MOBA_BLOCK)
    assert nsb % 2 == 0, "the attention loop visits key super-blocks in pairs"
    nb = s // MOBA_BLOCK
    w = 2 * HEAD_DIM * ATT_HP
    slopes = (2.0 ** (-8.0 * jnp.arange(1, N_HEADS + 1, dtype=F32) / N_HEADS)).astype(F32)
    return pl.pallas_call(
        _attn_body,
        grid_spec=pltpu.PrefetchScalarGridSpec(
            num_scalar_prefetch=1,
            grid=(bsz, N_HEADS // (2 * ATT_HP), nb),
            in_specs=[
                pl.BlockSpec((1, MOBA_BLOCK, w), lambda b, h, c, sl: (b, c, h)),
                pl.BlockSpec((1, nsb, ATT_U * MOBA_BLOCK, w), lambda b, h, c, sl: (b, 0, 0, h)),
                pl.BlockSpec((1, nsb, w, ATT_U * MOBA_BLOCK), lambda b, h, c, sl: (b, 0, h, 0)),
                pl.BlockSpec((1, nb, w), lambda b, h, c, sl: (b, 0, h)),
            ],
            out_specs=pl.BlockSpec((1, w, MOBA_BLOCK), lambda b, h, c, sl: (b, h, c)),
            scratch_shapes=[pltpu.VMEM((2, MOBA_BLOCK, MOBA_BLOCK), F32),
                            pltpu.VMEM((ATT_U * MOBA_BLOCK, ATT_AUG), BF16),
                            pltpu.VMEM((2 * ATT_HP, MOBA_BLOCK, ATT_AUG), BF16),
                            pltpu.VMEM((2, 2 * ATT_HP, ATT_U * MOBA_BLOCK, MOBA_BLOCK), F32)],
        ),
        out_shape=SDS((bsz, d, s), BF16),
        compiler_params=_cparams(("parallel", "parallel", "arbitrary")),
        name="attn",
    )(slopes, q, kb, vtb, kmean)


OPROJ_TS = 1024


def _oproj_body(at_ref, x_ref, mod_ref, wo_ref, o_ref):
    g1 = mod_ref[0][2:3]
    m = lax.dot_general(at_ref[0], wo_ref[...], (((0,), (0,)), ((), ())), preferred_element_type=F32)
    o_ref[0] = x_ref[0] + g1 * m


def _oproj(at, x, mod, w_o):
    bsz, s, d = x.shape
    ts = min(OPROJ_TS, s)
    tok = pl.BlockSpec((1, ts, d), lambda b, i: (b, i, 0))
    return pl.pallas_call(
        _oproj_body,
        grid=(bsz, s // ts),
        in_specs=[pl.BlockSpec((1, d, ts), lambda b, i: (b, 0, i)), tok,
                  pl.BlockSpec((1, 6, d), lambda b, i: (b, 0, 0)),
                  pl.BlockSpec((d, d), lambda b, i: (0, 0))],
        out_specs=tok,
        out_shape=SDS((bsz, s, d), F32),
        compiler_params=_cparams(("parallel", "arbitrary")),
        name="oproj",
    )(at, x, mod, w_o.astype(BF16))


def kernel(x, c, ada_w, ada_b, norm1_g, norm2_g, sgu_w_in, sgu_b_in, sgu_v_g, sgu_w_s, sgu_b_s, sgu_w_out,
           kv_ada_w, kv_ada_b, kv_norm_g, w_kv, k_norm_g, attn_w_q, q_norm_g, attn_w_o,
           moe_w_router, moe_b_router, moe_w_gate, moe_b_gate, moe_w_up, moe_b_up, moe_w_down, moe_b_down):
    bsz, s, d = x.shape
    moe = lambda l, xx, mod: _moe(xx, mod, l, norm2_g[l], moe_w_router[l], moe_b_router[l],
                                  moe_w_gate, moe_b_gate, moe_w_up, moe_b_up, moe_w_down, moe_b_down)
    mod0 = _ada(c, ada_w, ada_b[0], 0).reshape(bsz, 6, d)
    x = _sgu(x, mod0, norm1_g[0], sgu_w_in[0], sgu_b_in[0], sgu_v_g[0], sgu_w_s[0], sgu_b_s[0], sgu_w_out[0])
    x = moe(0, x, mod0)
    mod1 = _ada(c, ada_w, ada_b[1], 1).reshape(bsz, 6, d)
    kvmod = _ada(c, kv_ada_w, kv_ada_b).reshape(bsz, 2, d)
    q, kb, vtb, kmean = _kvq(x, mod1, kvmod, norm1_g[1], kv_norm_g, attn_w_q[0], w_kv, q_norm_g[0], k_norm_g)
    a = _attn(q, kb, vtb, kmean.reshape(bsz, s // MOBA_BLOCK, d))
    x = _oproj(a, x, mod1, attn_w_o[0])
    x = moe(1, x, mod1)
    return x
```

```python
import functools
import math

import jax
import jax.numpy as jnp
import numpy as np
from jax import lax
from jax.experimental import pallas as pl
from jax.experimental.pallas import tpu as pltpu

F32 = jnp.float32
BF16 = jnp.bfloat16
HIGHEST = lax.Precision.HIGHEST
SDS = jax.ShapeDtypeStruct

D_MODEL = 1024
SGU_CHUNK = 128
SGU_GROUPS = 8
D_SGU = 3 * D_MODEL
SGU_GROUP_DIM = D_SGU // SGU_GROUPS
HEAD_DIM = 64
N_HEADS = D_MODEL // HEAD_DIM
MOBA_BLOCK = 256
MOBA_TOPK = 3
ATT_U = 2
ATT_ONES = 16
ATT_AUG = 128
ATT_HP = 4
LOG2E = math.log2(math.e)
N_EXPERTS = 32
TOP_K = 4
SWIGLU_LIMIT = 7.0
SWIGLU_ALPHA = 1.702
EXPERT_BLOCK = 512
EPS = 1e-6

VMEM_LIMIT_BYTES = 56 * 1024 * 1024
NT_DIMS = (((1,), (1,)), ((), ()))


def _cparams(sem):
    return pltpu.CompilerParams(dimension_semantics=sem, vmem_limit_bytes=VMEM_LIMIT_BYTES)


def _dot(a, b, **kw):
    return jnp.dot(a, b, preferred_element_type=F32, **kw)


LANES = 128
ROW_TILE = D_MODEL // LANES


def _store_row_tiles(ref, val):
    rows = val.shape[0]
    for j in range(ROW_TILE):
        ref[pl.ds(j, rows, stride=ROW_TILE), :] = val[:, j * LANES:(j + 1) * LANES]


def _load_row_tiles(ref, rows):
    return jnp.concatenate([ref[pl.ds(j, rows, stride=ROW_TILE), :] for j in range(ROW_TILE)], axis=-1)


def _dot_3pass_nt(a, b):
    a_hi = a.astype(BF16)
    a_lo = (a - a_hi.astype(F32)).astype(BF16)
    b_hi = b.astype(BF16)
    b_lo = (b - b_hi.astype(F32)).astype(BF16)
    nt = lambda u, v: lax.dot_general(u, v, NT_DIMS, preferred_element_type=F32)
    return nt(a_hi, b_hi) + (nt(a_lo, b_hi) + nt(a_hi, b_lo))


def _norm_mod(x, g, sc, sh):
    ms = jnp.mean(x * x, axis=-1, keepdims=True)
    return (x * lax.rsqrt(ms + EPS) * g) * (1.0 + sc) + sh


def _ada_body(c_ref, w_ref, b_ref, o_ref):
    c = c_ref[...]
    ca = c * jax.nn.sigmoid(c)
    o_ref[...] = _dot(ca, w_ref[...], precision=HIGHEST) + b_ref[...]


def _ada(c, w, b, layer=0):
    bsz, d = c.shape
    n = w.shape[-1]
    tn = 1024
    w = w.reshape(-1, d, n)
    return pl.pallas_call(
        _ada_body,
        grid=(n // tn,),
        in_specs=[
            pl.BlockSpec((bsz, d), lambda j: (0, 0)),
            pl.BlockSpec((None, d, tn), lambda j: (layer, 0, j)),
            pl.BlockSpec((1, tn), lambda j: (0, j)),
        ],
        out_specs=pl.BlockSpec((bsz, tn), lambda j: (0, j)),
        out_shape=SDS((bsz, n), F32),
        compiler_params=_cparams(("arbitrary",)),
        name="ada",
    )(c, w, b.reshape(1, n))


SGU_TS = 512
SGU_CB = 768


def _gelu_tanh(x):
    c = np.float32(np.sqrt(2.0 / np.pi))
    return x * (0.5 * (1.0 + jnp.tanh(c * (x + 0.044715 * (x * x * x)))))


def _sgu_body(x_ref, mod_ref, n1g_ref, win_ref, bin_ref, vg_ref, ws_ref, bst_ref, wout_ref,
              o_ref, u_scr, v_scr, y_scr):
    ts = x_ref.shape[1]
    x = x_ref[0]
    mod = mod_ref[0]
    sh1, sc1, g1 = mod[0:1], mod[1:2], mod[2:3]
    h = _norm_mod(x, n1g_ref[...], sc1, sh1).astype(BF16)

    ssq = jnp.zeros((ts, 1), F32)
    for j in range(D_SGU // SGU_CB):
        cs = slice(j * SGU_CB, (j + 1) * SGU_CB)
        ws_cols = slice(D_SGU + j * SGU_CB, D_SGU + (j + 1) * SGU_CB)
        z = _gelu_tanh(_dot(h, win_ref[:, ws_cols]) + bin_ref[:, ws_cols])
        v_scr[:, cs] = z
        ssq = ssq + jnp.sum(z * z, axis=-1, keepdims=True)
    rs = lax.rsqrt(ssq / D_SGU + EPS)

    r_i = lax.broadcasted_iota(jnp.int32, (SGU_CHUNK, SGU_CHUNK), 0)
    c_i = lax.broadcasted_iota(jnp.int32, (SGU_CHUNK, SGU_CHUNK), 1)
    causal = c_i <= r_i
    m = jnp.zeros((ts, x.shape[1]), F32)
    for j in range(D_SGU // SGU_CB):
        cb = slice(j * SGU_CB, (j + 1) * SGU_CB)
        u_scr[:, cb] = _gelu_tanh(_dot(h, win_ref[:, cb]) + bin_ref[:, cb])
        for g in range(j * SGU_CB // SGU_GROUP_DIM, (j + 1) * SGU_CB // SGU_GROUP_DIM):
            cs = slice(g * SGU_GROUP_DIM, (g + 1) * SGU_GROUP_DIM)
            wsg = jnp.where(causal, ws_ref[g], 0.0).astype(BF16)
            bsg = bst_ref[:, g:g + 1]
            for n in range(ts // SGU_CHUNK):
                rows = slice(n * SGU_CHUNK, (n + 1) * SGU_CHUNK)
                vn = (v_scr[rows, cs] * rs[rows] * vg_ref[:, cs]).astype(BF16)
                mixed = _dot(wsg, vn) + bsg
                y_scr[rows, cs] = (u_scr[rows, cs] * mixed).astype(BF16)
        m = m + _dot(y_scr[:, cb], wout_ref[cb, :])
    o_ref[0] = x + g1 * m


def _sgu(x, mod, n1g, w_in, b_in, v_g, w_s, b_s, w_out):
    bsz, s, d = x.shape
    ts = SGU_TS
    const = lambda *shape: pl.BlockSpec(shape, lambda b, i: (0,) * len(shape),
                                        pipeline_mode=pl.Buffered(1))
    return pl.pallas_call(
        _sgu_body,
        grid=(bsz, s // ts),
        in_specs=[
            pl.BlockSpec((1, ts, d), lambda b, i: (b, i, 0)),
            pl.BlockSpec((1, 6, d), lambda b, i: (b, 0, 0)),
            const(1, d),
            const(d, 2 * D_SGU),
            const(1, 2 * D_SGU),
            const(1, D_SGU),
            const(SGU_GROUPS, SGU_CHUNK, SGU_CHUNK),
            const(SGU_CHUNK, SGU_GROUPS),
            const(D_SGU, d),
        ],
        out_specs=pl.BlockSpec((1, ts, d), lambda b, i: (b, i, 0)),
        out_shape=SDS((bsz, s, d), F32),
        scratch_shapes=[
            pltpu.VMEM((ts, D_SGU), F32),
            pltpu.VMEM((ts, D_SGU), F32),
            pltpu.VMEM((ts, D_SGU), BF16),
        ],
        compiler_params=_cparams(("parallel", "arbitrary")),
        name="sgu",
    )(x, mod, n1g.reshape(1, d), w_in.astype(BF16), b_in.reshape(1, -1), v_g.reshape(1, -1),
      w_s, b_s.T, w_out.astype(BF16))


ROUTER_TS = 512


def _router_body(x_ref, mod_ref, n2g_ref, wrt_ref, brc_ref, h_ref, idx_ref, p_ref, rk_ref, cnt_ref, run_scr):
    @pl.when((pl.program_id(0) == 0) & (pl.program_id(1) == 0))
    def _():
        run_scr[...] = jnp.zeros(run_scr.shape, run_scr.dtype)

    x = x_ref[0]
    mod = mod_ref[0]
    sh2, sc2 = mod[3:4], mod[4:5]
    h2 = _norm_mod(x, n2g_ref[...], sc2, sh2)
    _store_row_tiles(h_ref, h2)
    ts = x.shape[0]
    logits = _dot_3pass_nt(wrt_ref[...], h2) + brc_ref[...]
    row = lax.broadcasted_iota(jnp.int32, logits.shape, 0)
    work = logits
    vals, idxs = [], []
    for _ in range(TOP_K):
        mx = jnp.max(work, axis=0, keepdims=True)
        ix = jnp.min(jnp.where(work == mx, row, N_EXPERTS), axis=0, keepdims=True)
        vals.append(mx)
        idxs.append(ix)
        work = jnp.where(row == ix, -jnp.inf, work)
    e = jnp.exp(jnp.concatenate(vals, axis=0) - vals[0])
    p_t = e / jnp.sum(e, axis=0, keepdims=True)
    idx_ref[0] = jnp.concatenate(idxs, axis=0)

    onehots = [row == ix for ix in idxs]
    mask = jnp.zeros(logits.shape, F32)
    for oh in onehots:
        mask = mask + jnp.where(oh, 1.0, 0.0)
    r_i = lax.broadcasted_iota(jnp.int32, (ts, ts), 0)
    c_i = lax.broadcasted_iota(jnp.int32, (ts, ts), 1)
    before = _dot(mask.astype(BF16), jnp.where(r_i < c_i, 1.0, 0.0).astype(BF16)) + run_scr[...]
    rk = [jnp.sum(jnp.where(oh, before, 0.0), axis=0, keepdims=True) for oh in onehots]
    rk_ref[0] = jnp.concatenate(rk, axis=0).astype(jnp.int32)
    run_scr[...] = run_scr[...] + jnp.sum(mask, axis=1, keepdims=True)
    cnt_ref[...] = run_scr[...].astype(jnp.int32)

    eye = jnp.where(r_i == c_i, 1.0, 0.0).astype(BF16)
    p_ref[0] = sum(lax.dot_general(eye, piece.astype(BF16), NT_DIMS, preferred_element_type=F32)
                   for piece in _bf16_pieces(p_t))


def _router(x, mod, n2g, w_r, b_r):
    bsz, s, d = x.shape
    ts = min(ROUTER_TS, s)
    kt = pl.BlockSpec((1, TOP_K, ts), lambda b, i: (b, 0, i))
    return pl.pallas_call(
        _router_body,
        grid=(bsz, s // ts),
        in_specs=[
            pl.BlockSpec((1, ts, d), lambda b, i: (b, i, 0)),
            pl.BlockSpec((1, 6, d), lambda b, i: (b, 0, 0)),
            pl.BlockSpec((1, d), lambda b, i: (0, 0)),
            pl.BlockSpec((N_EXPERTS, d), lambda b, i: (0, 0)),
            pl.BlockSpec((N_EXPERTS, 1), lambda b, i: (0, 0)),
        ],
        out_specs=[pl.BlockSpec((ts * ROW_TILE, LANES), lambda b, i: (b * (s // ts) + i, 0)),
                   kt, pl.BlockSpec((1, ts, TOP_K), lambda b, i: (b, i, 0)), kt,
                   pl.BlockSpec((N_EXPERTS, 1), lambda b, i: (0, 0))],
        out_shape=[SDS((bsz * s * ROW_TILE, LANES), F32), SDS((bsz, TOP_K, s), jnp.int32),
                   SDS((bsz, s, TOP_K), F32),
                   SDS((bsz, TOP_K, s), jnp.int32), SDS((N_EXPERTS, 1), jnp.int32)],
        scratch_shapes=[pltpu.VMEM((N_EXPERTS, 1), F32)],
        compiler_params=_cparams(("arbitrary", "arbitrary")),
        name="router",
    )(x, mod, n2g.reshape(1, d), w_r.T, b_r.reshape(-1, 1))


def _route_meta(top_idx, rank, counts):
    n = top_idx.size // TOP_K
    padded = (counts + EXPERT_BLOCK - 1) // EXPERT_BLOCK * EXPERT_BLOCK
    pend = jnp.cumsum(padded)
    pstart = pend - padded
    first_row = jnp.sum(jnp.where(top_idx[..., None] == jnp.arange(N_EXPERTS), pstart, 0), axis=-1)
    dest = (first_row + rank).astype(jnp.int32)
    n_rows = n * TOP_K + N_EXPERTS * EXPERT_BLOCK
    n_blk = n_rows // EXPERT_BLOCK
    blk_row0 = jnp.arange(n_blk) * EXPERT_BLOCK
    blk_exp = jnp.minimum(jnp.sum(pend[None, :] <= blk_row0[:, None], axis=1), N_EXPERTS - 1)
    n_used = (pend[-1] // EXPERT_BLOCK).astype(jnp.int32).reshape(1)
    eids = jnp.arange(N_EXPERTS)
    has_rows = counts > 0
    last_exp = jnp.max(jnp.where(has_rows, eids, 0))
    blk_exp = jnp.where(blk_row0 < pend[-1], blk_exp, last_exp).astype(jnp.int32)
    nxt_e = jnp.min(jnp.where((eids[None, :] > eids[:, None]) & has_rows[None, :], eids[None, :], N_EXPERTS), axis=1)
    nxt_e = jnp.where(nxt_e < N_EXPERTS, nxt_e, -1)
    slot_e = (jnp.cumsum(has_rows.astype(jnp.int32)) - 1) % 2
    of_blk = lambda tab: jnp.sum(jnp.where(blk_exp[:, None] == eids[None, :], tab[None, :], 0), axis=1)
    blk_nxt = of_blk(nxt_e).astype(jnp.int32)
    blk_slot = of_blk(slot_e).astype(jnp.int32)
    blk_valid = jnp.clip(of_blk(pstart + counts) - blk_row0, 0, EXPERT_BLOCK)
    blk_valid = jnp.where(blk_row0 < pend[-1], blk_valid, 0).astype(jnp.int32)
    pad_blk = jnp.where(padded > 0, pend - EXPERT_BLOCK, -1)
    tail = pend[-1] + jnp.arange(N_EXPERTS) * EXPERT_BLOCK
    tail_blk = jnp.where(tail < n_rows, tail, -1)
    pad_blk = jnp.concatenate([pad_blk, tail_blk]).astype(jnp.int32)
    return dest, (blk_exp, n_used, blk_nxt, blk_slot, blk_valid), pad_blk, n_rows


MOVE_TT = 2048
COMBINE_TT = 1024


def _per_step(dest, tt):
    bsz, k, s = dest.shape
    return dest.reshape(bsz, k, s // tt, tt).transpose(0, 2, 1, 3).reshape(bsz * (s // tt), 1, k * tt)


def _dispatch_body(pad_ref, dest_ref, h_ref, xs_ref, zero_scr, sem, zsem):
    tt = h_ref.shape[0] // ROW_TILE
    blk = EXPERT_BLOCK * ROW_TILE

    @pl.when(pl.program_id(0) == 0)
    def _():
        zero_scr[...] = jnp.zeros(zero_scr.shape, zero_scr.dtype)

        def zero_copy(e):
            row0 = pl.multiple_of(pad_ref[e], blk)
            return pltpu.make_async_copy(zero_scr, xs_ref.at[pl.ds(row0, blk)], zsem)

        for e in range(2 * N_EXPERTS):
            pl.when(pad_ref[e] >= 0)(lambda e=e: zero_copy(e).start())
        for e in range(2 * N_EXPERTS):
            pl.when(pad_ref[e] >= 0)(lambda e=e: zero_copy(e).wait())

    def issue(t, carry):
        src = h_ref.at[pl.ds(pl.multiple_of(t * ROW_TILE, ROW_TILE), ROW_TILE)]
        for k in range(TOP_K):
            d = pl.multiple_of(dest_ref[0, k * tt + t], ROW_TILE)
            pltpu.make_async_copy(src, xs_ref.at[pl.ds(d, ROW_TILE)], sem).start(priority=k % 2)
        return carry

    lax.fori_loop(0, tt, issue, 0)
    for k in range(TOP_K):
        pltpu.make_async_copy(h_ref, h_ref, sem).wait()


def _dispatch(h, dest8, pad_blk, n_rows):
    n = h.shape[0] // ROW_TILE
    tt = min(MOVE_TT, dest8.shape[-1])
    return pl.pallas_call(
        _dispatch_body,
        grid_spec=pltpu.PrefetchScalarGridSpec(
            num_scalar_prefetch=1,
            grid=(n // tt,),
            in_specs=[
                pl.BlockSpec((None, 1, TOP_K * tt), lambda i, pad: (i, 0, 0), memory_space=pltpu.SMEM),
                pl.BlockSpec((tt * ROW_TILE, LANES), lambda i, pad: (i, 0)),
            ],
            out_specs=pl.BlockSpec(memory_space=pl.ANY),
            scratch_shapes=[pltpu.VMEM((EXPERT_BLOCK * ROW_TILE, LANES), h.dtype), pltpu.SemaphoreType.DMA(()),
                            pltpu.SemaphoreType.DMA(())],
        ),
        out_shape=SDS((n_rows * ROW_TILE, LANES), h.dtype),
        compiler_params=_cparams(("arbitrary",)),
        name="dispatch",
    )(pad_blk * ROW_TILE, _per_step(dest8, tt), h)


def _ffn_body(layer, be_ref, nu_ref, nx_ref, sl_ref, br_ref, xs_ref, wg_hbm, bg_ref, wu_hbm, bu_ref, wd_hbm,
              bd_ref, y_ref, w_f32, w_bf, sem):
    i = pl.program_id(0)
    slot = sl_ref[i]

    def fetch(e, s):
        return [pltpu.make_async_copy(w.at[layer, e], w_f32.at[s, n], sem.at[s])
                for n, w in enumerate((wg_hbm, wu_hbm, wd_hbm))]

    @pl.when(i == 0)
    def _():
        for c in fetch(be_ref[0], slot):
            c.start()

    @pl.when((i == 0) | (be_ref[i] != be_ref[jnp.maximum(i - 1, 0)]))
    def _():
        for c in fetch(be_ref[i], slot):
            c.wait()

        @pl.when(nx_ref[i] >= 0)
        def _():
            for c in fetch(nx_ref[i], 1 - slot):
                c.start()

        for n in range(3):
            w_bf[n] = w_f32[slot, n].astype(BF16)

    def expert_rows(rows):
        x = _load_row_tiles(xs_ref, rows).astype(BF16)
        g = jnp.minimum(_dot(x, w_bf[0]) + bg_ref[...], SWIGLU_LIMIT)
        u = jnp.clip(_dot(x, w_bf[1]) + bu_ref[...], -SWIGLU_LIMIT, SWIGLU_LIMIT)
        a = g * jax.nn.sigmoid(SWIGLU_ALPHA * g) * (u + 1.0)
        _store_row_tiles(y_ref, _dot(a.astype(BF16), w_bf[2]) + bd_ref[...])

    half = EXPERT_BLOCK // 2
    valid = br_ref[i]

    @pl.when(valid > half)
    def _():
        expert_rows(EXPERT_BLOCK)

    @pl.when((valid > 0) & (valid <= half))
    def _():
        expert_rows(half)
        y_ref[half * ROW_TILE:, :] = jnp.zeros((half * ROW_TILE, LANES), y_ref.dtype)

    @pl.when(valid == 0)
    def _():
        y_ref[...] = jnp.zeros(y_ref.shape, y_ref.dtype)


def _ffn(xs, blk_meta, layer, w_gate, b_gate, w_up, b_up, w_down, b_down):
    n_rows = xs.shape[0] // ROW_TILE
    _, e, d, de = w_gate.shape
    assert d == de == D_MODEL
    n_blk = n_rows // EXPERT_BLOCK
    blk = (EXPERT_BLOCK * ROW_TILE, LANES)
    bspec = lambda c: pl.BlockSpec((None, None, 1, c), lambda i, be, *_: (layer, be[i], 0, 0))
    hbm = pl.BlockSpec(memory_space=pl.ANY)
    xs_map = lambda i, be, nu, *_: (jnp.minimum(i, jnp.maximum(nu[0] - 1, 0)), 0)
    return pl.pallas_call(
        functools.partial(_ffn_body, layer),
        grid_spec=pltpu.PrefetchScalarGridSpec(
            num_scalar_prefetch=len(blk_meta),
            grid=(n_blk,),
            in_specs=[
                pl.BlockSpec(blk, xs_map),
                hbm, bspec(de), hbm, bspec(de), hbm, bspec(d),
            ],
            out_specs=pl.BlockSpec(blk, lambda i, *_: (i, 0)),
            scratch_shapes=[pltpu.VMEM((2, 3, d, de), F32), pltpu.VMEM((3, d, de), BF16),
                            pltpu.SemaphoreType.DMA((2,))],
        ),
        out_shape=SDS((n_rows * ROW_TILE, LANES), F32),
        compiler_params=_cparams(("arbitrary",)),
        name="ffn",
    )(*blk_meta, xs, w_gate, b_gate.reshape(-1, e, 1, de), w_up, b_up.reshape(-1, e, 1, de),
      w_down, b_down.reshape(-1, e, 1, d))


COMBINE_RC = 128
COMBINE_PARTS = 8


def _combine_body(dest_ref, x_ref, mod_ref, p_ref, y_ref, o_ref, buf, sem):
    tt = x_ref.shape[0]
    tp = tt // COMBINE_PARTS

    def part_rows(part):
        return pl.ds(part * tp * ROW_TILE, tp * ROW_TILE)

    for part in range(COMBINE_PARTS):
        def issue(t, carry, part=part):
            t8 = pl.multiple_of(t * ROW_TILE, ROW_TILE)
            for k in range(TOP_K):
                d = pl.multiple_of(dest_ref[0, k * tt + t], ROW_TILE)
                pltpu.make_async_copy(y_ref.at[pl.ds(d, ROW_TILE)], buf.at[k, pl.ds(t8, ROW_TILE)],
                                      sem.at[part]).start(priority=k % 2)
            return carry

        lax.fori_loop(part * tp, (part + 1) * tp, issue, 0)

    g2 = mod_ref[0][5:6]
    for part in range(COMBINE_PARTS):
        pltpu.make_async_copy(buf.at[:, part_rows(part)], buf.at[:, part_rows(part)], sem.at[part]).wait()
        for r0 in range(part * tp, (part + 1) * tp, COMBINE_RC):
            rows = slice(r0, r0 + COMBINE_RC)
            p = p_ref[rows, :]
            for j in range(ROW_TILE):
                cols = slice(j * LANES, (j + 1) * LANES)
                tile = lambda k: buf[k, pl.ds(r0 * ROW_TILE + j, COMBINE_RC, stride=ROW_TILE), :]
                acc = p[:, 0:1] * tile(0)
                for k in range(1, TOP_K):
                    acc = acc + p[:, k:k + 1] * tile(k)
                o_ref[rows, cols] = x_ref[rows, cols] + g2[:, cols] * acc


def _combine(x, mod, p, y, dest8, seq):
    n, d = x.shape
    tt = min(COMBINE_TT, seq)
    n_steps = n // tt
    per_b = seq // tt
    return pl.pallas_call(
        _combine_body,
        grid=(n_steps,),
        in_specs=[
            pl.BlockSpec((None, 1, TOP_K * tt), lambda i: (i, 0, 0), memory_space=pltpu.SMEM),
            pl.BlockSpec((tt, d), lambda i: (i, 0)),
            pl.BlockSpec((1, 6, d), lambda i: (i // per_b, 0, 0)),
            pl.BlockSpec((tt, TOP_K), lambda i: (i, 0)),
            pl.BlockSpec(memory_space=pl.ANY),
        ],
        out_specs=pl.BlockSpec((tt, d), lambda i: (i, 0)),
        out_shape=SDS((n, d), F32),
        scratch_shapes=[pltpu.VMEM((TOP_K, tt * ROW_TILE, LANES), F32),
                        pltpu.SemaphoreType.DMA((COMBINE_PARTS,))],
        compiler_params=_cparams(("arbitrary",)),
        name="combine",
    )(_per_step(dest8, tt), x, mod, p, y)


def _moe(x, mod, layer, n2g, w_r, b_r, w_gate, b_gate, w_up, b_up, w_down, b_down):
    bsz, s, d = x.shape
    n = bsz * s
    h2, top_idx, probs, rank, counts = _router(x, mod, n2g, w_r, b_r)
    dest, blk_meta, pad_blk, n_rows = _route_meta(top_idx, rank, counts.reshape(N_EXPERTS))
    dest8 = dest * ROW_TILE
    xs = _dispatch(h2, dest8, pad_blk, n_rows)
    y = _ffn(xs, blk_meta, layer, w_gate, b_gate, w_up, b_up, w_down, b_down)
    out = _combine(x.reshape(n, d), mod, probs.reshape(n, TOP_K), y, dest8, s)
    return out.reshape(bsz, s, d)


def _seg_mats():
    r = lax.broadcasted_iota(jnp.int32, (D_MODEL, N_HEADS), 0) // HEAD_DIM
    c = lax.broadcasted_iota(jnp.int32, (D_MODEL, N_HEADS), 1)
    seg = jnp.where(r == c, 1.0, 0.0).astype(BF16)
    rt = lax.broadcasted_iota(jnp.int32, (N_HEADS, D_MODEL), 0)
    ct = lax.broadcasted_iota(jnp.int32, (N_HEADS, D_MODEL), 1) // HEAD_DIM
    seg_t = jnp.where(rt == ct, 1.0, 0.0).astype(BF16)
    return seg, seg_t


def _split_dot(a, b):
    hi = a.astype(BF16)
    lo = (a - hi.astype(F32)).astype(BF16)
    return _dot(jnp.concatenate([hi, lo], axis=1), jnp.concatenate([b, b], axis=0))


def _head_rmsnorm(x, g, seg, seg_t):
    ms = _dot((x * x).astype(BF16), seg) * (1.0 / HEAD_DIM)
    r = lax.rsqrt(ms + EPS)
    return x * _split_dot(r, seg_t) * g


def _kvq_body(x_ref, mod_ref, kvmod_ref, n1g_ref, kvg_ref, wq_ref, wkv_ref, qg_ref, kg_ref,
              q_ref, k_ref, vt_ref, km_ref):
    x = x_ref[0]
    mod = mod_ref[0]
    kvmod = kvmod_ref[0]
    ms = jnp.mean(x * x, axis=-1, keepdims=True)
    xn = x * lax.rsqrt(ms + EPS)
    hq = ((xn * n1g_ref[...]) * (1.0 + mod[1:2]) + mod[0:1]).astype(BF16)
    hk = ((xn * kvg_ref[...]) * (1.0 + kvmod[1:2]) + kvmod[0:1]).astype(BF16)
    seg, seg_t = _seg_mats()
    q = _head_rmsnorm(_dot(hq, wq_ref[...]), qg_ref[...], seg, seg_t)
    q_ref[0] = q
    kv = _dot(hk, wkv_ref[...])
    k = _head_rmsnorm(kv[:, :D_MODEL], kg_ref[...], seg, seg_t)
    k_ref[0, 0] = k.astype(BF16)
    for u in range(ATT_U):
        rows = slice(u * MOBA_BLOCK, (u + 1) * MOBA_BLOCK)
        km_ref[0, 0, u:u + 1, :] = jnp.mean(k[rows], axis=0, keepdims=True)
    vt_ref[0, 0] = kv[:, D_MODEL:].T.astype(BF16)


def _kvq(x, mod, kvmod, n1g, kvg, w_q, w_kv, q_g, k_g):
    bsz, s, d = x.shape
    ts = ATT_U * MOBA_BLOCK
    nb = s // ts
    const = lambda *shape: pl.BlockSpec(shape, lambda b, i: (0,) * len(shape))
    return pl.pallas_call(
        _kvq_body,
        grid=(bsz, nb),
        in_specs=[
            pl.BlockSpec((1, ts, d), lambda b, i: (b, i, 0)),
            pl.BlockSpec((1, 6, d), lambda b, i: (b, 0, 0)),
            pl.BlockSpec((1, 2, d), lambda b, i: (b, 0, 0)),
            const(1, d), const(1, d), const(d, d), const(d, 2 * d), const(1, d), const(1, d),
        ],
        out_specs=[
            pl.BlockSpec((1, ts, d), lambda b, i: (b, i, 0)),
            pl.BlockSpec((1, 1, ts, d), lambda b, i: (b, i, 0, 0)),
            pl.BlockSpec((1, 1, d, ts), lambda b, i: (b, i, 0, 0)),
            pl.BlockSpec((1, 1, ATT_U, d), lambda b, i: (b, i, 0, 0)),
        ],
        out_shape=[
            SDS((bsz, s, d), F32),
            SDS((bsz, nb, ts, d), BF16),
            SDS((bsz, nb, d, ts), BF16),
            SDS((bsz, nb, ATT_U, d), F32),
        ],
        compiler_params=_cparams(("parallel", "arbitrary")),
        name="kvq",
    )(x, mod, kvmod, n1g.reshape(1, d), kvg.reshape(1, d), w_q.astype(BF16), w_kv.astype(BF16),
      jnp.tile(q_g, N_HEADS).reshape(1, d), jnp.tile(k_g, N_HEADS).reshape(1, d))


def _bf16_pieces(v):
    hi = v.astype(BF16).astype(F32)
    mid = (v - hi).astype(BF16).astype(F32)
    lo = ((v - hi) - mid).astype(BF16).astype(F32)
    return hi, mid, lo


def _attn_body(slopes_ref, q_ref, k_ref, vt_ref, km_ref, o_ref, cm_scr, kaug_scr, qaug_scr, s_scr):
    hp = pl.program_id(1)
    own = pl.program_id(2)
    bs = MOBA_BLOCK
    nb = k_ref.shape[1] * ATT_U
    pw = 2 * HEAD_DIM
    nh = 2 * ATT_HP

    @pl.when(own == 0)
    def _():
        rel = (lax.broadcasted_iota(jnp.int32, (bs, bs), 1)
               - lax.broadcasted_iota(jnp.int32, (bs, bs), 0))
        cm_scr[0] = jnp.zeros((bs, bs), F32)
        cm_scr[1] = jnp.where(rel >= 0, 0.0, -jnp.inf)
        klane = lax.broadcasted_iota(jnp.int32, (ATT_U * bs, ATT_AUG), 1)
        koff = (lax.broadcasted_iota(jnp.int32, (ATT_U * bs, ATT_AUG), 0) % bs).astype(F32)
        kaug_scr[...] = jnp.where(klane < 3, 1.0, jnp.where(klane < 6, koff, 0.0)).astype(BF16)
        qlane = lax.broadcasted_iota(jnp.int32, (bs, ATT_AUG), 1)
        qoff = lax.broadcasted_iota(jnp.int32, (bs, ATT_AUG), 0).astype(F32)
        for hd in range(nh):
            s2 = jnp.full((bs, ATT_AUG), slopes_ref[nh * hp + hd], F32) * LOG2E
            pieces = _bf16_pieces(-(s2 * qoff)) + _bf16_pieces(s2)
            aug = jnp.zeros((bs, ATT_AUG), F32)
            for lane_ix, piece in enumerate(pieces):
                aug = jnp.where(qlane == lane_ix, piece, aug)
            qaug_scr[hd] = aug.astype(BF16)

    dcol = lax.broadcasted_iota(jnp.int32, (1, pw), 1)
    blk_row = lax.broadcasted_iota(jnp.int32, (nb, 1), 0)
    valid = blk_row < own
    in_head = [(dcol >= h * HEAD_DIM) & (dcol < (h + 1) * HEAD_DIM) for h in range(2)]
    pair_cols = lambda hd: slice((hd // 2) * pw, (hd // 2 + 1) * pw)

    gates = []
    for pair in range(nh // 2):
        qp = q_ref[0, :, pair_cols(2 * pair)]
        kmp = km_ref[0, :, pair_cols(2 * pair)]
        km2 = jnp.concatenate([jnp.where(in_head[h], kmp, 0.0) for h in range(2)], axis=0)
        gates.append(_dot_3pass_nt(km2, qp))

    qs, selbias = [], []
    for hd in range(nh):
        slope = slopes_ref[nh * hp + hd]
        qm = jnp.where(in_head[hd % 2], q_ref[0, :, pair_cols(hd)], 0.0)
        gate = jnp.where(valid, gates[hd // 2][(hd % 2) * nb:(hd % 2 + 1) * nb], -jnp.inf)
        rank = jnp.zeros(gate.shape, F32)
        for i in range(nb):
            gi = gate[i:i + 1, :]
            tie = jnp.where(blk_row > i, 1.0, 0.0)
            rank = rank + jnp.where(gi > gate, 1.0, jnp.where(gi == gate, tie, 0.0))
        chosen = jnp.where(valid, jnp.where(rank < MOBA_TOPK, 1.0, 0.0),
                           jnp.where(blk_row == own, 1.0, 0.0))
        off = (own - blk_row).astype(F32) * (bs * LOG2E * slope)
        selbias.append(jnp.where(chosen > 0.5, -off, -jnp.inf))
        qs.append(jnp.concatenate([(qm * (LOG2E * HEAD_DIM ** -0.5)).astype(BF16), qaug_scr[hd]], axis=1))

    last = k_ref.shape[1] - 1
    ones = jnp.ones((ATT_ONES, ATT_U * bs), BF16)

    def qk(it, hd):
        kc = jnp.concatenate([k_ref[0, it, :, pair_cols(hd)], kaug_scr[...]], axis=1)
        return lax.dot_general(kc, qs[hd], NT_DIMS, preferred_element_type=F32)

    def visit(it, slot, carry, next_it, has_own):
        out = []
        if next_it is not None:
            for hd in range(nh):
                s_scr[1 - slot, hd] = qk(jnp.minimum(next_it, last), hd)
        for hd in range(nh):
            m, acc = carry[hd]

            def tile(u):
                t = s_scr[slot, hd, u * bs:(u + 1) * bs]
                if has_own:
                    t = t + cm_scr[(it * ATT_U + u == own).astype(jnp.int32)]
                return t

            rows = [jnp.sum(jnp.where(blk_row == it * ATT_U + u, selbias[hd], 0.0), axis=0, keepdims=True)
                    for u in range(ATT_U)]
            mx = m
            for u in range(ATT_U):
                mx = jnp.maximum(mx, jnp.max(tile(u), axis=0, keepdims=True) + rows[u])
            m_safe = jnp.where(mx == -jnp.inf, 0.0, mx)
            alpha = jnp.exp2(m - m_safe)
            p = jnp.concatenate([jnp.exp2((tile(u) + (rows[u] - m_safe)).astype(BF16)) for u in range(ATT_U)],
                                axis=0)
            vj = jnp.concatenate([vt_ref[0, it, hd * HEAD_DIM:(hd + 1) * HEAD_DIM, :], ones], axis=0)
            acc = alpha * acc + _dot(vj, p)
            out.append((mx, acc))
        return tuple(out)

    def body(i2, carry):
        carry = visit(2 * i2, 1, carry, 2 * i2 + 1, False)
        return visit(2 * i2 + 1, 0, carry, 2 * i2 + 2, False)

    n_past = own // ATT_U
    for hd in range(nh):
        s_scr[0, hd] = qk(n_past, hd)
    init = tuple((jnp.full((1, bs), -jnp.inf, F32), jnp.zeros((HEAD_DIM + ATT_ONES, bs), F32))
                 for hd in range(nh))
    carry = visit(n_past, 0, init, 0, True)
    carry = lax.fori_loop(0, n_past // 2, body, carry)
    carry = lax.cond(n_past % 2 == 1, lambda c: visit(n_past - 1, 1, c, None, False), lambda c: c, carry)
    outs = [acc[:HEAD_DIM] / acc[HEAD_DIM:HEAD_DIM + 1] for (_, acc) in carry]
    o_ref[0] = jnp.concatenate(outs, axis=0).astype(o_ref.dtype)


def _attn(q, kb, vtb, kmean):
    bsz, s, d = q.shape
    nsb = s // (ATT_U * MOBA_BLOCK)
    assert nsb % 2 == 0, "the attention loop visits key super-blocks in pairs"
    nb = s // MOBA_BLOCK
    w = 2 * HEAD_DIM * ATT_HP
    slopes = (2.0 ** (-8.0 * jnp.arange(1, N_HEADS + 1, dtype=F32) / N_HEADS)).astype(F32)
    return pl.pallas_call(
        _attn_body,
        grid_spec=pltpu.PrefetchScalarGridSpec(
            num_scalar_prefetch=1,
            grid=(bsz, N_HEADS // (2 * ATT_HP), nb),
            in_specs=[
                pl.BlockSpec((1, MOBA_BLOCK, w), lambda b, h, c, sl: (b, c, h)),
                pl.BlockSpec((1, nsb, ATT_U * MOBA_BLOCK, w), lambda b, h, c, sl: (b, 0, 0, h)),
                pl.BlockSpec((1, nsb, w, ATT_U * MOBA_BLOCK), lambda b, h, c, sl: (b, 0, h, 0)),
                pl.BlockSpec((1, nb, w), lambda b, h, c, sl: (b, 0, h)),
            ],
            out_specs=pl.BlockSpec((1, w, MOBA_BLOCK), lambda b, h, c, sl: (b, h, c)),
            scratch_shapes=[pltpu.VMEM((2, MOBA_BLOCK, MOBA_BLOCK), F32),
                            pltpu.VMEM((ATT_U * MOBA_BLOCK, ATT_AUG), BF16),
                            pltpu.VMEM((2 * ATT_HP, MOBA_BLOCK, ATT_AUG), BF16),
                            pltpu.VMEM((2, 2 * ATT_HP, ATT_U * MOBA_BLOCK, MOBA_BLOCK), F32)],
        ),
        out_shape=SDS((bsz, d, s), BF16),
        compiler_params=_cparams(("parallel", "parallel", "arbitrary")),
        name="attn",
    )(slopes, q, kb, vtb, kmean)


OPROJ_TS = 1024


def _oproj_body(at_ref, x_ref, mod_ref, wo_ref, o_ref):
    g1 = mod_ref[0][2:3]
    m = lax.dot_general(at_ref[0], wo_ref[...], (((0,), (0,)), ((), ())), preferred_element_type=F32)
    o_ref[0] = x_ref[0] + g1 * m


def _oproj(at, x, mod, w_o):
    bsz, s, d = x.shape
    ts = min(OPROJ_TS, s)
    tok = pl.BlockSpec((1, ts, d), lambda b, i: (b, i, 0))
    return pl.pallas_call(
        _oproj_body,
        grid=(bsz, s // ts),
        in_specs=[pl.BlockSpec((1, d, ts), lambda b, i: (b, 0, i)), tok,
                  pl.BlockSpec((1, 6, d), lambda b, i: (b, 0, 0)),
                  pl.BlockSpec((d, d), lambda b, i: (0, 0))],
        out_specs=tok,
        out_shape=SDS((bsz, s, d), F32),
        compiler_params=_cparams(("parallel", "arbitrary")),
        name="oproj",
    )(at, x, mod, w_o.astype(BF16))


def kernel(x, c, ada_w, ada_b, norm1_g, norm2_g, sgu_w_in, sgu_b_in, sgu_v_g, sgu_w_s, sgu_b_s, sgu_w_out,
           kv_ada_w, kv_ada_b, kv_norm_g, w_kv, k_norm_g, attn_w_q, q_norm_g, attn_w_o,
           moe_w_router, moe_b_router, moe_w_gate, moe_b_gate, moe_w_up, moe_b_up, moe_w_down, moe_b_down):
    bsz, s, d = x.shape
    moe = lambda l, xx, mod: _moe(xx, mod, l, norm2_g[l], moe_w_router[l], moe_b_router[l],
                                  moe_w_gate, moe_b_gate, moe_w_up, moe_b_up, moe_w_down, moe_b_down)
    mod0 = _ada(c, ada_w, ada_b[0], 0).reshape(bsz, 6, d)
    x = _sgu(x, mod0, norm1_g[0], sgu_w_in[0], sgu_b_in[0], sgu_v_g[0], sgu_w_s[0], sgu_b_s[0], sgu_w_out[0])
    x = moe(0, x, mod0)
    mod1 = _ada(c, ada_w, ada_b[1], 1).reshape(bsz, 6, d)
    kvmod = _ada(c, kv_ada_w, kv_ada_b).reshape(bsz, 2, d)
    q, kb, vtb, kmean = _kvq(x, mod1, kvmod, norm1_g[1], kv_norm_g, attn_w_q[0], w_kv, q_norm_g[0], k_norm_g)
    a = _attn(q, kb, vtb, kmean.reshape(bsz, s // MOBA_BLOCK, d))
    x = _oproj(a, x, mod1, attn_w_o[0])
    x = moe(1, x, mod1)
    return x
```
